```python
import math
import jax, jax.numpy as jnp
from jax import lax
import numpy as np

D_MODEL = 1024
BATCH = 4
SEQ = 4096
DEPTH = 2

MLA_HEADS = 4
MLA_NOPE = 128
MLA_ROPE = 64
MLA_VDIM = 128
MLA_Q_LORA = 256
MLA_KV_LORA = 256
MLA_WIDTH = MLA_HEADS * MLA_VDIM
ROPE_THETA = 10000.0
Q_BLOCK = 128

RWKV_HEADS = 4
RWKV_HEAD = 64
RWKV_WIDTH = RWKV_HEADS * RWKV_HEAD
RWKV_DECAY_LORA = 32
RWKV_AAA_LORA = 32
RWKV_GATE_LORA = 64
RWKV_LN_EPS = 64e-5

MLSTM_HEADS = 4
MLSTM_QK = 32
MLSTM_V = 64
MLSTM_WIDTH = MLSTM_HEADS * MLSTM_V
MLSTM_CHUNK = 64
MLSTM_CONV = 4

D_MIX = MLA_WIDTH + RWKV_WIDTH + MLSTM_WIDTH

D_FF = -(-8 * D_MODEL // (3 * 256)) * 256
NORM_EPS = 1e-6

MLA_SPLITS = (MLA_Q_LORA, MLA_KV_LORA, MLA_ROPE)
RWKV_SPLITS = (RWKV_WIDTH, RWKV_WIDTH, RWKV_WIDTH, RWKV_DECAY_LORA, RWKV_AAA_LORA, RWKV_GATE_LORA)
MLSTM_SPLITS = (MLSTM_HEADS * MLSTM_QK, MLSTM_HEADS * MLSTM_QK, MLSTM_WIDTH, MLSTM_HEADS, MLSTM_HEADS, MLSTM_WIDTH)
GROUP_SPLITS = (sum(MLA_SPLITS), sum(RWKV_SPLITS), sum(MLSTM_SPLITS))
P_IN = sum(GROUP_SPLITS)

kernel_name = "hybrid_mla_rwkv7_mlstm_block"


def _split(x, sizes):
    offs = np.cumsum(np.array(sizes))[:-1]
    return jnp.split(x, [int(o) for o in offs], axis=-1)


def _rmsnorm(x, g):
    xf = x.astype(jnp.float32)
    y = xf * lax.rsqrt(jnp.mean(xf * xf, axis=-1, keepdims=True) + NORM_EPS)
    return (y * g.astype(jnp.float32)).astype(x.dtype)


def _head_rmsnorm(x, g):
    B, S, H, dh = x.shape
    xf = x.astype(jnp.float32)
    y = xf * lax.rsqrt(jnp.mean(xf * xf, axis=-1, keepdims=True) + NORM_EPS)
    return y.reshape(B, S, H * dh) * g.astype(jnp.float32)


def _rope_tables(positions, dim):
    inv_freq = ROPE_THETA ** (-jnp.arange(0, dim, 2, dtype=jnp.float32) / dim)
    ang = positions.astype(jnp.float32)[..., None] * inv_freq
    return jnp.cos(ang), jnp.sin(ang)


def _rope(x, cos, sin):
    xf = x.astype(jnp.float32)
    x1, x2 = jnp.split(xf, 2, axis=-1)
    return jnp.concatenate([x1 * cos - x2 * sin, x1 * sin + x2 * cos], axis=-1).astype(x.dtype)


def _causal_conv(x, w, b):
    K = w.shape[0]
    S = x.shape[1]
    xp = jnp.pad(x, ((0, 0), (K - 1, 0), (0, 0)))
    out = b
    for j in range(K):
        out = out + xp[:, j:j + S] * w[j]
    return out


def _mla(c_q, c_kv, k_pe, positions, q_norm, w_uq, kv_norm, w_ukv, out_norm):
    B, S, _ = c_q.shape
    H = MLA_HEADS
    q = (_rmsnorm(c_q, q_norm) @ w_uq).reshape(B, S, H, MLA_NOPE + MLA_ROPE)
    q_nope, q_pe = q[..., :MLA_NOPE], q[..., MLA_NOPE:]
    kv = (_rmsnorm(c_kv, kv_norm) @ w_ukv).reshape(B, S, H, MLA_NOPE + MLA_VDIM)
    k_nope, v = kv[..., :MLA_NOPE], kv[..., MLA_NOPE:]
    cos, sin = _rope_tables(positions, MLA_ROPE)
    q_pe = _rope(q_pe, cos[:, :, None, :], sin[:, :, None, :])
    k_pe = _rope(k_pe, cos, sin)
    scale = (MLA_NOPE + MLA_ROPE) ** -0.5
    nb = S // Q_BLOCK
    qn_b = jnp.moveaxis(q_nope.reshape(B, nb, Q_BLOCK, H, MLA_NOPE), 1, 0)
    qp_b = jnp.moveaxis(q_pe.reshape(B, nb, Q_BLOCK, H, MLA_ROPE), 1, 0)
    key_pos = jnp.arange(S)

    def block(args):
        qn, qp, i = args
        s = (jnp.einsum('bqhd,bkhd->bhqk', qn, k_nope)
             + jnp.einsum('bqhr,bkr->bhqk', qp, k_pe)).astype(jnp.float32) * scale
        q_idx = i * Q_BLOCK + jnp.arange(Q_BLOCK)
        s = jnp.where(key_pos[None, :] <= q_idx[:, None], s, -jnp.inf)
        p = jax.nn.softmax(s, axis=-1).astype(v.dtype)
        return jnp.einsum('bhqk,bkhd->bqhd', p, v)

    o = lax.map(block, (qn_b, qp_b, jnp.arange(nb)))
    o = jnp.moveaxis(o, 0, 1).reshape(B, S, H, MLA_VDIM)
    return _head_rmsnorm(o, out_norm)


def _rwkv7(p, mu, w0, w2, a0, a2, g2, k_k, k_a, r_k, ln_w, ln_b):
    B, S, _ = p.shape
    H, N = RWKV_HEADS, RWKV_HEAD
    prev = jnp.pad(p, ((0, 0), (1, 0), (0, 0)))[:, :-1]
    p = p + (prev - p) * mu
    r, k, v, xw, xa, xg = _split(p, RWKV_SPLITS)
    w_log = -jax.nn.softplus(-(w0 + jnp.tanh(xw) @ w2).astype(jnp.float32)) - 0.5
    decay = jnp.exp(-jnp.exp(w_log))
    a = jax.nn.sigmoid((a0 + xa @ a2).astype(jnp.float32))
    g = jax.nn.sigmoid(xg) @ g2
    r = r.astype(jnp.float32).reshape(B, S, H, N)
    k = k.astype(jnp.float32)
    v = v.astype(jnp.float32).reshape(B, S, H, N)
    kk = (k * k_k.astype(jnp.float32)).reshape(B, S, H, N)
    kk = kk / jnp.maximum(jnp.sqrt(jnp.sum(kk * kk, axis=-1, keepdims=True)), 1e-12)
    k = (k * (1.0 + (a - 1.0) * k_a.astype(jnp.float32))).reshape(B, S, H, N)
    a = a.reshape(B, S, H, N)
    decay = decay.reshape(B, S, H, N)

    def step(state, inp):
        r_t, w_t, k_t, v_t, kk_t, a_t = inp
        sa = jnp.einsum('bhij,bhj->bhi', state, -kk_t)
        state = (state * w_t[:, :, None, :]
                 + sa[..., :, None] * (kk_t * a_t)[..., None, :]
                 + v_t[..., :, None] * k_t[..., None, :])
        return state, jnp.einsum('bhij,bhj->bhi', state, r_t)

    xs = tuple(jnp.moveaxis(t, 1, 0) for t in (r, decay, k, v, kk, a))
    _, y = lax.scan(step, jnp.zeros((B, H, N, N), jnp.float32), xs)
    y = jnp.moveaxis(y, 0, 1)
    mean = jnp.mean(y, axis=-1, keepdims=True)
    var = jnp.mean(jnp.square(y - mean), axis=-1, keepdims=True)
    y = ((y - mean) * lax.rsqrt(var + RWKV_LN_EPS)).reshape(B, S, RWKV_WIDTH)
    y = y * ln_w.astype(jnp.float32) + ln_b.astype(jnp.float32)
    bonus = jnp.sum(r * k * r_k.astype(jnp.float32).reshape(H, N), axis=-1, keepdims=True) * v
    y = y + bonus.reshape(B, S, RWKV_WIDTH)
    return y * g.astype(jnp.float32)


def _mlstm(p, conv_w, conv_b, i_bias, f_bias, out_norm):
    B, S, _ = p.shape
    H, DK, DV, L = MLSTM_HEADS, MLSTM_QK, MLSTM_V, MLSTM_CHUNK
    NC = S // L
    q, k, v, i_pre, f_pre, o_pre = _split(p, MLSTM_SPLITS)
    qk = jax.nn.silu(_causal_conv(jnp.concatenate([q, k], axis=-1), conv_w, conv_b))
    q, k = jnp.split(qk, 2, axis=-1)

    def heads(t, d):
        return t.astype(jnp.float32).reshape(B, NC, L, H, d).transpose(0, 3, 1, 2, 4)

    q = heads(q, DK) * DK ** -0.5
    k = heads(k, DK)
    v = heads(v, DV)
    log_i = (i_pre + i_bias).astype(jnp.float32).reshape(B, NC, L, H).transpose(0, 3, 1, 2)
    log_f = jax.nn.log_sigmoid((f_pre + f_bias).astype(jnp.float32)).reshape(B, NC, L, H).transpose(0, 3, 1, 2)
    g = jnp.cumsum(log_f, axis=-1)
    g_last = g[..., -1]
    a = g_last[..., None] - g + log_i

    def chunk_step(carry, inp):
        C, n, m = carry
        k_c, v_c, a_c, gl_c = inp
        m_new = jnp.maximum(gl_c + m, jnp.max(a_c, axis=-1))
        dec = jnp.exp(gl_c + m - m_new)
        wts = jnp.exp(a_c - m_new[..., None])
        C_new = dec[..., None, None] * C + jnp.einsum('bhl,bhld,bhle->bhde', wts, k_c, v_c)
        n_new = dec[..., None] * n + jnp.einsum('bhl,bhld->bhd', wts, k_c)
        return (C_new, n_new, m_new), (C, n, m)

    init = (jnp.zeros((B, H, DK, DV), jnp.float32), jnp.zeros((B, H, DK), jnp.float32),
            jnp.zeros((B, H), jnp.float32))
    xs = (jnp.moveaxis(k, 2, 0), jnp.moveaxis(v, 2, 0), jnp.moveaxis(a, 2, 0), jnp.moveaxis(g_last, 2, 0))
    _, (C_prev, n_prev, m_prev) = lax.scan(chunk_step, init, xs)
    C_prev = jnp.moveaxis(C_prev, 0, 2)
    n_prev = jnp.moveaxis(n_prev, 0, 2)
    m_prev = jnp.moveaxis(m_prev, 0, 2)

    causal = jnp.tril(jnp.ones((L, L), dtype=bool))
    D = jnp.where(causal, g[..., :, None] - g[..., None, :] + log_i[..., None, :], -jnp.inf)
    inter_log = g + m_prev[..., None]
    m_t = jnp.maximum(inter_log, jnp.max(D, axis=-1))
    inter_w = jnp.exp(inter_log - m_t)
    s = jnp.einsum('bhctd,bhcjd->bhctj', q, k) * jnp.exp(D - m_t[..., None])
    num = (inter_w[..., None] * jnp.einsum('bhctd,bhcde->bhcte', q, C_prev)
           + jnp.einsum('bhctj,bhcje->bhcte', s, v))
    den = inter_w * jnp.einsum('bhctd,bhcd->bhct', q, n_prev) + jnp.sum(s, axis=-1)
    den = jnp.maximum(jnp.abs(den), jnp.exp(-m_t))
    h = (num / den[..., None]).transpose(0, 2, 3, 1, 4).reshape(B, S, H, DV)
    return _head_rmsnorm(h, out_norm) * jax.nn.sigmoid(o_pre.astype(jnp.float32))


def setup_inputs(seed: int = 0) -> dict:
    key = jax.random.key(seed)
    ks = jax.random.split(key, 32)
    f32 = jnp.float32

    def nrm(k, shape, scale):
        return jax.random.normal(k, shape, f32) * scale

    def gain(k, shape):
        return 1.0 + 0.02 * jax.random.normal(k, shape, f32)

    offset = jax.random.randint(ks[1], (BATCH, 1), 0, 2048, dtype=jnp.int32)
    positions = offset + jnp.arange(SEQ, dtype=jnp.int32)[None, :]
    w0 = jnp.linspace(-6.0, -1.0, RWKV_WIDTH, dtype=f32) + 0.5
    f_b = jnp.linspace(3.0, 6.0, MLSTM_HEADS, dtype=f32)
    return {
        "x": nrm(ks[0], (BATCH, SEQ, D_MODEL), 1.0),
        "positions": positions,
        "mix_norm": gain(ks[2], (DEPTH, D_MODEL)),
        "w_in": nrm(ks[3], (DEPTH, D_MODEL, P_IN), D_MODEL ** -0.5),
        "mla_q_norm": gain(ks[4], (DEPTH, MLA_Q_LORA)),
        "mla_w_uq": nrm(ks[5], (DEPTH, MLA_Q_LORA, MLA_HEADS * (MLA_NOPE + MLA_ROPE)), MLA_Q_LORA ** -0.5),
        "mla_kv_norm": gain(ks[6], (DEPTH, MLA_KV_LORA)),
        "mla_w_ukv": nrm(ks[7], (DEPTH, MLA_KV_LORA, MLA_HEADS * (MLA_NOPE + MLA_VDIM)), MLA_KV_LORA ** -0.5),
        "mla_out_norm": gain(ks[8], (DEPTH, MLA_WIDTH)),
        "rwkv_mu": jax.random.uniform(ks[9], (DEPTH, GROUP_SPLITS[1]), f32),
        "rwkv_w0": w0[None, :] + nrm(ks[10], (DEPTH, RWKV_WIDTH), 0.1),
        "rwkv_w2": nrm(ks[11], (DEPTH, RWKV_DECAY_LORA, RWKV_WIDTH), 0.1 * RWKV_DECAY_LORA ** -0.5),
        "rwkv_a0": nrm(ks[12], (DEPTH, RWKV_WIDTH), 0.1),
        "rwkv_a2": nrm(ks[13], (DEPTH, RWKV_AAA_LORA, RWKV_WIDTH), RWKV_AAA_LORA ** -0.5),
        "rwkv_g2": nrm(ks[14], (DEPTH, RWKV_GATE_LORA, RWKV_WIDTH), RWKV_GATE_LORA ** -0.5),
        "rwkv_k_k": 0.85 + nrm(ks[15], (DEPTH, RWKV_WIDTH), 0.02),
        "rwkv_k_a": gain(ks[16], (DEPTH, RWKV_WIDTH)),
        "rwkv_r_k": nrm(ks[17], (DEPTH, RWKV_WIDTH), 0.1),
        "rwkv_ln_w": gain(ks[18], (DEPTH, RWKV_WIDTH)),
        "rwkv_ln_b": nrm(ks[19], (DEPTH, RWKV_WIDTH), 0.02),
        "mlstm_conv_w": nrm(ks[20], (DEPTH, MLSTM_CONV, 2 * MLSTM_HEADS * MLSTM_QK), MLSTM_CONV ** -0.5),
        "mlstm_conv_b": nrm(ks[21], (DEPTH, 2 * MLSTM_HEADS * MLSTM_QK), 0.02),
        "mlstm_i_bias": -1.0 + nrm(ks[22], (DEPTH, MLSTM_HEADS), 0.1),
        "mlstm_f_bias": f_b[None, :] + nrm(ks[23], (DEPTH, MLSTM_HEADS), 0.1),
        "mlstm_out_norm": gain(ks[24], (DEPTH, MLSTM_WIDTH)),
        "w_out": nrm(ks[25], (DEPTH, D_MIX, D_MODEL), D_MIX ** -0.5),
        "ffn_norm": gain(ks[26], (DEPTH, D_MODEL)),
        "w_gate": nrm(ks[27], (DEPTH, D_MODEL, D_FF), D_MODEL ** -0.5),
        "w_up": nrm(ks[28], (DEPTH, D_MODEL, D_FF), D_MODEL ** -0.5),
        "w_down": nrm(ks[29], (DEPTH, D_FF, D_MODEL), D_FF ** -0.5),
        "final_norm": gain(ks[30], (D_MODEL,)),
    }


def reference(x, positions, mix_norm, w_in, mla_q_norm, mla_w_uq, mla_kv_norm, mla_w_ukv, mla_out_norm,
              rwkv_mu, rwkv_w0, rwkv_w2, rwkv_a0, rwkv_a2, rwkv_g2, rwkv_k_k, rwkv_k_a, rwkv_r_k,
              rwkv_ln_w, rwkv_ln_b, mlstm_conv_w, mlstm_conv_b, mlstm_i_bias, mlstm_f_bias, mlstm_out_norm,
              w_out, ffn_norm, w_gate, w_up, w_down, final_norm):
    for l in range(DEPTH):
        h = _rmsnorm(x, mix_norm[l])
        p = h @ w_in[l]
        p_mla, p_rwkv, p_mlstm = _split(p, GROUP_SPLITS)
        c_q, c_kv, k_pe = _split(p_mla, MLA_SPLITS)
        y_mla = _mla(c_q, c_kv, k_pe, positions, mla_q_norm[l], mla_w_uq[l], mla_kv_norm[l],
                     mla_w_ukv[l], mla_out_norm[l])
        y_rwkv = _rwkv7(p_rwkv, rwkv_mu[l], rwkv_w0[l], rwkv_w2[l], rwkv_a0[l], rwkv_a2[l], rwkv_g2[l],
                        rwkv_k_k[l], rwkv_k_a[l], rwkv_r_k[l], rwkv_ln_w[l], rwkv_ln_b[l])
        y_mlstm = _mlstm(p_mlstm, mlstm_conv_w[l], mlstm_conv_b[l], mlstm_i_bias[l], mlstm_f_bias[l],
                         mlstm_out_norm[l])
        y = jnp.concatenate([y_mla.astype(x.dtype), y_rwkv.astype(x.dtype), y_mlstm.astype(x.dtype)], axis=-1)
        x = x + y @ w_out[l]
        h = _rmsnorm(x, ffn_norm[l])
        x = x + (jax.nn.silu(h @ w_gate[l]) * (h @ w_up[l])) @ w_down[l]
    return _rmsnorm(x, final_norm)
```

```python
import functools
import math

import jax
import jax.numpy as jnp
from jax import lax
from jax.experimental import pallas as pl
from jax.experimental.pallas import tpu as pltpu

F32 = jnp.float32
BF16 = jnp.bfloat16

D_MODEL = 1024
DEPTH = 2
MLA_HEADS = 4
MLA_NOPE = 128
MLA_ROPE = 64
MLA_VDIM = 128
MLA_Q_LORA = 256
MLA_KV_LORA = 256
MLA_WIDTH = MLA_HEADS * MLA_VDIM
MLA_QK = 256
ROPE_THETA = 10000.0
RWKV_HEADS = 4
RWKV_HEAD = 64
RWKV_WIDTH = 256
RWKV_DECAY_LORA = 32
RWKV_AAA_LORA = 32
RWKV_GATE_LORA = 64
RWKV_IN = 3 * RWKV_WIDTH + 128
RWKV_LN_EPS = 64e-5
MLSTM_HEADS = 4
MLSTM_QK = 32
MLSTM_V = 64
MLSTM_WIDTH = 256
MLSTM_CONV = 4
MLSTM_IN = 1024
MLA_IN = 768
D_FF = 2816
NORM_EPS = 1e-6
LANES = 128

TM_INPROJ = 512
TM_PREP = 512
TQ_ATTN = 512
TK_ATTN = 512
RWKV_CHUNK = 64
MLSTM_CHUNK = 256
TM_FFN = 512
TF_FFN = 256
VMEM_LIMIT = 56 * 1024 * 1024


def _cparams(sem):
    return pltpu.CompilerParams(dimension_semantics=sem, vmem_limit_bytes=VMEM_LIMIT)


def _bdot(a, b):
    return jnp.dot(a.astype(BF16), b.astype(BF16), preferred_element_type=F32)


def _bdot_nt(a, b):
    return lax.dot_general(a.astype(BF16), b.astype(BF16), (((1,), (1,)), ((), ())),
                           preferred_element_type=F32)


def _bdot_tn(a, b):
    return lax.dot_general(a.astype(BF16), b.astype(BF16), (((0,), (0,)), ((), ())),
                           preferred_element_type=F32)


def _split3(x):
    h = x.astype(BF16)
    r1 = x - h.astype(F32)
    m = r1.astype(BF16)
    lo = (r1 - m.astype(F32)).astype(BF16)
    return h, m, lo


def _exact_left_dot(sel, x):
    h, m, lo = _split3(x)
    s = sel.astype(BF16)
    return (jnp.dot(s, h, preferred_element_type=F32) + jnp.dot(s, m, preferred_element_type=F32)
            + jnp.dot(s, lo, preferred_element_type=F32))


def _rms(x, g):
    return x * lax.rsqrt(jnp.mean(x * x, axis=-1, keepdims=True) + NORM_EPS) * g


def _sigmoid(x):
    return 1.0 / (1.0 + jnp.exp(-x))


def _log_sigmoid(x):
    return jnp.minimum(x, 0.0) - jnp.log1p(jnp.exp(-jnp.abs(x)))


def _div(x, d):
    assert d & (d - 1) == 0
    return lax.shift_right_logical(x, d.bit_length() - 1)


def _mod(x, d):
    assert d & (d - 1) == 0
    return lax.bitwise_and(x, d - 1)


def _segsum(x, seg):
    lane = lax.broadcasted_iota(jnp.int32, x.shape, 1)
    out = jnp.zeros_like(x)
    for h in range(x.shape[1] // seg):
        m = (lane >= h * seg) & (lane < (h + 1) * seg)
        s = jnp.sum(jnp.where(m, x, 0.0), axis=-1, keepdims=True)
        out = jnp.where(m, s, out)
    return out


def _shift_rows(x, prev, s, chunk):
    n = x.shape[0]
    row = lax.broadcasted_iota(jnp.int32, x.shape, 0)
    return jnp.where(_mod(row, chunk) >= s, pltpu.roll(x, s, 0), pltpu.roll(prev, n - chunk + s, 0))


def _rope_kernel(pos_ref, invf_ref, cos_ref, sin_ref):
    ang = pos_ref[...].astype(F32) * invf_ref[...]
    cos_ref[...] = jnp.cos(ang)
    sin_ref[...] = jnp.sin(ang)


def _rope_tables(positions):
    t = positions.size
    tm = min(1024, t)
    inv_freq = ROPE_THETA ** (-jnp.arange(0, MLA_ROPE, 2, dtype=F32) / MLA_ROPE)
    invf = jnp.tile(inv_freq, LANES // (MLA_ROPE // 2))[None, :]
    return pl.pallas_call(
        _rope_kernel,
        out_shape=(jax.ShapeDtypeStruct((t, LANES), F32), jax.ShapeDtypeStruct((t, LANES), F32)),
        grid=(t // tm,),
        in_specs=[pl.BlockSpec((tm, 1), lambda i: (i, 0)), pl.BlockSpec((1, LANES), lambda i: (0, 0))],
        out_specs=(pl.BlockSpec((tm, LANES), lambda i: (i, 0)), pl.BlockSpec((tm, LANES), lambda i: (i, 0))),
        compiler_params=_cparams(("parallel",)),
        name="rope_tables",
    )(positions.reshape(t, 1), invf)


def _inproj_kernel(x_ref, g_ref, w_ref, mla_ref, rwkv_ref, mlstm_ref):
    hb = _rms(x_ref[...], g_ref[...]).astype(BF16)
    mla_ref[...] = jnp.dot(hb, w_ref[:, 0:MLA_IN], preferred_element_type=F32)
    rwkv_ref[...] = jnp.dot(hb, w_ref[:, MLA_IN:MLA_IN + RWKV_IN], preferred_element_type=F32)
    mlstm_ref[...] = jnp.dot(hb, w_ref[:, MLA_IN + RWKV_IN:], preferred_element_type=F32)


def _inproj(x, g, w):
    t = x.shape[0]
    tm = TM_INPROJ
    n = w.shape[1]
    return pl.pallas_call(
        _inproj_kernel,
        out_shape=(jax.ShapeDtypeStruct((t, MLA_IN), F32), jax.ShapeDtypeStruct((t, RWKV_IN), F32),
                   jax.ShapeDtypeStruct((t, MLSTM_IN), F32)),
        grid=(t // tm,),
        in_specs=[pl.BlockSpec((tm, D_MODEL), lambda i: (i, 0)), pl.BlockSpec((1, D_MODEL), lambda i: (0, 0)),
                  pl.BlockSpec((D_MODEL, n), lambda i: (0, 0))],
        out_specs=(pl.BlockSpec((tm, MLA_IN), lambda i: (i, 0)), pl.BlockSpec((tm, RWKV_IN), lambda i: (i, 0)),
                   pl.BlockSpec((tm, MLSTM_IN), lambda i: (i, 0))),
        compiler_params=_cparams(("parallel",)),
        name="inproj",
    )(x, g, w)


def _mla_prep_kernel(in_ref, cos_ref, sin_ref, qn_ref, kvn_ref, wq_ref, wkv_ref, q_ref, k_ref, v_ref):
    cos = cos_ref[...]
    sin = sin_ref[...]
    scale = (MLA_NOPE + MLA_ROPE) ** -0.5
    hw = MLA_HEADS * LANES
    cqn = _rms(in_ref[:, 0:MLA_Q_LORA], qn_ref[...]).astype(BF16)
    q = jnp.dot(cqn, wq_ref[...], preferred_element_type=F32)
    ckvn = _rms(in_ref[:, MLA_Q_LORA:MLA_Q_LORA + MLA_KV_LORA], kvn_ref[...]).astype(BF16)
    kv = jnp.dot(ckvn, wkv_ref[...], preferred_element_type=F32)
    kp = (in_ref[:, 512:640] * cos + in_ref[:, 640:768] * sin).astype(BF16)
    for h in range(MLA_HEADS):
        c0 = h * LANES
        pe = q[:, hw + c0:hw + c0 + LANES] * cos + q[:, 2 * hw + c0:2 * hw + c0 + LANES] * sin
        q_ref[:, h * MLA_QK:h * MLA_QK + LANES] = (q[:, c0:c0 + LANES] * scale).astype(BF16)
        q_ref[:, h * MLA_QK + LANES:(h + 1) * MLA_QK] = (pe * scale).astype(BF16)
        k_ref[:, h * MLA_QK:h * MLA_QK + LANES] = kv[:, c0:c0 + LANES].astype(BF16)
        k_ref[:, h * MLA_QK + LANES:(h + 1) * MLA_QK] = kp
    v_ref[...] = kv[:, hw:].astype(BF16)


def _mla_prep(mla_in, cos, sin, qn, kvn, wq, wkv):
    t = mla_in.shape[0]
    tm = TM_PREP
    row = lambda w: pl.BlockSpec((tm, w), lambda i: (i, 0))
    full = lambda a: pl.BlockSpec(a.shape, lambda i: (0, 0))
    return pl.pallas_call(
        _mla_prep_kernel,
        out_shape=(jax.ShapeDtypeStruct((t, MLA_HEADS * MLA_QK), BF16),
                   jax.ShapeDtypeStruct((t, MLA_HEADS * MLA_QK), BF16),
                   jax.ShapeDtypeStruct((t, MLA_WIDTH), BF16)),
        grid=(t // tm,),
        in_specs=[row(MLA_IN), row(LANES), row(LANES), full(qn), full(kvn), full(wq), full(wkv)],
        out_specs=(row(MLA_HEADS * MLA_QK), row(MLA_HEADS * MLA_QK), row(MLA_WIDTH)),
        compiler_params=_cparams(("parallel",)),
        name="mla_prep",
    )(mla_in, cos, sin, qn, kvn, wq, wkv)


def _attn_kernel(q_ref, k_ref, v_ref, g_ref, o_ref, m_sc, l_sc, acc_sc, *, tq, tk):
    i = pl.program_id(2)
    q = q_ref[0]
    m_sc[...] = jnp.full(m_sc.shape, -jnp.inf, F32)
    l_sc[...] = jnp.zeros(l_sc.shape, F32)
    acc_sc[...] = jnp.zeros(acc_sc.shape, F32)

    def step(j, masked):
        off = pl.multiple_of(j * tk, tk)
        ks = k_ref[0, pl.ds(off, tk), :]
        vs = v_ref[0, pl.ds(off, tk), :]
        s = lax.dot_general(q, ks, (((1,), (1,)), ((), ())), preferred_element_type=F32)
        if masked:
            rows = lax.broadcasted_iota(jnp.int32, s.shape, 0)
            cols = lax.broadcasted_iota(jnp.int32, s.shape, 1)
            s = jnp.where(cols <= rows, s, -jnp.inf)
        m_old = m_sc[...]
        m_new = jnp.maximum(m_old, jnp.max(s, axis=-1, keepdims=True))
        alpha = jnp.exp(m_old - m_new)
        p = jnp.exp(s - m_new)
        l_sc[...] = alpha * l_sc[...] + jnp.sum(p, axis=-1, keepdims=True)
        acc_sc[...] = alpha * acc_sc[...] + jnp.dot(p.astype(BF16), vs, preferred_element_type=F32)
        m_sc[...] = m_new

    def body(j, c):
        step(j, False)
        return c

    lax.fori_loop(0, i, body, 0)
    step(i, True)
    o = acc_sc[...] / l_sc[...]
    o_ref[0] = _rms(o, g_ref[...]).astype(o_ref.dtype)


def _mla_attention(q, k, v, g, batch, seq):
    tq, tk = TQ_ATTN, TK_ATTN
    assert tq == tk
    q = q.reshape(batch, seq, MLA_HEADS * MLA_QK)
    k = k.reshape(batch, seq, MLA_HEADS * MLA_QK)
    v = v.reshape(batch, seq, MLA_WIDTH)
    out = pl.pallas_call(
        functools.partial(_attn_kernel, tq=tq, tk=tk),
        out_shape=jax.ShapeDtypeStruct((batch, seq, MLA_WIDTH), BF16),
        grid=(batch, MLA_HEADS, seq // tq),
        in_specs=[pl.BlockSpec((1, tq, MLA_QK), lambda b, h, i: (b, i, h)),
                  pl.BlockSpec((1, seq, MLA_QK), lambda b, h, i: (b, 0, h)),
                  pl.BlockSpec((1, seq, MLA_VDIM), lambda b, h, i: (b, 0, h)),
                  pl.BlockSpec((1, MLA_VDIM), lambda b, h, i: (0, h))],
        out_specs=pl.BlockSpec((1, tq, MLA_VDIM), lambda b, h, i: (b, i, h)),
        scratch_shapes=[pltpu.VMEM((tq, 1), F32), pltpu.VMEM((tq, 1), F32), pltpu.VMEM((tq, MLA_VDIM), F32)],
        compiler_params=_cparams(("parallel", "parallel", "arbitrary")),
        name="mla_attention",
    )(q, k, v, g)
    return out.reshape(batch * seq, MLA_WIDTH)


def _rwkv_kernel(x_ref, mu_ref, w0_ref, a0_ref, kk_ref, ka_ref, rk_ref, lnw_ref, lnb_ref,
                 w2_ref, a2_ref, g2_ref, o_ref, prev_sc, state_sc, *, nb, chunk):
    n = nb * chunk
    w = RWKV_WIDTH
    hd = RWKV_HEAD

    @pl.when(pl.program_id(0) == 0)
    def _():
        prev_sc[...] = jnp.zeros(prev_sc.shape, F32)
        state_sc[...] = jnp.zeros(state_sc.shape, F32)

    x = x_ref[...].reshape(n, RWKV_IN)
    shifted = _shift_rows(x, prev_sc[...], 1, chunk)
    prev_sc[...] = x
    xs = x + (shifted - x) * mu_ref[...]
    r = xs[:, 0:w]
    k = xs[:, w:2 * w]
    v = xs[:, 2 * w:3 * w]
    lor = xs[:, 3 * w:]
    w_log = _log_sigmoid(w0_ref[...] + _bdot(jnp.tanh(lor), w2_ref[...])) - 0.5
    ld = -jnp.exp(w_log)
    a = _sigmoid(a0_ref[...] + _bdot(lor, a2_ref[...]))
    g = _bdot(_sigmoid(lor), g2_ref[...])
    kk = k * kk_ref[...]
    kk = kk / jnp.maximum(jnp.sqrt(_segsum(kk * kk, hd)), 1e-12)
    k2 = k * (1.0 + (a - 1.0) * ka_ref[...])
    kb = kk * a

    ri = lax.broadcasted_iota(jnp.int32, (n, n), 0)
    ci = lax.broadcasted_iota(jnp.int32, (n, n), 1)
    same_chunk = _div(ri, chunk) == _div(ci, chunk)
    tri = jnp.where(same_chunk & (ci <= ri), 1.0, 0.0)
    cl = _exact_left_dot(tri, ld)
    cl_last = jnp.concatenate(
        [jnp.broadcast_to(cl[(b + 1) * chunk - 1:(b + 1) * chunk, :], (chunk, w)) for b in range(nb)], axis=0)
    e_in = jnp.exp(cl)
    e_neg = jnp.exp(-cl)
    e_end = jnp.exp(cl_last - cl)
    a_t = kk * jnp.exp(cl - ld)
    k_t = k2 * e_neg
    b_t = kb * e_neg
    r_t = r * e_in
    k_h = k2 * e_end
    b_h = kb * e_end

    nh = RWKV_HEADS
    hs = nh * chunk
    rs = lax.broadcasted_iota(jnp.int32, (hs, w), 0)
    cs = lax.broadcasted_iota(jnp.int32, (hs, w), 1)
    bd = _div(rs, chunk) == _div(cs, hd)
    rq = lax.broadcasted_iota(jnp.int32, (hs, hs), 0)
    cq = lax.broadcasted_iota(jnp.int32, (hs, hs), 1)
    bdq = _div(rq, chunk) == _div(cq, chunk)
    strict = bdq & (cq < rq)
    incl = bdq & (cq <= rq)
    bd16 = _div(rq, 16) == _div(cq, 16)
    bd32 = _div(rq, 32) == _div(cq, 32)
    eye = jnp.where(rq == cq, 1.0, 0.0)
    rv = lax.broadcasted_iota(jnp.int32, (w, w), 0)
    cv = lax.broadcasted_iota(jnp.int32, (w, w), 1)
    bdv = _div(rv, hd) == _div(cv, hd)

    def tile(z):
        return jnp.concatenate([z] * nh, axis=0)

    def fold(z):
        out = z[0:chunk]
        for h in range(1, nh):
            out = out + z[h * chunk:(h + 1) * chunk]
        return out

    ys = []
    for b in range(nb):
        sl = slice(b * chunk, (b + 1) * chunk)
        a_st = jnp.where(bd, tile(a_t[sl]), 0.0)
        r_st = jnp.where(bd, tile(r_t[sl]), 0.0)
        v_st = jnp.where(bd, tile(v[sl]), 0.0)
        sc = _bdot_nt(jnp.concatenate([a_st, r_st], axis=0),
                      jnp.concatenate([tile(k_t[sl]), tile(b_t[sl])], axis=0))
        l_ak = jnp.where(strict, sc[0:hs, 0:hs], 0.0)
        l_ab = jnp.where(strict, sc[0:hs, hs:], 0.0)
        a_rk = jnp.where(incl, sc[hs:, 0:hs], 0.0)
        a_rb = jnp.where(incl, sc[hs:, hs:], 0.0)
        xm = -jnp.where(bd16, l_ab, 0.0)
        t_inv = eye + xm
        p = _bdot(xm, xm)
        t_inv = t_inv + _bdot(t_inv, p)
        p = _bdot(p, p)
        t_inv = t_inv + _bdot(t_inv, p)
        p = _bdot(p, p)
        t_inv = t_inv + _bdot(t_inv, p)
        off = jnp.where(bd32 & jnp.logical_not(bd16), l_ab, 0.0)
        t_inv = t_inv - _bdot(_bdot(t_inv, off), t_inv)
        off = jnp.where(jnp.logical_not(bd32), l_ab, 0.0)
        t_inv = t_inv - _bdot(_bdot(t_inv, off), t_inv)

        gs = state_sc[b]
        u = _bdot(t_inv, _bdot_nt(a_st, gs) + _bdot(l_ak, v_st))
        y_st = _bdot(a_rk, v_st) - _bdot(a_rb, u)
        ys.append(_bdot_nt(r_t[sl], gs) + fold(y_st))
        upd = _bdot_tn(jnp.concatenate([v[sl], -fold(u)], axis=0),
                       jnp.concatenate([k_h[sl], b_h[sl]], axis=0))
        state_sc[b] = gs * jnp.exp(cl_last[sl][0:1, :]) + jnp.where(bdv, upd, 0.0)

    y = jnp.concatenate(ys, axis=0)
    mean = _segsum(y, hd) * (1.0 / hd)
    d = y - mean
    var = _segsum(d * d, hd) * (1.0 / hd)
    yn = d * lax.rsqrt(var + RWKV_LN_EPS) * lnw_ref[...] + lnb_ref[...]
    bonus = _segsum(r * k2 * rk_ref[...], hd) * v
    o_ref[...] = ((yn + bonus) * g).astype(o_ref.dtype).reshape(o_ref.shape)


def _rwkv(x, batch, seq, mu, w0, a0, k_k, k_a, r_k, ln_w, ln_b, w2p, a2p, g2p):
    chunk = RWKV_CHUNK
    assert RWKV_HEADS * chunk == RWKV_WIDTH
    x = x.reshape(batch, seq, RWKV_IN)
    vec = lambda a: pl.BlockSpec(a.shape, lambda c: (0, 0))
    out = pl.pallas_call(
        functools.partial(_rwkv_kernel, nb=batch, chunk=chunk),
        out_shape=jax.ShapeDtypeStruct((batch, seq, RWKV_WIDTH), BF16),
        grid=(seq // chunk,),
        in_specs=[pl.BlockSpec((batch, chunk, RWKV_IN), lambda c: (0, c, 0)),
                  vec(mu), vec(w0), vec(a0), vec(k_k), vec(k_a), vec(r_k), vec(ln_w), vec(ln_b),
                  vec(w2p), vec(a2p), vec(g2p)],
        out_specs=pl.BlockSpec((batch, chunk, RWKV_WIDTH), lambda c: (0, c, 0)),
        scratch_shapes=[pltpu.VMEM((batch * chunk, RWKV_IN), F32),
                        pltpu.VMEM((batch, RWKV_WIDTH, RWKV_WIDTH), F32)],
        compiler_params=_cparams(("arbitrary",)),
        name="rwkv7",
    )(x, mu, w0, a0, k_k, k_a, r_k, ln_w, ln_b, w2p, a2p, g2p)
    return out.reshape(batch * seq, RWKV_WIDTH)


def _mlstm_kernel(x_ref, cw_ref, cb_ref, ib_ref, fb_ref, on_ref, o_ref,
                  prev_sc, c_sc, n_sc, m_sc, *, nb, chunk):
    n = nb * chunk
    nh = MLSTM_HEADS
    dk = MLSTM_QK
    dv = MLSTM_V
    qkw = nh * dk

    @pl.when(pl.program_id(0) == 0)
    def _():
        prev_sc[...] = jnp.zeros(prev_sc.shape, F32)
        c_sc[...] = jnp.zeros(c_sc.shape, F32)
        n_sc[...] = jnp.zeros(n_sc.shape, F32)
        m_sc[...] = jnp.zeros(m_sc.shape, F32)

    x = x_ref[...].reshape(n, MLSTM_IN)
    qk_raw = x[:, 0:2 * qkw]
    prev = prev_sc[...]
    conv = cb_ref[...] + qk_raw * cw_ref[MLSTM_CONV - 1:MLSTM_CONV, :]
    for s in range(1, MLSTM_CONV):
        conv = conv + _shift_rows(qk_raw, prev, s, chunk) * cw_ref[MLSTM_CONV - 1 - s:MLSTM_CONV - s, :]
    prev_sc[...] = qk_raw
    qk = conv * _sigmoid(conv)
    q_all = qk[:, 0:qkw] * (dk ** -0.5)
    k_all = qk[:, qkw:]
    v_all = x[:, 2 * qkw:2 * qkw + MLSTM_WIDTH]
    o_pre = x[:, 2 * qkw + MLSTM_WIDTH:2 * qkw + 2 * MLSTM_WIDTH]
    li_all = x[:, 768:896] + ib_ref[...]
    lf_all = _log_sigmoid(x[:, 896:1024] + fb_ref[...])

    ri = lax.broadcasted_iota(jnp.int32, (chunk, chunk), 0)
    ci = lax.broadcasted_iota(jnp.int32, (chunk, chunk), 1)
    causal = ci <= ri
    tri = jnp.where(causal, 1.0, 0.0)
    lane_k = lax.broadcasted_iota(jnp.int32, (chunk, qkw), 1)
    lane_v = lax.broadcasted_iota(jnp.int32, (chunk, MLSTM_WIDTH), 1)
    lane_row_k = lax.broadcasted_iota(jnp.int32, (1, qkw), 1)
    lane_row_v = lax.broadcasted_iota(jnp.int32, (1, MLSTM_WIDTH), 1)
    rc = lax.broadcasted_iota(jnp.int32, (qkw, MLSTM_WIDTH), 0)
    cc = lax.broadcasted_iota(jnp.int32, (qkw, MLSTM_WIDTH), 1)
    cmask = _div(rc, dk) == _div(cc, dv)

    hs = []
    for b in range(nb):
        sl = slice(b * chunk, (b + 1) * chunk)
        q = q_all[sl]
        k = k_all[sl]
        v = v_all[sl]
        li = li_all[sl]
        g = _exact_left_dot(tri, lf_all[sl])
        lig_t = jnp.transpose(li - g)
        c_b = c_sc[b]
        n_b = n_sc[b]
        m_b = m_sc[b]
        h_out = jnp.zeros((chunk, MLSTM_WIDTH), F32)
        w_exp_v = jnp.zeros((chunk, MLSTM_WIDTH), F32)
        w_exp_k = jnp.zeros((chunk, qkw), F32)
        dec_v = jnp.zeros((1, MLSTM_WIDTH), F32)
        dec_k = jnp.zeros((1, qkw), F32)
        m_next = jnp.zeros((1, LANES), F32)
        q_c = _bdot(q, c_b)
        for h in range(nh):
            mk = (lane_k >= h * dk) & (lane_k < (h + 1) * dk)
            mv = (lane_v >= h * dv) & (lane_v < (h + 1) * dv)
            g_col = g[:, h:h + 1]
            li_col = li[:, h:h + 1]
            m_prev = m_b[:, h:h + 1]
            d = jnp.where(causal, g_col + lig_t[h:h + 1, :], -jnp.inf)
            inter_log = g_col + m_prev
            m_t = jnp.maximum(inter_log, jnp.max(d, axis=-1, keepdims=True))
            inter_w = jnp.exp(inter_log - m_t)
            qh = jnp.where(mk, q, 0.0)
            s = _bdot_nt(qh, k) * jnp.exp(d - m_t)
            num = inter_w * jnp.where(mv, q_c, 0.0) + _bdot(s, jnp.where(mv, v, 0.0))
            den = inter_w * jnp.sum(qh * n_b, axis=-1, keepdims=True) + jnp.sum(s, axis=-1, keepdims=True)
            den = jnp.maximum(jnp.abs(den), jnp.exp(-m_t))
            h_out = h_out + num / den
            g_last = g[chunk - 1:chunk, h:h + 1]
            a_col = g_last - g_col + li_col
            m_new = jnp.maximum(g_last + m_prev, jnp.max(a_col, axis=0, keepdims=True))
            dec = jnp.exp(g_last + m_prev - m_new)
            wts = jnp.exp(a_col - m_new)
            w_exp_v = jnp.where(mv, wts, w_exp_v)
            w_exp_k = jnp.where(mk, wts, w_exp_k)
            dec_v = jnp.where((lane_row_v >= h * dv) & (lane_row_v < (h + 1) * dv), dec, dec_v)
            dec_k = jnp.where((lane_row_k >= h * dk) & (lane_row_k < (h + 1) * dk), dec, dec_k)
            m_next = jnp.where(lax.broadcasted_iota(jnp.int32, (1, LANES), 1) == h, m_new, m_next)
        c_sc[b] = c_b * dec_v + jnp.where(cmask, _bdot_tn(k, w_exp_v * v), 0.0)
        n_sc[b] = n_b * dec_k + jnp.sum(w_exp_k * k, axis=0, keepdims=True)
        m_sc[b] = m_next
        hs.append(h_out)

    hh = jnp.concatenate(hs, axis=0)
    ms = _segsum(hh * hh, dv) * (1.0 / dv)
    out = hh * lax.rsqrt(ms + NORM_EPS) * on_ref[...] * _sigmoid(o_pre)
    o_ref[...] = out.astype(o_ref.dtype).reshape(o_ref.shape)


def _mlstm(x, batch, seq, cw, cb, ib, fb, on):
    chunk = MLSTM_CHUNK
    x = x.reshape(batch, seq, MLSTM_IN)
    vec = lambda a: pl.BlockSpec(a.shape, lambda c: (0, 0))
    out = pl.pallas_call(
        functools.partial(_mlstm_kernel, nb=batch, chunk=chunk),
        out_shape=jax.ShapeDtypeStruct((batch, seq, MLSTM_WIDTH), BF16),
        grid=(seq // chunk,),
        in_specs=[pl.BlockSpec((batch, chunk, MLSTM_IN), lambda c: (0, c, 0)),
                  vec(cw), vec(cb), vec(ib), vec(fb), vec(on)],
        out_specs=pl.BlockSpec((batch, chunk, MLSTM_WIDTH), lambda c: (0, c, 0)),
        scratch_shapes=[pltpu.VMEM((batch * chunk, 2 * MLSTM_HEADS * MLSTM_QK), F32),
                        pltpu.VMEM((batch, MLSTM_HEADS * MLSTM_QK, MLSTM_WIDTH), F32),
                        pltpu.VMEM((batch, 1, MLSTM_HEADS * MLSTM_QK), F32),
                        pltpu.VMEM((batch, 1, LANES), F32)],
        compiler_params=_cparams(("arbitrary",)),
        name="mlstm",
    )(x, cw, cb, ib, fb, on)
    return out.reshape(batch * seq, MLSTM_WIDTH)


def _ffn_kernel(x_ref, ya_ref, yb_ref, yc_ref, wo_ref, g_ref, wg_ref, wu_ref, wd_ref, fg_ref,
                o_ref, x1_sc, h_sc, acc_sc, *, final_norm):
    j = pl.program_id(1)

    @pl.when(j == 0)
    def _():
        x1 = (x_ref[...]
              + jnp.dot(ya_ref[...], wo_ref[0:MLA_WIDTH, :], preferred_element_type=F32)
              + jnp.dot(yb_ref[...], wo_ref[MLA_WIDTH:MLA_WIDTH + RWKV_WIDTH, :], preferred_element_type=F32)
              + jnp.dot(yc_ref[...], wo_ref[MLA_WIDTH + RWKV_WIDTH:, :], preferred_element_type=F32))
        x1_sc[...] = x1
        h_sc[...] = _rms(x1, g_ref[...]).astype(BF16)
        acc_sc[...] = jnp.zeros(acc_sc.shape, F32)

    h = h_sc[...]
    gate = jnp.dot(h, wg_ref[...], preferred_element_type=F32)
    up = jnp.dot(h, wu_ref[...], preferred_element_type=F32)
    act = (gate * _sigmoid(gate) * up).astype(BF16)
    acc_sc[...] += jnp.dot(act, wd_ref[...], preferred_element_type=F32)

    @pl.when(j == pl.num_programs(1) - 1)
    def _():
        out = x1_sc[...] + acc_sc[...]
        if final_norm:
            out = _rms(out, fg_ref[...])
        o_ref[...] = out


def _out_ffn(x, ya, yb, yc, wo, g, wg, wu, wd, fg, final_norm):
    t = x.shape[0]
    tm, tf = TM_FFN, TF_FFN
    row = lambda w: pl.BlockSpec((tm, w), lambda i, j: (i, 0))
    full = lambda a: pl.BlockSpec(a.shape, lambda i, j: (0, 0))
    return pl.pallas_call(
        functools.partial(_ffn_kernel, final_norm=final_norm),
        out_shape=jax.ShapeDtypeStruct((t, D_MODEL), F32),
        grid=(t // tm, D_FF // tf),
        in_specs=[row(D_MODEL), row(MLA_WIDTH), row(RWKV_WIDTH), row(MLSTM_WIDTH), full(wo), full(g),
                  pl.BlockSpec((D_MODEL, tf), lambda i, j: (0, j)),
                  pl.BlockSpec((D_MODEL, tf), lambda i, j: (0, j)),
                  pl.BlockSpec((tf, D_MODEL), lambda i, j: (j, 0)),
                  full(fg)],
        out_specs=row(D_MODEL),
        scratch_shapes=[pltpu.VMEM((tm, D_MODEL), F32), pltpu.VMEM((tm, D_MODEL), BF16),
                        pltpu.VMEM((tm, D_MODEL), F32)],
        compiler_params=_cparams(("parallel", "arbitrary")),
        name="out_ffn",
    )(x, ya, yb, yc, wo, g, wg, wu, wd, fg)


def _pad_cols(w, width):
    return jnp.pad(w, ((0, 0), (0, width - w.shape[1])))


def _rot_half_cols(w):
    half = w.shape[1] // 2
    return jnp.concatenate([-w[:, half:], w[:, :half]], axis=1)


def _layer_weights(l, w_in, mla_w_uq, mla_w_ukv, rwkv_w2, rwkv_a2, rwkv_g2):
    wi = w_in[l]
    c_q, c_kv, k_pe = wi[:, 0:256], wi[:, 256:512], wi[:, 512:576]
    rw = wi[:, 576:1472]
    ml = wi[:, 1472:2248]
    w_mla = jnp.concatenate([c_q, c_kv, _pad_cols(k_pe, LANES), _pad_cols(_rot_half_cols(k_pe), LANES)], axis=1)
    w_mlstm = jnp.concatenate([ml[:, 0:256], ml[:, 256:512], ml[:, 520:776],
                               _pad_cols(ml[:, 512:516], LANES), _pad_cols(ml[:, 516:520], LANES)], axis=1)
    w_all = jnp.concatenate([w_mla, rw, w_mlstm], axis=1).astype(BF16)

    uq = mla_w_uq[l].reshape(MLA_Q_LORA, MLA_HEADS, MLA_NOPE + MLA_ROPE)
    nope = uq[:, :, :MLA_NOPE].reshape(MLA_Q_LORA, MLA_HEADS * MLA_NOPE)
    pe = jnp.concatenate([_pad_cols(uq[:, h, MLA_NOPE:], LANES) for h in range(MLA_HEADS)], axis=1)
    per = jnp.concatenate([_pad_cols(_rot_half_cols(uq[:, h, MLA_NOPE:]), LANES) for h in range(MLA_HEADS)], axis=1)
    wq = jnp.concatenate([nope, pe, per], axis=1).astype(BF16)
    ukv = mla_w_ukv[l].reshape(MLA_KV_LORA, MLA_HEADS, MLA_NOPE + MLA_VDIM)
    wkv = jnp.concatenate([ukv[:, :, :MLA_NOPE].reshape(MLA_KV_LORA, -1),
                           ukv[:, :, MLA_NOPE:].reshape(MLA_KV_LORA, -1)], axis=1).astype(BF16)

    z = lambda r: jnp.zeros((r, RWKV_WIDTH), F32)
    w2p = jnp.concatenate([rwkv_w2[l], z(LANES - RWKV_DECAY_LORA)], axis=0).astype(BF16)
    a2p = jnp.concatenate([z(RWKV_DECAY_LORA), rwkv_a2[l], z(RWKV_GATE_LORA)], axis=0).astype(BF16)
    g2p = jnp.concatenate([z(RWKV_DECAY_LORA + RWKV_AAA_LORA), rwkv_g2[l]], axis=0).astype(BF16)
    return w_all, wq, wkv, w2p, a2p, g2p


def kernel(x, positions, mix_norm, w_in, mla_q_norm, mla_w_uq, mla_kv_norm, mla_w_ukv, mla_out_norm, rwkv_mu, rwkv_w0, rwkv_w2, rwkv_a0, rwkv_a2, rwkv_g2, rwkv_k_k, rwkv_k_a, rwkv_r_k, rwkv_ln_w, rwkv_ln_b, mlstm_conv_w, mlstm_conv_b, mlstm_i_bias, mlstm_f_bias, mlstm_out_norm, w_out, ffn_norm, w_gate, w_up, w_down, final_norm):
    batch, seq, _ = x.shape
    depth = w_in.shape[0]
    xt = x.reshape(batch * seq, D_MODEL)
    cos, sin = _rope_tables(positions)
    row = lambda a: a.reshape(1, -1)
    for l in range(depth):
        w_all, wq, wkv, w2p, a2p, g2p = _layer_weights(l, w_in, mla_w_uq, mla_w_ukv, rwkv_w2, rwkv_a2, rwkv_g2)
        mla_in, rwkv_in, mlstm_in = _inproj(xt, row(mix_norm[l]), w_all)
        q, k, v = _mla_prep(mla_in, cos, sin, row(mla_q_norm[l]), row(mla_kv_norm[l]), wq, wkv)
        y_mla = _mla_attention(q, k, v, row(mla_out_norm[l]), batch, seq)
        y_rwkv = _rwkv(rwkv_in, batch, seq, row(rwkv_mu[l]), row(rwkv_w0[l]), row(rwkv_a0[l]), row(rwkv_k_k[l]),
                       row(rwkv_k_a[l]), row(rwkv_r_k[l]), row(rwkv_ln_w[l]), row(rwkv_ln_b[l]), w2p, a2p, g2p)
        y_mlstm = _mlstm(mlstm_in, batch, seq, mlstm_conv_w[l], row(mlstm_conv_b[l]),
                         _pad_cols(row(mlstm_i_bias[l]), LANES), _pad_cols(row(mlstm_f_bias[l]), LANES),
                         row(mlstm_out_norm[l]))
        xt = _out_ffn(xt, y_mla, y_rwkv, y_mlstm, w_out[l].astype(BF16), row(ffn_norm[l]),
                      w_gate[l].astype(BF16), w_up[l].astype(BF16), w_down[l].astype(BF16),
                      row(final_norm), final_norm=(l == depth - 1))
    return xt.reshape(batch, seq, D_MODEL)
```

```python
import functools
import math

import jax
import jax.numpy as jnp
from jax import lax
from jax.experimental import pallas as pl
from jax.experimental.pallas import tpu as pltpu

F32 = jnp.float32
BF16 = jnp.bfloat16

D_MODEL = 1024
DEPTH = 2
MLA_HEADS = 4
MLA_NOPE = 128
MLA_ROPE = 64
MLA_VDIM = 128
MLA_Q_LORA = 256
MLA_KV_LORA = 256
MLA_WIDTH = MLA_HEADS * MLA_VDIM
MLA_QK = 256
ROPE_THETA = 10000.0
RWKV_HEADS = 4
RWKV_HEAD = 64
RWKV_WIDTH = 256
RWKV_DECAY_LORA = 32
RWKV_AAA_LORA = 32
RWKV_GATE_LORA = 64
RWKV_IN = 3 * RWKV_WIDTH + 128
RWKV_LN_EPS = 64e-5
MLSTM_HEADS = 4
MLSTM_QK = 32
MLSTM_V = 64
MLSTM_WIDTH = 256
MLSTM_CONV = 4
MLSTM_IN = 1024
MLA_IN = 768
D_FF = 2816
NORM_EPS = 1e-6
LANES = 128

TM_INPROJ = 512
TM_PREP = 512
TQ_ATTN = 512
TK_ATTN = 512
HP_ATTN = 4
RWKV_CHUNK = 64
MLSTM_CHUNK = 256
TM_FFN = 512
TF_FFN = 256
VMEM_LIMIT = 56 * 1024 * 1024


def _cparams(sem):
    return pltpu.CompilerParams(dimension_semantics=sem, vmem_limit_bytes=VMEM_LIMIT)


def _resident(a):
    nd = a.ndim
    return pl.BlockSpec(a.shape, lambda *_: (0,) * nd, pipeline_mode=pl.Buffered(1))


def _bdot(a, b):
    return jnp.dot(a.astype(BF16), b.astype(BF16), preferred_element_type=F32)


def _bdot_nt(a, b):
    return lax.dot_general(a.astype(BF16), b.astype(BF16), (((1,), (1,)), ((), ())),
                           preferred_element_type=F32)


def _bdot_tn(a, b):
    return lax.dot_general(a.astype(BF16), b.astype(BF16), (((0,), (0,)), ((), ())),
                           preferred_element_type=F32)


def _split3(x):
    h = x.astype(BF16)
    r1 = x - h.astype(F32)
    m = r1.astype(BF16)
    lo = (r1 - m.astype(F32)).astype(BF16)
    return h, m, lo


def _exact_left_dot(sel, x):
    h, m, lo = _split3(x)
    s = sel.astype(BF16)
    return (jnp.dot(s, h, preferred_element_type=F32) + jnp.dot(s, m, preferred_element_type=F32)
            + jnp.dot(s, lo, preferred_element_type=F32))


def _rms(x, g):
    return x * lax.rsqrt(jnp.mean(x * x, axis=-1, keepdims=True) + NORM_EPS) * g


def _sigmoid(x):
    return 1.0 / (1.0 + jnp.exp(-x))


def _log_sigmoid(x):
    return jnp.minimum(x, 0.0) - jnp.log1p(jnp.exp(-jnp.abs(x)))


def _div(x, d):
    assert d & (d - 1) == 0
    return lax.shift_right_logical(x, d.bit_length() - 1)


def _mod(x, d):
    assert d & (d - 1) == 0
    return lax.bitwise_and(x, d - 1)


def _segsum(x, seg):
    lane = lax.broadcasted_iota(jnp.int32, x.shape, 1)
    out = jnp.zeros_like(x)
    for h in range(x.shape[1] // seg):
        m = (lane >= h * seg) & (lane < (h + 1) * seg)
        s = jnp.sum(jnp.where(m, x, 0.0), axis=-1, keepdims=True)
        out = jnp.where(m, s, out)
    return out


def _shift_rows(x, prev, s, chunk):
    n = x.shape[0]
    row = lax.broadcasted_iota(jnp.int32, x.shape, 0)
    return jnp.where(_mod(row, chunk) >= s, pltpu.roll(x, s, 0), pltpu.roll(prev, n - chunk + s, 0))


def _rope_kernel(pos_ref, invf_ref, cos_ref, sin_ref):
    ang = pos_ref[...].astype(F32) * invf_ref[...]
    cos_ref[...] = jnp.cos(ang)
    sin_ref[...] = jnp.sin(ang)


def _rope_tables(positions):
    t = positions.size
    tm = min(1024, t)
    inv_freq = ROPE_THETA ** (-jnp.arange(0, MLA_ROPE, 2, dtype=F32) / MLA_ROPE)
    invf = jnp.tile(inv_freq, LANES // (MLA_ROPE // 2))[None, :]
    return pl.pallas_call(
        _rope_kernel,
        out_shape=(jax.ShapeDtypeStruct((t, LANES), F32), jax.ShapeDtypeStruct((t, LANES), F32)),
        grid=(t // tm,),
        in_specs=[pl.BlockSpec((tm, 1), lambda i: (i, 0)), pl.BlockSpec((1, LANES), lambda i: (0, 0))],
        out_specs=(pl.BlockSpec((tm, LANES), lambda i: (i, 0)), pl.BlockSpec((tm, LANES), lambda i: (i, 0))),
        compiler_params=_cparams(("parallel",)),
        name="rope_tables",
    )(positions.reshape(t, 1), invf)


def _inproj_kernel(x_ref, g_ref, w_ref, mla_ref, rwkv_ref, mlstm_ref):
    hb = _rms(x_ref[...], g_ref[...]).astype(BF16)
    mla_ref[...] = jnp.dot(hb, w_ref[:, 0:MLA_IN], preferred_element_type=F32)
    rwkv_ref[...] = jnp.dot(hb, w_ref[:, MLA_IN:MLA_IN + RWKV_IN], preferred_element_type=F32)
    mlstm_ref[...] = jnp.dot(hb, w_ref[:, MLA_IN + RWKV_IN:], preferred_element_type=F32)


def _inproj(x, g, w):
    t = x.shape[0]
    tm = TM_INPROJ
    n = w.shape[1]
    return pl.pallas_call(
        _inproj_kernel,
        out_shape=(jax.ShapeDtypeStruct((t, MLA_IN), F32), jax.ShapeDtypeStruct((t, RWKV_IN), F32),
                   jax.ShapeDtypeStruct((t, MLSTM_IN), F32)),
        grid=(t // tm,),
        in_specs=[pl.BlockSpec((tm, D_MODEL), lambda i: (i, 0)), pl.BlockSpec((1, D_MODEL), lambda i: (0, 0)),
                  pl.BlockSpec((D_MODEL, n), lambda i: (0, 0))],
        out_specs=(pl.BlockSpec((tm, MLA_IN), lambda i: (i, 0)), pl.BlockSpec((tm, RWKV_IN), lambda i: (i, 0)),
                   pl.BlockSpec((tm, MLSTM_IN), lambda i: (i, 0))),
        compiler_params=_cparams(("parallel",)),
        name="inproj",
    )(x, g, w)


def _mla_prep_kernel(in_ref, cos_ref, sin_ref, qn_ref, kvn_ref, wq_ref, wkv_ref, q_ref, k_ref, v_ref):
    cos = cos_ref[...]
    sin = sin_ref[...]
    scale = (MLA_NOPE + MLA_ROPE) ** -0.5 * math.log2(math.e)
    hw = MLA_HEADS * LANES
    cqn = _rms(in_ref[:, 0:MLA_Q_LORA], qn_ref[...]).astype(BF16)
    q = jnp.dot(cqn, wq_ref[...], preferred_element_type=F32)
    ckvn = _rms(in_ref[:, MLA_Q_LORA:MLA_Q_LORA + MLA_KV_LORA], kvn_ref[...]).astype(BF16)
    kv = jnp.dot(ckvn, wkv_ref[...], preferred_element_type=F32)
    kp = (in_ref[:, 512:640] * cos + in_ref[:, 640:768] * sin).astype(BF16)
    for h in range(MLA_HEADS):
        c0 = h * LANES
        pe = q[:, hw + c0:hw + c0 + LANES] * cos + q[:, 2 * hw + c0:2 * hw + c0 + LANES] * sin
        q_ref[:, h * MLA_QK:h * MLA_QK + LANES] = (q[:, c0:c0 + LANES] * scale).astype(BF16)
        q_ref[:, h * MLA_QK + LANES:(h + 1) * MLA_QK] = (pe * scale).astype(BF16)
        k_ref[:, h * MLA_QK:h * MLA_QK + LANES] = kv[:, c0:c0 + LANES].astype(BF16)
        k_ref[:, h * MLA_QK + LANES:(h + 1) * MLA_QK] = kp
    v_ref[...] = kv[:, hw:].astype(BF16)


def _mla_prep(mla_in, cos, sin, qn, kvn, wq, wkv):
    t = mla_in.shape[0]
    tm = TM_PREP
    row = lambda w: pl.BlockSpec((tm, w), lambda i: (i, 0))
    full = lambda a: pl.BlockSpec(a.shape, lambda i: (0, 0))
    return pl.pallas_call(
        _mla_prep_kernel,
        out_shape=(jax.ShapeDtypeStruct((t, MLA_HEADS * MLA_QK), BF16),
                   jax.ShapeDtypeStruct((t, MLA_HEADS * MLA_QK), BF16),
                   jax.ShapeDtypeStruct((t, MLA_WIDTH), BF16)),
        grid=(t // tm,),
        in_specs=[row(MLA_IN), row(LANES), row(LANES), full(qn), full(kvn), full(wq), full(wkv)],
        out_specs=(row(MLA_HEADS * MLA_QK), row(MLA_HEADS * MLA_QK), row(MLA_WIDTH)),
        compiler_params=_cparams(("parallel",)),
        name="mla_prep",
    )(mla_in, cos, sin, qn, kvn, wq, wkv)


def _attn_kernel(q_ref, k_ref, v_ref, g_ref, o_ref, m_sc, l_sc, acc_sc, *, tq, tk, hp):
    i = pl.program_id(2)
    m_sc[...] = jnp.full(m_sc.shape, -jnp.inf, F32)
    l_sc[...] = jnp.zeros(l_sc.shape, F32)
    acc_sc[...] = jnp.zeros(acc_sc.shape, F32)

    def step(j, masked):
        off = pl.multiple_of(j * tk, tk)
        for h in range(hp):
            q = q_ref[0, :, h * MLA_QK:(h + 1) * MLA_QK]
            ks = k_ref[0, pl.ds(off, tk), h * MLA_QK:(h + 1) * MLA_QK]
            vs = v_ref[0, pl.ds(off, tk), h * MLA_VDIM:(h + 1) * MLA_VDIM]
            s = lax.dot_general(q, ks, (((1,), (1,)), ((), ())), preferred_element_type=F32)
            if masked:
                rows = lax.broadcasted_iota(jnp.int32, s.shape, 0)
                cols = lax.broadcasted_iota(jnp.int32, s.shape, 1)
                s = jnp.where(cols <= rows, s, -jnp.inf)
            m_old = m_sc[h]
            m_new = jnp.maximum(m_old, jnp.max(s, axis=-1, keepdims=True))
            alpha = jnp.exp2(m_old - m_new)
            p = jnp.exp2(s - jnp.tile(m_new, (1, tk // LANES)))
            l_sc[h] = alpha * l_sc[h] + jnp.sum(p, axis=-1, keepdims=True)
            acc_sc[h] = alpha * acc_sc[h] + jnp.dot(p.astype(BF16), vs, preferred_element_type=F32)
            m_sc[h] = m_new

    def body(j, c):
        step(j, False)
        return c

    lax.fori_loop(0, i, body, 0)
    step(i, True)
    for h in range(hp):
        o = acc_sc[h] / l_sc[h]
        o_ref[0, :, h * MLA_VDIM:(h + 1) * MLA_VDIM] = _rms(
            o, g_ref[:, h * MLA_VDIM:(h + 1) * MLA_VDIM]).astype(o_ref.dtype)


def _mla_attention(q, k, v, g, batch, seq):
    tq, tk, hp = TQ_ATTN, TK_ATTN, HP_ATTN
    assert tq == tk and MLA_VDIM == LANES
    q = q.reshape(batch, seq, MLA_HEADS * MLA_QK)
    k = k.reshape(batch, seq, MLA_HEADS * MLA_QK)
    v = v.reshape(batch, seq, MLA_WIDTH)
    out = pl.pallas_call(
        functools.partial(_attn_kernel, tq=tq, tk=tk, hp=hp),
        out_shape=jax.ShapeDtypeStruct((batch, seq, MLA_WIDTH), BF16),
        grid=(batch, MLA_HEADS // hp, seq // tq),
        in_specs=[pl.BlockSpec((1, tq, hp * MLA_QK), lambda b, h, i: (b, i, h)),
                  pl.BlockSpec((1, seq, hp * MLA_QK), lambda b, h, i: (b, 0, h)),
                  pl.BlockSpec((1, seq, hp * MLA_VDIM), lambda b, h, i: (b, 0, h)),
                  pl.BlockSpec((1, hp * MLA_VDIM), lambda b, h, i: (0, h))],
        out_specs=pl.BlockSpec((1, tq, hp * MLA_VDIM), lambda b, h, i: (b, i, h)),
        scratch_shapes=[pltpu.VMEM((hp, tq, LANES), F32), pltpu.VMEM((hp, tq, LANES), F32),
                        pltpu.VMEM((hp, tq, MLA_VDIM), F32)],
        compiler_params=_cparams(("parallel", "parallel", "arbitrary")),
        name="mla_attention",
    )(q, k, v, g)
    return out.reshape(batch * seq, MLA_WIDTH)


def _rwkv_kernel(x_ref, mu_ref, w0_ref, a0_ref, kk_ref, ka_ref, rk_ref, lnw_ref, lnb_ref,
                 w2_ref, a2_ref, g2_ref, o_ref, prev_sc, state_sc, *, nb, chunk):
    n = nb * chunk
    w = RWKV_WIDTH
    hd = RWKV_HEAD

    @pl.when(pl.program_id(0) == 0)
    def _():
        prev_sc[...] = jnp.zeros(prev_sc.shape, F32)
        state_sc[...] = jnp.zeros(state_sc.shape, F32)

    x = x_ref[...].reshape(n, RWKV_IN)
    shifted = _shift_rows(x, prev_sc[...], 1, chunk)
    prev_sc[...] = x
    xs = x + (shifted - x) * mu_ref[...]
    r = xs[:, 0:w]
    k = xs[:, w:2 * w]
    v = xs[:, 2 * w:3 * w]
    lor = xs[:, 3 * w:]
    w_log = _log_sigmoid(w0_ref[...] + _bdot(jnp.tanh(lor), w2_ref[...])) - 0.5
    ld = -jnp.exp(w_log)
    a = _sigmoid(a0_ref[...] + _bdot(lor, a2_ref[...]))
    g = _bdot(_sigmoid(lor), g2_ref[...])
    kk = k * kk_ref[...]
    kk = kk / jnp.maximum(jnp.sqrt(_segsum(kk * kk, hd)), 1e-12)
    k2 = k * (1.0 + (a - 1.0) * ka_ref[...])
    kb = kk * a

    ri = lax.broadcasted_iota(jnp.int32, (n, n), 0)
    ci = lax.broadcasted_iota(jnp.int32, (n, n), 1)
    same_chunk = _div(ri, chunk) == _div(ci, chunk)
    tri = jnp.where(same_chunk & (ci <= ri), 1.0, 0.0)
    cl = _exact_left_dot(tri, ld)
    cl_last = jnp.concatenate(
        [jnp.broadcast_to(cl[(b + 1) * chunk - 1:(b + 1) * chunk, :], (chunk, w)) for b in range(nb)], axis=0)
    e_in = jnp.exp(cl)
    e_neg = jnp.exp(-cl)
    e_end = jnp.exp(cl_last - cl)
    a_t = kk * jnp.exp(cl - ld)
    k_t = k2 * e_neg
    b_t = kb * e_neg
    r_t = r * e_in
    k_h = k2 * e_end
    b_h = kb * e_end

    nh = RWKV_HEADS
    hs = nh * chunk
    rs = lax.broadcasted_iota(jnp.int32, (hs, w), 0)
    cs = lax.broadcasted_iota(jnp.int32, (hs, w), 1)
    bd = _div(rs, chunk) == _div(cs, hd)
    rq = lax.broadcasted_iota(jnp.int32, (hs, hs), 0)
    cq = lax.broadcasted_iota(jnp.int32, (hs, hs), 1)
    bdq = _div(rq, chunk) == _div(cq, chunk)
    strict = bdq & (cq < rq)
    incl = bdq & (cq <= rq)
    bd16 = _div(rq, 16) == _div(cq, 16)
    bd32 = _div(rq, 32) == _div(cq, 32)
    eye = jnp.where(rq == cq, 1.0, 0.0)
    rv = lax.broadcasted_iota(jnp.int32, (w, w), 0)
    cv = lax.broadcasted_iota(jnp.int32, (w, w), 1)
    bdv = _div(rv, hd) == _div(cv, hd)

    def tile(z):
        return jnp.concatenate([z] * nh, axis=0)

    def fold(z):
        out = z[0:chunk]
        for h in range(1, nh):
            out = out + z[h * chunk:(h + 1) * chunk]
        return out

    ys = []
    for b in range(nb):
        sl = slice(b * chunk, (b + 1) * chunk)
        a_st = jnp.where(bd, tile(a_t[sl]), 0.0)
        r_st = jnp.where(bd, tile(r_t[sl]), 0.0)
        v_st = jnp.where(bd, tile(v[sl]), 0.0)
        sc = _bdot_nt(jnp.concatenate([a_st, r_st], axis=0),
                      jnp.concatenate([tile(k_t[sl]), tile(b_t[sl])], axis=0))
        l_ak = jnp.where(strict, sc[0:hs, 0:hs], 0.0)
        l_ab = jnp.where(strict, sc[0:hs, hs:], 0.0)
        a_rk = jnp.where(incl, sc[hs:, 0:hs], 0.0)
        a_rb = jnp.where(incl, sc[hs:, hs:], 0.0)
        xm = -jnp.where(bd16, l_ab, 0.0)
        t_inv = eye + xm
        p = _bdot(xm, xm)
        t_inv = t_inv + _bdot(t_inv, p)
        p = _bdot(p, p)
        t_inv = t_inv + _bdot(t_inv, p)
        p = _bdot(p, p)
        t_inv = t_inv + _bdot(t_inv, p)
        off = jnp.where(bd32 & jnp.logical_not(bd16), l_ab, 0.0)
        t_inv = t_inv - _bdot(_bdot(t_inv, off), t_inv)
        off = jnp.where(jnp.logical_not(bd32), l_ab, 0.0)
        t_inv = t_inv - _bdot(_bdot(t_inv, off), t_inv)

        gs = state_sc[b]
        u = _bdot(t_inv, _bdot_nt(a_st, gs) + _bdot(l_ak, v_st))
        y_st = _bdot(a_rk, v_st) - _bdot(a_rb, u)
        ys.append(_bdot_nt(r_t[sl], gs) + fold(y_st))
        upd = _bdot_tn(jnp.concatenate([v[sl], -fold(u)], axis=0),
                       jnp.concatenate([k_h[sl], b_h[sl]], axis=0))
        state_sc[b] = gs * jnp.exp(cl_last[sl][0:1, :]) + jnp.where(bdv, upd, 0.0)

    y = jnp.concatenate(ys, axis=0)
    mean = _segsum(y, hd) * (1.0 / hd)
    d = y - mean
    var = _segsum(d * d, hd) * (1.0 / hd)
    yn = d * lax.rsqrt(var + RWKV_LN_EPS) * lnw_ref[...] + lnb_ref[...]
    bonus = _segsum(r * k2 * rk_ref[...], hd) * v
    o_ref[...] = ((yn + bonus) * g).astype(o_ref.dtype).reshape(o_ref.shape)


def _rwkv(x, batch, seq, mu, w0, a0, k_k, k_a, r_k, ln_w, ln_b, w2p, a2p, g2p):
    chunk = RWKV_CHUNK
    assert RWKV_HEADS * chunk == RWKV_WIDTH
    x = x.reshape(batch, seq, RWKV_IN)
    vec = lambda a: pl.BlockSpec(a.shape, lambda c: (0, 0))
    out = pl.pallas_call(
        functools.partial(_rwkv_kernel, nb=batch, chunk=chunk),
        out_shape=jax.ShapeDtypeStruct((batch, seq, RWKV_WIDTH), BF16),
        grid=(seq // chunk,),
        in_specs=[pl.BlockSpec((batch, chunk, RWKV_IN), lambda c: (0, c, 0)),
                  vec(mu), vec(w0), vec(a0), vec(k_k), vec(k_a), vec(r_k), vec(ln_w), vec(ln_b),
                  vec(w2p), vec(a2p), vec(g2p)],
        out_specs=pl.BlockSpec((batch, chunk, RWKV_WIDTH), lambda c: (0, c, 0)),
        scratch_shapes=[pltpu.VMEM((batch * chunk, RWKV_IN), F32),
                        pltpu.VMEM((batch, RWKV_WIDTH, RWKV_WIDTH), F32)],
        compiler_params=_cparams(("arbitrary",)),
        name="rwkv7",
    )(x, mu, w0, a0, k_k, k_a, r_k, ln_w, ln_b, w2p, a2p, g2p)
    return out.reshape(batch * seq, RWKV_WIDTH)


def _mlstm_kernel(x_ref, cw_ref, cb_ref, ib_ref, fb_ref, on_ref, o_ref,
                  prev_sc, c_sc, n_sc, m_sc, *, nb, chunk):
    n = nb * chunk
    nh = MLSTM_HEADS
    dk = MLSTM_QK
    dv = MLSTM_V
    qkw = nh * dk

    @pl.when(pl.program_id(0) == 0)
    def _():
        prev_sc[...] = jnp.zeros(prev_sc.shape, F32)
        c_sc[...] = jnp.zeros(c_sc.shape, F32)
        n_sc[...] = jnp.zeros(n_sc.shape, F32)
        m_sc[...] = jnp.zeros(m_sc.shape, F32)

    x = x_ref[...].reshape(n, MLSTM_IN)
    qk_raw = x[:, 0:2 * qkw]
    prev = prev_sc[...]
    conv = cb_ref[...] + qk_raw * cw_ref[MLSTM_CONV - 1:MLSTM_CONV, :]
    for s in range(1, MLSTM_CONV):
        conv = conv + _shift_rows(qk_raw, prev, s, chunk) * cw_ref[MLSTM_CONV - 1 - s:MLSTM_CONV - s, :]
    prev_sc[...] = qk_raw
    qk = conv * _sigmoid(conv)
    q_all = qk[:, 0:qkw] * (dk ** -0.5)
    k_all = qk[:, qkw:]
    v_all = x[:, 2 * qkw:2 * qkw + MLSTM_WIDTH]
    o_pre = x[:, 2 * qkw + MLSTM_WIDTH:2 * qkw + 2 * MLSTM_WIDTH]
    li_all = x[:, 768:896] + ib_ref[...]
    lf_all = _log_sigmoid(x[:, 896:1024] + fb_ref[...])

    ri = lax.broadcasted_iota(jnp.int32, (chunk, chunk), 0)
    ci = lax.broadcasted_iota(jnp.int32, (chunk, chunk), 1)
    causal = ci <= ri
    tri = jnp.where(causal, 1.0, 0.0)
    lane_k = lax.broadcasted_iota(jnp.int32, (chunk, qkw), 1)
    lane_v = lax.broadcasted_iota(jnp.int32, (chunk, MLSTM_WIDTH), 1)
    lane_row_k = lax.broadcasted_iota(jnp.int32, (1, qkw), 1)
    lane_row_v = lax.broadcasted_iota(jnp.int32, (1, MLSTM_WIDTH), 1)
    rc = lax.broadcasted_iota(jnp.int32, (qkw, MLSTM_WIDTH), 0)
    cc = lax.broadcasted_iota(jnp.int32, (qkw, MLSTM_WIDTH), 1)
    cmask = _div(rc, dk) == _div(cc, dv)

    hs = []
    for b in range(nb):
        sl = slice(b * chunk, (b + 1) * chunk)
        q = q_all[sl]
        k = k_all[sl]
        v = v_all[sl]
        li = li_all[sl]
        g = _exact_left_dot(tri, lf_all[sl])
        lig_t = jnp.transpose(li - g)
        c_b = c_sc[b]
        n_b = n_sc[b]
        m_b = m_sc[b]
        h_out = jnp.zeros((chunk, MLSTM_WIDTH), F32)
        w_exp_v = jnp.zeros((chunk, MLSTM_WIDTH), F32)
        w_exp_k = jnp.zeros((chunk, qkw), F32)
        dec_v = jnp.zeros((1, MLSTM_WIDTH), F32)
        dec_k = jnp.zeros((1, qkw), F32)
        m_next = jnp.zeros((1, LANES), F32)
        q_c = _bdot(q, c_b)
        for h in range(nh):
            mk = (lane_k >= h * dk) & (lane_k < (h + 1) * dk)
            mv = (lane_v >= h * dv) & (lane_v < (h + 1) * dv)
            g_col = g[:, h:h + 1]
            li_col = li[:, h:h + 1]
            m_prev = m_b[:, h:h + 1]
            d = jnp.where(causal, g_col + lig_t[h:h + 1, :], -jnp.inf)
            inter_log = g_col + m_prev
            m_t = jnp.maximum(inter_log, jnp.max(d, axis=-1, keepdims=True))
            inter_w = jnp.exp(inter_log - m_t)
            qh = jnp.where(mk, q, 0.0)
            s = _bdot_nt(qh, k) * jnp.exp(d - m_t)
            num = inter_w * jnp.where(mv, q_c, 0.0) + _bdot(s, jnp.where(mv, v, 0.0))
            den = inter_w * jnp.sum(qh * n_b, axis=-1, keepdims=True) + jnp.sum(s, axis=-1, keepdims=True)
            den = jnp.maximum(jnp.abs(den), jnp.exp(-m_t))
            h_out = h_out + num / den
            g_last = g[chunk - 1:chunk, h:h + 1]
            a_col = g_last - g_col + li_col
            m_new = jnp.maximum(g_last + m_prev, jnp.max(a_col, axis=0, keepdims=True))
            dec = jnp.exp(g_last + m_prev - m_new)
            wts = jnp.exp(a_col - m_new)
            w_exp_v = jnp.where(mv, wts, w_exp_v)
            w_exp_k = jnp.where(mk, wts, w_exp_k)
            dec_v = jnp.where((lane_row_v >= h * dv) & (lane_row_v < (h + 1) * dv), dec, dec_v)
            dec_k = jnp.where((lane_row_k >= h * dk) & (lane_row_k < (h + 1) * dk), dec, dec_k)
            m_next = jnp.where(lax.broadcasted_iota(jnp.int32, (1, LANES), 1) == h, m_new, m_next)
        c_sc[b] = c_b * dec_v + jnp.where(cmask, _bdot_tn(k, w_exp_v * v), 0.0)
        n_sc[b] = n_b * dec_k + jnp.sum(w_exp_k * k, axis=0, keepdims=True)
        m_sc[b] = m_next
        hs.append(h_out)

    hh = jnp.concatenate(hs, axis=0)
    ms = _segsum(hh * hh, dv) * (1.0 / dv)
    out = hh * lax.rsqrt(ms + NORM_EPS) * on_ref[...] * _sigmoid(o_pre)
    o_ref[...] = out.astype(o_ref.dtype).reshape(o_ref.shape)


def _mlstm(x, batch, seq, cw, cb, ib, fb, on):
    chunk = MLSTM_CHUNK
    x = x.reshape(batch, seq, MLSTM_IN)
    vec = lambda a: pl.BlockSpec(a.shape, lambda c: (0, 0))
    out = pl.pallas_call(
        functools.partial(_mlstm_kernel, nb=batch, chunk=chunk),
        out_shape=jax.ShapeDtypeStruct((batch, seq, MLSTM_WIDTH), BF16),
        grid=(seq // chunk,),
        in_specs=[pl.BlockSpec((batch, chunk, MLSTM_IN), lambda c: (0, c, 0)),
                  vec(cw), vec(cb), vec(ib), vec(fb), vec(on)],
        out_specs=pl.BlockSpec((batch, chunk, MLSTM_WIDTH), lambda c: (0, c, 0)),
        scratch_shapes=[pltpu.VMEM((batch * chunk, 2 * MLSTM_HEADS * MLSTM_QK), F32),
                        pltpu.VMEM((batch, MLSTM_HEADS * MLSTM_QK, MLSTM_WIDTH), F32),
                        pltpu.VMEM((batch, 1, MLSTM_HEADS * MLSTM_QK), F32),
                        pltpu.VMEM((batch, 1, LANES), F32)],
        compiler_params=_cparams(("arbitrary",)),
        name="mlstm",
    )(x, cw, cb, ib, fb, on)
    return out.reshape(batch * seq, MLSTM_WIDTH)


def _ffn_kernel(x_ref, ya_ref, yb_ref, yc_ref, wo_ref, g_ref, wg_ref, wu_ref, wd_ref, fg_ref,
                o_ref, act_sc, *, final_norm, tf):
    y = jnp.concatenate([ya_ref[...], yb_ref[...], yc_ref[...]], axis=-1)
    x1 = x_ref[...] + jnp.dot(y, wo_ref[...], preferred_element_type=F32)
    h = _rms(x1, g_ref[...]).astype(BF16)
    for c in range(D_FF // tf):
        gate = jnp.dot(h, wg_ref[:, c * tf:(c + 1) * tf], preferred_element_type=F32)
        up = jnp.dot(h, wu_ref[:, c * tf:(c + 1) * tf], preferred_element_type=F32)
        act_sc[:, c * tf:(c + 1) * tf] = (gate * _sigmoid(gate) * up).astype(BF16)
    out = x1 + jnp.dot(act_sc[...], wd_ref[...], preferred_element_type=F32)
    if final_norm:
        out = _rms(out, fg_ref[...])
    o_ref[...] = out


def _out_ffn(x, ya, yb, yc, wo, g, wg, wu, wd, fg, final_norm):
    t = x.shape[0]
    tm, tf = TM_FFN, TF_FFN
    row = lambda w: pl.BlockSpec((tm, w), lambda i: (i, 0))
    return pl.pallas_call(
        functools.partial(_ffn_kernel, final_norm=final_norm, tf=tf),
        out_shape=jax.ShapeDtypeStruct((t, D_MODEL), F32),
        grid=(t // tm,),
        in_specs=[row(D_MODEL), row(MLA_WIDTH), row(RWKV_WIDTH), row(MLSTM_WIDTH), _resident(wo), _resident(g),
                  _resident(wg), _resident(wu), _resident(wd), _resident(fg)],
        out_specs=row(D_MODEL),
        scratch_shapes=[pltpu.VMEM((tm, D_FF), BF16)],
        compiler_params=_cparams(("parallel",)),
        name="out_ffn",
    )(x, ya, yb, yc, wo, g, wg, wu, wd, fg)


def _pad_cols(w, width):
    return jnp.pad(w, ((0, 0), (0, width - w.shape[1])))


def _rot_half_cols(w):
    half = w.shape[1] // 2
    return jnp.concatenate([-w[:, half:], w[:, :half]], axis=1)


def _layer_weights(l, w_in, mla_w_uq, mla_w_ukv, rwkv_w2, rwkv_a2, rwkv_g2):
    wi = w_in[l]
    c_q, c_kv, k_pe = wi[:, 0:256], wi[:, 256:512], wi[:, 512:576]
    rw = wi[:, 576:1472]
    ml = wi[:, 1472:2248]
    w_mla = jnp.concatenate([c_q, c_kv, _pad_cols(k_pe, LANES), _pad_cols(_rot_half_cols(k_pe), LANES)], axis=1)
    w_mlstm = jnp.concatenate([ml[:, 0:256], ml[:, 256:512], ml[:, 520:776],
                               _pad_cols(ml[:, 512:516], LANES), _pad_cols(ml[:, 516:520], LANES)], axis=1)
    w_all = jnp.concatenate([w_mla, rw, w_mlstm], axis=1).astype(BF16)

    uq = mla_w_uq[l].reshape(MLA_Q_LORA, MLA_HEADS, MLA_NOPE + MLA_ROPE)
    nope = uq[:, :, :MLA_NOPE].reshape(MLA_Q_LORA, MLA_HEADS * MLA_NOPE)
    pe = jnp.concatenate([_pad_cols(uq[:, h, MLA_NOPE:], LANES) for h in range(MLA_HEADS)], axis=1)
    per = jnp.concatenate([_pad_cols(_rot_half_cols(uq[:, h, MLA_NOPE:]), LANES) for h in range(MLA_HEADS)], axis=1)
    wq = jnp.concatenate([nope, pe, per], axis=1).astype(BF16)
    ukv = mla_w_ukv[l].reshape(MLA_KV_LORA, MLA_HEADS, MLA_NOPE + MLA_VDIM)
    wkv = jnp.concatenate([ukv[:, :, :MLA_NOPE].reshape(MLA_KV_LORA, -1),
                           ukv[:, :, MLA_NOPE:].reshape(MLA_KV_LORA, -1)], axis=1).astype(BF16)

    z = lambda r: jnp.zeros((r, RWKV_WIDTH), F32)
    w2p = jnp.concatenate([rwkv_w2[l], z(LANES - RWKV_DECAY_LORA)], axis=0).astype(BF16)
    a2p = jnp.concatenate([z(RWKV_DECAY_LORA), rwkv_a2[l], z(RWKV_GATE_LORA)], axis=0).astype(BF16)
    g2p = jnp.concatenate([z(RWKV_DECAY_LORA + RWKV_AAA_LORA), rwkv_g2[l]], axis=0).astype(BF16)
    return w_all, wq, wkv, w2p, a2p, g2p


def kernel(x, positions, mix_norm, w_in, mla_q_norm, mla_w_uq, mla_kv_norm, mla_w_ukv, mla_out_norm, rwkv_mu, rwkv_w0, rwkv_w2, rwkv_a0, rwkv_a2, rwkv_g2, rwkv_k_k, rwkv_k_a, rwkv_r_k, rwkv_ln_w, rwkv_ln_b, mlstm_conv_w, mlstm_conv_b, mlstm_i_bias, mlstm_f_bias, mlstm_out_norm, w_out, ffn_norm, w_gate, w_up, w_down, final_norm):
    batch, seq, _ = x.shape
    depth = w_in.shape[0]
    xt = x.reshape(batch * seq, D_MODEL)
    cos, sin = _rope_tables(positions)
    row = lambda a: a.reshape(1, -1)
    for l in range(depth):
        w_all, wq, wkv, w2p, a2p, g2p = _layer_weights(l, w_in, mla_w_uq, mla_w_ukv, rwkv_w2, rwkv_a2, rwkv_g2)
        mla_in, rwkv_in, mlstm_in = _inproj(xt, row(mix_norm[l]), w_all)
        q, k, v = _mla_prep(mla_in, cos, sin, row(mla_q_norm[l]), row(mla_kv_norm[l]), wq, wkv)
        y_mla = _mla_attention(q, k, v, row(mla_out_norm[l]), batch, seq)
        y_rwkv = _rwkv(rwkv_in, batch, seq, row(rwkv_mu[l]), row(rwkv_w0[l]), row(rwkv_a0[l]), row(rwkv_k_k[l]),
                       row(rwkv_k_a[l]), row(rwkv_r_k[l]), row(rwkv_ln_w[l]), row(rwkv_ln_b[l]), w2p, a2p, g2p)
        y_mlstm = _mlstm(mlstm_in, batch, seq, mlstm_conv_w[l], row(mlstm_conv_b[l]),
                         _pad_cols(row(mlstm_i_bias[l]), LANES), _pad_cols(row(mlstm_f_bias[l]), LANES),
                         row(mlstm_out_norm[l]))
        xt = _out_ffn(xt, y_mla, y_rwkv, y_mlstm, w_out[l].astype(BF16), row(ffn_norm[l]),
                      w_gate[l].astype(BF16), w_up[l].astype(BF16), w_down[l].astype(BF16),
                      row(final_norm), final_norm=(l == depth - 1))
    return xt.reshape(batch, seq, D_MODEL)
```

```python
import functools
import math

import jax
import jax.numpy as jnp
from jax import lax
from jax.experimental import pallas as pl
from jax.experimental.pallas import tpu as pltpu

F32 = jnp.float32
BF16 = jnp.bfloat16

D_MODEL = 1024
DEPTH = 2
MLA_HEADS = 4
MLA_NOPE = 128
MLA_ROPE = 64
MLA_VDIM = 128
MLA_Q_LORA = 256
MLA_KV_LORA = 256
MLA_WIDTH = MLA_HEADS * MLA_VDIM
MLA_QK = 256
ROPE_THETA = 10000.0
RWKV_HEADS = 4
RWKV_HEAD = 64
RWKV_WIDTH = 256
RWKV_DECAY_LORA = 32
RWKV_AAA_LORA = 32
RWKV_GATE_LORA = 64
RWKV_IN = 3 * RWKV_WIDTH + 128
RWKV_LN_EPS = 64e-5
MLSTM_HEADS = 4
MLSTM_QK = 32
MLSTM_V = 64
MLSTM_WIDTH = 256
MLSTM_CONV = 4
MLSTM_IN = 1024
D_FF = 2816
NORM_EPS = 1e-6
LANES = 128

MLA_CQ = 0
MLA_CKV = MLA_CQ + MLA_Q_LORA
MLA_KPE = MLA_CKV + MLA_KV_LORA
MLA_KPER = MLA_KPE + LANES
MLA_IN = MLA_KPER + LANES

TM_INPROJ = 512
TM_PREP = 512
TQ_ATTN = 512
TK_ATTN = 512
HP_ATTN = 4
RWKV_CHUNK = 64
RWKV_PREP_CHUNKS = 2
RWKV_SCAN_CHUNKS = 2
MLSTM_CHUNK = 256
TM_FFN = 512
TF_FFN = 256
VMEM_LIMIT = 56 * 1024 * 1024


def _cparams(sem):
    return pltpu.CompilerParams(dimension_semantics=sem, vmem_limit_bytes=VMEM_LIMIT)


def _resident(a):
    nd = a.ndim
    return pl.BlockSpec(a.shape, lambda *_: (0,) * nd, pipeline_mode=pl.Buffered(1))


def _bdot(a, b):
    return jnp.dot(a.astype(BF16), b.astype(BF16), preferred_element_type=F32)


def _bdot_nt(a, b):
    return lax.dot_general(a.astype(BF16), b.astype(BF16), (((1,), (1,)), ((), ())),
                           preferred_element_type=F32)


def _bdot_tn(a, b):
    return lax.dot_general(a.astype(BF16), b.astype(BF16), (((0,), (0,)), ((), ())),
                           preferred_element_type=F32)


def _split3(x):
    h = x.astype(BF16)
    r1 = x - h.astype(F32)
    m = r1.astype(BF16)
    lo = (r1 - m.astype(F32)).astype(BF16)
    return h, m, lo


def _exact_left_dot(sel, x):
    h, m, lo = _split3(x)
    s = sel.astype(BF16)
    return (jnp.dot(s, h, preferred_element_type=F32) + jnp.dot(s, m, preferred_element_type=F32)
            + jnp.dot(s, lo, preferred_element_type=F32))


def _exact_right_dot(x, sel, parts=3):
    pieces = _split3(x)[:parts]
    s = sel.astype(BF16)
    out = jnp.dot(pieces[0], s, preferred_element_type=F32)
    for p in pieces[1:]:
        out = out + jnp.dot(p, s, preferred_element_type=F32)
    return out


def _rms(x, g):
    return x * lax.rsqrt(jnp.mean(x * x, axis=-1, keepdims=True) + NORM_EPS) * g


def _sigmoid(x):
    return 1.0 / (1.0 + jnp.exp(-x))


def _log_sigmoid(x):
    return jnp.minimum(x, 0.0) - jnp.log1p(jnp.exp(-jnp.abs(x)))


def _div(x, d):
    assert d & (d - 1) == 0
    return lax.shift_right_logical(x, d.bit_length() - 1)


def _mod(x, d):
    assert d & (d - 1) == 0
    return lax.bitwise_and(x, d - 1)


def _segsum(x, seg):
    lane = lax.broadcasted_iota(jnp.int32, x.shape, 1)
    out = jnp.zeros_like(x)
    for h in range(x.shape[1] // seg):
        m = (lane >= h * seg) & (lane < (h + 1) * seg)
        s = jnp.sum(jnp.where(m, x, 0.0), axis=-1, keepdims=True)
        out = jnp.where(m, s, out)
    return out


def _shift_rows(x, prev, s, chunk):
    n = x.shape[0]
    row = lax.broadcasted_iota(jnp.int32, x.shape, 0)
    return jnp.where(_mod(row, chunk) >= s, pltpu.roll(x, s, 0), pltpu.roll(prev, n - chunk + s, 0))


def _rope_kernel(pos_ref, invf_ref, cos_ref, sin_ref):
    ang = pos_ref[...].astype(F32) * invf_ref[...]
    cos_ref[...] = jnp.cos(ang)
    sin_ref[...] = jnp.sin(ang)


def _rope_tables(positions):
    t = positions.size
    tm = min(1024, t)
    inv_freq = ROPE_THETA ** (-jnp.arange(0, MLA_ROPE, 2, dtype=F32) / MLA_ROPE)
    invf = jnp.tile(inv_freq, LANES // (MLA_ROPE // 2))[None, :]
    return pl.pallas_call(
        _rope_kernel,
        out_shape=(jax.ShapeDtypeStruct((t, LANES), F32), jax.ShapeDtypeStruct((t, LANES), F32)),
        grid=(t // tm,),
        in_specs=[pl.BlockSpec((tm, 1), lambda i: (i, 0)), pl.BlockSpec((1, LANES), lambda i: (0, 0))],
        out_specs=(pl.BlockSpec((tm, LANES), lambda i: (i, 0)), pl.BlockSpec((tm, LANES), lambda i: (i, 0))),
        compiler_params=_cparams(("parallel",)),
        name="rope_tables",
    )(positions.reshape(t, 1), invf)


def _inproj_kernel(x_ref, g_ref, w_ref, mla_ref, rwkv_ref, mlstm_ref):
    hb = _rms(x_ref[...], g_ref[...]).astype(BF16)
    mla_ref[...] = jnp.dot(hb, w_ref[:, 0:MLA_IN], preferred_element_type=F32)
    rwkv_ref[...] = jnp.dot(hb, w_ref[:, MLA_IN:MLA_IN + RWKV_IN], preferred_element_type=F32)
    mlstm_ref[...] = jnp.dot(hb, w_ref[:, MLA_IN + RWKV_IN:], preferred_element_type=F32)


def _inproj(x, g, w):
    t = x.shape[0]
    tm = TM_INPROJ
    row = lambda width: pl.BlockSpec((tm, width), lambda i: (i, 0))
    return pl.pallas_call(
        _inproj_kernel,
        out_shape=(jax.ShapeDtypeStruct((t, MLA_IN), F32), jax.ShapeDtypeStruct((t, RWKV_IN), F32),
                   jax.ShapeDtypeStruct((t, MLSTM_IN), F32)),
        grid=(t // tm,),
        in_specs=[row(D_MODEL), _resident(g), _resident(w)],
        out_specs=(row(MLA_IN), row(RWKV_IN), row(MLSTM_IN)),
        compiler_params=_cparams(("parallel",)),
        name="inproj",
    )(x, g, w)


def _mla_prep_kernel(in_ref, cos_ref, sin_ref, qn_ref, kvn_ref, wq_ref, wkv_ref, q_ref, k_ref, v_ref):
    cos = cos_ref[...]
    sin = sin_ref[...]
    scale = (MLA_NOPE + MLA_ROPE) ** -0.5 * math.log2(math.e)
    hw = MLA_HEADS * LANES
    cqn = _rms(in_ref[:, MLA_CQ:MLA_CKV], qn_ref[...]).astype(BF16)
    q = jnp.dot(cqn, wq_ref[...], preferred_element_type=F32)
    ckvn = _rms(in_ref[:, MLA_CKV:MLA_KPE], kvn_ref[...]).astype(BF16)
    kv = jnp.dot(ckvn, wkv_ref[...], preferred_element_type=F32)
    kp = (in_ref[:, MLA_KPE:MLA_KPER] * cos + in_ref[:, MLA_KPER:MLA_IN] * sin).astype(BF16)
    for h in range(MLA_HEADS):
        c0 = h * LANES
        pe = q[:, hw + c0:hw + c0 + LANES] * cos + q[:, 2 * hw + c0:2 * hw + c0 + LANES] * sin
        q_ref[:, h * MLA_QK:h * MLA_QK + LANES] = (q[:, c0:c0 + LANES] * scale).astype(BF16)
        q_ref[:, h * MLA_QK + LANES:(h + 1) * MLA_QK] = (pe * scale).astype(BF16)
        k_ref[:, h * MLA_QK:h * MLA_QK + LANES] = kv[:, c0:c0 + LANES].astype(BF16)
        k_ref[:, h * MLA_QK + LANES:(h + 1) * MLA_QK] = kp
    v_ref[...] = kv[:, hw:].astype(BF16)


def _mla_prep(mla_in, cos, sin, qn, kvn, wq, wkv):
    t = mla_in.shape[0]
    tm = TM_PREP
    row = lambda w: pl.BlockSpec((tm, w), lambda i: (i, 0))
    return pl.pallas_call(
        _mla_prep_kernel,
        out_shape=(jax.ShapeDtypeStruct((t, MLA_HEADS * MLA_QK), BF16),
                   jax.ShapeDtypeStruct((t, MLA_HEADS * MLA_QK), BF16),
                   jax.ShapeDtypeStruct((t, MLA_WIDTH), BF16)),
        grid=(t // tm,),
        in_specs=[row(MLA_IN), row(LANES), row(LANES), _resident(qn), _resident(kvn), _resident(wq),
                  _resident(wkv)],
        out_specs=(row(MLA_HEADS * MLA_QK), row(MLA_HEADS * MLA_QK), row(MLA_WIDTH)),
        compiler_params=_cparams(("parallel",)),
        name="mla_prep",
    )(mla_in, cos, sin, qn, kvn, wq, wkv)


def _attn_kernel(q_ref, k_ref, v_ref, g_ref, o_ref, m_sc, l_sc, acc_sc, *, tq, tk, hp):
    i = pl.program_id(2)
    m_sc[...] = jnp.full(m_sc.shape, -jnp.inf, F32)
    l_sc[...] = jnp.zeros(l_sc.shape, F32)
    acc_sc[...] = jnp.zeros(acc_sc.shape, F32)

    def step(j, masked):
        off = pl.multiple_of(j * tk, tk)
        hs = range(hp)
        s = [lax.dot_general(q_ref[0, :, h * MLA_QK:(h + 1) * MLA_QK],
                             k_ref[0, pl.ds(off, tk), h * MLA_QK:(h + 1) * MLA_QK],
                             (((1,), (1,)), ((), ())), preferred_element_type=F32) for h in hs]
        if masked:
            rows = lax.broadcasted_iota(jnp.int32, (tq, tk), 0)
            cols = lax.broadcasted_iota(jnp.int32, (tq, tk), 1)
            s = [jnp.where(cols <= rows, s[h], -jnp.inf) for h in hs]
        m_old = [m_sc[h] for h in hs]
        m_new = [jnp.maximum(m_old[h], jnp.max(s[h], axis=-1, keepdims=True)) for h in hs]
        p = [jnp.exp2(s[h] - jnp.tile(m_new[h], (1, tk // LANES))) for h in hs]
        alpha = [jnp.exp2(m_old[h] - m_new[h]) for h in hs]
        for h in hs:
            l_sc[h] = alpha[h] * l_sc[h] + jnp.sum(p[h], axis=-1, keepdims=True)
            m_sc[h] = m_new[h]
        pv = [jnp.dot(p[h].astype(BF16), v_ref[0, pl.ds(off, tk), h * MLA_VDIM:(h + 1) * MLA_VDIM],
                      preferred_element_type=F32) for h in hs]
        for h in hs:
            acc_sc[h] = alpha[h] * acc_sc[h] + pv[h]

    def body(j, c):
        step(j, False)
        return c

    lax.fori_loop(0, i, body, 0)
    step(i, True)
    for h in range(hp):
        o = acc_sc[h] / l_sc[h]
        o_ref[0, :, h * MLA_VDIM:(h + 1) * MLA_VDIM] = _rms(
            o, g_ref[:, h * MLA_VDIM:(h + 1) * MLA_VDIM]).astype(o_ref.dtype)


def _mla_attention(q, k, v, g, batch, seq):
    tq, tk, hp = TQ_ATTN, TK_ATTN, HP_ATTN
    assert tq == tk and MLA_VDIM == LANES
    q = q.reshape(batch, seq, MLA_HEADS * MLA_QK)
    k = k.reshape(batch, seq, MLA_HEADS * MLA_QK)
    v = v.reshape(batch, seq, MLA_WIDTH)
    out = pl.pallas_call(
        functools.partial(_attn_kernel, tq=tq, tk=tk, hp=hp),
        out_shape=jax.ShapeDtypeStruct((batch, seq, MLA_WIDTH), BF16),
        grid=(batch, MLA_HEADS // hp, seq // tq),
        in_specs=[pl.BlockSpec((1, tq, hp * MLA_QK), lambda b, h, i: (b, i, h)),
                  pl.BlockSpec((1, seq, hp * MLA_QK), lambda b, h, i: (b, 0, h)),
                  pl.BlockSpec((1, seq, hp * MLA_VDIM), lambda b, h, i: (b, 0, h)),
                  pl.BlockSpec((1, hp * MLA_VDIM), lambda b, h, i: (0, h))],
        out_specs=pl.BlockSpec((1, tq, hp * MLA_VDIM), lambda b, h, i: (b, i, h)),
        scratch_shapes=[pltpu.VMEM((hp, tq, LANES), F32), pltpu.VMEM((hp, tq, LANES), F32),
                        pltpu.VMEM((hp, tq, MLA_VDIM), F32)],
        compiler_params=_cparams(("parallel", "parallel", "arbitrary")),
        name="mla_attention",
    )(q, k, v, g)
    return out.reshape(batch * seq, MLA_WIDTH)


(SEC_TA, SEC_RT, SEC_UV, SEC_YV, SEC_ARB, SEC_KH, SEC_BH, SEC_V, SEC_BONUS, SEC_GATE) = range(10)
RWKV_SECTIONS = 10


def _rwkv_masks(chunk):
    w = RWKV_WIDTH
    r = lax.broadcasted_iota(jnp.int32, (w, w), 0)
    c = lax.broadcasted_iota(jnp.int32, (w, w), 1)
    return r, c, _div(r, chunk) == _div(c, chunk)


def _tile_heads(z):
    return jnp.concatenate([z] * RWKV_HEADS, axis=0)


def _fold_heads(z, chunk):
    out = z[0:chunk]
    for h in range(1, RWKV_HEADS):
        out = out + z[h * chunk:(h + 1) * chunk]
    return out


def _rwkv_prep_kernel(x_ref, mu_ref, w0_ref, a0_ref, kk_ref, ka_ref, rk_ref, w2_ref, a2_ref, g2_ref,
                      p_ref, pc_ref, prev_sc, *, nb, chunk, cps):
    seg = cps * chunk
    n = nb * seg
    w = RWKV_WIDTH
    hd = RWKV_HEAD

    @pl.when(pl.program_id(0) == 0)
    def _():
        prev_sc[...] = jnp.zeros(prev_sc.shape, F32)

    x = x_ref[...].reshape(n, RWKV_IN)
    shifted = _shift_rows(x, prev_sc[...], 1, seg)
    prev_sc[...] = x
    xs = x + (shifted - x) * mu_ref[...]
    r = xs[:, 0:w]
    k = xs[:, w:2 * w]
    v = xs[:, 2 * w:3 * w]
    lor = xs[:, 3 * w:]
    w_log = _log_sigmoid(w0_ref[...] + _bdot(jnp.tanh(lor), w2_ref[...])) - 0.5
    ld = -jnp.exp(w_log)
    a = _sigmoid(a0_ref[...] + _bdot(lor, a2_ref[...]))
    g = _bdot(_sigmoid(lor), g2_ref[...])
    kk = k * kk_ref[...]
    kk = kk / jnp.maximum(jnp.sqrt(_segsum(kk * kk, hd)), 1e-12)
    k2 = k * (1.0 + (a - 1.0) * ka_ref[...])
    kb = kk * a
    bonus = _segsum(r * k2 * rk_ref[...], hd) * v

    ri, ci, same_chunk = _rwkv_masks(chunk)
    tri = jnp.where(same_chunk & (ci <= ri), 1.0, 0.0)
    cl = jnp.concatenate([_exact_left_dot(tri, ld[i * w:(i + 1) * w]) for i in range(n // w)], axis=0)
    units = nb * cps
    cl_last = jnp.concatenate(
        [jnp.broadcast_to(cl[(u + 1) * chunk - 1:(u + 1) * chunk, :], (chunk, w)) for u in range(units)], axis=0)
    e_neg = jnp.exp(-cl)
    e_end = jnp.exp(cl_last - cl)
    a_t = kk * jnp.exp(cl - ld)
    k_t = k2 * e_neg
    b_t = kb * e_neg
    r_t = r * jnp.exp(cl)
    k_h = k2 * e_end
    b_h = kb * e_end

    bd = _div(ri, chunk) == _div(ci, hd)
    strict = same_chunk & (ci < ri)
    incl = same_chunk & (ci <= ri)
    bd16 = _div(ri, 16) == _div(ci, 16)
    bd32 = _div(ri, 32) == _div(ci, 32)
    eye = jnp.where(ri == ci, 1.0, 0.0)
    hs = RWKV_HEADS * chunk

    def put(u, sec, val):
        b, j = divmod(u, cps)
        p_ref[b, j * chunk:(j + 1) * chunk, sec * w:(sec + 1) * w] = val.astype(p_ref.dtype)

    def mm(x, y):
        return jnp.dot(x, y, preferred_element_type=F32)

    us = range(units)
    sls = [slice(u * chunk, (u + 1) * chunk) for u in us]
    a16, r16, v16, k16, b16 = (z.astype(BF16) for z in (a_t, r_t, v, k_t, b_t))
    a_st = [jnp.where(bd, _tile_heads(a16[sl]), 0.0) for sl in sls]
    r_st = [jnp.where(bd, _tile_heads(r16[sl]), 0.0) for sl in sls]
    v_st = [jnp.where(bd, _tile_heads(v16[sl]), 0.0) for sl in sls]
    sc = [lax.dot_general(jnp.concatenate([a_st[u], r_st[u]], axis=0),
                          jnp.concatenate([_tile_heads(k16[sls[u]]), _tile_heads(b16[sls[u]])], axis=0),
                          (((1,), (1,)), ((), ())), preferred_element_type=F32) for u in us]
    l_ab = [jnp.where(strict, sc[u][0:hs, hs:], 0.0) for u in us]
    l_ak = [jnp.where(strict, sc[u][0:hs, 0:hs], 0.0).astype(BF16) for u in us]
    a_rk = [jnp.where(incl, sc[u][hs:, 0:hs], 0.0).astype(BF16) for u in us]
    for u in us:
        put(u, SEC_ARB, _fold_heads(jnp.where(incl, sc[u][hs:, hs:], 0.0), chunk))
    xm = [-jnp.where(bd16, l_ab[u], 0.0) for u in us]
    xm16 = [z.astype(BF16) for z in xm]
    off32 = [jnp.where(bd32 & jnp.logical_not(bd16), l_ab[u], 0.0).astype(BF16) for u in us]
    off64 = [jnp.where(jnp.logical_not(bd32), l_ab[u], 0.0).astype(BF16) for u in us]
    x2 = [mm(xm16[u], xm16[u]).astype(BF16) for u in us]
    wv = [mm(l_ak[u], v_st[u]).astype(BF16) for u in us]
    t_lo = [eye + xm[u] for u in us]
    t_lo = [(t_lo[u] + mm(t_lo[u].astype(BF16), x2[u])).astype(BF16) for u in us]
    x4 = [mm(x2[u], x2[u]) for u in us]
    x4b = [z.astype(BF16) for z in x4]
    for u in us:
        put(u, SEC_YV, _fold_heads(mm(a_rk[u], v_st[u]), chunk))
    x8 = [mm(x4b[u], x4b[u]).astype(BF16) for u in us]
    t_hi = [eye + x4[u] for u in us]
    t_hi = [(t_hi[u] + mm(t_hi[u].astype(BF16), x8[u])).astype(BF16) for u in us]
    t_inv = [mm(t_lo[u], t_hi[u]) for u in us]
    for off in (off32, off64):
        tb = [z.astype(BF16) for z in t_inv]
        mid = [mm(tb[u], off[u]).astype(BF16) for u in us]
        t_inv = [t_inv[u] - mm(mid[u], tb[u]) for u in us]
    tb = [z.astype(BF16) for z in t_inv]
    for u in us:
        put(u, SEC_TA, _fold_heads(mm(tb[u], a_st[u]), chunk))
    for u in us:
        put(u, SEC_UV, _fold_heads(mm(tb[u], wv[u]), chunk))
    for u in us:
        b, j = divmod(u, cps)
        put(u, SEC_RT, r16[sls[u]])
        put(u, SEC_KH, k_h[sls[u]])
        put(u, SEC_BH, b_h[sls[u]])
        put(u, SEC_V, v16[sls[u]])
        put(u, SEC_BONUS, bonus[sls[u]])
        put(u, SEC_GATE, g[sls[u]])
        pc_ref[j, b:b + 1, :] = jnp.exp(cl_last[sls[u]][0:1, :])


def _rwkv_scan_kernel(p_ref, pc_ref, lnw_ref, lnb_ref, o_ref, state_sc, *, nb, chunk, cps):
    seg = cps * chunk
    n = nb * seg
    w = RWKV_WIDTH
    hd = RWKV_HEAD

    @pl.when(pl.program_id(0) == 0)
    def _():
        state_sc[...] = jnp.zeros(state_sc.shape, F32)

    ri, ci, _ = _rwkv_masks(chunk)
    bd = _div(ri, chunk) == _div(ci, hd)
    bdv = _div(ri, hd) == _div(ci, hd)
    ones_bd = jnp.where(bdv, 1.0, 0.0).astype(BF16)

    def sec(b, j, s):
        return p_ref[b, j * chunk:(j + 1) * chunk, s * w:(s + 1) * w]

    ys = [[None] * cps for _ in range(nb)]
    bs = range(nb)
    for j in range(cps):
        gs = [state_sc[b] for b in bs]
        p1 = [lax.dot_general(jnp.concatenate([sec(b, j, SEC_TA), sec(b, j, SEC_RT)], axis=0),
                              gs[b].astype(BF16), (((1,), (1,)), ((), ())), preferred_element_type=F32)
              for b in bs]
        u = [(p1[b][0:chunk] + sec(b, j, SEC_UV).astype(F32)).astype(BF16) for b in bs]
        upd = [lax.dot_general(jnp.concatenate([sec(b, j, SEC_V), -u[b]], axis=0),
                               jnp.concatenate([sec(b, j, SEC_KH), sec(b, j, SEC_BH)], axis=0),
                               (((0,), (0,)), ((), ())), preferred_element_type=F32) for b in bs]
        for b in bs:
            state_sc[b] = gs[b] * pc_ref[j, b:b + 1, :] + jnp.where(bdv, upd[b], 0.0)
        for b in bs:
            u_st = jnp.where(bd, _tile_heads(u[b]), 0.0)
            ys[b][j] = (p1[b][chunk:] + sec(b, j, SEC_YV).astype(F32)
                        - jnp.dot(sec(b, j, SEC_ARB), u_st, preferred_element_type=F32))

    y = jnp.concatenate([ys[b][j] for b in range(nb) for j in range(cps)], axis=0)
    mean = _exact_right_dot(y, ones_bd) * (1.0 / hd)
    d = y - mean
    var = _exact_right_dot(d * d, ones_bd) * (1.0 / hd)
    yn = d * lax.rsqrt(var + RWKV_LN_EPS) * lnw_ref[...] + lnb_ref[...]
    bonus = p_ref[:, :, SEC_BONUS * w:(SEC_BONUS + 1) * w].reshape(n, w).astype(F32)
    gate = p_ref[:, :, SEC_GATE * w:(SEC_GATE + 1) * w].reshape(n, w).astype(F32)
    o_ref[...] = ((yn + bonus) * gate).astype(o_ref.dtype).reshape(o_ref.shape)


def _rwkv(x, batch, seq, mu, w0, a0, k_k, k_a, r_k, ln_w, ln_b, w2p, a2p, g2p):
    chunk = RWKV_CHUNK
    assert RWKV_HEADS * chunk == RWKV_WIDTH
    x = x.reshape(batch, seq, RWKV_IN)
    pw = RWKV_SECTIONS * RWKV_WIDTH
    cps = RWKV_PREP_CHUNKS
    seg = cps * chunk
    packed, pc = pl.pallas_call(
        functools.partial(_rwkv_prep_kernel, nb=batch, chunk=chunk, cps=cps),
        out_shape=(jax.ShapeDtypeStruct((batch, seq, pw), BF16),
                   jax.ShapeDtypeStruct((seq // chunk, batch, RWKV_WIDTH), F32)),
        grid=(seq // seg,),
        in_specs=[pl.BlockSpec((batch, seg, RWKV_IN), lambda c: (0, c, 0)),
                  _resident(mu), _resident(w0), _resident(a0), _resident(k_k), _resident(k_a), _resident(r_k),
                  _resident(w2p), _resident(a2p), _resident(g2p)],
        out_specs=(pl.BlockSpec((batch, seg, pw), lambda c: (0, c, 0)),
                   pl.BlockSpec((cps, batch, RWKV_WIDTH), lambda c: (c, 0, 0))),
        scratch_shapes=[pltpu.VMEM((batch * seg, RWKV_IN), F32)],
        compiler_params=_cparams(("arbitrary",)),
        name="rwkv7_prep",
    )(x, mu, w0, a0, k_k, k_a, r_k, w2p, a2p, g2p)
    cps = RWKV_SCAN_CHUNKS
    seg = cps * chunk
    out = pl.pallas_call(
        functools.partial(_rwkv_scan_kernel, nb=batch, chunk=chunk, cps=cps),
        out_shape=jax.ShapeDtypeStruct((batch, seq, RWKV_WIDTH), BF16),
        grid=(seq // seg,),
        in_specs=[pl.BlockSpec((batch, seg, pw), lambda c: (0, c, 0)),
                  pl.BlockSpec((cps, batch, RWKV_WIDTH), lambda c: (c, 0, 0)),
                  _resident(ln_w), _resident(ln_b)],
        out_specs=pl.BlockSpec((batch, seg, RWKV_WIDTH), lambda c: (0, c, 0)),
        scratch_shapes=[pltpu.VMEM((batch, RWKV_WIDTH, RWKV_WIDTH), F32)],
        compiler_params=_cparams(("arbitrary",)),
        name="rwkv7_scan",
    )(packed, pc, ln_w, ln_b)
    return out.reshape(batch * seq, RWKV_WIDTH)


ML_QK = 0
ML_V = 2 * MLSTM_HEADS * MLSTM_QK
ML_O = ML_V + MLSTM_WIDTH
ML_I = ML_O + MLSTM_WIDTH
ML_F = ML_I + LANES


def _cummax_rows(x):
    n = x.shape[0]
    row = lax.broadcasted_iota(jnp.int32, x.shape, 0)
    sh = 1
    while sh < n:
        x = jnp.maximum(x, jnp.where(row >= sh, pltpu.roll(x, sh, 0), -jnp.inf))
        sh *= 2
    return x


def _mlstm_kernel(x_ref, cw_ref, cb_ref, ib_ref, fb_ref, on_ref, o_ref,
                  prev_sc, c_sc, n_sc, m_sc, *, nb, chunk):
    n = nb * chunk
    nh = MLSTM_HEADS
    dk = MLSTM_QK
    dv = MLSTM_V
    qkw = nh * dk
    vw = MLSTM_WIDTH

    @pl.when(pl.program_id(0) == 0)
    def _():
        prev_sc[...] = jnp.zeros(prev_sc.shape, F32)
        c_sc[...] = jnp.zeros(c_sc.shape, F32)
        n_sc[...] = jnp.zeros(n_sc.shape, F32)
        m_sc[...] = jnp.zeros(m_sc.shape, F32)

    x = x_ref[...].reshape(n, MLSTM_IN)
    qk_raw = x[:, ML_QK:ML_V]
    prev = prev_sc[...]
    conv = cb_ref[...] + qk_raw * cw_ref[MLSTM_CONV - 1:MLSTM_CONV, :]
    for s in range(1, MLSTM_CONV):
        conv = conv + _shift_rows(qk_raw, prev, s, chunk) * cw_ref[MLSTM_CONV - 1 - s:MLSTM_CONV - s, :]
    prev_sc[...] = qk_raw
    qk = conv * _sigmoid(conv)
    q_all = qk[:, 0:qkw] * (dk ** -0.5)
    k_all = qk[:, qkw:]
    v_all = x[:, ML_V:ML_O]
    o_pre = x[:, ML_O:ML_I]
    li_all = x[:, ML_I:ML_F] + ib_ref[...]
    lf_all = _log_sigmoid(x[:, ML_F:ML_F + LANES] + fb_ref[...])

    ri = lax.broadcasted_iota(jnp.int32, (chunk, chunk), 0)
    ci = lax.broadcasted_iota(jnp.int32, (chunk, chunk), 1)
    causal = ci <= ri
    tri = jnp.where(causal, 1.0, 0.0)
    lane_k = lax.broadcasted_iota(jnp.int32, (chunk, qkw), 1)
    lane_v = lax.broadcasted_iota(jnp.int32, (chunk, vw), 1)
    rc = lax.broadcasted_iota(jnp.int32, (qkw, vw), 0)
    cc = lax.broadcasted_iota(jnp.int32, (qkw, vw), 1)
    cmask = _div(rc, dk) == _div(cc, dv)
    expand_v = jnp.where(rc == _div(cc, dv), 1.0, 0.0).astype(BF16)
    rk = lax.broadcasted_iota(jnp.int32, (qkw, qkw), 0)
    ck = lax.broadcasted_iota(jnp.int32, (qkw, qkw), 1)
    expand_k = jnp.where(rk == _div(ck, dk), 1.0, 0.0).astype(BF16)
    gather_k = jnp.where(_div(rk, dk) == ck, 1.0, 0.0).astype(BF16)

    bs = range(nb)
    sls = [slice(b * chunk, (b + 1) * chunk) for b in bs]
    q = [q_all[sl] for sl in sls]
    k = [k_all[sl] for sl in sls]
    k16 = [z.astype(BF16) for z in k]
    v = [v_all[sl] for sl in sls]
    li = [li_all[sl] for sl in sls]
    c_old = [c_sc[b] for b in bs]
    n_old = [n_sc[b] for b in bs]
    m_prev = [m_sc[b] for b in bs]
    g = [_exact_left_dot(tri, lf_all[sl]) for sl in sls]
    lig = [li[b] - g[b] for b in bs]
    inter_log = [g[b] + m_prev[b] for b in bs]
    m_t = [jnp.maximum(inter_log[b], g[b] + _cummax_rows(lig[b])) for b in bs]
    inter_w = [jnp.exp(inter_log[b] - m_t[b]) for b in bs]
    gm = [g[b] - m_t[b] for b in bs]
    lig_t = [jnp.transpose(z) for z in lig]
    qn = [_exact_right_dot(q[b] * n_old[b], gather_k, parts=2) for b in bs]
    q_c = [_bdot(q[b], c_old[b]) for b in bs]
    ssum = [jnp.zeros((chunk, LANES), F32) for _ in bs]
    num = [jnp.zeros((chunk, vw), F32) for _ in bs]
    for h in range(nh):
        mk = (lane_k >= h * dk) & (lane_k < (h + 1) * dk)
        mv = (lane_v >= h * dv) & (lane_v < (h + 1) * dv)
        qk_h = [lax.dot_general(jnp.where(mk, q[b], 0.0).astype(BF16), k16[b], (((1,), (1,)), ((), ())),
                                preferred_element_type=F32) for b in bs]
        d = [jnp.broadcast_to(gm[b][:, h:h + 1], (chunk, chunk)) + lig_t[b][h:h + 1, :] for b in bs]
        s = [qk_h[b] * jnp.exp(jnp.where(causal, d[b], -jnp.inf)) for b in bs]
        ssum = [jnp.where(lane_k == h, jnp.sum(s[b], axis=-1, keepdims=True), ssum[b]) for b in bs]
        num = [num[b] + _bdot(s[b], jnp.where(mv, v[b], 0.0)) for b in bs]
    den = [inter_w[b] * qn[b] + ssum[b] for b in bs]
    rden = [1.0 / jnp.maximum(jnp.abs(den[b]), jnp.exp(-m_t[b])) for b in bs]
    g_last = [g[b][chunk - 1:chunk, :] for b in bs]
    a_all = [g_last[b] - g[b] + li[b] for b in bs]
    m_new = [jnp.maximum(g_last[b] + m_prev[b], jnp.max(a_all[b], axis=0, keepdims=True)) for b in bs]
    dec = [jnp.exp(g_last[b] + m_prev[b] - m_new[b]) for b in bs]
    wts = [jnp.exp(a_all[b] - m_new[b]) for b in bs]
    per_head = [jnp.concatenate([inter_w[b], rden[b], wts[b], jnp.broadcast_to(dec[b], (8, LANES))], axis=0)
                for b in bs]
    on_v = [_exact_right_dot(per_head[b], expand_v, parts=2) for b in bs]
    on_k = [_exact_right_dot(per_head[b][2 * chunk:], expand_k, parts=2) for b in bs]
    hs = [(on_v[b][0:chunk] * q_c[b] + num[b]) * on_v[b][chunk:2 * chunk] for b in bs]
    for b in bs:
        c_sc[b] = c_old[b] * on_v[b][3 * chunk:3 * chunk + 1] + jnp.where(
            cmask, _bdot_tn(k16[b], on_v[b][2 * chunk:3 * chunk] * v[b]), 0.0)
        n_sc[b] = n_old[b] * on_k[b][chunk:chunk + 1] + jnp.sum(on_k[b][0:chunk] * k[b], axis=0, keepdims=True)
        m_sc[b] = m_new[b]

    hh = jnp.concatenate(hs, axis=0)
    ms = _segsum(hh * hh, dv) * (1.0 / dv)
    out = hh * lax.rsqrt(ms + NORM_EPS) * on_ref[...] * _sigmoid(o_pre)
    o_ref[...] = out.astype(o_ref.dtype).reshape(o_ref.shape)


def _mlstm(x, batch, seq, cw, cb, ib, fb, on):
    chunk = MLSTM_CHUNK
    x = x.reshape(batch, seq, MLSTM_IN)
    out = pl.pallas_call(
        functools.partial(_mlstm_kernel, nb=batch, chunk=chunk),
        out_shape=jax.ShapeDtypeStruct((batch, seq, MLSTM_WIDTH), BF16),
        grid=(seq // chunk,),
        in_specs=[pl.BlockSpec((batch, chunk, MLSTM_IN), lambda c: (0, c, 0)),
                  _resident(cw), _resident(cb), _resident(ib), _resident(fb), _resident(on)],
        out_specs=pl.BlockSpec((batch, chunk, MLSTM_WIDTH), lambda c: (0, c, 0)),
        scratch_shapes=[pltpu.VMEM((batch * chunk, 2 * MLSTM_HEADS * MLSTM_QK), F32),
                        pltpu.VMEM((batch, MLSTM_HEADS * MLSTM_QK, MLSTM_WIDTH), F32),
                        pltpu.VMEM((batch, 1, MLSTM_HEADS * MLSTM_QK), F32),
                        pltpu.VMEM((batch, 1, LANES), F32)],
        compiler_params=_cparams(("arbitrary",)),
        name="mlstm",
    )(x, cw, cb, ib, fb, on)
    return out.reshape(batch * seq, MLSTM_WIDTH)


def _ffn_kernel(x_ref, ya_ref, yb_ref, yc_ref, wo_ref, g_ref, wg_ref, wu_ref, wd_ref, fg_ref,
                o_ref, act_sc, *, final_norm, tf):
    y = jnp.concatenate([ya_ref[...], yb_ref[...], yc_ref[...]], axis=-1)
    x1 = x_ref[...] + jnp.dot(y, wo_ref[...], preferred_element_type=F32)
    h = _rms(x1, g_ref[...]).astype(BF16)
    for c in range(D_FF // tf):
        gate = jnp.dot(h, wg_ref[:, c * tf:(c + 1) * tf], preferred_element_type=F32)
        up = jnp.dot(h, wu_ref[:, c * tf:(c + 1) * tf], preferred_element_type=F32)
        act_sc[:, c * tf:(c + 1) * tf] = (gate * _sigmoid(gate) * up).astype(BF16)
    out = x1 + jnp.dot(act_sc[...], wd_ref[...], preferred_element_type=F32)
    if final_norm:
        out = _rms(out, fg_ref[...])
    o_ref[...] = out


def _out_ffn(x, ya, yb, yc, wo, g, wg, wu, wd, fg, final_norm):
    t = x.shape[0]
    tm, tf = TM_FFN, TF_FFN
    row = lambda w: pl.BlockSpec((tm, w), lambda i: (i, 0))
    return pl.pallas_call(
        functools.partial(_ffn_kernel, final_norm=final_norm, tf=tf),
        out_shape=jax.ShapeDtypeStruct((t, D_MODEL), F32),
        grid=(t // tm,),
        in_specs=[row(D_MODEL), row(MLA_WIDTH), row(RWKV_WIDTH), row(MLSTM_WIDTH), _resident(wo), _resident(g),
                  _resident(wg), _resident(wu), _resident(wd), _resident(fg)],
        out_specs=row(D_MODEL),
        scratch_shapes=[pltpu.VMEM((tm, D_FF), BF16)],
        compiler_params=_cparams(("parallel",)),
        name="out_ffn",
    )(x, ya, yb, yc, wo, g, wg, wu, wd, fg)


def _pad_cols(w, width):
    return jnp.pad(w, ((0, 0), (0, width - w.shape[1])))


def _rot_half_cols(w):
    half = w.shape[1] // 2
    return jnp.concatenate([-w[:, half:], w[:, :half]], axis=1)


def _layer_weights(l, w_in, mla_w_uq, mla_w_ukv, rwkv_w2, rwkv_a2, rwkv_g2):
    wi = w_in[l]
    c_q, c_kv, k_pe = wi[:, 0:256], wi[:, 256:512], wi[:, 512:576]
    rw = wi[:, 576:1472]
    ml = wi[:, 1472:2248]
    w_mla = jnp.concatenate([c_q, c_kv, _pad_cols(k_pe, LANES), _pad_cols(_rot_half_cols(k_pe), LANES)], axis=1)
    w_mlstm = jnp.concatenate([ml[:, 0:256], ml[:, 256:512], ml[:, 520:776],
                               _pad_cols(ml[:, 512:516], LANES), _pad_cols(ml[:, 516:520], LANES)], axis=1)
    w_all = jnp.concatenate([w_mla, rw, w_mlstm], axis=1).astype(BF16)

    uq = mla_w_uq[l].reshape(MLA_Q_LORA, MLA_HEADS, MLA_NOPE + MLA_ROPE)
    nope = uq[:, :, :MLA_NOPE].reshape(MLA_Q_LORA, MLA_HEADS * MLA_NOPE)
    pe = jnp.concatenate([_pad_cols(uq[:, h, MLA_NOPE:], LANES) for h in range(MLA_HEADS)], axis=1)
    per = jnp.concatenate([_pad_cols(_rot_half_cols(uq[:, h, MLA_NOPE:]), LANES) for h in range(MLA_HEADS)], axis=1)
    wq = jnp.concatenate([nope, pe, per], axis=1).astype(BF16)
    ukv = mla_w_ukv[l].reshape(MLA_KV_LORA, MLA_HEADS, MLA_NOPE + MLA_VDIM)
    wkv = jnp.concatenate([ukv[:, :, :MLA_NOPE].reshape(MLA_KV_LORA, -1),
                           ukv[:, :, MLA_NOPE:].reshape(MLA_KV_LORA, -1)], axis=1).astype(BF16)

    z = lambda r: jnp.zeros((r, RWKV_WIDTH), F32)
    w2p = jnp.concatenate([rwkv_w2[l], z(LANES - RWKV_DECAY_LORA)], axis=0).astype(BF16)
    a2p = jnp.concatenate([z(RWKV_DECAY_LORA), rwkv_a2[l], z(RWKV_GATE_LORA)], axis=0).astype(BF16)
    g2p = jnp.concatenate([z(RWKV_DECAY_LORA + RWKV_AAA_LORA), rwkv_g2[l]], axis=0).astype(BF16)
    return w_all, wq, wkv, w2p, a2p, g2p


def kernel(x, positions, mix_norm, w_in, mla_q_norm, mla_w_uq, mla_kv_norm, mla_w_ukv, mla_out_norm, rwkv_mu, rwkv_w0, rwkv_w2, rwkv_a0, rwkv_a2, rwkv_g2, rwkv_k_k, rwkv_k_a, rwkv_r_k, rwkv_ln_w, rwkv_ln_b, mlstm_conv_w, mlstm_conv_b, mlstm_i_bias, mlstm_f_bias, mlstm_out_norm, w_out, ffn_norm, w_gate, w_up, w_down, final_norm):
    batch, seq, _ = x.shape
    depth = w_in.shape[0]
    xt = x.reshape(batch * seq, D_MODEL)
    cos, sin = _rope_tables(positions)
    row = lambda a: a.reshape(1, -1)
    for l in range(depth):
        w_all, wq, wkv, w2p, a2p, g2p = _layer_weights(l, w_in, mla_w_uq, mla_w_ukv, rwkv_w2, rwkv_a2, rwkv_g2)
        mla_in, rwkv_in, mlstm_in = _inproj(xt, row(mix_norm[l]), w_all)
        q, k, v = _mla_prep(mla_in, cos, sin, row(mla_q_norm[l]), row(mla_kv_norm[l]), wq, wkv)
        y_mla = _mla_attention(q, k, v, row(mla_out_norm[l]), batch, seq)
        y_rwkv = _rwkv(rwkv_in, batch, seq, row(rwkv_mu[l]), row(rwkv_w0[l]), row(rwkv_a0[l]), row(rwkv_k_k[l]),
                       row(rwkv_k_a[l]), row(rwkv_r_k[l]), row(rwkv_ln_w[l]), row(rwkv_ln_b[l]), w2p, a2p, g2p)
        y_mlstm = _mlstm(mlstm_in, batch, seq, mlstm_conv_w[l], row(mlstm_conv_b[l]),
                         _pad_cols(row(mlstm_i_bias[l]), LANES), _pad_cols(row(mlstm_f_bias[l]), LANES),
                         row(mlstm_out_norm[l]))
        xt = _out_ffn(xt, y_mla, y_rwkv, y_mlstm, w_out[l].astype(BF16), row(ffn_norm[l]),
                      w_gate[l].astype(BF16), w_up[l].astype(BF16), w_down[l].astype(BF16),
                      row(final_norm), final_norm=(l == depth - 1))
    return xt.reshape(batch, seq, D_MODEL)
```

```python
import functools
import math

import jax
import jax.numpy as jnp
from jax import lax
from jax.experimental import pallas as pl
from jax.experimental.pallas import tpu as pltpu

F32 = jnp.float32
BF16 = jnp.bfloat16

D_MODEL = 1024
DEPTH = 2
MLA_HEADS = 4
MLA_NOPE = 128
MLA_ROPE = 64
MLA_VDIM = 128
MLA_Q_LORA = 256
MLA_KV_LORA = 256
MLA_WIDTH = MLA_HEADS * MLA_VDIM
MLA_QK = 256
ROPE_THETA = 10000.0
RWKV_HEADS = 4
RWKV_HEAD = 64
RWKV_WIDTH = 256
RWKV_DECAY_LORA = 32
RWKV_AAA_LORA = 32
RWKV_GATE_LORA = 64
RWKV_IN = 3 * RWKV_WIDTH + 128
RWKV_LN_EPS = 64e-5
MLSTM_HEADS = 4
MLSTM_QK = 32
MLSTM_V = 64
MLSTM_WIDTH = 256
MLSTM_CONV = 4
MLSTM_IN = 1024
D_FF = 2816
NORM_EPS = 1e-6
LANES = 128

MLA_CQ = 0
MLA_CKV = MLA_CQ + MLA_Q_LORA
MLA_KPE = MLA_CKV + MLA_KV_LORA
MLA_KPER = MLA_KPE + LANES
MLA_IN = MLA_KPER + LANES

TM_INPROJ = 512
TQ_ATTN = 512
TK_ATTN = 512
HP_ATTN = 4
RWKV_CHUNK = 64
RWKV_PREP_CHUNKS = 2
RWKV_SCAN_CHUNKS = 2
MLSTM_CHUNK = 256
TM_FFN = 512
TF_FFN = 256
VMEM_LIMIT = 56 * 1024 * 1024


def _cparams(sem):
    return pltpu.CompilerParams(dimension_semantics=sem, vmem_limit_bytes=VMEM_LIMIT)


def _resident(a, layer=None):
    if layer is None:
        nd = a.ndim
        return pl.BlockSpec(a.shape, lambda *_: (0,) * nd, pipeline_mode=pl.Buffered(1))
    nd = a.ndim - 1
    return pl.BlockSpec((None,) + a.shape[1:], lambda *_: (layer,) + (0,) * nd, pipeline_mode=pl.Buffered(1))


def _bdot(a, b):
    return jnp.dot(a.astype(BF16), b.astype(BF16), preferred_element_type=F32)


def _bdot_nt(a, b):
    return lax.dot_general(a.astype(BF16), b.astype(BF16), (((1,), (1,)), ((), ())),
                           preferred_element_type=F32)


def _bdot_tn(a, b):
    return lax.dot_general(a.astype(BF16), b.astype(BF16), (((0,), (0,)), ((), ())),
                           preferred_element_type=F32)


def _split3(x):
    h = x.astype(BF16)
    r1 = x - h.astype(F32)
    m = r1.astype(BF16)
    lo = (r1 - m.astype(F32)).astype(BF16)
    return h, m, lo


def _exact_left_dot(sel, x):
    h, m, lo = _split3(x)
    s = sel.astype(BF16)
    return (jnp.dot(s, h, preferred_element_type=F32) + jnp.dot(s, m, preferred_element_type=F32)
            + jnp.dot(s, lo, preferred_element_type=F32))


def _exact_right_dot(x, sel, parts=3):
    pieces = _split3(x)[:parts]
    s = sel.astype(BF16)
    out = jnp.dot(pieces[0], s, preferred_element_type=F32)
    for p in pieces[1:]:
        out = out + jnp.dot(p, s, preferred_element_type=F32)
    return out


def _rms(x, g):
    return x * lax.rsqrt(jnp.mean(x * x, axis=-1, keepdims=True) + NORM_EPS) * g


def _sigmoid(x):
    return 1.0 / (1.0 + jnp.exp(-x))


def _log_sigmoid(x):
    return jnp.minimum(x, 0.0) - jnp.log1p(jnp.exp(-jnp.abs(x)))


def _div(x, d):
    assert d & (d - 1) == 0
    return lax.shift_right_logical(x, d.bit_length() - 1)


def _mod(x, d):
    assert d & (d - 1) == 0
    return lax.bitwise_and(x, d - 1)


def _shift_rows(x, prev, s, chunk):
    n = x.shape[0]
    row = lax.broadcasted_iota(jnp.int32, x.shape, 0)
    return jnp.where(_mod(row, chunk) >= s, pltpu.roll(x, s, 0), pltpu.roll(prev, n - chunk + s, 0))


def _rope_kernel(pos_ref, invf_ref, cos_ref, sin_ref):
    ang = pos_ref[...].astype(F32) * invf_ref[...]
    cos_ref[...] = jnp.cos(ang)
    sin_ref[...] = jnp.sin(ang)


def _rope_tables(positions):
    t = positions.size
    tm = min(1024, t)
    inv_freq = ROPE_THETA ** (-jnp.arange(0, MLA_ROPE, 2, dtype=F32) / MLA_ROPE)
    invf = jnp.tile(inv_freq, LANES // (MLA_ROPE // 2))[None, :]
    return pl.pallas_call(
        _rope_kernel,
        out_shape=(jax.ShapeDtypeStruct((t, LANES), F32), jax.ShapeDtypeStruct((t, LANES), F32)),
        grid=(t // tm,),
        in_specs=[pl.BlockSpec((tm, 1), lambda i: (i, 0)), pl.BlockSpec((1, LANES), lambda i: (0, 0))],
        out_specs=(pl.BlockSpec((tm, LANES), lambda i: (i, 0)), pl.BlockSpec((tm, LANES), lambda i: (i, 0))),
        compiler_params=_cparams(("parallel",)),
        name="rope_tables",
    )(positions.reshape(t, 1), invf)


def _inproj_kernel(x_ref, g_ref, w_ref, cos_ref, sin_ref, qn_ref, kvn_ref, wq_ref, wkv_ref,
                   q_ref, k_ref, v_ref, rwkv_ref, mlstm_ref):
    hb = _rms(x_ref[...], g_ref[...]).astype(BF16)
    mla = jnp.dot(hb, w_ref[:, 0:MLA_IN], preferred_element_type=F32)
    rwkv_ref[...] = jnp.dot(hb, w_ref[:, MLA_IN:MLA_IN + RWKV_IN], preferred_element_type=F32)
    mlstm_ref[...] = jnp.dot(hb, w_ref[:, MLA_IN + RWKV_IN:], preferred_element_type=F32)

    cos = cos_ref[...]
    sin = sin_ref[...]
    scale = (MLA_NOPE + MLA_ROPE) ** -0.5 * math.log2(math.e)
    hw = MLA_HEADS * LANES
    cqn = _rms(mla[:, MLA_CQ:MLA_CKV], qn_ref[...]).astype(BF16)
    q = jnp.dot(cqn, wq_ref[...], preferred_element_type=F32)
    ckvn = _rms(mla[:, MLA_CKV:MLA_KPE], kvn_ref[...]).astype(BF16)
    kv = jnp.dot(ckvn, wkv_ref[...], preferred_element_type=F32)
    kp = (mla[:, MLA_KPE:MLA_KPER] * cos + mla[:, MLA_KPER:MLA_IN] * sin).astype(BF16)
    for h in range(MLA_HEADS):
        c0 = h * LANES
        pe = q[:, hw + c0:hw + c0 + LANES] * cos + q[:, 2 * hw + c0:2 * hw + c0 + LANES] * sin
        q_ref[:, h * MLA_QK:h * MLA_QK + LANES] = (q[:, c0:c0 + LANES] * scale).astype(BF16)
        q_ref[:, h * MLA_QK + LANES:(h + 1) * MLA_QK] = (pe * scale).astype(BF16)
        k_ref[:, h * MLA_QK:h * MLA_QK + LANES] = kv[:, c0:c0 + LANES].astype(BF16)
        k_ref[:, h * MLA_QK + LANES:(h + 1) * MLA_QK] = kp
    v_ref[...] = kv[:, hw:].astype(BF16)


def _inproj(l, x, g, w, cos, sin, qn, kvn, wq, wkv):
    t = x.shape[0]
    tm = TM_INPROJ
    row = lambda width: pl.BlockSpec((tm, width), lambda i: (i, 0))
    return pl.pallas_call(
        _inproj_kernel,
        out_shape=(jax.ShapeDtypeStruct((t, MLA_HEADS * MLA_QK), BF16),
                   jax.ShapeDtypeStruct((t, MLA_HEADS * MLA_QK), BF16),
                   jax.ShapeDtypeStruct((t, MLA_WIDTH), BF16),
                   jax.ShapeDtypeStruct((t, RWKV_IN), F32),
                   jax.ShapeDtypeStruct((t, MLSTM_IN), F32)),
        grid=(t // tm,),
        in_specs=[row(D_MODEL), _resident(g, l), _resident(w, l), row(LANES), row(LANES), _resident(qn, l),
                  _resident(kvn, l), _resident(wq, l), _resident(wkv, l)],
        out_specs=(row(MLA_HEADS * MLA_QK), row(MLA_HEADS * MLA_QK), row(MLA_WIDTH), row(RWKV_IN),
                   row(MLSTM_IN)),
        compiler_params=_cparams(("parallel",)),
        name="inproj",
    )(x, g, w, cos, sin, qn, kvn, wq, wkv)


def _attn_kernel(q_ref, k_ref, v_ref, g_ref, o_ref, m_sc, l_sc, acc_sc, *, tq, tk, hp):
    i = pl.program_id(2)
    m_sc[...] = jnp.full(m_sc.shape, -jnp.inf, F32)
    l_sc[...] = jnp.zeros(l_sc.shape, F32)
    acc_sc[...] = jnp.zeros(acc_sc.shape, F32)

    def step(j, masked):
        off = pl.multiple_of(j * tk, tk)
        hs = range(hp)
        s = [lax.dot_general(q_ref[0, :, h * MLA_QK:(h + 1) * MLA_QK],
                             k_ref[0, pl.ds(off, tk), h * MLA_QK:(h + 1) * MLA_QK],
                             (((1,), (1,)), ((), ())), preferred_element_type=F32) for h in hs]
        if masked:
            rows = lax.broadcasted_iota(jnp.int32, (tq, tk), 0)
            cols = lax.broadcasted_iota(jnp.int32, (tq, tk), 1)
            s = [jnp.where(cols <= rows, s[h], -jnp.inf) for h in hs]
        m_old = [m_sc[h] for h in hs]
        m_new = [jnp.maximum(m_old[h], jnp.max(s[h], axis=-1, keepdims=True)) for h in hs]
        p = [jnp.exp2(s[h] - jnp.tile(m_new[h], (1, tk // LANES))) for h in hs]
        alpha = [jnp.exp2(m_old[h] - m_new[h]) for h in hs]
        for h in hs:
            l_sc[h] = alpha[h] * l_sc[h] + jnp.sum(p[h], axis=-1, keepdims=True)
            m_sc[h] = m_new[h]
        pv = [jnp.dot(p[h].astype(BF16), v_ref[0, pl.ds(off, tk), h * MLA_VDIM:(h + 1) * MLA_VDIM],
                      preferred_element_type=F32) for h in hs]
        for h in hs:
            acc_sc[h] = alpha[h] * acc_sc[h] + pv[h]

    def body(j, c):
        step(j, False)
        return c

    lax.fori_loop(0, i, body, 0)
    step(i, True)
    for h in range(hp):
        o = acc_sc[h] / l_sc[h]
        o_ref[0, :, h * MLA_VDIM:(h + 1) * MLA_VDIM] = _rms(
            o, g_ref[:, h * MLA_VDIM:(h + 1) * MLA_VDIM]).astype(o_ref.dtype)


def _mla_attention(l, q, k, v, g, batch, seq):
    tq, tk, hp = TQ_ATTN, TK_ATTN, HP_ATTN
    assert tq == tk and MLA_VDIM == LANES
    q = q.reshape(batch, seq, MLA_HEADS * MLA_QK)
    k = k.reshape(batch, seq, MLA_HEADS * MLA_QK)
    v = v.reshape(batch, seq, MLA_WIDTH)
    out = pl.pallas_call(
        functools.partial(_attn_kernel, tq=tq, tk=tk, hp=hp),
        out_shape=jax.ShapeDtypeStruct((batch, seq, MLA_WIDTH), BF16),
        grid=(batch, MLA_HEADS // hp, seq // tq),
        in_specs=[pl.BlockSpec((1, tq, hp * MLA_QK), lambda b, h, i: (b, i, h)),
                  pl.BlockSpec((1, seq, hp * MLA_QK), lambda b, h, i: (b, 0, h)),
                  pl.BlockSpec((1, seq, hp * MLA_VDIM), lambda b, h, i: (b, 0, h)),
                  pl.BlockSpec((None, 1, hp * MLA_VDIM), lambda b, h, i: (l, 0, h))],
        out_specs=pl.BlockSpec((1, tq, hp * MLA_VDIM), lambda b, h, i: (b, i, h)),
        scratch_shapes=[pltpu.VMEM((hp, tq, LANES), F32), pltpu.VMEM((hp, tq, LANES), F32),
                        pltpu.VMEM((hp, tq, MLA_VDIM), F32)],
        compiler_params=_cparams(("parallel", "parallel", "arbitrary")),
        name="mla_attention",
    )(q, k, v, g)
    return out.reshape(batch * seq, MLA_WIDTH)


(SEC_TA, SEC_RT, SEC_UV, SEC_YV, SEC_ARB, SEC_KH, SEC_BH, SEC_V, SEC_BONUS, SEC_GATE) = range(10)
RWKV_SECTIONS = 10


def _rwkv_masks(chunk):
    w = RWKV_WIDTH
    r = lax.broadcasted_iota(jnp.int32, (w, w), 0)
    c = lax.broadcasted_iota(jnp.int32, (w, w), 1)
    return r, c, _div(r, chunk) == _div(c, chunk)


def _tile_heads(z):
    return jnp.concatenate([z] * RWKV_HEADS, axis=0)


def _fold_heads(z, chunk):
    out = z[0:chunk]
    for h in range(1, RWKV_HEADS):
        out = out + z[h * chunk:(h + 1) * chunk]
    return out


def _rwkv_prep_kernel(x_ref, mu_ref, w0_ref, a0_ref, kk_ref, ka_ref, rk_ref, w2_ref, a2_ref, g2_ref,
                      p_ref, pc_ref, prev_sc, *, nb, chunk, cps):
    seg = cps * chunk
    n = nb * seg
    w = RWKV_WIDTH
    hd = RWKV_HEAD

    @pl.when(pl.program_id(0) == 0)
    def _():
        prev_sc[...] = jnp.zeros(prev_sc.shape, F32)

    x = x_ref[...].reshape(n, RWKV_IN)
    shifted = _shift_rows(x, prev_sc[...], 1, seg)
    prev_sc[...] = x
    xs = x + (shifted - x) * mu_ref[...]
    r = xs[:, 0:w]
    k = xs[:, w:2 * w]
    v = xs[:, 2 * w:3 * w]
    lor = xs[:, 3 * w:]
    ld = -math.exp(-0.5) * _sigmoid(w0_ref[...] + _bdot(jnp.tanh(lor), w2_ref[...]))
    a = _sigmoid(a0_ref[...] + _bdot(lor, a2_ref[...]))
    g = _bdot(_sigmoid(lor), g2_ref[...])
    ri, ci, same_chunk = _rwkv_masks(chunk)
    head_ones = jnp.where(_div(ri, hd) == _div(ci, hd), 1.0, 0.0).astype(BF16)
    kk = k * kk_ref[...]
    kk = kk / jnp.maximum(jnp.sqrt(_exact_right_dot(kk * kk, head_ones, parts=2)), 1e-12)
    k2 = k * (1.0 + (a - 1.0) * ka_ref[...])
    kb = kk * a
    bonus = _exact_right_dot(r * k2 * rk_ref[...], head_ones, parts=2) * v

    tri = jnp.where(same_chunk & (ci <= ri), 1.0, 0.0)
    cl = jnp.concatenate([_exact_left_dot(tri, ld[i * w:(i + 1) * w]) for i in range(n // w)], axis=0)
    units = nb * cps
    cl_last = jnp.concatenate(
        [jnp.broadcast_to(cl[(u + 1) * chunk - 1:(u + 1) * chunk, :], (chunk, w)) for u in range(units)], axis=0)
    e_neg = jnp.exp(-cl)
    e_end = jnp.exp(cl_last - cl)
    a_t = kk * jnp.exp(cl - ld)
    k_t = k2 * e_neg
    b_t = kb * e_neg
    r_t = r * jnp.exp(cl)
    k_h = k2 * e_end
    b_h = kb * e_end

    bd = _div(ri, chunk) == _div(ci, hd)
    rt = lax.broadcasted_iota(jnp.int32, (chunk, w), 0)
    cs = _mod(lax.broadcasted_iota(jnp.int32, (chunk, w), 1), chunk)
    strict = cs < rt
    incl = cs <= rt
    c16 = _div(rt, 16) == _div(cs, 16)
    c32 = _div(rt, 32) == _div(cs, 32)
    eye = jnp.where(rt == cs, 1.0, 0.0)

    def block_diag(z):
        return jnp.where(same_chunk, _tile_heads(z), 0.0)

    def put(u, sec, val):
        b, j = divmod(u, cps)
        p_ref[b, j * chunk:(j + 1) * chunk, sec * w:(sec + 1) * w] = val.astype(p_ref.dtype)

    def mm(x, y):
        return jnp.dot(x, y, preferred_element_type=F32)

    us = range(units)
    sls = [slice(u * chunk, (u + 1) * chunk) for u in us]
    a16, r16, v16, k16, b16 = (z.astype(BF16) for z in (a_t, r_t, v, k_t, b_t))
    a_st = [jnp.where(bd, _tile_heads(a16[sl]), 0.0) for sl in sls]
    v_st = [jnp.where(bd, _tile_heads(v16[sl]), 0.0) for sl in sls]
    sc = [lax.dot_general(
        jnp.concatenate([a16[sl], r16[sl]], axis=0),
        jnp.concatenate([jnp.where(bd, _tile_heads(k16[sl]), 0.0), jnp.where(bd, _tile_heads(b16[sl]), 0.0)], axis=0),
        (((1,), (1,)), ((), ())), preferred_element_type=F32) for sl in sls]
    l_ab = [jnp.where(strict, sc[u][0:chunk, w:], 0.0) for u in us]
    l_ak = [jnp.where(strict, sc[u][0:chunk, 0:w], 0.0).astype(BF16) for u in us]
    a_rk = [jnp.where(incl, sc[u][chunk:, 0:w], 0.0).astype(BF16) for u in us]
    for u in us:
        put(u, SEC_ARB, jnp.where(incl, sc[u][chunk:, w:], 0.0))
    xm = [-jnp.where(c16, l_ab[u], 0.0) for u in us]
    xm16 = [z.astype(BF16) for z in xm]
    off32 = [block_diag(jnp.where(c32 & jnp.logical_not(c16), l_ab[u], 0.0).astype(BF16)) for u in us]
    off64 = [block_diag(jnp.where(jnp.logical_not(c32), l_ab[u], 0.0).astype(BF16)) for u in us]
    x2 = [mm(xm16[u], block_diag(xm16[u])).astype(BF16) for u in us]
    x2_bd = [block_diag(z) for z in x2]
    wv = [block_diag(mm(l_ak[u], v_st[u]).astype(BF16)) for u in us]
    t_lo = [eye + xm[u] for u in us]
    t_lo = [(t_lo[u] + mm(t_lo[u].astype(BF16), x2_bd[u])).astype(BF16) for u in us]
    x4 = [mm(x2[u], x2_bd[u]) for u in us]
    x4b = [z.astype(BF16) for z in x4]
    x4_bd = [block_diag(z) for z in x4b]
    for u in us:
        put(u, SEC_YV, mm(a_rk[u], v_st[u]))
    x8_bd = [block_diag(mm(x4b[u], x4_bd[u]).astype(BF16)) for u in us]
    t_hi = [eye + x4[u] for u in us]
    t_hi = [block_diag((t_hi[u] + mm(t_hi[u].astype(BF16), x8_bd[u])).astype(BF16)) for u in us]
    t_inv = [mm(t_lo[u], t_hi[u]) for u in us]
    for off in (off32, off64):
        tb = [z.astype(BF16) for z in t_inv]
        mid = [mm(tb[u], off[u]).astype(BF16) for u in us]
        t_inv = [t_inv[u] - mm(mid[u], block_diag(tb[u])) for u in us]
    tb = [z.astype(BF16) for z in t_inv]
    for u in us:
        put(u, SEC_TA, mm(tb[u], a_st[u]))
    for u in us:
        put(u, SEC_UV, mm(tb[u], wv[u]))
    for u in us:
        b, j = divmod(u, cps)
        put(u, SEC_RT, r16[sls[u]])
        put(u, SEC_KH, k_h[sls[u]])
        put(u, SEC_BH, b_h[sls[u]])
        put(u, SEC_V, v16[sls[u]])
        put(u, SEC_BONUS, bonus[sls[u]])
        put(u, SEC_GATE, g[sls[u]])
        pc_ref[j, b:b + 1, :] = jnp.exp(cl_last[sls[u]][0:1, :])


def _rwkv_scan_kernel(p_ref, pc_ref, lnw_ref, lnb_ref, o_ref, state_sc, *, nb, chunk, cps):
    seg = cps * chunk
    n = nb * seg
    w = RWKV_WIDTH
    hd = RWKV_HEAD

    @pl.when(pl.program_id(0) == 0)
    def _():
        state_sc[...] = jnp.zeros(state_sc.shape, F32)

    ri, ci, _ = _rwkv_masks(chunk)
    bd = _div(ri, chunk) == _div(ci, hd)
    bdv = _div(ri, hd) == _div(ci, hd)
    ones_bd = jnp.where(bdv, 1.0, 0.0).astype(BF16)

    def sec(b, j, s):
        return p_ref[b, j * chunk:(j + 1) * chunk, s * w:(s + 1) * w]

    ys = [[None] * cps for _ in range(nb)]
    bs = range(nb)
    for j in range(cps):
        gs = [state_sc[b] for b in bs]
        p1 = [lax.dot_general(jnp.concatenate([sec(b, j, SEC_TA), sec(b, j, SEC_RT)], axis=0),
                              gs[b].astype(BF16), (((1,), (1,)), ((), ())), preferred_element_type=F32)
              for b in bs]
        u = [(p1[b][0:chunk] + sec(b, j, SEC_UV).astype(F32)).astype(BF16) for b in bs]
        upd = [lax.dot_general(jnp.concatenate([sec(b, j, SEC_V), -u[b]], axis=0),
                               jnp.concatenate([sec(b, j, SEC_KH), sec(b, j, SEC_BH)], axis=0),
                               (((0,), (0,)), ((), ())), preferred_element_type=F32) for b in bs]
        for b in bs:
            state_sc[b] = gs[b] * pc_ref[j, b:b + 1, :] + jnp.where(bdv, upd[b], 0.0)
        for b in bs:
            u_st = jnp.where(bd, _tile_heads(u[b]), 0.0)
            ys[b][j] = (p1[b][chunk:] + sec(b, j, SEC_YV).astype(F32)
                        - jnp.dot(sec(b, j, SEC_ARB), u_st, preferred_element_type=F32))

    y = jnp.concatenate([ys[b][j] for b in range(nb) for j in range(cps)], axis=0)
    mean = _exact_right_dot(y, ones_bd) * (1.0 / hd)
    d = y - mean
    var = _exact_right_dot(d * d, ones_bd) * (1.0 / hd)
    yn = d * lax.rsqrt(var + RWKV_LN_EPS) * lnw_ref[...] + lnb_ref[...]
    bonus = p_ref[:, :, SEC_BONUS * w:(SEC_BONUS + 1) * w].reshape(n, w).astype(F32)
    gate = p_ref[:, :, SEC_GATE * w:(SEC_GATE + 1) * w].reshape(n, w).astype(F32)
    o_ref[...] = ((yn + bonus) * gate).astype(o_ref.dtype).reshape(o_ref.shape)


def _rwkv(l, x, batch, seq, mu, w0, a0, k_k, k_a, r_k, ln_w, ln_b, w2p, a2p, g2p):
    chunk = RWKV_CHUNK
    assert RWKV_HEADS * chunk == RWKV_WIDTH
    x = x.reshape(batch, seq, RWKV_IN)
    pw = RWKV_SECTIONS * RWKV_WIDTH
    cps = RWKV_PREP_CHUNKS
    seg = cps * chunk
    packed, pc = pl.pallas_call(
        functools.partial(_rwkv_prep_kernel, nb=batch, chunk=chunk, cps=cps),
        out_shape=(jax.ShapeDtypeStruct((batch, seq, pw), BF16),
                   jax.ShapeDtypeStruct((seq // chunk, batch, RWKV_WIDTH), F32)),
        grid=(seq // seg,),
        in_specs=[pl.BlockSpec((batch, seg, RWKV_IN), lambda c: (0, c, 0)),
                  _resident(mu, l), _resident(w0, l), _resident(a0, l), _resident(k_k, l), _resident(k_a, l),
                  _resident(r_k, l), _resident(w2p, l), _resident(a2p, l), _resident(g2p, l)],
        out_specs=(pl.BlockSpec((batch, seg, pw), lambda c: (0, c, 0)),
                   pl.BlockSpec((cps, batch, RWKV_WIDTH), lambda c: (c, 0, 0))),
        scratch_shapes=[pltpu.VMEM((batch * seg, RWKV_IN), F32)],
        compiler_params=_cparams(("arbitrary",)),
        name="rwkv7_prep",
    )(x, mu, w0, a0, k_k, k_a, r_k, w2p, a2p, g2p)
    cps = RWKV_SCAN_CHUNKS
    seg = cps * chunk
    out = pl.pallas_call(
        functools.partial(_rwkv_scan_kernel, nb=batch, chunk=chunk, cps=cps),
        out_shape=jax.ShapeDtypeStruct((batch, seq, RWKV_WIDTH), BF16),
        grid=(seq // seg,),
        in_specs=[pl.BlockSpec((batch, seg, pw), lambda c: (0, c, 0)),
                  pl.BlockSpec((cps, batch, RWKV_WIDTH), lambda c: (c, 0, 0)),
                  _resident(ln_w, l), _resident(ln_b, l)],
        out_specs=pl.BlockSpec((batch, seg, RWKV_WIDTH), lambda c: (0, c, 0)),
        scratch_shapes=[pltpu.VMEM((batch, RWKV_WIDTH, RWKV_WIDTH), F32)],
        compiler_params=_cparams(("arbitrary",)),
        name="rwkv7_scan",
    )(packed, pc, ln_w, ln_b)
    return out.reshape(batch * seq, RWKV_WIDTH)


ML_QK = 0
ML_V = 2 * MLSTM_HEADS * MLSTM_QK
ML_O = ML_V + MLSTM_WIDTH
ML_I = ML_O + MLSTM_WIDTH
ML_F = ML_I + LANES


def _cummax_rows(x):
    n = x.shape[0]
    row = lax.broadcasted_iota(jnp.int32, x.shape, 0)
    sh = 1
    while sh < n:
        x = jnp.maximum(x, jnp.where(row >= sh, pltpu.roll(x, sh, 0), -jnp.inf))
        sh *= 2
    return x


def _mlstm_kernel(x_ref, cw_ref, cb_ref, ib_ref, fb_ref, on_ref, o_ref,
                  prev_sc, c_sc, n_sc, m_sc, *, nb, chunk):
    n = nb * chunk
    nh = MLSTM_HEADS
    dk = MLSTM_QK
    dv = MLSTM_V
    qkw = nh * dk
    vw = MLSTM_WIDTH

    @pl.when(pl.program_id(0) == 0)
    def _():
        prev_sc[...] = jnp.zeros(prev_sc.shape, F32)
        c_sc[...] = jnp.zeros(c_sc.shape, F32)
        n_sc[...] = jnp.zeros(n_sc.shape, F32)
        m_sc[...] = jnp.zeros(m_sc.shape, F32)

    x = x_ref[...].reshape(n, MLSTM_IN)
    qk_raw = x[:, ML_QK:ML_V]
    prev = prev_sc[...]
    conv = cb_ref[...] + qk_raw * cw_ref[MLSTM_CONV - 1:MLSTM_CONV, :]
    for s in range(1, MLSTM_CONV):
        conv = conv + _shift_rows(qk_raw, prev, s, chunk) * cw_ref[MLSTM_CONV - 1 - s:MLSTM_CONV - s, :]
    prev_sc[...] = qk_raw
    qk = conv * _sigmoid(conv)
    q_all = qk[:, 0:qkw] * (dk ** -0.5)
    k_all = qk[:, qkw:]
    v_all = x[:, ML_V:ML_O]
    o_pre = x[:, ML_O:ML_I]
    li_all = x[:, ML_I:ML_F] + ib_ref[...]
    lf_all = _log_sigmoid(x[:, ML_F:ML_F + LANES] + fb_ref[...])

    ri = lax.broadcasted_iota(jnp.int32, (chunk, chunk), 0)
    ci = lax.broadcasted_iota(jnp.int32, (chunk, chunk), 1)
    causal = ci <= ri
    tri = jnp.where(causal, 1.0, 0.0)
    lane_k = lax.broadcasted_iota(jnp.int32, (chunk, qkw), 1)
    lane_v = lax.broadcasted_iota(jnp.int32, (chunk, vw), 1)
    rc = lax.broadcasted_iota(jnp.int32, (qkw, vw), 0)
    cc = lax.broadcasted_iota(jnp.int32, (qkw, vw), 1)
    cmask = _div(rc, dk) == _div(cc, dv)
    expand_v = jnp.where(rc == _div(cc, dv), 1.0, 0.0).astype(BF16)
    rk = lax.broadcasted_iota(jnp.int32, (qkw, qkw), 0)
    ck = lax.broadcasted_iota(jnp.int32, (qkw, qkw), 1)
    expand_k = jnp.where(rk == _div(ck, dk), 1.0, 0.0).astype(BF16)
    gather_k = jnp.where(_div(rk, dk) == ck, 1.0, 0.0).astype(BF16)

    bs = range(nb)
    sls = [slice(b * chunk, (b + 1) * chunk) for b in bs]
    q = [q_all[sl] for sl in sls]
    k = [k_all[sl] for sl in sls]
    k16 = [z.astype(BF16) for z in k]
    v = [v_all[sl] for sl in sls]
    li = [li_all[sl] for sl in sls]
    c_old = [c_sc[b] for b in bs]
    n_old = [n_sc[b] for b in bs]
    m_prev = [m_sc[b] for b in bs]
    g = [_exact_left_dot(tri, lf_all[sl]) for sl in sls]
    lig = [li[b] - g[b] for b in bs]
    inter_log = [g[b] + m_prev[b] for b in bs]
    m_t = [jnp.maximum(inter_log[b], g[b] + _cummax_rows(lig[b])) for b in bs]
    inter_w = [jnp.exp(inter_log[b] - m_t[b]) for b in bs]
    log2e = math.log2(math.e)
    gm = [(g[b] - m_t[b]) * log2e for b in bs]
    lig_t = [jnp.transpose(z * log2e) for z in lig]
    qn = [_exact_right_dot(q[b] * n_old[b], gather_k, parts=2) for b in bs]
    q_c = [_bdot(q[b], c_old[b]) for b in bs]
    ssum = [jnp.zeros((chunk, LANES), F32) for _ in bs]
    num = [jnp.zeros((chunk, vw), F32) for _ in bs]
    for h in range(nh):
        mk = (lane_k >= h * dk) & (lane_k < (h + 1) * dk)
        mv = (lane_v >= h * dv) & (lane_v < (h + 1) * dv)
        qk_h = [lax.dot_general(jnp.where(mk, q[b], 0.0).astype(BF16), k16[b], (((1,), (1,)), ((), ())),
                                preferred_element_type=F32) for b in bs]
        d = [jnp.broadcast_to(gm[b][:, h:h + 1], (chunk, chunk)) + lig_t[b][h:h + 1, :] for b in bs]
        s = [qk_h[b] * jnp.exp2(jnp.where(causal, d[b], -jnp.inf)) for b in bs]
        ssum = [jnp.where(lane_k == h, jnp.sum(s[b], axis=-1, keepdims=True), ssum[b]) for b in bs]
        num = [num[b] + _bdot(s[b], jnp.where(mv, v[b], 0.0)) for b in bs]
    den = [inter_w[b] * qn[b] + ssum[b] for b in bs]
    rden = [1.0 / jnp.maximum(jnp.abs(den[b]), jnp.exp(-m_t[b])) for b in bs]
    g_last = [g[b][chunk - 1:chunk, :] for b in bs]
    a_all = [g_last[b] - g[b] + li[b] for b in bs]
    m_new = [jnp.maximum(g_last[b] + m_prev[b], jnp.max(a_all[b], axis=0, keepdims=True)) for b in bs]
    dec = [jnp.exp(g_last[b] + m_prev[b] - m_new[b]) for b in bs]
    wts = [jnp.exp(a_all[b] - m_new[b]) for b in bs]
    per_head = [jnp.concatenate([inter_w[b], rden[b], wts[b], jnp.broadcast_to(dec[b], (8, LANES))], axis=0)
                for b in bs]
    on_v = [_exact_right_dot(per_head[b], expand_v, parts=2) for b in bs]
    on_k = [_exact_right_dot(per_head[b][2 * chunk:], expand_k, parts=2) for b in bs]
    hs = [(on_v[b][0:chunk] * q_c[b] + num[b]) * on_v[b][chunk:2 * chunk] for b in bs]
    for b in bs:
        c_sc[b] = c_old[b] * on_v[b][3 * chunk:3 * chunk + 1] + jnp.where(
            cmask, _bdot_tn(k16[b], on_v[b][2 * chunk:3 * chunk] * v[b]), 0.0)
        n_sc[b] = n_old[b] * on_k[b][chunk:chunk + 1] + jnp.sum(on_k[b][0:chunk] * k[b], axis=0, keepdims=True)
        m_sc[b] = m_new[b]

    hh = jnp.concatenate(hs, axis=0)
    rv = lax.broadcasted_iota(jnp.int32, (vw, vw), 0)
    cv = lax.broadcasted_iota(jnp.int32, (vw, vw), 1)
    head_ones = jnp.where(_div(rv, dv) == _div(cv, dv), 1.0, 0.0).astype(BF16)
    ms = _exact_right_dot(hh * hh, head_ones, parts=2) * (1.0 / dv)
    out = hh * lax.rsqrt(ms + NORM_EPS) * on_ref[...] * _sigmoid(o_pre)
    o_ref[...] = out.astype(o_ref.dtype).reshape(o_ref.shape)


def _mlstm(l, x, batch, seq, cw, cb, ib, fb, on):
    chunk = MLSTM_CHUNK
    x = x.reshape(batch, seq, MLSTM_IN)
    out = pl.pallas_call(
        functools.partial(_mlstm_kernel, nb=batch, chunk=chunk),
        out_shape=jax.ShapeDtypeStruct((batch, seq, MLSTM_WIDTH), BF16),
        grid=(seq // chunk,),
        in_specs=[pl.BlockSpec((batch, chunk, MLSTM_IN), lambda c: (0, c, 0)),
                  _resident(cw, l), _resident(cb, l), _resident(ib, l), _resident(fb, l), _resident(on, l)],
        out_specs=pl.BlockSpec((batch, chunk, MLSTM_WIDTH), lambda c: (0, c, 0)),
        scratch_shapes=[pltpu.VMEM((batch * chunk, 2 * MLSTM_HEADS * MLSTM_QK), F32),
                        pltpu.VMEM((batch, MLSTM_HEADS * MLSTM_QK, MLSTM_WIDTH), F32),
                        pltpu.VMEM((batch, 1, MLSTM_HEADS * MLSTM_QK), F32),
                        pltpu.VMEM((batch, 1, LANES), F32)],
        compiler_params=_cparams(("arbitrary",)),
        name="mlstm",
    )(x, cw, cb, ib, fb, on)
    return out.reshape(batch * seq, MLSTM_WIDTH)


def _ffn_kernel(x_ref, ya_ref, yb_ref, yc_ref, wo_ref, g_ref, wg_ref, wu_ref, wd_ref, fg_ref,
                o_ref, act_sc, *, final_norm, tf):
    y = jnp.concatenate([ya_ref[...], yb_ref[...], yc_ref[...]], axis=-1)
    x1 = x_ref[...] + jnp.dot(y, wo_ref[...], preferred_element_type=F32)
    h = _rms(x1, g_ref[...]).astype(BF16)
    for c in range(D_FF // tf):
        gate = jnp.dot(h, wg_ref[:, c * tf:(c + 1) * tf], preferred_element_type=F32)
        up = jnp.dot(h, wu_ref[:, c * tf:(c + 1) * tf], preferred_element_type=F32)
        act_sc[:, c * tf:(c + 1) * tf] = (gate * _sigmoid(gate) * up).astype(BF16)
    out = x1 + jnp.dot(act_sc[...], wd_ref[...], preferred_element_type=F32)
    if final_norm:
        out = _rms(out, fg_ref[...])
    o_ref[...] = out


def _out_ffn(l, x, ya, yb, yc, wo, g, wg, wu, wd, fg, final_norm):
    t = x.shape[0]
    tm, tf = TM_FFN, TF_FFN
    row = lambda w: pl.BlockSpec((tm, w), lambda i: (i, 0))
    return pl.pallas_call(
        functools.partial(_ffn_kernel, final_norm=final_norm, tf=tf),
        out_shape=jax.ShapeDtypeStruct((t, D_MODEL), F32),
        grid=(t // tm,),
        in_specs=[row(D_MODEL), row(MLA_WIDTH), row(RWKV_WIDTH), row(MLSTM_WIDTH), _resident(wo, l),
                  _resident(g, l), _resident(wg, l), _resident(wu, l), _resident(wd, l), _resident(fg)],
        out_specs=row(D_MODEL),
        scratch_shapes=[pltpu.VMEM((tm, D_FF), BF16)],
        compiler_params=_cparams(("parallel",)),
        name="out_ffn",
    )(x, ya, yb, yc, wo, g, wg, wu, wd, fg)


def _pad_cols(w, width):
    return jnp.pad(w, [(0, 0)] * (w.ndim - 1) + [(0, width - w.shape[-1])])


def _rot_half_cols(w):
    half = w.shape[-1] // 2
    return jnp.concatenate([-w[..., half:], w[..., :half]], axis=-1)


def _stacked_weights(w_in, mla_w_uq, mla_w_ukv, rwkv_w2, rwkv_a2, rwkv_g2):
    depth = w_in.shape[0]
    c_q, c_kv, k_pe = w_in[..., 0:256], w_in[..., 256:512], w_in[..., 512:576]
    rw = w_in[..., 576:1472]
    ml = w_in[..., 1472:2248]
    w_mla = jnp.concatenate([c_q, c_kv, _pad_cols(k_pe, LANES), _pad_cols(_rot_half_cols(k_pe), LANES)], axis=-1)
    w_mlstm = jnp.concatenate([ml[..., 0:256], ml[..., 256:512], ml[..., 520:776],
                               _pad_cols(ml[..., 512:516], LANES), _pad_cols(ml[..., 516:520], LANES)], axis=-1)
    w_all = jnp.concatenate([w_mla, rw, w_mlstm], axis=-1).astype(BF16)

    uq = mla_w_uq.reshape(depth, MLA_Q_LORA, MLA_HEADS, MLA_NOPE + MLA_ROPE)
    nope = uq[..., :MLA_NOPE].reshape(depth, MLA_Q_LORA, MLA_HEADS * MLA_NOPE)
    pe = _pad_cols(uq[..., MLA_NOPE:], LANES).reshape(depth, MLA_Q_LORA, MLA_HEADS * LANES)
    per = _pad_cols(_rot_half_cols(uq[..., MLA_NOPE:]), LANES).reshape(depth, MLA_Q_LORA, MLA_HEADS * LANES)
    wq = jnp.concatenate([nope, pe, per], axis=-1).astype(BF16)
    ukv = mla_w_ukv.reshape(depth, MLA_KV_LORA, MLA_HEADS, MLA_NOPE + MLA_VDIM)
    wkv = jnp.concatenate([ukv[..., :MLA_NOPE].reshape(depth, MLA_KV_LORA, -1),
                           ukv[..., MLA_NOPE:].reshape(depth, MLA_KV_LORA, -1)], axis=-1).astype(BF16)

    rows = lambda before, wt: jnp.pad(wt, ((0, 0), (before, LANES - before - wt.shape[1]), (0, 0))).astype(BF16)
    w2p = rows(0, rwkv_w2)
    a2p = rows(RWKV_DECAY_LORA, rwkv_a2)
    g2p = rows(RWKV_DECAY_LORA + RWKV_AAA_LORA, rwkv_g2)
    return w_all, wq, wkv, w2p, a2p, g2p


def kernel(x, positions, mix_norm, w_in, mla_q_norm, mla_w_uq, mla_kv_norm, mla_w_ukv, mla_out_norm, rwkv_mu, rwkv_w0, rwkv_w2, rwkv_a0, rwkv_a2, rwkv_g2, rwkv_k_k, rwkv_k_a, rwkv_r_k, rwkv_ln_w, rwkv_ln_b, mlstm_conv_w, mlstm_conv_b, mlstm_i_bias, mlstm_f_bias, mlstm_out_norm, w_out, ffn_norm, w_gate, w_up, w_down, final_norm):
    batch, seq, _ = x.shape
    depth = w_in.shape[0]
    xt = x.reshape(batch * seq, D_MODEL)
    cos, sin = _rope_tables(positions)
    rows = lambda a: a.reshape(a.shape[0], 1, a.shape[1])
    w_all, wq, wkv, w2p, a2p, g2p = _stacked_weights(w_in, mla_w_uq, mla_w_ukv, rwkv_w2, rwkv_a2, rwkv_g2)
    wo, wg, wu, wd = (w.astype(BF16) for w in (w_out, w_gate, w_up, w_down))
    ib = rows(_pad_cols(mlstm_i_bias, LANES))
    fb = rows(_pad_cols(mlstm_f_bias, LANES))
    for l in range(depth):
        q, k, v, rwkv_in, mlstm_in = _inproj(l, xt, rows(mix_norm), w_all, cos, sin, rows(mla_q_norm),
                                             rows(mla_kv_norm), wq, wkv)
        y_mla = _mla_attention(l, q, k, v, rows(mla_out_norm), batch, seq)
        y_rwkv = _rwkv(l, rwkv_in, batch, seq, rows(rwkv_mu), rows(rwkv_w0), rows(rwkv_a0), rows(rwkv_k_k),
                       rows(rwkv_k_a), rows(rwkv_r_k), rows(rwkv_ln_w), rows(rwkv_ln_b), w2p, a2p, g2p)
        y_mlstm = _mlstm(l, mlstm_in, batch, seq, mlstm_conv_w, rows(mlstm_conv_b), ib, fb, rows(mlstm_out_norm))
        xt = _out_ffn(l, xt, y_mla, y_rwkv, y_mlstm, wo, rows(ffn_norm), wg, wu, wd,
                      final_norm.reshape(1, -1), final_norm=(l == depth - 1))
    return xt.reshape(batch, seq, D_MODEL)
```

```python
import functools
import math

import jax
import jax.numpy as jnp
from jax import lax
from jax.experimental import pallas as pl
from jax.experimental.pallas import tpu as pltpu

F32 = jnp.float32
BF16 = jnp.bfloat16

D_MODEL = 1024
DEPTH = 2
MLA_HEADS = 4
MLA_NOPE = 128
MLA_ROPE = 64
MLA_VDIM = 128
MLA_Q_LORA = 256
MLA_KV_LORA = 256
MLA_WIDTH = MLA_HEADS * MLA_VDIM
MLA_QK = 256
ROPE_THETA = 10000.0
RWKV_HEADS = 4
RWKV_HEAD = 64
RWKV_WIDTH = 256
RWKV_DECAY_LORA = 32
RWKV_AAA_LORA = 32
RWKV_GATE_LORA = 64
RWKV_IN = 3 * RWKV_WIDTH + 128
RWKV_LN_EPS = 64e-5
MLSTM_HEADS = 4
MLSTM_QK = 32
MLSTM_V = 64
MLSTM_WIDTH = 256
MLSTM_CONV = 4
MLSTM_IN = 1024
D_FF = 2816
NORM_EPS = 1e-6
LANES = 128
SUBLANES = 8

MLA_CQ = 0
MLA_CKV = MLA_CQ + MLA_Q_LORA
MLA_KPE = MLA_CKV + MLA_KV_LORA
MLA_KPER = MLA_KPE + LANES
MLA_IN = MLA_KPER + LANES

TM_INPROJ = 512
TQ_ATTN = 512
TK_ATTN = 512
HP_ATTN = 4
RWKV_CHUNK = 64
RWKV_PREP_CHUNKS = 2
RWKV_SCAN_CHUNKS = 2
MLSTM_CHUNK = 256
TM_FFN = 512
TF_FFN = 256
VMEM_LIMIT = 56 * 1024 * 1024


def _cparams(sem):
    return pltpu.CompilerParams(dimension_semantics=sem, vmem_limit_bytes=VMEM_LIMIT)


def _resident(a, layer=None):
    if layer is None:
        nd = a.ndim
        return pl.BlockSpec(a.shape, lambda *_: (0,) * nd, pipeline_mode=pl.Buffered(1))
    nd = a.ndim - 1
    return pl.BlockSpec((None,) + a.shape[1:], lambda *_: (layer,) + (0,) * nd, pipeline_mode=pl.Buffered(1))


def _bdot(a, b):
    return jnp.dot(a.astype(BF16), b.astype(BF16), preferred_element_type=F32)


def _bdot_nt(a, b):
    return lax.dot_general(a.astype(BF16), b.astype(BF16), (((1,), (1,)), ((), ())),
                           preferred_element_type=F32)


def _bdot_tn(a, b):
    return lax.dot_general(a.astype(BF16), b.astype(BF16), (((0,), (0,)), ((), ())),
                           preferred_element_type=F32)


def _split3(x):
    h = x.astype(BF16)
    r1 = x - h.astype(F32)
    m = r1.astype(BF16)
    lo = (r1 - m.astype(F32)).astype(BF16)
    return h, m, lo


def _exact_left_dot(sel, x):
    h, m, lo = _split3(x)
    s = sel.astype(BF16)
    return (jnp.dot(s, h, preferred_element_type=F32) + jnp.dot(s, m, preferred_element_type=F32)
            + jnp.dot(s, lo, preferred_element_type=F32))


def _exact_right_dot(x, sel, parts=3):
    pieces = _split3(x)[:parts]
    s = sel.astype(BF16)
    out = jnp.dot(pieces[0], s, preferred_element_type=F32)
    for p in pieces[1:]:
        out = out + jnp.dot(p, s, preferred_element_type=F32)
    return out


def _rms(x, g):
    return x * lax.rsqrt(jnp.mean(x * x, axis=-1, keepdims=True) + NORM_EPS) * g


def _sigmoid(x):
    return 1.0 / (1.0 + jnp.exp(-x))


def _log_sigmoid(x):
    return jnp.minimum(x, 0.0) - jnp.log1p(jnp.exp(-jnp.abs(x)))


def _div(x, d):
    assert d & (d - 1) == 0
    return lax.shift_right_logical(x, d.bit_length() - 1)


def _mod(x, d):
    assert d & (d - 1) == 0
    return lax.bitwise_and(x, d - 1)


def _shift_rows(x, prev, s, chunk):
    n = x.shape[0]
    row = lax.broadcasted_iota(jnp.int32, x.shape, 0)
    return jnp.where(_mod(row, chunk) >= s, pltpu.roll(x, s, 0), pltpu.roll(prev, n - chunk + s, 0))


def _rope_kernel(pos_ref, invf_ref, cos_ref, sin_ref):
    ang = pos_ref[...].astype(F32) * invf_ref[...]
    cos_ref[...] = jnp.cos(ang)
    sin_ref[...] = jnp.sin(ang)


def _rope_tables(positions):
    t = positions.size
    tm = min(1024, t)
    inv_freq = ROPE_THETA ** (-jnp.arange(0, MLA_ROPE, 2, dtype=F32) / MLA_ROPE)
    invf = jnp.tile(inv_freq, LANES // (MLA_ROPE // 2))[None, :]
    return pl.pallas_call(
        _rope_kernel,
        out_shape=(jax.ShapeDtypeStruct((t, LANES), F32), jax.ShapeDtypeStruct((t, LANES), F32)),
        grid=(t // tm,),
        in_specs=[pl.BlockSpec((tm, 1), lambda i: (i, 0)), pl.BlockSpec((1, LANES), lambda i: (0, 0))],
        out_specs=(pl.BlockSpec((tm, LANES), lambda i: (i, 0)), pl.BlockSpec((tm, LANES), lambda i: (i, 0))),
        compiler_params=_cparams(("parallel",)),
        name="rope_tables",
    )(positions.reshape(t, 1), invf)


def _inproj_kernel(x_ref, g_ref, w_ref, cos_ref, sin_ref, qn_ref, kvn_ref, wq_ref, wkv_ref,
                   q_ref, k_ref, v_ref, rwkv_ref, mlstm_ref):
    hb = _rms(x_ref[...], g_ref[...]).astype(BF16)
    mla = jnp.dot(hb, w_ref[:, 0:MLA_IN], preferred_element_type=F32)
    rwkv_ref[...] = jnp.dot(hb, w_ref[:, MLA_IN:MLA_IN + RWKV_IN], preferred_element_type=F32)
    mlstm_ref[...] = jnp.dot(hb, w_ref[:, MLA_IN + RWKV_IN:], preferred_element_type=F32)

    cos = cos_ref[...]
    sin = sin_ref[...]
    scale = (MLA_NOPE + MLA_ROPE) ** -0.5 * math.log2(math.e)
    hw = MLA_HEADS * LANES
    cqn = _rms(mla[:, MLA_CQ:MLA_CKV], qn_ref[...]).astype(BF16)
    q = jnp.dot(cqn, wq_ref[...], preferred_element_type=F32)
    ckvn = _rms(mla[:, MLA_CKV:MLA_KPE], kvn_ref[...]).astype(BF16)
    kv = jnp.dot(ckvn, wkv_ref[...], preferred_element_type=F32)
    kp = (mla[:, MLA_KPE:MLA_KPER] * cos + mla[:, MLA_KPER:MLA_IN] * sin).astype(BF16)
    for h in range(MLA_HEADS):
        c0 = h * LANES
        pe = q[:, hw + c0:hw + c0 + LANES] * cos + q[:, 2 * hw + c0:2 * hw + c0 + LANES] * sin
        q_ref[:, h * MLA_QK:h * MLA_QK + LANES] = (q[:, c0:c0 + LANES] * scale).astype(BF16)
        q_ref[:, h * MLA_QK + LANES:(h + 1) * MLA_QK] = (pe * scale).astype(BF16)
        k_ref[:, h * MLA_QK:h * MLA_QK + LANES] = kv[:, c0:c0 + LANES].astype(BF16)
        k_ref[:, h * MLA_QK + LANES:(h + 1) * MLA_QK] = kp
    v_ref[...] = kv[:, hw:].astype(BF16)


def _inproj(l, x, g, w, cos, sin, qn, kvn, wq, wkv):
    t = x.shape[0]
    tm = TM_INPROJ
    row = lambda width: pl.BlockSpec((tm, width), lambda i: (i, 0))
    return pl.pallas_call(
        _inproj_kernel,
        out_shape=(jax.ShapeDtypeStruct((t, MLA_HEADS * MLA_QK), BF16),
                   jax.ShapeDtypeStruct((t, MLA_HEADS * MLA_QK), BF16),
                   jax.ShapeDtypeStruct((t, MLA_WIDTH), BF16),
                   jax.ShapeDtypeStruct((t, RWKV_IN), F32),
                   jax.ShapeDtypeStruct((t, MLSTM_IN), F32)),
        grid=(t // tm,),
        in_specs=[row(D_MODEL), _resident(g, l), _resident(w, l), row(LANES), row(LANES), _resident(qn, l),
                  _resident(kvn, l), _resident(wq, l), _resident(wkv, l)],
        out_specs=(row(MLA_HEADS * MLA_QK), row(MLA_HEADS * MLA_QK), row(MLA_WIDTH), row(RWKV_IN),
                   row(MLSTM_IN)),
        compiler_params=_cparams(("parallel",)),
        name="inproj",
    )(x, g, w, cos, sin, qn, kvn, wq, wkv)


def _attn_kernel(q_ref, k_ref, v_ref, g_ref, o_ref, m_sc, l_sc, acc_sc, sa_sc, sb_sc, *, tq, tk, hp):
    i = pl.program_id(2)
    m_sc[...] = jnp.full(m_sc.shape, -jnp.inf, F32)
    l_sc[...] = jnp.zeros(l_sc.shape, F32)
    acc_sc[...] = jnp.zeros(acc_sc.shape, F32)
    sub = m_sc.shape[1]
    hs = range(hp)

    def produce(j, s_ref):
        off = pl.multiple_of(j * tk, tk)
        for h in hs:
            s_ref[h] = lax.dot_general(k_ref[0, pl.ds(off, tk), h * MLA_QK:(h + 1) * MLA_QK],
                                       q_ref[0, :, h * MLA_QK:(h + 1) * MLA_QK],
                                       (((1,), (1,)), ((), ())), preferred_element_type=F32)

    def consume(j, s_ref, masked):
        off = pl.multiple_of(j * tk, tk)
        s = [s_ref[h] for h in hs]
        if masked:
            keys = lax.broadcasted_iota(jnp.int32, (tk, tq), 0)
            queries = lax.broadcasted_iota(jnp.int32, (tk, tq), 1)
            s = [jnp.where(keys <= queries, s[h], -jnp.inf) for h in hs]
        m_old = [m_sc[h][0:1] for h in hs]
        m_new = [jnp.maximum(m_old[h], jnp.max(s[h], axis=0, keepdims=True)) for h in hs]
        p = [jnp.exp2(s[h] - m_new[h]) for h in hs]
        alpha = [jnp.exp2(m_old[h] - m_new[h]) for h in hs]
        for h in hs:
            l_new = alpha[h] * l_sc[h][0:1] + jnp.sum(p[h], axis=0, keepdims=True)
            l_sc[h] = jnp.broadcast_to(l_new, (sub, tq))
            m_sc[h] = jnp.broadcast_to(m_new[h], (sub, tq))
        pv = [lax.dot_general(v_ref[0, pl.ds(off, tk), h * MLA_VDIM:(h + 1) * MLA_VDIM], p[h].astype(BF16),
                              (((0,), (0,)), ((), ())), preferred_element_type=F32) for h in hs]
        for h in hs:
            acc_sc[h] = alpha[h] * acc_sc[h] + pv[h]

    def pair(jj, c):
        j = 2 * jj
        produce(j + 1, sb_sc)
        consume(j, sa_sc, False)
        produce(j + 2, sa_sc)
        consume(j + 1, sb_sc, False)
        return c

    produce(0, sa_sc)
    lax.fori_loop(0, i // 2, pair, 0)

    @pl.when(i % 2 == 0)
    def _():
        consume(i, sa_sc, True)

    @pl.when(i % 2 == 1)
    def _():
        produce(i, sb_sc)
        consume(i - 1, sa_sc, False)
        consume(i, sb_sc, True)

    for h in hs:
        o = acc_sc[h] / l_sc[h][0:1]
        o = o * lax.rsqrt(jnp.mean(o * o, axis=0, keepdims=True) + NORM_EPS)
        o_ref[0, :, h * MLA_VDIM:(h + 1) * MLA_VDIM] = (
            jnp.transpose(o) * g_ref[:, h * MLA_VDIM:(h + 1) * MLA_VDIM]).astype(o_ref.dtype)


def _mla_attention(l, q, k, v, g, batch, seq):
    tq, tk, hp = TQ_ATTN, TK_ATTN, HP_ATTN
    assert tq == tk and MLA_VDIM == LANES
    q = q.reshape(batch, seq, MLA_HEADS * MLA_QK)
    k = k.reshape(batch, seq, MLA_HEADS * MLA_QK)
    v = v.reshape(batch, seq, MLA_WIDTH)
    out = pl.pallas_call(
        functools.partial(_attn_kernel, tq=tq, tk=tk, hp=hp),
        out_shape=jax.ShapeDtypeStruct((batch, seq, MLA_WIDTH), BF16),
        grid=(batch, MLA_HEADS // hp, seq // tq),
        in_specs=[pl.BlockSpec((1, tq, hp * MLA_QK), lambda b, h, i: (b, i, h)),
                  pl.BlockSpec((1, seq, hp * MLA_QK), lambda b, h, i: (b, 0, h)),
                  pl.BlockSpec((1, seq, hp * MLA_VDIM), lambda b, h, i: (b, 0, h)),
                  pl.BlockSpec((None, 1, hp * MLA_VDIM), lambda b, h, i: (l, 0, h))],
        out_specs=pl.BlockSpec((1, tq, hp * MLA_VDIM), lambda b, h, i: (b, i, h)),
        scratch_shapes=[pltpu.VMEM((hp, SUBLANES, tq), F32), pltpu.VMEM((hp, SUBLANES, tq), F32),
                        pltpu.VMEM((hp, MLA_VDIM, tq), F32),
                        pltpu.VMEM((hp, tk, tq), F32), pltpu.VMEM((hp, tk, tq), F32)],
        compiler_params=_cparams(("parallel", "parallel", "arbitrary")),
        name="mla_attention",
    )(q, k, v, g)
    return out.reshape(batch * seq, MLA_WIDTH)


(SEC_TA, SEC_RT, SEC_UV, SEC_YV, SEC_ARB, SEC_KH, SEC_BH, SEC_V, SEC_BONUS, SEC_GATE) = range(10)
RWKV_SECTIONS = 10


def _rwkv_masks(chunk):
    w = RWKV_WIDTH
    r = lax.broadcasted_iota(jnp.int32, (w, w), 0)
    c = lax.broadcasted_iota(jnp.int32, (w, w), 1)
    return r, c, _div(r, chunk) == _div(c, chunk)


def _tile_heads(z):
    return jnp.concatenate([z] * RWKV_HEADS, axis=0)


def _fold_heads(z, chunk):
    out = z[0:chunk]
    for h in range(1, RWKV_HEADS):
        out = out + z[h * chunk:(h + 1) * chunk]
    return out


def _rwkv_prep_kernel(x_ref, mu_ref, w0_ref, a0_ref, kk_ref, ka_ref, rk_ref, w2_ref, a2_ref, g2_ref,
                      p_ref, pc_ref, prev_sc, *, nb, chunk, cps):
    seg = cps * chunk
    n = nb * seg
    w = RWKV_WIDTH
    hd = RWKV_HEAD

    @pl.when(pl.program_id(0) == 0)
    def _():
        prev_sc[...] = jnp.zeros(prev_sc.shape, F32)

    x = x_ref[...].reshape(n, RWKV_IN)
    shifted = _shift_rows(x, prev_sc[...], 1, seg)
    prev_sc[...] = x
    xs = x + (shifted - x) * mu_ref[...]
    r = xs[:, 0:w]
    k = xs[:, w:2 * w]
    v = xs[:, 2 * w:3 * w]
    lor = xs[:, 3 * w:]
    ld = -math.exp(-0.5) * _sigmoid(w0_ref[...] + _bdot(jnp.tanh(lor), w2_ref[...]))
    a = _sigmoid(a0_ref[...] + _bdot(lor, a2_ref[...]))
    g = _bdot(_sigmoid(lor), g2_ref[...])
    ri, ci, same_chunk = _rwkv_masks(chunk)
    head_ones = jnp.where(_div(ri, hd) == _div(ci, hd), 1.0, 0.0).astype(BF16)
    kk = k * kk_ref[...]
    kk = kk / jnp.maximum(jnp.sqrt(_exact_right_dot(kk * kk, head_ones, parts=2)), 1e-12)
    k2 = k * (1.0 + (a - 1.0) * ka_ref[...])
    kb = kk * a
    bonus = _exact_right_dot(r * k2 * rk_ref[...], head_ones, parts=2) * v

    tri = jnp.where(same_chunk & (ci <= ri), 1.0, 0.0)
    cl = jnp.concatenate([_exact_left_dot(tri, ld[i * w:(i + 1) * w]) for i in range(n // w)], axis=0)
    units = nb * cps
    cl_last = jnp.concatenate(
        [jnp.broadcast_to(cl[(u + 1) * chunk - 1:(u + 1) * chunk, :], (chunk, w)) for u in range(units)], axis=0)
    e_neg = jnp.exp(-cl)
    e_end = jnp.exp(cl_last - cl)
    a_t = kk * jnp.exp(cl - ld)
    k_t = k2 * e_neg
    b_t = kb * e_neg
    r_t = r * jnp.exp(cl)
    k_h = k2 * e_end
    b_h = kb * e_end

    bd = _div(ri, chunk) == _div(ci, hd)
    rt = lax.broadcasted_iota(jnp.int32, (chunk, w), 0)
    cs = _mod(lax.broadcasted_iota(jnp.int32, (chunk, w), 1), chunk)
    strict = cs < rt
    incl = cs <= rt
    c16 = _div(rt, 16) == _div(cs, 16)
    c32 = _div(rt, 32) == _div(cs, 32)
    eye = jnp.where(rt == cs, 1.0, 0.0)

    def block_diag(z):
        return jnp.where(same_chunk, _tile_heads(z), 0.0)

    def put(u, sec, val):
        b, j = divmod(u, cps)
        p_ref[b, j * chunk:(j + 1) * chunk, sec * w:(sec + 1) * w] = val.astype(p_ref.dtype)

    def mm(x, y):
        return jnp.dot(x, y, preferred_element_type=F32)

    us = range(units)
    sls = [slice(u * chunk, (u + 1) * chunk) for u in us]
    a16, r16, v16, k16, b16 = (z.astype(BF16) for z in (a_t, r_t, v, k_t, b_t))
    a_st = [jnp.where(bd, _tile_heads(a16[sl]), 0.0) for sl in sls]
    v_st = [jnp.where(bd, _tile_heads(v16[sl]), 0.0) for sl in sls]
    sc = [lax.dot_general(
        jnp.concatenate([a16[sl], r16[sl]], axis=0),
        jnp.concatenate([jnp.where(bd, _tile_heads(k16[sl]), 0.0), jnp.where(bd, _tile_heads(b16[sl]), 0.0)], axis=0),
        (((1,), (1,)), ((), ())), preferred_element_type=F32) for sl in sls]
    l_ab = [jnp.where(strict, sc[u][0:chunk, w:], 0.0) for u in us]
    l_ak = [jnp.where(strict, sc[u][0:chunk, 0:w], 0.0).astype(BF16) for u in us]
    a_rk = [jnp.where(incl, sc[u][chunk:, 0:w], 0.0).astype(BF16) for u in us]
    for u in us:
        put(u, SEC_ARB, jnp.where(incl, sc[u][chunk:, w:], 0.0))
    xm = [-jnp.where(c16, l_ab[u], 0.0) for u in us]
    xm16 = [z.astype(BF16) for z in xm]
    off32 = [block_diag(jnp.where(c32 & jnp.logical_not(c16), l_ab[u], 0.0).astype(BF16)) for u in us]
    off64 = [block_diag(jnp.where(jnp.logical_not(c32), l_ab[u], 0.0).astype(BF16)) for u in us]
    x2 = [mm(xm16[u], block_diag(xm16[u])).astype(BF16) for u in us]
    x2_bd = [block_diag(z) for z in x2]
    wv = [block_diag(mm(l_ak[u], v_st[u]).astype(BF16)) for u in us]
    t_lo = [eye + xm[u] for u in us]
    t_lo = [(t_lo[u] + mm(t_lo[u].astype(BF16), x2_bd[u])).astype(BF16) for u in us]
    x4 = [mm(x2[u], x2_bd[u]) for u in us]
    x4b = [z.astype(BF16) for z in x4]
    x4_bd = [block_diag(z) for z in x4b]
    for u in us:
        put(u, SEC_YV, mm(a_rk[u], v_st[u]))
    x8_bd = [block_diag(mm(x4b[u], x4_bd[u]).astype(BF16)) for u in us]
    t_hi = [eye + x4[u] for u in us]
    t_hi = [block_diag((t_hi[u] + mm(t_hi[u].astype(BF16), x8_bd[u])).astype(BF16)) for u in us]
    t_inv = [mm(t_lo[u], t_hi[u]) for u in us]
    for off in (off32, off64):
        tb = [z.astype(BF16) for z in t_inv]
        mid = [mm(tb[u], off[u]).astype(BF16) for u in us]
        t_inv = [t_inv[u] - mm(mid[u], block_diag(tb[u])) for u in us]
    tb = [z.astype(BF16) for z in t_inv]
    for u in us:
        put(u, SEC_TA, mm(tb[u], a_st[u]))
    for u in us:
        put(u, SEC_UV, mm(tb[u], wv[u]))
    for u in us:
        b, j = divmod(u, cps)
        put(u, SEC_RT, r16[sls[u]])
        put(u, SEC_KH, k_h[sls[u]])
        put(u, SEC_BH, b_h[sls[u]])
        put(u, SEC_V, v16[sls[u]])
        put(u, SEC_BONUS, bonus[sls[u]])
        put(u, SEC_GATE, g[sls[u]])
        pc_ref[j, b:b + 1, :] = jnp.exp(cl_last[sls[u]][0:1, :])


def _rwkv_scan_kernel(p_ref, pc_ref, lnw_ref, lnb_ref, o_ref, state_sc, *, nb, chunk, cps):
    seg = cps * chunk
    n = nb * seg
    w = RWKV_WIDTH
    hd = RWKV_HEAD

    @pl.when(pl.program_id(0) == 0)
    def _():
        state_sc[...] = jnp.zeros(state_sc.shape, F32)

    ri, ci, _ = _rwkv_masks(chunk)
    bd = _div(ri, chunk) == _div(ci, hd)
    bdv = _div(ri, hd) == _div(ci, hd)
    ones_bd = jnp.where(bdv, 1.0, 0.0).astype(BF16)

    def sec(b, j, s):
        return p_ref[b, j * chunk:(j + 1) * chunk, s * w:(s + 1) * w]

    ys = [[None] * cps for _ in range(nb)]
    bs = range(nb)
    for j in range(cps):
        gs = [state_sc[b] for b in bs]
        p1 = [lax.dot_general(jnp.concatenate([sec(b, j, SEC_TA), sec(b, j, SEC_RT)], axis=0),
                              gs[b].astype(BF16), (((1,), (1,)), ((), ())), preferred_element_type=F32)
              for b in bs]
        u = [(p1[b][0:chunk] + sec(b, j, SEC_UV).astype(F32)).astype(BF16) for b in bs]
        upd = [lax.dot_general(jnp.concatenate([sec(b, j, SEC_V), -u[b]], axis=0),
                               jnp.concatenate([sec(b, j, SEC_KH), sec(b, j, SEC_BH)], axis=0),
                               (((0,), (0,)), ((), ())), preferred_element_type=F32) for b in bs]
        for b in bs:
            state_sc[b] = gs[b] * pc_ref[j, b:b + 1, :] + jnp.where(bdv, upd[b], 0.0)
        for b in bs:
            u_st = jnp.where(bd, _tile_heads(u[b]), 0.0)
            ys[b][j] = (p1[b][chunk:] + sec(b, j, SEC_YV).astype(F32)
                        - jnp.dot(sec(b, j, SEC_ARB), u_st, preferred_element_type=F32))

    y = jnp.concatenate([ys[b][j] for b in range(nb) for j in range(cps)], axis=0)
    mean = _exact_right_dot(y, ones_bd) * (1.0 / hd)
    d = y - mean
    var = _exact_right_dot(d * d, ones_bd) * (1.0 / hd)
    yn = d * lax.rsqrt(var + RWKV_LN_EPS) * lnw_ref[...] + lnb_ref[...]
    bonus = p_ref[:, :, SEC_BONUS * w:(SEC_BONUS + 1) * w].reshape(n, w).astype(F32)
    gate = p_ref[:, :, SEC_GATE * w:(SEC_GATE + 1) * w].reshape(n, w).astype(F32)
    o_ref[...] = ((yn + bonus) * gate).astype(o_ref.dtype).reshape(o_ref.shape)


def _rwkv(l, x, batch, seq, mu, w0, a0, k_k, k_a, r_k, ln_w, ln_b, w2p, a2p, g2p):
    chunk = RWKV_CHUNK
    assert RWKV_HEADS * chunk == RWKV_WIDTH
    x = x.reshape(batch, seq, RWKV_IN)
    pw = RWKV_SECTIONS * RWKV_WIDTH
    cps = RWKV_PREP_CHUNKS
    seg = cps * chunk
    packed, pc = pl.pallas_call(
        functools.partial(_rwkv_prep_kernel, nb=batch, chunk=chunk, cps=cps),
        out_shape=(jax.ShapeDtypeStruct((batch, seq, pw), BF16),
                   jax.ShapeDtypeStruct((seq // chunk, batch, RWKV_WIDTH), F32)),
        grid=(seq // seg,),
        in_specs=[pl.BlockSpec((batch, seg, RWKV_IN), lambda c: (0, c, 0)),
                  _resident(mu, l), _resident(w0, l), _resident(a0, l), _resident(k_k, l), _resident(k_a, l),
                  _resident(r_k, l), _resident(w2p, l), _resident(a2p, l), _resident(g2p, l)],
        out_specs=(pl.BlockSpec((batch, seg, pw), lambda c: (0, c, 0)),
                   pl.BlockSpec((cps, batch, RWKV_WIDTH), lambda c: (c, 0, 0))),
        scratch_shapes=[pltpu.VMEM((batch * seg, RWKV_IN), F32)],
        compiler_params=_cparams(("arbitrary",)),
        name="rwkv7_prep",
    )(x, mu, w0, a0, k_k, k_a, r_k, w2p, a2p, g2p)
    cps = RWKV_SCAN_CHUNKS
    seg = cps * chunk
    out = pl.pallas_call(
        functools.partial(_rwkv_scan_kernel, nb=batch, chunk=chunk, cps=cps),
        out_shape=jax.ShapeDtypeStruct((batch, seq, RWKV_WIDTH), BF16),
        grid=(seq // seg,),
        in_specs=[pl.BlockSpec((batch, seg, pw), lambda c: (0, c, 0)),
                  pl.BlockSpec((cps, batch, RWKV_WIDTH), lambda c: (c, 0, 0)),
                  _resident(ln_w, l), _resident(ln_b, l)],
        out_specs=pl.BlockSpec((batch, seg, RWKV_WIDTH), lambda c: (0, c, 0)),
        scratch_shapes=[pltpu.VMEM((batch, RWKV_WIDTH, RWKV_WIDTH), F32)],
        compiler_params=_cparams(("arbitrary",)),
        name="rwkv7_scan",
    )(packed, pc, ln_w, ln_b)
    return out.reshape(batch * seq, RWKV_WIDTH)


ML_QK = 0
ML_V = 2 * MLSTM_HEADS * MLSTM_QK
ML_O = ML_V + MLSTM_WIDTH
ML_I = ML_O + MLSTM_WIDTH
ML_F = ML_I + LANES


def _cummax_rows(x):
    n = x.shape[0]
    row = lax.broadcasted_iota(jnp.int32, x.shape, 0)
    sh = 1
    while sh < n:
        x = jnp.maximum(x, jnp.where(row >= sh, pltpu.roll(x, sh, 0), -jnp.inf))
        sh *= 2
    return x


def _mlstm_kernel(x_ref, cw_ref, cb_ref, ib_ref, fb_ref, on_ref, o_ref,
                  prev_sc, c_sc, n_sc, m_sc, *, nb, chunk):
    n = nb * chunk
    nh = MLSTM_HEADS
    dk = MLSTM_QK
    dv = MLSTM_V
    qkw = nh * dk
    vw = MLSTM_WIDTH

    @pl.when(pl.program_id(0) == 0)
    def _():
        prev_sc[...] = jnp.zeros(prev_sc.shape, F32)
        c_sc[...] = jnp.zeros(c_sc.shape, F32)
        n_sc[...] = jnp.zeros(n_sc.shape, F32)
        m_sc[...] = jnp.zeros(m_sc.shape, F32)

    x = x_ref[...].reshape(n, MLSTM_IN)
    qk_raw = x[:, ML_QK:ML_V]
    prev = prev_sc[...]
    conv = cb_ref[...] + qk_raw * cw_ref[MLSTM_CONV - 1:MLSTM_CONV, :]
    for s in range(1, MLSTM_CONV):
        conv = conv + _shift_rows(qk_raw, prev, s, chunk) * cw_ref[MLSTM_CONV - 1 - s:MLSTM_CONV - s, :]
    prev_sc[...] = qk_raw
    qk = conv * _sigmoid(conv)
    q_all = qk[:, 0:qkw] * (dk ** -0.5)
    k_all = qk[:, qkw:]
    v_all = x[:, ML_V:ML_O]
    o_pre = x[:, ML_O:ML_I]
    li_all = x[:, ML_I:ML_F] + ib_ref[...]
    lf_all = _log_sigmoid(x[:, ML_F:ML_F + LANES] + fb_ref[...])

    ri = lax.broadcasted_iota(jnp.int32, (chunk, chunk), 0)
    ci = lax.broadcasted_iota(jnp.int32, (chunk, chunk), 1)
    causal = ci <= ri
    tri = jnp.where(causal, 1.0, 0.0)
    lane_k = lax.broadcasted_iota(jnp.int32, (chunk, qkw), 1)
    lane_v = lax.broadcasted_iota(jnp.int32, (chunk, vw), 1)
    rc = lax.broadcasted_iota(jnp.int32, (qkw, vw), 0)
    cc = lax.broadcasted_iota(jnp.int32, (qkw, vw), 1)
    cmask = _div(rc, dk) == _div(cc, dv)
    expand_v = jnp.where(rc == _div(cc, dv), 1.0, 0.0).astype(BF16)
    rk = lax.broadcasted_iota(jnp.int32, (qkw, qkw), 0)
    ck = lax.broadcasted_iota(jnp.int32, (qkw, qkw), 1)
    expand_k = jnp.where(rk == _div(ck, dk), 1.0, 0.0).astype(BF16)
    gather_k = jnp.where(_div(rk, dk) == ck, 1.0, 0.0).astype(BF16)

    bs = range(nb)
    sls = [slice(b * chunk, (b + 1) * chunk) for b in bs]
    q = [q_all[sl] for sl in sls]
    k = [k_all[sl] for sl in sls]
    k16 = [z.astype(BF16) for z in k]
    v = [v_all[sl] for sl in sls]
    li = [li_all[sl] for sl in sls]
    c_old = [c_sc[b] for b in bs]
    n_old = [n_sc[b] for b in bs]
    m_prev = [m_sc[b] for b in bs]
    g = [_exact_left_dot(tri, lf_all[sl]) for sl in sls]
    lig = [li[b] - g[b] for b in bs]
    inter_log = [g[b] + m_prev[b] for b in bs]
    m_t = [jnp.maximum(inter_log[b], g[b] + _cummax_rows(lig[b])) for b in bs]
    inter_w = [jnp.exp(inter_log[b] - m_t[b]) for b in bs]
    log2e = math.log2(math.e)
    gm = [(g[b] - m_t[b]) * log2e for b in bs]
    lig_t = [jnp.transpose(z * log2e) for z in lig]
    qn = [_exact_right_dot(q[b] * n_old[b], gather_k, parts=2) for b in bs]
    q_c = [_bdot(q[b], c_old[b]) for b in bs]
    ssum = [jnp.zeros((chunk, LANES), F32) for _ in bs]
    num = [jnp.zeros((chunk, vw), F32) for _ in bs]
    for h in range(nh):
        mk = (lane_k >= h * dk) & (lane_k < (h + 1) * dk)
        mv = (lane_v >= h * dv) & (lane_v < (h + 1) * dv)
        qk_h = [lax.dot_general(jnp.where(mk, q[b], 0.0).astype(BF16), k16[b], (((1,), (1,)), ((), ())),
                                preferred_element_type=F32) for b in bs]
        d = [jnp.broadcast_to(gm[b][:, h:h + 1], (chunk, chunk)) + lig_t[b][h:h + 1, :] for b in bs]
        s = [qk_h[b] * jnp.exp2(jnp.where(causal, d[b], -jnp.inf)) for b in bs]
        ssum = [jnp.where(lane_k == h, jnp.sum(s[b], axis=-1, keepdims=True), ssum[b]) for b in bs]
        num = [num[b] + _bdot(s[b], jnp.where(mv, v[b], 0.0)) for b in bs]
    den = [inter_w[b] * qn[b] + ssum[b] for b in bs]
    rden = [1.0 / jnp.maximum(jnp.abs(den[b]), jnp.exp(-m_t[b])) for b in bs]
    g_last = [g[b][chunk - 1:chunk, :] for b in bs]
    a_all = [g_last[b] - g[b] + li[b] for b in bs]
    m_new = [jnp.maximum(g_last[b] + m_prev[b], jnp.max(a_all[b], axis=0, keepdims=True)) for b in bs]
    dec = [jnp.exp(g_last[b] + m_prev[b] - m_new[b]) for b in bs]
    wts = [jnp.exp(a_all[b] - m_new[b]) for b in bs]
    per_head = [jnp.concatenate([inter_w[b], rden[b], wts[b], jnp.broadcast_to(dec[b], (8, LANES))], axis=0)
                for b in bs]
    on_v = [_exact_right_dot(per_head[b], expand_v, parts=2) for b in bs]
    on_k = [_exact_right_dot(per_head[b][2 * chunk:], expand_k, parts=2) for b in bs]
    hs = [(on_v[b][0:chunk] * q_c[b] + num[b]) * on_v[b][chunk:2 * chunk] for b in bs]
    for b in bs:
        c_sc[b] = c_old[b] * on_v[b][3 * chunk:3 * chunk + 1] + jnp.where(
            cmask, _bdot_tn(k16[b], on_v[b][2 * chunk:3 * chunk] * v[b]), 0.0)
        n_sc[b] = n_old[b] * on_k[b][chunk:chunk + 1] + jnp.sum(on_k[b][0:chunk] * k[b], axis=0, keepdims=True)
        m_sc[b] = m_new[b]

    hh = jnp.concatenate(hs, axis=0)
    rv = lax.broadcasted_iota(jnp.int32, (vw, vw), 0)
    cv = lax.broadcasted_iota(jnp.int32, (vw, vw), 1)
    head_ones = jnp.where(_div(rv, dv) == _div(cv, dv), 1.0, 0.0).astype(BF16)
    ms = _exact_right_dot(hh * hh, head_ones, parts=2) * (1.0 / dv)
    out = hh * lax.rsqrt(ms + NORM_EPS) * on_ref[...] * _sigmoid(o_pre)
    o_ref[...] = out.astype(o_ref.dtype).reshape(o_ref.shape)


def _mlstm(l, x, batch, seq, cw, cb, ib, fb, on):
    chunk = MLSTM_CHUNK
    x = x.reshape(batch, seq, MLSTM_IN)
    out = pl.pallas_call(
        functools.partial(_mlstm_kernel, nb=batch, chunk=chunk),
        out_shape=jax.ShapeDtypeStruct((batch, seq, MLSTM_WIDTH), BF16),
        grid=(seq // chunk,),
        in_specs=[pl.BlockSpec((batch, chunk, MLSTM_IN), lambda c: (0, c, 0)),
                  _resident(cw, l), _resident(cb, l), _resident(ib, l), _resident(fb, l), _resident(on, l)],
        out_specs=pl.BlockSpec((batch, chunk, MLSTM_WIDTH), lambda c: (0, c, 0)),
        scratch_shapes=[pltpu.VMEM((batch * chunk, 2 * MLSTM_HEADS * MLSTM_QK), F32),
                        pltpu.VMEM((batch, MLSTM_HEADS * MLSTM_QK, MLSTM_WIDTH), F32),
                        pltpu.VMEM((batch, 1, MLSTM_HEADS * MLSTM_QK), F32),
                        pltpu.VMEM((batch, 1, LANES), F32)],
        compiler_params=_cparams(("arbitrary",)),
        name="mlstm",
    )(x, cw, cb, ib, fb, on)
    return out.reshape(batch * seq, MLSTM_WIDTH)


def _ffn_kernel(x_ref, ya_ref, yb_ref, yc_ref, wo_ref, g_ref, wg_ref, wu_ref, wd_ref, fg_ref,
                o_ref, act_sc, *, final_norm, tf):
    y = jnp.concatenate([ya_ref[...], yb_ref[...], yc_ref[...]], axis=-1)
    x1 = x_ref[...] + jnp.dot(y, wo_ref[...], preferred_element_type=F32)
    h = _rms(x1, g_ref[...]).astype(BF16)
    for c in range(D_FF // tf):
        gate = jnp.dot(h, wg_ref[:, c * tf:(c + 1) * tf], preferred_element_type=F32)
        up = jnp.dot(h, wu_ref[:, c * tf:(c + 1) * tf], preferred_element_type=F32)
        act_sc[:, c * tf:(c + 1) * tf] = (gate * _sigmoid(gate) * up).astype(BF16)
    out = x1 + jnp.dot(act_sc[...], wd_ref[...], preferred_element_type=F32)
    if final_norm:
        out = _rms(out, fg_ref[...])
    o_ref[...] = out


def _out_ffn(l, x, ya, yb, yc, wo, g, wg, wu, wd, fg, final_norm):
    t = x.shape[0]
    tm, tf = TM_FFN, TF_FFN
    row = lambda w: pl.BlockSpec((tm, w), lambda i: (i, 0))
    return pl.pallas_call(
        functools.partial(_ffn_kernel, final_norm=final_norm, tf=tf),
        out_shape=jax.ShapeDtypeStruct((t, D_MODEL), F32),
        grid=(t // tm,),
        in_specs=[row(D_MODEL), row(MLA_WIDTH), row(RWKV_WIDTH), row(MLSTM_WIDTH), _resident(wo, l),
                  _resident(g, l), _resident(wg, l), _resident(wu, l), _resident(wd, l), _resident(fg)],
        out_specs=row(D_MODEL),
        scratch_shapes=[pltpu.VMEM((tm, D_FF), BF16)],
        compiler_params=_cparams(("parallel",)),
        name="out_ffn",
    )(x, ya, yb, yc, wo, g, wg, wu, wd, fg)


def _pad_cols(w, width):
    return jnp.pad(w, [(0, 0)] * (w.ndim - 1) + [(0, width - w.shape[-1])])


def _rot_half_cols(w):
    half = w.shape[-1] // 2
    return jnp.concatenate([-w[..., half:], w[..., :half]], axis=-1)


def _stacked_weights(w_in, mla_w_uq, mla_w_ukv, rwkv_w2, rwkv_a2, rwkv_g2):
    depth = w_in.shape[0]
    c_q, c_kv, k_pe = w_in[..., 0:256], w_in[..., 256:512], w_in[..., 512:576]
    rw = w_in[..., 576:1472]
    ml = w_in[..., 1472:2248]
    w_mla = jnp.concatenate([c_q, c_kv, _pad_cols(k_pe, LANES), _pad_cols(_rot_half_cols(k_pe), LANES)], axis=-1)
    w_mlstm = jnp.concatenate([ml[..., 0:256], ml[..., 256:512], ml[..., 520:776],
                               _pad_cols(ml[..., 512:516], LANES), _pad_cols(ml[..., 516:520], LANES)], axis=-1)
    w_all = jnp.concatenate([w_mla, rw, w_mlstm], axis=-1).astype(BF16)

    uq = mla_w_uq.reshape(depth, MLA_Q_LORA, MLA_HEADS, MLA_NOPE + MLA_ROPE)
    nope = uq[..., :MLA_NOPE].reshape(depth, MLA_Q_LORA, MLA_HEADS * MLA_NOPE)
    pe = _pad_cols(uq[..., MLA_NOPE:], LANES).reshape(depth, MLA_Q_LORA, MLA_HEADS * LANES)
    per = _pad_cols(_rot_half_cols(uq[..., MLA_NOPE:]), LANES).reshape(depth, MLA_Q_LORA, MLA_HEADS * LANES)
    wq = jnp.concatenate([nope, pe, per], axis=-1).astype(BF16)
    ukv = mla_w_ukv.reshape(depth, MLA_KV_LORA, MLA_HEADS, MLA_NOPE + MLA_VDIM)
    wkv = jnp.concatenate([ukv[..., :MLA_NOPE].reshape(depth, MLA_KV_LORA, -1),
                           ukv[..., MLA_NOPE:].reshape(depth, MLA_KV_LORA, -1)], axis=-1).astype(BF16)

    rows = lambda before, wt: jnp.pad(wt, ((0, 0), (before, LANES - before - wt.shape[1]), (0, 0))).astype(BF16)
    w2p = rows(0, rwkv_w2)
    a2p = rows(RWKV_DECAY_LORA, rwkv_a2)
    g2p = rows(RWKV_DECAY_LORA + RWKV_AAA_LORA, rwkv_g2)
    return w_all, wq, wkv, w2p, a2p, g2p


def kernel(x, positions, mix_norm, w_in, mla_q_norm, mla_w_uq, mla_kv_norm, mla_w_ukv, mla_out_norm, rwkv_mu, rwkv_w0, rwkv_w2, rwkv_a0, rwkv_a2, rwkv_g2, rwkv_k_k, rwkv_k_a, rwkv_r_k, rwkv_ln_w, rwkv_ln_b, mlstm_conv_w, mlstm_conv_b, mlstm_i_bias, mlstm_f_bias, mlstm_out_norm, w_out, ffn_norm, w_gate, w_up, w_down, final_norm):
    batch, seq, _ = x.shape
    depth = w_in.shape[0]
    xt = x.reshape(batch * seq, D_MODEL)
    cos, sin = _rope_tables(positions)
    rows = lambda a: a.reshape(a.shape[0], 1, a.shape[1])
    w_all, wq, wkv, w2p, a2p, g2p = _stacked_weights(w_in, mla_w_uq, mla_w_ukv, rwkv_w2, rwkv_a2, rwkv_g2)
    wo, wg, wu, wd = (w.astype(BF16) for w in (w_out, w_gate, w_up, w_down))
    ib = rows(_pad_cols(mlstm_i_bias, LANES))
    fb = rows(_pad_cols(mlstm_f_bias, LANES))
    for l in range(depth):
        q, k, v, rwkv_in, mlstm_in = _inproj(l, xt, rows(mix_norm), w_all, cos, sin, rows(mla_q_norm),
                                             rows(mla_kv_norm), wq, wkv)
        y_mla = _mla_attention(l, q, k, v, rows(mla_out_norm), batch, seq)
        y_rwkv = _rwkv(l, rwkv_in, batch, seq, rows(rwkv_mu), rows(rwkv_w0), rows(rwkv_a0), rows(rwkv_k_k),
                       rows(rwkv_k_a), rows(rwkv_r_k), rows(rwkv_ln_w), rows(rwkv_ln_b), w2p, a2p, g2p)
        y_mlstm = _mlstm(l, mlstm_in, batch, seq, mlstm_conv_w, rows(mlstm_conv_b), ib, fb, rows(mlstm_out_norm))
        xt = _out_ffn(l, xt, y_mla, y_rwkv, y_mlstm, wo, rows(ffn_norm), wg, wu, wd,
                      final_norm.reshape(1, -1), final_norm=(l == depth - 1))
    return xt.reshape(batch, seq, D_MODEL)
```

```python
import functools
import math

import jax
import jax.numpy as jnp
from jax import lax
from jax.experimental import pallas as pl
from jax.experimental.pallas import tpu as pltpu

F32 = jnp.float32
BF16 = jnp.bfloat16

D_MODEL = 1024
DEPTH = 2
MLA_HEADS = 4
MLA_NOPE = 128
MLA_ROPE = 64
MLA_VDIM = 128
MLA_Q_LORA = 256
MLA_KV_LORA = 256
MLA_WIDTH = MLA_HEADS * MLA_VDIM
MLA_QK = 256
ROPE_THETA = 10000.0
RWKV_HEADS = 4
RWKV_HEAD = 64
RWKV_WIDTH = 256
RWKV_DECAY_LORA = 32
RWKV_AAA_LORA = 32
RWKV_GATE_LORA = 64
RWKV_IN = 3 * RWKV_WIDTH + 128
RWKV_LN_EPS = 64e-5
MLSTM_HEADS = 4
MLSTM_QK = 32
MLSTM_V = 64
MLSTM_WIDTH = 256
MLSTM_CONV = 4
MLSTM_IN = 1024
D_FF = 2816
NORM_EPS = 1e-6
LANES = 128
SUBLANES = 8

MLA_CQ = 0
MLA_CKV = MLA_CQ + MLA_Q_LORA
MLA_KPE = MLA_CKV + MLA_KV_LORA
MLA_KPER = MLA_KPE + LANES
MLA_IN = MLA_KPER + LANES

TM_INPROJ = 512
TQ_ATTN = 512
TK_ATTN = 512
HP_ATTN = 4
RWKV_CHUNK = 64
RWKV_PREP_CHUNKS = 2
RWKV_SCAN_CHUNKS = 2
MLSTM_CHUNK = 256
TM_FFN = 512
TF_FFN = 256
VMEM_LIMIT = 56 * 1024 * 1024


def _cparams(sem):
    return pltpu.CompilerParams(dimension_semantics=sem, vmem_limit_bytes=VMEM_LIMIT)


def _resident(a, layer=None):
    if layer is None:
        nd = a.ndim
        return pl.BlockSpec(a.shape, lambda *_: (0,) * nd, pipeline_mode=pl.Buffered(1))
    nd = a.ndim - 1
    return pl.BlockSpec((None,) + a.shape[1:], lambda *_: (layer,) + (0,) * nd, pipeline_mode=pl.Buffered(1))


def _bdot(a, b):
    return jnp.dot(a.astype(BF16), b.astype(BF16), preferred_element_type=F32)


def _bdot_nt(a, b):
    return lax.dot_general(a.astype(BF16), b.astype(BF16), (((1,), (1,)), ((), ())),
                           preferred_element_type=F32)


def _bdot_tn(a, b):
    return lax.dot_general(a.astype(BF16), b.astype(BF16), (((0,), (0,)), ((), ())),
                           preferred_element_type=F32)


def _split3(x):
    h = x.astype(BF16)
    r1 = x - h.astype(F32)
    m = r1.astype(BF16)
    lo = (r1 - m.astype(F32)).astype(BF16)
    return h, m, lo


def _exact_left_dot(sel, x):
    h, m, lo = _split3(x)
    s = sel.astype(BF16)
    return (jnp.dot(s, h, preferred_element_type=F32) + jnp.dot(s, m, preferred_element_type=F32)
            + jnp.dot(s, lo, preferred_element_type=F32))


def _exact_right_dot(x, sel, parts=3):
    pieces = _split3(x)[:parts]
    s = sel.astype(BF16)
    out = jnp.dot(pieces[0], s, preferred_element_type=F32)
    for p in pieces[1:]:
        out = out + jnp.dot(p, s, preferred_element_type=F32)
    return out


def _rms(x, g):
    return x * lax.rsqrt(jnp.mean(x * x, axis=-1, keepdims=True) + NORM_EPS) * g


def _sigmoid(x):
    return 1.0 / (1.0 + jnp.exp(-x))


def _log_sigmoid(x):
    return jnp.minimum(x, 0.0) - jnp.log1p(jnp.exp(-jnp.abs(x)))


def _div(x, d):
    assert d & (d - 1) == 0
    return lax.shift_right_logical(x, d.bit_length() - 1)


def _mod(x, d):
    assert d & (d - 1) == 0
    return lax.bitwise_and(x, d - 1)


def _shift_rows(x, prev, s, chunk):
    n = x.shape[0]
    row = lax.broadcasted_iota(jnp.int32, x.shape, 0)
    return jnp.where(_mod(row, chunk) >= s, pltpu.roll(x, s, 0), pltpu.roll(prev, n - chunk + s, 0))


def _rope_kernel(pos_ref, invf_ref, cos_ref, sin_ref):
    ang = pos_ref[...].astype(F32) * invf_ref[...]
    cos_ref[...] = jnp.cos(ang)
    sin_ref[...] = jnp.sin(ang)


def _rope_tables(positions):
    t = positions.size
    tm = min(1024, t)
    inv_freq = ROPE_THETA ** (-jnp.arange(0, MLA_ROPE, 2, dtype=F32) / MLA_ROPE)
    invf = jnp.tile(inv_freq, LANES // (MLA_ROPE // 2))[None, :]
    return pl.pallas_call(
        _rope_kernel,
        out_shape=(jax.ShapeDtypeStruct((t, LANES), F32), jax.ShapeDtypeStruct((t, LANES), F32)),
        grid=(t // tm,),
        in_specs=[pl.BlockSpec((tm, 1), lambda i: (i, 0)), pl.BlockSpec((1, LANES), lambda i: (0, 0))],
        out_specs=(pl.BlockSpec((tm, LANES), lambda i: (i, 0)), pl.BlockSpec((tm, LANES), lambda i: (i, 0))),
        compiler_params=_cparams(("parallel",)),
        name="rope_tables",
    )(positions.reshape(t, 1), invf)


(FA_RT, FA_V, FA_AT, FA_KT, FA_BT) = range(5)
(FB_KH, FB_BH, FB_BONUS, FB_GATE) = range(4)
(FT_TA, FT_UV, FT_YV, FT_ARB) = range(4)
RWKV_GROUP = 256


def _segsum(x, seg):
    lane = lax.broadcasted_iota(jnp.int32, x.shape, 1)
    out = jnp.zeros_like(x)
    for h in range(x.shape[1] // seg):
        m = (lane >= h * seg) & (lane < (h + 1) * seg)
        s = jnp.sum(jnp.where(m, x, 0.0), axis=-1, keepdims=True)
        out = jnp.where(m, s, out)
    return out


def _rwkv_masks(chunk):
    w = RWKV_WIDTH
    r = lax.broadcasted_iota(jnp.int32, (w, w), 0)
    c = lax.broadcasted_iota(jnp.int32, (w, w), 1)
    return r, c, _div(r, chunk) == _div(c, chunk)


def _rwkv_features(xs, r0, prm, fa_ref, fb_ref, pc_ref, chunk):
    w0_ref, a0_ref, kk_ref, ka_ref, rk_ref, w2_ref, a2_ref, g2_ref = prm
    w = RWKV_WIDTH
    hd = RWKV_HEAD
    n = xs.shape[0]
    r = xs[:, 0:w]
    k = xs[:, w:2 * w]
    v = xs[:, 2 * w:3 * w]
    lor = xs[:, 3 * w:]
    ld = -math.exp(-0.5) * _sigmoid(w0_ref[...] + _bdot(jnp.tanh(lor), w2_ref[...]))
    a = _sigmoid(a0_ref[...] + _bdot(lor, a2_ref[...]))
    g = _bdot(_sigmoid(lor), g2_ref[...])
    kk = k * kk_ref[...]
    kk = kk / jnp.maximum(jnp.sqrt(_segsum(kk * kk, hd)), 1e-12)
    k2 = k * (1.0 + (a - 1.0) * ka_ref[...])
    kb = kk * a
    bonus = _segsum(r * k2 * rk_ref[...], hd) * v

    assert n <= RWKV_WIDTH
    ri, ci, same_chunk = _rwkv_masks(chunk)
    tri = jnp.where(same_chunk & (ci <= ri), 1.0, 0.0)[0:n, 0:n]
    cl = _exact_left_dot(tri, ld)
    units = n // chunk
    cl_last = jnp.concatenate(
        [jnp.broadcast_to(cl[(u + 1) * chunk - 1:(u + 1) * chunk, :], (chunk, w)) for u in range(units)], axis=0)
    e_neg = jnp.exp(-cl)
    e_end = jnp.exp(cl_last - cl)
    rows = slice(r0, r0 + n)

    def put(ref, sec, val):
        ref[rows, sec * w:(sec + 1) * w] = val.astype(ref.dtype)

    put(fa_ref, FA_RT, r * jnp.exp(cl))
    put(fa_ref, FA_V, v)
    put(fa_ref, FA_AT, kk * jnp.exp(cl - ld))
    put(fa_ref, FA_KT, k2 * e_neg)
    put(fa_ref, FA_BT, kb * e_neg)
    put(fb_ref, FB_KH, k2 * e_end)
    put(fb_ref, FB_BH, kb * e_end)
    put(fb_ref, FB_BONUS, bonus)
    put(fb_ref, FB_GATE, g)
    for u in range(units):
        c = r0 // chunk + u
        pc_ref[c:c + 1, :] = jnp.exp(cl_last[u * chunk:u * chunk + 1, :])


def _inproj_kernel(x_ref, g_ref, w_ref, cos_ref, sin_ref, qn_ref, kvn_ref, wq_ref, wkv_ref,
                   mu_ref, w0_ref, a0_ref, kk_ref, ka_ref, rk_ref, w2_ref, a2_ref, g2_ref,
                   q_ref, k_ref, v_ref, fa_ref, fb_ref, pc_ref, mlstm_ref, prev_sc, *, chunk, tiles_per_seq):
    tm = x_ref.shape[0]

    @pl.when(pl.program_id(0) % tiles_per_seq == 0)
    def _():
        prev_sc[...] = jnp.zeros(prev_sc.shape, F32)

    hb = _rms(x_ref[...], g_ref[...]).astype(BF16)
    rw = jnp.dot(hb, w_ref[:, MLA_IN:MLA_IN + RWKV_IN], preferred_element_type=F32)
    row = lax.broadcasted_iota(jnp.int32, rw.shape, 0)
    shifted = jnp.where(row >= 1, pltpu.roll(rw, 1, 0), prev_sc[0:1, :])
    prev_sc[0:1, :] = rw[tm - 1:tm, :]
    xs = rw + (shifted - rw) * mu_ref[...]
    prm = (w0_ref, a0_ref, kk_ref, ka_ref, rk_ref, w2_ref, a2_ref, g2_ref)

    mla = jnp.dot(hb, w_ref[:, 0:MLA_IN], preferred_element_type=F32)
    groups = tm // RWKV_GROUP
    for gi in range(groups // 2):
        _rwkv_features(xs[gi * RWKV_GROUP:(gi + 1) * RWKV_GROUP], gi * RWKV_GROUP, prm, fa_ref, fb_ref, pc_ref, chunk)
    mlstm_ref[...] = jnp.dot(hb, w_ref[:, MLA_IN + RWKV_IN:], preferred_element_type=F32)
    for gi in range(groups // 2, groups):
        _rwkv_features(xs[gi * RWKV_GROUP:(gi + 1) * RWKV_GROUP], gi * RWKV_GROUP, prm, fa_ref, fb_ref, pc_ref, chunk)

    cos = cos_ref[...]
    sin = sin_ref[...]
    scale = (MLA_NOPE + MLA_ROPE) ** -0.5 * math.log2(math.e)
    hw = MLA_HEADS * LANES
    cqn = _rms(mla[:, MLA_CQ:MLA_CKV], qn_ref[...]).astype(BF16)
    q = jnp.dot(cqn, wq_ref[...], preferred_element_type=F32)
    ckvn = _rms(mla[:, MLA_CKV:MLA_KPE], kvn_ref[...]).astype(BF16)
    kv = jnp.dot(ckvn, wkv_ref[...], preferred_element_type=F32)
    kp = (mla[:, MLA_KPE:MLA_KPER] * cos + mla[:, MLA_KPER:MLA_IN] * sin).astype(BF16)
    for h in range(MLA_HEADS):
        c0 = h * LANES
        pe = q[:, hw + c0:hw + c0 + LANES] * cos + q[:, 2 * hw + c0:2 * hw + c0 + LANES] * sin
        q_ref[:, h * MLA_QK:h * MLA_QK + LANES] = (q[:, c0:c0 + LANES] * scale).astype(BF16)
        q_ref[:, h * MLA_QK + LANES:(h + 1) * MLA_QK] = (pe * scale).astype(BF16)
        k_ref[:, h * MLA_QK:h * MLA_QK + LANES] = kv[:, c0:c0 + LANES].astype(BF16)
        k_ref[:, h * MLA_QK + LANES:(h + 1) * MLA_QK] = kp
    v_ref[...] = kv[:, hw:].astype(BF16)


def _inproj(l, x, seq, g, w, cos, sin, qn, kvn, wq, wkv, mu, w0, a0, k_k, k_a, r_k, w2p, a2p, g2p):
    t = x.shape[0]
    tm = TM_INPROJ
    chunk = RWKV_CHUNK
    assert seq % tm == 0 and tm % RWKV_GROUP == 0 and RWKV_GROUP % chunk == 0
    row = lambda width: pl.BlockSpec((tm, width), lambda i: (i, 0))
    rw = RWKV_WIDTH
    return pl.pallas_call(
        functools.partial(_inproj_kernel, chunk=chunk, tiles_per_seq=seq // tm),
        out_shape=(jax.ShapeDtypeStruct((t, MLA_HEADS * MLA_QK), BF16),
                   jax.ShapeDtypeStruct((t, MLA_HEADS * MLA_QK), BF16),
                   jax.ShapeDtypeStruct((t, MLA_WIDTH), BF16),
                   jax.ShapeDtypeStruct((t, 5 * rw), BF16),
                   jax.ShapeDtypeStruct((t, 4 * rw), BF16),
                   jax.ShapeDtypeStruct((t // chunk, rw), F32),
                   jax.ShapeDtypeStruct((t, MLSTM_IN), F32)),
        grid=(t // tm,),
        in_specs=[row(D_MODEL), _resident(g, l), _resident(w, l), row(LANES), row(LANES), _resident(qn, l),
                  _resident(kvn, l), _resident(wq, l), _resident(wkv, l),
                  _resident(mu, l), _resident(w0, l), _resident(a0, l), _resident(k_k, l), _resident(k_a, l),
                  _resident(r_k, l), _resident(w2p, l), _resident(a2p, l), _resident(g2p, l)],
        out_specs=(row(MLA_HEADS * MLA_QK), row(MLA_HEADS * MLA_QK), row(MLA_WIDTH), row(5 * rw), row(4 * rw),
                   pl.BlockSpec((tm // chunk, rw), lambda i: (i, 0)), row(MLSTM_IN)),
        scratch_shapes=[pltpu.VMEM((SUBLANES, RWKV_IN), F32)],
        compiler_params=_cparams(("arbitrary",)),
        name="inproj",
    )(x, g, w, cos, sin, qn, kvn, wq, wkv, mu, w0, a0, k_k, k_a, r_k, w2p, a2p, g2p)


def _attn_kernel(q_ref, k_ref, v_ref, g_ref, o_ref, m_sc, l_sc, acc_sc, sa_sc, sb_sc, *, tq, tk, hp):
    i = pl.program_id(2)
    m_sc[...] = jnp.full(m_sc.shape, -jnp.inf, F32)
    l_sc[...] = jnp.zeros(l_sc.shape, F32)
    acc_sc[...] = jnp.zeros(acc_sc.shape, F32)
    sub = m_sc.shape[1]
    hs = range(hp)

    def produce(j, s_ref):
        off = pl.multiple_of(j * tk, tk)
        for h in hs:
            s_ref[h] = lax.dot_general(k_ref[0, pl.ds(off, tk), h * MLA_QK:(h + 1) * MLA_QK],
                                       q_ref[0, :, h * MLA_QK:(h + 1) * MLA_QK],
                                       (((1,), (1,)), ((), ())), preferred_element_type=F32)

    def consume(j, s_ref, masked):
        off = pl.multiple_of(j * tk, tk)
        s = [s_ref[h] for h in hs]
        if masked:
            keys = lax.broadcasted_iota(jnp.int32, (tk, tq), 0)
            queries = lax.broadcasted_iota(jnp.int32, (tk, tq), 1)
            s = [jnp.where(keys <= queries, s[h], -jnp.inf) for h in hs]
        m_old = [m_sc[h][0:1] for h in hs]
        m_new = [jnp.maximum(m_old[h], jnp.max(s[h], axis=0, keepdims=True)) for h in hs]
        p = [jnp.exp2(s[h] - m_new[h]) for h in hs]
        alpha = [jnp.exp2(m_old[h] - m_new[h]) for h in hs]
        for h in hs:
            l_new = alpha[h] * l_sc[h][0:1] + jnp.sum(p[h], axis=0, keepdims=True)
            l_sc[h] = jnp.broadcast_to(l_new, (sub, tq))
            m_sc[h] = jnp.broadcast_to(m_new[h], (sub, tq))
        pv = [lax.dot_general(v_ref[0, pl.ds(off, tk), h * MLA_VDIM:(h + 1) * MLA_VDIM], p[h].astype(BF16),
                              (((0,), (0,)), ((), ())), preferred_element_type=F32) for h in hs]
        for h in hs:
            acc_sc[h] = alpha[h] * acc_sc[h] + pv[h]

    def pair(jj, c):
        j = 2 * jj
        produce(j + 1, sb_sc)
        consume(j, sa_sc, False)
        produce(j + 2, sa_sc)
        consume(j + 1, sb_sc, False)
        return c

    produce(0, sa_sc)
    lax.fori_loop(0, i // 2, pair, 0)

    @pl.when(i % 2 == 0)
    def _():
        consume(i, sa_sc, True)

    @pl.when(i % 2 == 1)
    def _():
        produce(i, sb_sc)
        consume(i - 1, sa_sc, False)
        consume(i, sb_sc, True)

    for h in hs:
        o = acc_sc[h] / l_sc[h][0:1]
        o = o * lax.rsqrt(jnp.mean(o * o, axis=0, keepdims=True) + NORM_EPS)
        o_ref[0, :, h * MLA_VDIM:(h + 1) * MLA_VDIM] = (
            jnp.transpose(o) * g_ref[:, h * MLA_VDIM:(h + 1) * MLA_VDIM]).astype(o_ref.dtype)


def _mla_attention(l, q, k, v, g, batch, seq):
    tq, tk, hp = TQ_ATTN, TK_ATTN, HP_ATTN
    assert tq == tk and MLA_VDIM == LANES
    q = q.reshape(batch, seq, MLA_HEADS * MLA_QK)
    k = k.reshape(batch, seq, MLA_HEADS * MLA_QK)
    v = v.reshape(batch, seq, MLA_WIDTH)
    out = pl.pallas_call(
        functools.partial(_attn_kernel, tq=tq, tk=tk, hp=hp),
        out_shape=jax.ShapeDtypeStruct((batch, seq, MLA_WIDTH), BF16),
        grid=(batch, MLA_HEADS // hp, seq // tq),
        in_specs=[pl.BlockSpec((1, tq, hp * MLA_QK), lambda b, h, i: (b, i, h)),
                  pl.BlockSpec((1, seq, hp * MLA_QK), lambda b, h, i: (b, 0, h)),
                  pl.BlockSpec((1, seq, hp * MLA_VDIM), lambda b, h, i: (b, 0, h)),
                  pl.BlockSpec((None, 1, hp * MLA_VDIM), lambda b, h, i: (l, 0, h))],
        out_specs=pl.BlockSpec((1, tq, hp * MLA_VDIM), lambda b, h, i: (b, i, h)),
        scratch_shapes=[pltpu.VMEM((hp, SUBLANES, tq), F32), pltpu.VMEM((hp, SUBLANES, tq), F32),
                        pltpu.VMEM((hp, MLA_VDIM, tq), F32),
                        pltpu.VMEM((hp, tk, tq), F32), pltpu.VMEM((hp, tk, tq), F32)],
        compiler_params=_cparams(("parallel", "parallel", "arbitrary")),
        name="mla_attention",
    )(q, k, v, g)
    return out.reshape(batch * seq, MLA_WIDTH)


def _tile_heads(z):
    return jnp.concatenate([z] * RWKV_HEADS, axis=0)


def _rwkv_chunk_kernel(fa_ref, ft_ref, *, nb, chunk, cps):
    seg = cps * chunk
    n = nb * seg
    w = RWKV_WIDTH
    hd = RWKV_HEAD
    x = fa_ref[...].reshape(n, fa_ref.shape[-1])
    r16, v16, a16, k16, b16 = (x[:, s * w:(s + 1) * w] for s in (FA_RT, FA_V, FA_AT, FA_KT, FA_BT))

    ri, ci, same_chunk = _rwkv_masks(chunk)
    bd = _div(ri, chunk) == _div(ci, hd)
    rt = lax.broadcasted_iota(jnp.int32, (chunk, w), 0)
    cs = _mod(lax.broadcasted_iota(jnp.int32, (chunk, w), 1), chunk)
    strict = cs < rt
    incl = cs <= rt
    c16 = _div(rt, 16) == _div(cs, 16)
    c32 = _div(rt, 32) == _div(cs, 32)
    eye = jnp.where(rt == cs, 1.0, 0.0)
    units = nb * cps

    def block_diag(z):
        return jnp.where(same_chunk, _tile_heads(z), 0.0)

    def put(u, sec, val):
        b, j = divmod(u, cps)
        ft_ref[b, j * chunk:(j + 1) * chunk, sec * w:(sec + 1) * w] = val.astype(ft_ref.dtype)

    def mm(x, y):
        return jnp.dot(x, y, preferred_element_type=F32)

    us = range(units)
    sls = [slice(u * chunk, (u + 1) * chunk) for u in us]
    a_st = [jnp.where(bd, _tile_heads(a16[sl]), 0.0) for sl in sls]
    v_st = [jnp.where(bd, _tile_heads(v16[sl]), 0.0) for sl in sls]
    kb_t = [jnp.concatenate([jnp.transpose(jnp.where(bd, _tile_heads(k16[sl]), 0.0)),
                             jnp.transpose(jnp.where(bd, _tile_heads(b16[sl]), 0.0))], axis=1) for sl in sls]
    sc = [mm(jnp.concatenate([a16[sls[u]], r16[sls[u]]], axis=0), kb_t[u]) for u in us]
    l_ab = [jnp.where(strict, sc[u][0:chunk, w:], 0.0) for u in us]
    l_ak = [jnp.where(strict, sc[u][0:chunk, 0:w], 0.0).astype(BF16) for u in us]
    a_rk = [jnp.where(incl, sc[u][chunk:, 0:w], 0.0).astype(BF16) for u in us]
    for u in us:
        put(u, FT_ARB, jnp.where(incl, sc[u][chunk:, w:], 0.0))
    xm = [-jnp.where(c16, l_ab[u], 0.0) for u in us]
    xm16 = [z.astype(BF16) for z in xm]
    off32 = [block_diag(jnp.where(c32 & jnp.logical_not(c16), l_ab[u], 0.0).astype(BF16)) for u in us]
    off64 = [block_diag(jnp.where(jnp.logical_not(c32), l_ab[u], 0.0).astype(BF16)) for u in us]
    x2 = [mm(xm16[u], block_diag(xm16[u])).astype(BF16) for u in us]
    x2_bd = [block_diag(z) for z in x2]
    lv = [mm(jnp.concatenate([l_ak[u], a_rk[u]], axis=0), v_st[u]) for u in us]
    wv = [block_diag(lv[u][0:chunk].astype(BF16)) for u in us]
    t_lo = [eye + xm[u] for u in us]
    tx = [mm(jnp.concatenate([t_lo[u].astype(BF16), x2[u]], axis=0), x2_bd[u]) for u in us]
    t_lo = [(t_lo[u] + tx[u][0:chunk]).astype(BF16) for u in us]
    x4 = [tx[u][chunk:] for u in us]
    x4b = [z.astype(BF16) for z in x4]
    x4_bd = [block_diag(z) for z in x4b]
    for u in us:
        put(u, FT_YV, lv[u][chunk:])
    x8_bd = [block_diag(mm(x4b[u], x4_bd[u]).astype(BF16)) for u in us]
    t_hi = [eye + x4[u] for u in us]
    t_hi = [block_diag((t_hi[u] + mm(t_hi[u].astype(BF16), x8_bd[u])).astype(BF16)) for u in us]
    t_inv = [mm(t_lo[u], t_hi[u]) for u in us]
    for off in (off32, off64):
        tb = [z.astype(BF16) for z in t_inv]
        mid = [mm(tb[u], off[u]).astype(BF16) for u in us]
        t_inv = [t_inv[u] - mm(mid[u], block_diag(tb[u])) for u in us]
    tb = [z.astype(BF16) for z in t_inv]
    for u in us:
        put(u, FT_TA, mm(tb[u], a_st[u]))
    for u in us:
        put(u, FT_UV, mm(tb[u], wv[u]))


def _rwkv_scan_kernel(fa_ref, fb_ref, ft_ref, pc_ref, lnw_ref, lnb_ref, o_ref, state_sc, *, nb, chunk, cps):
    seg = cps * chunk
    n = nb * seg
    w = RWKV_WIDTH
    hd = RWKV_HEAD

    @pl.when(pl.program_id(0) == 0)
    def _():
        state_sc[...] = jnp.zeros(state_sc.shape, F32)

    ri, ci, _ = _rwkv_masks(chunk)
    bd = _div(ri, chunk) == _div(ci, hd)
    bdv = _div(ri, hd) == _div(ci, hd)
    ones_bd = jnp.where(bdv, 1.0, 0.0).astype(BF16)

    def sec(ref, b, j, s):
        return ref[b, j * chunk:(j + 1) * chunk, s * w:(s + 1) * w]

    ys = [[None] * cps for _ in range(nb)]
    bs = range(nb)
    for j in range(cps):
        gs = [state_sc[b] for b in bs]
        p1 = [lax.dot_general(jnp.concatenate([sec(ft_ref, b, j, FT_TA), sec(fa_ref, b, j, FA_RT)], axis=0),
                              gs[b].astype(BF16), (((1,), (1,)), ((), ())), preferred_element_type=F32)
              for b in bs]
        u = [(p1[b][0:chunk] + sec(ft_ref, b, j, FT_UV).astype(F32)).astype(BF16) for b in bs]
        upd = [lax.dot_general(jnp.concatenate([sec(fa_ref, b, j, FA_V), -u[b]], axis=0),
                               jnp.concatenate([sec(fb_ref, b, j, FB_KH), sec(fb_ref, b, j, FB_BH)], axis=0),
                               (((0,), (0,)), ((), ())), preferred_element_type=F32) for b in bs]
        for b in bs:
            state_sc[b] = gs[b] * pc_ref[b, pl.ds(pl.program_id(0) * cps + j, 1), :] + jnp.where(bdv, upd[b], 0.0)
        for b in bs:
            u_st = jnp.where(bd, _tile_heads(u[b]), 0.0)
            ys[b][j] = (p1[b][chunk:] + sec(ft_ref, b, j, FT_YV).astype(F32)
                        - jnp.dot(sec(ft_ref, b, j, FT_ARB), u_st, preferred_element_type=F32))

    y = jnp.concatenate([ys[b][j] for b in range(nb) for j in range(cps)], axis=0)
    mean = _exact_right_dot(y, ones_bd) * (1.0 / hd)
    d = y - mean
    var = _exact_right_dot(d * d, ones_bd) * (1.0 / hd)
    yn = d * lax.rsqrt(var + RWKV_LN_EPS) * lnw_ref[...] + lnb_ref[...]
    bonus = fb_ref[:, :, FB_BONUS * w:(FB_BONUS + 1) * w].reshape(n, w).astype(F32)
    gate = fb_ref[:, :, FB_GATE * w:(FB_GATE + 1) * w].reshape(n, w).astype(F32)
    o_ref[...] = ((yn + bonus) * gate).astype(o_ref.dtype).reshape(o_ref.shape)


def _rwkv(l, fa, fb, pc, batch, seq, ln_w, ln_b):
    chunk = RWKV_CHUNK
    w = RWKV_WIDTH
    assert RWKV_HEADS * chunk == w
    fa = fa.reshape(batch, seq, 5 * w)
    fb = fb.reshape(batch, seq, 4 * w)
    pc = pc.reshape(batch, seq // chunk, w)
    cps = RWKV_PREP_CHUNKS
    seg = cps * chunk
    ft = pl.pallas_call(
        functools.partial(_rwkv_chunk_kernel, nb=batch, chunk=chunk, cps=cps),
        out_shape=jax.ShapeDtypeStruct((batch, seq, 4 * w), BF16),
        grid=(seq // seg,),
        in_specs=[pl.BlockSpec((batch, seg, 5 * w), lambda c: (0, c, 0))],
        out_specs=pl.BlockSpec((batch, seg, 4 * w), lambda c: (0, c, 0)),
        compiler_params=_cparams(("parallel",)),
        name="rwkv7_chunk",
    )(fa)
    cps = RWKV_SCAN_CHUNKS
    seg = cps * chunk
    out = pl.pallas_call(
        functools.partial(_rwkv_scan_kernel, nb=batch, chunk=chunk, cps=cps),
        out_shape=jax.ShapeDtypeStruct((batch, seq, w), BF16),
        grid=(seq // seg,),
        in_specs=[pl.BlockSpec((batch, seg, 2 * w), lambda c: (0, c, 0)),
                  pl.BlockSpec((batch, seg, 4 * w), lambda c: (0, c, 0)),
                  pl.BlockSpec((batch, seg, 4 * w), lambda c: (0, c, 0)),
                  _resident(pc),
                  _resident(ln_w, l), _resident(ln_b, l)],
        out_specs=pl.BlockSpec((batch, seg, w), lambda c: (0, c, 0)),
        scratch_shapes=[pltpu.VMEM((batch, w, w), F32)],
        compiler_params=_cparams(("arbitrary",)),
        name="rwkv7_scan",
    )(fa, fb, ft, pc, ln_w, ln_b)
    return out.reshape(batch * seq, w)


ML_QK = 0
ML_V = 2 * MLSTM_HEADS * MLSTM_QK
ML_O = ML_V + MLSTM_WIDTH
ML_I = ML_O + MLSTM_WIDTH
ML_F = ML_I + LANES


def _cummax_rows(x):
    n = x.shape[0]
    row = lax.broadcasted_iota(jnp.int32, x.shape, 0)
    sh = 1
    while sh < n:
        x = jnp.maximum(x, jnp.where(row >= sh, pltpu.roll(x, sh, 0), -jnp.inf))
        sh *= 2
    return x


def _mlstm_kernel(x_ref, cw_ref, cb_ref, ib_ref, fb_ref, on_ref, o_ref,
                  prev_sc, c_sc, n_sc, m_sc, *, nb, chunk):
    n = nb * chunk
    nh = MLSTM_HEADS
    dk = MLSTM_QK
    dv = MLSTM_V
    qkw = nh * dk
    vw = MLSTM_WIDTH

    @pl.when(pl.program_id(0) == 0)
    def _():
        prev_sc[...] = jnp.zeros(prev_sc.shape, F32)
        c_sc[...] = jnp.zeros(c_sc.shape, F32)
        n_sc[...] = jnp.zeros(n_sc.shape, F32)
        m_sc[...] = jnp.zeros(m_sc.shape, F32)

    x = x_ref[...].reshape(n, MLSTM_IN)
    qk_raw = x[:, ML_QK:ML_V]
    prev = prev_sc[...]
    conv = cb_ref[...] + qk_raw * cw_ref[MLSTM_CONV - 1:MLSTM_CONV, :]
    for s in range(1, MLSTM_CONV):
        conv = conv + _shift_rows(qk_raw, prev, s, chunk) * cw_ref[MLSTM_CONV - 1 - s:MLSTM_CONV - s, :]
    prev_sc[...] = qk_raw
    qk = conv * _sigmoid(conv)
    q_all = qk[:, 0:qkw] * (dk ** -0.5)
    k_all = qk[:, qkw:]
    v_all = x[:, ML_V:ML_O]
    o_pre = x[:, ML_O:ML_I]
    li_all = x[:, ML_I:ML_F] + ib_ref[...]
    lf_all = _log_sigmoid(x[:, ML_F:ML_F + LANES] + fb_ref[...])

    ri = lax.broadcasted_iota(jnp.int32, (chunk, chunk), 0)
    ci = lax.broadcasted_iota(jnp.int32, (chunk, chunk), 1)
    causal = ci <= ri
    tri = jnp.where(causal, 1.0, 0.0)
    lane_k = lax.broadcasted_iota(jnp.int32, (chunk, qkw), 1)
    lane_v = lax.broadcasted_iota(jnp.int32, (chunk, vw), 1)
    rc = lax.broadcasted_iota(jnp.int32, (qkw, vw), 0)
    cc = lax.broadcasted_iota(jnp.int32, (qkw, vw), 1)
    cmask = _div(rc, dk) == _div(cc, dv)
    expand_v = jnp.where(rc == _div(cc, dv), 1.0, 0.0).astype(BF16)
    rk = lax.broadcasted_iota(jnp.int32, (qkw, qkw), 0)
    ck = lax.broadcasted_iota(jnp.int32, (qkw, qkw), 1)
    expand_k = jnp.where(rk == _div(ck, dk), 1.0, 0.0).astype(BF16)
    gather_k = jnp.where(_div(rk, dk) == ck, 1.0, 0.0).astype(BF16)

    bs = range(nb)
    sls = [slice(b * chunk, (b + 1) * chunk) for b in bs]
    q = [q_all[sl] for sl in sls]
    k = [k_all[sl] for sl in sls]
    k16 = [z.astype(BF16) for z in k]
    v = [v_all[sl] for sl in sls]
    li = [li_all[sl] for sl in sls]
    c_old = [c_sc[b] for b in bs]
    n_old = [n_sc[b] for b in bs]
    m_prev = [m_sc[b] for b in bs]
    g = [_exact_left_dot(tri, lf_all[sl]) for sl in sls]
    lig = [li[b] - g[b] for b in bs]
    inter_log = [g[b] + m_prev[b] for b in bs]
    m_t = [jnp.maximum(inter_log[b], g[b] + _cummax_rows(lig[b])) for b in bs]
    inter_w = [jnp.exp(inter_log[b] - m_t[b]) for b in bs]
    log2e = math.log2(math.e)
    gm = [(g[b] - m_t[b]) * log2e for b in bs]
    lig_t = [jnp.transpose(z * log2e) for z in lig]
    qn = [_exact_right_dot(q[b] * n_old[b], gather_k, parts=2) for b in bs]
    q_c = [_bdot(q[b], c_old[b]) for b in bs]
    ssum = [jnp.zeros((chunk, LANES), F32) for _ in bs]
    num = [jnp.zeros((chunk, vw), F32) for _ in bs]
    for h in range(nh):
        mk = (lane_k >= h * dk) & (lane_k < (h + 1) * dk)
        mv = (lane_v >= h * dv) & (lane_v < (h + 1) * dv)
        qk_h = [lax.dot_general(jnp.where(mk, q[b], 0.0).astype(BF16), k16[b], (((1,), (1,)), ((), ())),
                                preferred_element_type=F32) for b in bs]
        d = [jnp.broadcast_to(gm[b][:, h:h + 1], (chunk, chunk)) + lig_t[b][h:h + 1, :] for b in bs]
        s = [qk_h[b] * jnp.exp2(jnp.where(causal, d[b], -jnp.inf)) for b in bs]
        ssum = [jnp.where(lane_k == h, jnp.sum(s[b], axis=-1, keepdims=True), ssum[b]) for b in bs]
        num = [num[b] + _bdot(s[b], jnp.where(mv, v[b], 0.0)) for b in bs]
    den = [inter_w[b] * qn[b] + ssum[b] for b in bs]
    rden = [1.0 / jnp.maximum(jnp.abs(den[b]), jnp.exp(-m_t[b])) for b in bs]
    g_last = [g[b][chunk - 1:chunk, :] for b in bs]
    a_all = [g_last[b] - g[b] + li[b] for b in bs]
    m_new = [jnp.maximum(g_last[b] + m_prev[b], jnp.max(a_all[b], axis=0, keepdims=True)) for b in bs]
    dec = [jnp.exp(g_last[b] + m_prev[b] - m_new[b]) for b in bs]
    wts = [jnp.exp(a_all[b] - m_new[b]) for b in bs]
    per_head = [jnp.concatenate([inter_w[b], rden[b], wts[b], jnp.broadcast_to(dec[b], (8, LANES))], axis=0)
                for b in bs]
    on_v = [_exact_right_dot(per_head[b], expand_v, parts=2) for b in bs]
    on_k = [_exact_right_dot(per_head[b][2 * chunk:], expand_k, parts=2) for b in bs]
    hs = [(on_v[b][0:chunk] * q_c[b] + num[b]) * on_v[b][chunk:2 * chunk] for b in bs]
    for b in bs:
        c_sc[b] = c_old[b] * on_v[b][3 * chunk:3 * chunk + 1] + jnp.where(
            cmask, _bdot_tn(k16[b], on_v[b][2 * chunk:3 * chunk] * v[b]), 0.0)
        n_sc[b] = n_old[b] * on_k[b][chunk:chunk + 1] + jnp.sum(on_k[b][0:chunk] * k[b], axis=0, keepdims=True)
        m_sc[b] = m_new[b]

    hh = jnp.concatenate(hs, axis=0)
    rv = lax.broadcasted_iota(jnp.int32, (vw, vw), 0)
    cv = lax.broadcasted_iota(jnp.int32, (vw, vw), 1)
    head_ones = jnp.where(_div(rv, dv) == _div(cv, dv), 1.0, 0.0).astype(BF16)
    ms = _exact_right_dot(hh * hh, head_ones, parts=2) * (1.0 / dv)
    out = hh * lax.rsqrt(ms + NORM_EPS) * on_ref[...] * _sigmoid(o_pre)
    o_ref[...] = out.astype(o_ref.dtype).reshape(o_ref.shape)


def _mlstm(l, x, batch, seq, cw, cb, ib, fb, on):
    chunk = MLSTM_CHUNK
    x = x.reshape(batch, seq, MLSTM_IN)
    out = pl.pallas_call(
        functools.partial(_mlstm_kernel, nb=batch, chunk=chunk),
        out_shape=jax.ShapeDtypeStruct((batch, seq, MLSTM_WIDTH), BF16),
        grid=(seq // chunk,),
        in_specs=[pl.BlockSpec((batch, chunk, MLSTM_IN), lambda c: (0, c, 0)),
                  _resident(cw, l), _resident(cb, l), _resident(ib, l), _resident(fb, l), _resident(on, l)],
        out_specs=pl.BlockSpec((batch, chunk, MLSTM_WIDTH), lambda c: (0, c, 0)),
        scratch_shapes=[pltpu.VMEM((batch * chunk, 2 * MLSTM_HEADS * MLSTM_QK), F32),
                        pltpu.VMEM((batch, MLSTM_HEADS * MLSTM_QK, MLSTM_WIDTH), F32),
                        pltpu.VMEM((batch, 1, MLSTM_HEADS * MLSTM_QK), F32),
                        pltpu.VMEM((batch, 1, LANES), F32)],
        compiler_params=_cparams(("arbitrary",)),
        name="mlstm",
    )(x, cw, cb, ib, fb, on)
    return out.reshape(batch * seq, MLSTM_WIDTH)


def _ffn_kernel(x_ref, ya_ref, yb_ref, yc_ref, wo_ref, g_ref, wg_ref, wu_ref, wd_ref, fg_ref,
                o_ref, act_sc, *, final_norm, tf):
    y = jnp.concatenate([ya_ref[...], yb_ref[...], yc_ref[...]], axis=-1)
    x1 = x_ref[...] + jnp.dot(y, wo_ref[...], preferred_element_type=F32)
    h = _rms(x1, g_ref[...]).astype(BF16)
    for c in range(D_FF // tf):
        gate = jnp.dot(h, wg_ref[:, c * tf:(c + 1) * tf], preferred_element_type=F32)
        up = jnp.dot(h, wu_ref[:, c * tf:(c + 1) * tf], preferred_element_type=F32)
        act_sc[:, c * tf:(c + 1) * tf] = (gate * _sigmoid(gate) * up).astype(BF16)
    out = x1 + jnp.dot(act_sc[...], wd_ref[...], preferred_element_type=F32)
    if final_norm:
        out = _rms(out, fg_ref[...])
    o_ref[...] = out


def _out_ffn(l, x, ya, yb, yc, wo, g, wg, wu, wd, fg, final_norm):
    t = x.shape[0]
    tm, tf = TM_FFN, TF_FFN
    row = lambda w: pl.BlockSpec((tm, w), lambda i: (i, 0))
    return pl.pallas_call(
        functools.partial(_ffn_kernel, final_norm=final_norm, tf=tf),
        out_shape=jax.ShapeDtypeStruct((t, D_MODEL), F32),
        grid=(t // tm,),
        in_specs=[row(D_MODEL), row(MLA_WIDTH), row(RWKV_WIDTH), row(MLSTM_WIDTH), _resident(wo, l),
                  _resident(g, l), _resident(wg, l), _resident(wu, l), _resident(wd, l), _resident(fg)],
        out_specs=row(D_MODEL),
        scratch_shapes=[pltpu.VMEM((tm, D_FF), BF16)],
        compiler_params=_cparams(("parallel",)),
        name="out_ffn",
    )(x, ya, yb, yc, wo, g, wg, wu, wd, fg)


def _pad_cols(w, width):
    return jnp.pad(w, [(0, 0)] * (w.ndim - 1) + [(0, width - w.shape[-1])])


def _rot_half_cols(w):
    half = w.shape[-1] // 2
    return jnp.concatenate([-w[..., half:], w[..., :half]], axis=-1)


def _stacked_weights(w_in, mla_w_uq, mla_w_ukv, rwkv_w2, rwkv_a2, rwkv_g2):
    depth = w_in.shape[0]
    c_q, c_kv, k_pe = w_in[..., 0:256], w_in[..., 256:512], w_in[..., 512:576]
    rw = w_in[..., 576:1472]
    ml = w_in[..., 1472:2248]
    w_mla = jnp.concatenate([c_q, c_kv, _pad_cols(k_pe, LANES), _pad_cols(_rot_half_cols(k_pe), LANES)], axis=-1)
    w_mlstm = jnp.concatenate([ml[..., 0:256], ml[..., 256:512], ml[..., 520:776],
                               _pad_cols(ml[..., 512:516], LANES), _pad_cols(ml[..., 516:520], LANES)], axis=-1)
    w_all = jnp.concatenate([w_mla, rw, w_mlstm], axis=-1).astype(BF16)

    uq = mla_w_uq.reshape(depth, MLA_Q_LORA, MLA_HEADS, MLA_NOPE + MLA_ROPE)
    nope = uq[..., :MLA_NOPE].reshape(depth, MLA_Q_LORA, MLA_HEADS * MLA_NOPE)
    pe = _pad_cols(uq[..., MLA_NOPE:], LANES).reshape(depth, MLA_Q_LORA, MLA_HEADS * LANES)
    per = _pad_cols(_rot_half_cols(uq[..., MLA_NOPE:]), LANES).reshape(depth, MLA_Q_LORA, MLA_HEADS * LANES)
    wq = jnp.concatenate([nope, pe, per], axis=-1).astype(BF16)
    ukv = mla_w_ukv.reshape(depth, MLA_KV_LORA, MLA_HEADS, MLA_NOPE + MLA_VDIM)
    wkv = jnp.concatenate([ukv[..., :MLA_NOPE].reshape(depth, MLA_KV_LORA, -1),
                           ukv[..., MLA_NOPE:].reshape(depth, MLA_KV_LORA, -1)], axis=-1).astype(BF16)

    rows = lambda before, wt: jnp.pad(wt, ((0, 0), (before, LANES - before - wt.shape[1]), (0, 0))).astype(BF16)
    w2p = rows(0, rwkv_w2)
    a2p = rows(RWKV_DECAY_LORA, rwkv_a2)
    g2p = rows(RWKV_DECAY_LORA + RWKV_AAA_LORA, rwkv_g2)
    return w_all, wq, wkv, w2p, a2p, g2p


def kernel(x, positions, mix_norm, w_in, mla_q_norm, mla_w_uq, mla_kv_norm, mla_w_ukv, mla_out_norm, rwkv_mu, rwkv_w0, rwkv_w2, rwkv_a0, rwkv_a2, rwkv_g2, rwkv_k_k, rwkv_k_a, rwkv_r_k, rwkv_ln_w, rwkv_ln_b, mlstm_conv_w, mlstm_conv_b, mlstm_i_bias, mlstm_f_bias, mlstm_out_norm, w_out, ffn_norm, w_gate, w_up, w_down, final_norm):
    batch, seq, _ = x.shape
    depth = w_in.shape[0]
    xt = x.reshape(batch * seq, D_MODEL)
    cos, sin = _rope_tables(positions)
    rows = lambda a: a.reshape(a.shape[0], 1, a.shape[1])
    w_all, wq, wkv, w2p, a2p, g2p = _stacked_weights(w_in, mla_w_uq, mla_w_ukv, rwkv_w2, rwkv_a2, rwkv_g2)
    wo, wg, wu, wd = (w.astype(BF16) for w in (w_out, w_gate, w_up, w_down))
    ml_ib = rows(_pad_cols(mlstm_i_bias, LANES))
    ml_fb = rows(_pad_cols(mlstm_f_bias, LANES))
    for l in range(depth):
        q, k, v, fa, fb, pc, mlstm_in = _inproj(
            l, xt, seq, rows(mix_norm), w_all, cos, sin, rows(mla_q_norm), rows(mla_kv_norm), wq, wkv,
            rows(rwkv_mu), rows(rwkv_w0), rows(rwkv_a0), rows(rwkv_k_k), rows(rwkv_k_a), rows(rwkv_r_k),
            w2p, a2p, g2p)
        y_mla = _mla_attention(l, q, k, v, rows(mla_out_norm), batch, seq)
        y_rwkv = _rwkv(l, fa, fb, pc, batch, seq, rows(rwkv_ln_w), rows(rwkv_ln_b))
        y_mlstm = _mlstm(l, mlstm_in, batch, seq, mlstm_conv_w, rows(mlstm_conv_b), ml_ib, ml_fb,
                         rows(mlstm_out_norm))
        xt = _out_ffn(l, xt, y_mla, y_rwkv, y_mlstm, wo, rows(ffn_norm), wg, wu, wd,
                      final_norm.reshape(1, -1), final_norm=(l == depth - 1))
    return xt.reshape(batch, seq, D_MODEL)
```

```python
import functools
import math

import jax
import jax.numpy as jnp
from jax import lax
from jax.experimental import pallas as pl
from jax.experimental.pallas import tpu as pltpu

F32 = jnp.float32
BF16 = jnp.bfloat16

D_MODEL = 1024
DEPTH = 2
MLA_HEADS = 4
MLA_NOPE = 128
MLA_ROPE = 64
MLA_VDIM = 128
MLA_Q_LORA = 256
MLA_KV_LORA = 256
MLA_WIDTH = MLA_HEADS * MLA_VDIM
MLA_QK = 256
ROPE_THETA = 10000.0
RWKV_HEADS = 4
RWKV_HEAD = 64
RWKV_WIDTH = 256
RWKV_DECAY_LORA = 32
RWKV_AAA_LORA = 32
RWKV_GATE_LORA = 64
RWKV_IN = 3 * RWKV_WIDTH + 128
RWKV_LN_EPS = 64e-5
MLSTM_HEADS = 4
MLSTM_QK = 32
MLSTM_V = 64
MLSTM_WIDTH = 256
MLSTM_CONV = 4
MLSTM_IN = 1024
D_FF = 2816
NORM_EPS = 1e-6
LANES = 128
SUBLANES = 8

MLA_CQ = 0
MLA_CKV = MLA_CQ + MLA_Q_LORA
MLA_KPE = MLA_CKV + MLA_KV_LORA
MLA_KPER = MLA_KPE + LANES
MLA_IN = MLA_KPER + LANES

TM_INPROJ = 512
TQ_ATTN = 512
TK_ATTN = 512
HP_ATTN = 4
RWKV_CHUNK = 64
RWKV_PREP_CHUNKS = 2
RWKV_SCAN_CHUNKS = 2
MLSTM_CHUNK = 256
TM_FFN = 512
TF_FFN = 256
VMEM_LIMIT = 56 * 1024 * 1024


def _cparams(sem):
    return pltpu.CompilerParams(dimension_semantics=sem, vmem_limit_bytes=VMEM_LIMIT)


def _resident(a, layer=None):
    if layer is None:
        nd = a.ndim
        return pl.BlockSpec(a.shape, lambda *_: (0,) * nd, pipeline_mode=pl.Buffered(1))
    nd = a.ndim - 1
    return pl.BlockSpec((None,) + a.shape[1:], lambda *_: (layer,) + (0,) * nd, pipeline_mode=pl.Buffered(1))


def _rows(layer, *refs):
    return [r.at[layer:layer + 1] for r in refs]


def _bdot(a, b):
    return jnp.dot(a.astype(BF16), b.astype(BF16), preferred_element_type=F32)


def _dot_nt(a, b):
    return lax.dot_general(a, b, (((1,), (1,)), ((), ())), preferred_element_type=F32)


def _bdot_nt(a, b):
    return lax.dot_general(a.astype(BF16), b.astype(BF16), (((1,), (1,)), ((), ())),
                           preferred_element_type=F32)


def _bdot_tn(a, b):
    return lax.dot_general(a.astype(BF16), b.astype(BF16), (((0,), (0,)), ((), ())),
                           preferred_element_type=F32)


def _split3(x):
    h = x.astype(BF16)
    r1 = x - h.astype(F32)
    m = r1.astype(BF16)
    lo = (r1 - m.astype(F32)).astype(BF16)
    return h, m, lo


def _exact_left_dot(sel, x):
    h, m, lo = _split3(x)
    s = sel.astype(BF16)
    return (jnp.dot(s, h, preferred_element_type=F32) + jnp.dot(s, m, preferred_element_type=F32)
            + jnp.dot(s, lo, preferred_element_type=F32))


def _exact_right_dot(x, sel, parts=3):
    pieces = _split3(x)[:parts]
    s = sel.astype(BF16)
    out = jnp.dot(pieces[0], s, preferred_element_type=F32)
    for p in pieces[1:]:
        out = out + jnp.dot(p, s, preferred_element_type=F32)
    return out


def _rms(x, g):
    return x * lax.rsqrt(jnp.mean(x * x, axis=-1, keepdims=True) + NORM_EPS) * g


def _sigmoid(x):
    return 1.0 / (1.0 + jnp.exp(-x))


def _log_sigmoid(x):
    return jnp.minimum(x, 0.0) - jnp.log1p(jnp.exp(-jnp.abs(x)))


def _div(x, d):
    assert d & (d - 1) == 0
    return lax.shift_right_logical(x, d.bit_length() - 1)


def _mod(x, d):
    assert d & (d - 1) == 0
    return lax.bitwise_and(x, d - 1)


def _shift_rows(x, prev, s, chunk):
    n = x.shape[0]
    row = lax.broadcasted_iota(jnp.int32, x.shape, 0)
    return jnp.where(_mod(row, chunk) >= s, pltpu.roll(x, s, 0), pltpu.roll(prev, n - chunk + s, 0))


def _rope_kernel(pos_ref, invf_ref, cos_ref, sin_ref):
    ang = pos_ref[...].astype(F32) * invf_ref[...]
    cos_ref[...] = jnp.cos(ang)
    sin_ref[...] = jnp.sin(ang)


def _rope_tables(positions):
    t = positions.size
    tm = min(1024, t)
    inv_freq = ROPE_THETA ** (-jnp.arange(0, MLA_ROPE, 2, dtype=F32) / MLA_ROPE)
    invf = jnp.tile(inv_freq, LANES // (MLA_ROPE // 2))[None, :]
    return pl.pallas_call(
        _rope_kernel,
        out_shape=(jax.ShapeDtypeStruct((t, LANES), F32), jax.ShapeDtypeStruct((t, LANES), F32)),
        grid=(t // tm,),
        in_specs=[pl.BlockSpec((tm, 1), lambda i: (i, 0)), pl.BlockSpec((1, LANES), lambda i: (0, 0))],
        out_specs=(pl.BlockSpec((tm, LANES), lambda i: (i, 0)), pl.BlockSpec((tm, LANES), lambda i: (i, 0))),
        compiler_params=_cparams(("parallel",)),
        name="rope_tables",
    )(positions.reshape(t, 1), invf)


(FA_RT, FA_V, FA_AT, FA_KT, FA_BT) = range(5)
(FB_KH, FB_BH, FB_BONUS, FB_GATE) = range(4)
(FT_TA, FT_UV, FT_YV, FT_ARB) = range(4)
RWKV_GROUP = 256


def _segsum(x, seg):
    lane = lax.broadcasted_iota(jnp.int32, x.shape, 1)
    out = jnp.zeros_like(x)
    for h in range(x.shape[1] // seg):
        m = (lane >= h * seg) & (lane < (h + 1) * seg)
        s = jnp.sum(jnp.where(m, x, 0.0), axis=-1, keepdims=True)
        out = jnp.where(m, s, out)
    return out


def _rwkv_masks(chunk):
    w = RWKV_WIDTH
    r = lax.broadcasted_iota(jnp.int32, (w, w), 0)
    c = lax.broadcasted_iota(jnp.int32, (w, w), 1)
    return r, c, _div(r, chunk) == _div(c, chunk)


def _rwkv_features(xs, r0, prm, fa_ref, fb_ref, pc_ref, chunk):
    w0_ref, a0_ref, kk_ref, ka_ref, rk_ref, w2_ref, a2_ref, g2_ref = prm
    w = RWKV_WIDTH
    hd = RWKV_HEAD
    n = xs.shape[0]
    r = xs[:, 0:w]
    k = xs[:, w:2 * w]
    v = xs[:, 2 * w:3 * w]
    lor = xs[:, 3 * w:]
    ld = -math.exp(-0.5) * _sigmoid(w0_ref[...] + _bdot(jnp.tanh(lor), w2_ref[...]))
    a = _sigmoid(a0_ref[...] + _bdot(lor, a2_ref[...]))
    g = _bdot(_sigmoid(lor), g2_ref[...])
    kk = k * kk_ref[...]
    kk = kk / jnp.maximum(jnp.sqrt(_segsum(kk * kk, hd)), 1e-12)
    k2 = k * (1.0 + (a - 1.0) * ka_ref[...])
    kb = kk * a
    bonus = _segsum(r * k2 * rk_ref[...], hd) * v

    assert n <= RWKV_WIDTH
    ri, ci, same_chunk = _rwkv_masks(chunk)
    tri = jnp.where(same_chunk & (ci <= ri), 1.0, 0.0)[0:n, 0:n]
    cl = _exact_left_dot(tri, ld)
    units = n // chunk
    cl_last = jnp.concatenate(
        [jnp.broadcast_to(cl[(u + 1) * chunk - 1:(u + 1) * chunk, :], (chunk, w)) for u in range(units)], axis=0)
    e_neg = jnp.exp(-cl)
    e_end = jnp.exp(cl_last - cl)
    rows = slice(r0, r0 + n)

    def put(ref, sec, val):
        ref[rows, sec * w:(sec + 1) * w] = val.astype(ref.dtype)

    put(fa_ref, FA_RT, r * jnp.exp(cl))
    put(fa_ref, FA_V, v)
    put(fa_ref, FA_AT, kk * jnp.exp(cl - ld))
    put(fa_ref, FA_KT, k2 * e_neg)
    put(fa_ref, FA_BT, kb * e_neg)
    put(fb_ref, FB_KH, k2 * e_end)
    put(fb_ref, FB_BH, kb * e_end)
    put(fb_ref, FB_BONUS, bonus)
    put(fb_ref, FB_GATE, g)
    for u in range(units):
        c = r0 // chunk + u
        pc_ref[c:c + 1, :] = jnp.exp(cl_last[u * chunk:u * chunk + 1, :])


def _inproj_kernel(x_ref, g_ref, w_ref, cos_ref, sin_ref, qn_ref, kvn_ref, wq_ref, wkv_ref,
                   mu_ref, w0_ref, a0_ref, kk_ref, ka_ref, rk_ref, w2_ref, a2_ref, g2_ref,
                   q_ref, k_ref, v_ref, fa_ref, fb_ref, pc_ref, mlstm_ref, prev_sc, *, layer, chunk, tiles_per_seq):
    tm = x_ref.shape[0]
    g_ref, qn_ref, kvn_ref, mu_ref, w0_ref, a0_ref, kk_ref, ka_ref, rk_ref = _rows(
        layer, g_ref, qn_ref, kvn_ref, mu_ref, w0_ref, a0_ref, kk_ref, ka_ref, rk_ref)

    @pl.when(pl.program_id(0) % tiles_per_seq == 0)
    def _():
        prev_sc[...] = jnp.zeros(prev_sc.shape, F32)

    hb = _rms(x_ref[...], g_ref[...]).astype(BF16)
    rw = _dot_nt(hb, w_ref[MLA_IN:MLA_IN + RWKV_IN, :])
    row = lax.broadcasted_iota(jnp.int32, rw.shape, 0)
    shifted = jnp.where(row >= 1, pltpu.roll(rw, 1, 0), prev_sc[0:1, :])
    prev_sc[0:1, :] = rw[tm - 1:tm, :]
    xs = rw + (shifted - rw) * mu_ref[...]
    prm = (w0_ref, a0_ref, kk_ref, ka_ref, rk_ref, w2_ref, a2_ref, g2_ref)

    mla = _dot_nt(hb, w_ref[0:MLA_IN, :])
    groups = tm // RWKV_GROUP
    for gi in range(groups // 2):
        _rwkv_features(xs[gi * RWKV_GROUP:(gi + 1) * RWKV_GROUP], gi * RWKV_GROUP, prm, fa_ref, fb_ref, pc_ref, chunk)
    mlstm_ref[...] = _dot_nt(hb, w_ref[MLA_IN + RWKV_IN:, :])
    for gi in range(groups // 2, groups):
        _rwkv_features(xs[gi * RWKV_GROUP:(gi + 1) * RWKV_GROUP], gi * RWKV_GROUP, prm, fa_ref, fb_ref, pc_ref, chunk)

    cos = cos_ref[...]
    sin = sin_ref[...]
    scale = (MLA_NOPE + MLA_ROPE) ** -0.5 * math.log2(math.e)
    hw = MLA_HEADS * LANES
    cqn = _rms(mla[:, MLA_CQ:MLA_CKV], qn_ref[...]).astype(BF16)
    q = jnp.dot(cqn, wq_ref[...], preferred_element_type=F32)
    ckvn = _rms(mla[:, MLA_CKV:MLA_KPE], kvn_ref[...]).astype(BF16)
    kv = jnp.dot(ckvn, wkv_ref[...], preferred_element_type=F32)
    kp = (mla[:, MLA_KPE:MLA_KPER] * cos + mla[:, MLA_KPER:MLA_IN] * sin).astype(BF16)
    for h in range(MLA_HEADS):
        c0 = h * LANES
        pe = q[:, hw + c0:hw + c0 + LANES] * cos + q[:, 2 * hw + c0:2 * hw + c0 + LANES] * sin
        q_ref[:, h * MLA_QK:h * MLA_QK + LANES] = (q[:, c0:c0 + LANES] * scale).astype(BF16)
        q_ref[:, h * MLA_QK + LANES:(h + 1) * MLA_QK] = (pe * scale).astype(BF16)
        k_ref[:, h * MLA_QK:h * MLA_QK + LANES] = kv[:, c0:c0 + LANES].astype(BF16)
        k_ref[:, h * MLA_QK + LANES:(h + 1) * MLA_QK] = kp
    v_ref[...] = kv[:, hw:].astype(BF16)


def _inproj(l, x, seq, g, w, cos, sin, qn, kvn, wq, wkv, mu, w0, a0, k_k, k_a, r_k, w2p, a2p, g2p):
    t = x.shape[0]
    tm = TM_INPROJ
    chunk = RWKV_CHUNK
    assert seq % tm == 0 and tm % RWKV_GROUP == 0 and RWKV_GROUP % chunk == 0
    row = lambda width: pl.BlockSpec((tm, width), lambda i: (i, 0))
    rw = RWKV_WIDTH
    return pl.pallas_call(
        functools.partial(_inproj_kernel, layer=l, chunk=chunk, tiles_per_seq=seq // tm),
        out_shape=(jax.ShapeDtypeStruct((t, MLA_HEADS * MLA_QK), BF16),
                   jax.ShapeDtypeStruct((t, MLA_HEADS * MLA_QK), BF16),
                   jax.ShapeDtypeStruct((t, MLA_WIDTH), BF16),
                   jax.ShapeDtypeStruct((t, 5 * rw), BF16),
                   jax.ShapeDtypeStruct((t, 4 * rw), BF16),
                   jax.ShapeDtypeStruct((t // chunk, rw), F32),
                   jax.ShapeDtypeStruct((t, MLSTM_IN), F32)),
        grid=(t // tm,),
        in_specs=[row(D_MODEL), _resident(g), _resident(w, l), row(LANES), row(LANES), _resident(qn),
                  _resident(kvn), _resident(wq, l), _resident(wkv, l),
                  _resident(mu), _resident(w0), _resident(a0), _resident(k_k), _resident(k_a),
                  _resident(r_k), _resident(w2p, l), _resident(a2p, l), _resident(g2p, l)],
        out_specs=(row(MLA_HEADS * MLA_QK), row(MLA_HEADS * MLA_QK), row(MLA_WIDTH), row(5 * rw), row(4 * rw),
                   pl.BlockSpec((tm // chunk, rw), lambda i: (i, 0)), row(MLSTM_IN)),
        scratch_shapes=[pltpu.VMEM((SUBLANES, RWKV_IN), F32)],
        compiler_params=_cparams(("arbitrary",)),
        name="inproj",
    )(x, g, w, cos, sin, qn, kvn, wq, wkv, mu, w0, a0, k_k, k_a, r_k, w2p, a2p, g2p)


def _attn_kernel(q_ref, k_ref, v_ref, g_ref, o_ref, m_sc, l_sc, acc_sc, sa_sc, sb_sc, *, layer, tq, tk, hp):
    i = pl.program_id(2)
    m_sc[...] = jnp.full(m_sc.shape, -jnp.inf, F32)
    l_sc[...] = jnp.zeros(l_sc.shape, F32)
    acc_sc[...] = jnp.zeros(acc_sc.shape, F32)
    sub = m_sc.shape[1]
    hs = range(hp)
    (g_ref,) = _rows(layer, g_ref)

    def produce(j, s_ref):
        off = pl.multiple_of(j * tk, tk)
        for h in hs:
            s_ref[h] = lax.dot_general(k_ref[0, pl.ds(off, tk), h * MLA_QK:(h + 1) * MLA_QK],
                                       q_ref[0, :, h * MLA_QK:(h + 1) * MLA_QK],
                                       (((1,), (1,)), ((), ())), preferred_element_type=F32)

    def consume(j, s_ref, masked):
        off = pl.multiple_of(j * tk, tk)
        s = [s_ref[h] for h in hs]
        if masked:
            keys = lax.broadcasted_iota(jnp.int32, (tk, tq), 0)
            queries = lax.broadcasted_iota(jnp.int32, (tk, tq), 1)
            s = [jnp.where(keys <= queries, s[h], -jnp.inf) for h in hs]
        m_old = [m_sc[h][0:1] for h in hs]
        m_new = [jnp.maximum(m_old[h], jnp.max(s[h], axis=0, keepdims=True)) for h in hs]
        p = [jnp.exp2(s[h] - m_new[h]) for h in hs]
        alpha = [jnp.exp2(m_old[h] - m_new[h]) for h in hs]
        for h in hs:
            l_new = alpha[h] * l_sc[h][0:1] + jnp.sum(p[h], axis=0, keepdims=True)
            l_sc[h] = jnp.broadcast_to(l_new, (sub, tq))
            m_sc[h] = jnp.broadcast_to(m_new[h], (sub, tq))
        pv = [lax.dot_general(v_ref[0, pl.ds(off, tk), h * MLA_VDIM:(h + 1) * MLA_VDIM], p[h].astype(BF16),
                              (((0,), (0,)), ((), ())), preferred_element_type=F32) for h in hs]
        for h in hs:
            acc_sc[h] = alpha[h] * acc_sc[h] + pv[h]

    def pair(jj, c):
        j = 2 * jj
        produce(j + 1, sb_sc)
        consume(j, sa_sc, False)
        produce(j + 2, sa_sc)
        consume(j + 1, sb_sc, False)
        return c

    produce(0, sa_sc)
    lax.fori_loop(0, i // 2, pair, 0)

    @pl.when(i % 2 == 0)
    def _():
        consume(i, sa_sc, True)

    @pl.when(i % 2 == 1)
    def _():
        produce(i, sb_sc)
        consume(i - 1, sa_sc, False)
        consume(i, sb_sc, True)

    for h in hs:
        o = acc_sc[h] / l_sc[h][0:1]
        o = o * lax.rsqrt(jnp.mean(o * o, axis=0, keepdims=True) + NORM_EPS)
        o_ref[0, :, h * MLA_VDIM:(h + 1) * MLA_VDIM] = (
            jnp.transpose(o) * g_ref[:, h * MLA_VDIM:(h + 1) * MLA_VDIM]).astype(o_ref.dtype)


def _mla_attention(l, q, k, v, g, batch, seq):
    tq, tk, hp = TQ_ATTN, TK_ATTN, HP_ATTN
    assert tq == tk and MLA_VDIM == LANES and hp == MLA_HEADS
    q = q.reshape(batch, seq, MLA_HEADS * MLA_QK)
    k = k.reshape(batch, seq, MLA_HEADS * MLA_QK)
    v = v.reshape(batch, seq, MLA_WIDTH)
    out = pl.pallas_call(
        functools.partial(_attn_kernel, layer=l, tq=tq, tk=tk, hp=hp),
        out_shape=jax.ShapeDtypeStruct((batch, seq, MLA_WIDTH), BF16),
        grid=(batch, MLA_HEADS // hp, seq // tq),
        in_specs=[pl.BlockSpec((1, tq, hp * MLA_QK), lambda b, h, i: (b, i, h)),
                  pl.BlockSpec((1, seq, hp * MLA_QK), lambda b, h, i: (b, 0, h)),
                  pl.BlockSpec((1, seq, hp * MLA_VDIM), lambda b, h, i: (b, 0, h)),
                  _resident(g)],
        out_specs=pl.BlockSpec((1, tq, hp * MLA_VDIM), lambda b, h, i: (b, i, h)),
        scratch_shapes=[pltpu.VMEM((hp, SUBLANES, tq), F32), pltpu.VMEM((hp, SUBLANES, tq), F32),
                        pltpu.VMEM((hp, MLA_VDIM, tq), F32),
                        pltpu.VMEM((hp, tk, tq), F32), pltpu.VMEM((hp, tk, tq), F32)],
        compiler_params=_cparams(("parallel", "parallel", "arbitrary")),
        name="mla_attention",
    )(q, k, v, g)
    return out.reshape(batch * seq, MLA_WIDTH)


def _tile_heads(z):
    return jnp.concatenate([z] * RWKV_HEADS, axis=0)


def _rwkv_chunk_kernel(fa_ref, ft_ref, *, nb, chunk, cps):
    seg = cps * chunk
    n = nb * seg
    w = RWKV_WIDTH
    hd = RWKV_HEAD
    x = fa_ref[...].reshape(n, fa_ref.shape[-1])
    r16, v16, a16, k16, b16 = (x[:, s * w:(s + 1) * w] for s in (FA_RT, FA_V, FA_AT, FA_KT, FA_BT))

    ri, ci, same_chunk = _rwkv_masks(chunk)
    bd = _div(ri, chunk) == _div(ci, hd)
    rt = lax.broadcasted_iota(jnp.int32, (chunk, w), 0)
    cs = _mod(lax.broadcasted_iota(jnp.int32, (chunk, w), 1), chunk)
    strict = cs < rt
    incl = cs <= rt
    c16 = _div(rt, 16) == _div(cs, 16)
    c32 = _div(rt, 32) == _div(cs, 32)
    eye = jnp.where(rt == cs, 1.0, 0.0)
    units = nb * cps

    def block_diag(z):
        return jnp.where(same_chunk, _tile_heads(z), 0.0)

    def put(u, sec, val):
        b, j = divmod(u, cps)
        ft_ref[b, j * chunk:(j + 1) * chunk, sec * w:(sec + 1) * w] = val.astype(ft_ref.dtype)

    def mm(x, y):
        return jnp.dot(x, y, preferred_element_type=F32)

    us = range(units)
    sls = [slice(u * chunk, (u + 1) * chunk) for u in us]
    a_st = [jnp.where(bd, _tile_heads(a16[sl]), 0.0) for sl in sls]
    v_st = [jnp.where(bd, _tile_heads(v16[sl]), 0.0) for sl in sls]
    kb_t = [jnp.concatenate([jnp.transpose(jnp.where(bd, _tile_heads(k16[sl]), 0.0)),
                             jnp.transpose(jnp.where(bd, _tile_heads(b16[sl]), 0.0))], axis=1) for sl in sls]
    sc = [mm(jnp.concatenate([a16[sls[u]], r16[sls[u]]], axis=0), kb_t[u]) for u in us]
    l_ab = [jnp.where(strict, sc[u][0:chunk, w:], 0.0) for u in us]
    l_ak = [jnp.where(strict, sc[u][0:chunk, 0:w], 0.0).astype(BF16) for u in us]
    a_rk = [jnp.where(incl, sc[u][chunk:, 0:w], 0.0).astype(BF16) for u in us]
    for u in us:
        put(u, FT_ARB, jnp.where(incl, sc[u][chunk:, w:], 0.0))
    xm = [-jnp.where(c16, l_ab[u], 0.0) for u in us]
    xm16 = [z.astype(BF16) for z in xm]
    off32 = [block_diag(jnp.where(c32 & jnp.logical_not(c16), l_ab[u], 0.0).astype(BF16)) for u in us]
    off64 = [block_diag(jnp.where(jnp.logical_not(c32), l_ab[u], 0.0).astype(BF16)) for u in us]
    x2 = [mm(xm16[u], block_diag(xm16[u])).astype(BF16) for u in us]
    x2_bd = [block_diag(z) for z in x2]
    lv = [mm(jnp.concatenate([l_ak[u], a_rk[u]], axis=0), v_st[u]) for u in us]
    wv = [block_diag(lv[u][0:chunk].astype(BF16)) for u in us]
    t_lo = [eye + xm[u] for u in us]
    tx = [mm(jnp.concatenate([t_lo[u].astype(BF16), x2[u]], axis=0), x2_bd[u]) for u in us]
    t_lo = [(t_lo[u] + tx[u][0:chunk]).astype(BF16) for u in us]
    x4 = [tx[u][chunk:] for u in us]
    x4b = [z.astype(BF16) for z in x4]
    x4_bd = [block_diag(z) for z in x4b]
    for u in us:
        put(u, FT_YV, lv[u][chunk:])
    x8_bd = [block_diag(mm(x4b[u], x4_bd[u]).astype(BF16)) for u in us]
    t_hi = [eye + x4[u] for u in us]
    t_hi = [block_diag((t_hi[u] + mm(t_hi[u].astype(BF16), x8_bd[u])).astype(BF16)) for u in us]
    t_inv = [mm(t_lo[u], t_hi[u]) for u in us]
    for off in (off32, off64):
        tb = [z.astype(BF16) for z in t_inv]
        mid = [mm(tb[u], off[u]).astype(BF16) for u in us]
        t_inv = [t_inv[u] - mm(mid[u], block_diag(tb[u])) for u in us]
    tb = [z.astype(BF16) for z in t_inv]
    for u in us:
        put(u, FT_TA, mm(tb[u], a_st[u]))
    for u in us:
        put(u, FT_UV, mm(tb[u], wv[u]))


def _rwkv_scan_kernel(fa_ref, fb_ref, ft_ref, pc_ref, lnw_ref, lnb_ref, o_ref, state_sc, *, layer, nb, chunk, cps):
    seg = cps * chunk
    n = nb * seg
    w = RWKV_WIDTH
    hd = RWKV_HEAD

    @pl.when(pl.program_id(0) == 0)
    def _():
        state_sc[...] = jnp.zeros(state_sc.shape, F32)

    lnw_ref, lnb_ref = _rows(layer, lnw_ref, lnb_ref)
    ri, ci, _ = _rwkv_masks(chunk)
    bd = _div(ri, chunk) == _div(ci, hd)
    bdv = _div(ri, hd) == _div(ci, hd)
    ones_bd = jnp.where(bdv, 1.0, 0.0).astype(BF16)

    def sec(ref, b, j, s):
        return ref[b, j * chunk:(j + 1) * chunk, s * w:(s + 1) * w]

    ys = [[None] * cps for _ in range(nb)]
    bs = range(nb)
    for j in range(cps):
        gs = [state_sc[b] for b in bs]
        p1 = [lax.dot_general(jnp.concatenate([sec(ft_ref, b, j, FT_TA), sec(fa_ref, b, j, FA_RT)], axis=0),
                              gs[b].astype(BF16), (((1,), (1,)), ((), ())), preferred_element_type=F32)
              for b in bs]
        u = [(p1[b][0:chunk] + sec(ft_ref, b, j, FT_UV).astype(F32)).astype(BF16) for b in bs]
        upd = [lax.dot_general(jnp.concatenate([sec(fa_ref, b, j, FA_V), -u[b]], axis=0),
                               jnp.concatenate([sec(fb_ref, b, j, FB_KH), sec(fb_ref, b, j, FB_BH)], axis=0),
                               (((0,), (0,)), ((), ())), preferred_element_type=F32) for b in bs]
        for b in bs:
            state_sc[b] = gs[b] * pc_ref[b, pl.ds(pl.program_id(0) * cps + j, 1), :] + jnp.where(bdv, upd[b], 0.0)
        for b in bs:
            u_st = jnp.where(bd, _tile_heads(u[b]), 0.0)
            ys[b][j] = (p1[b][chunk:] + sec(ft_ref, b, j, FT_YV).astype(F32)
                        - jnp.dot(sec(ft_ref, b, j, FT_ARB), u_st, preferred_element_type=F32))

    y = jnp.concatenate([ys[b][j] for b in range(nb) for j in range(cps)], axis=0)
    mean = _exact_right_dot(y, ones_bd) * (1.0 / hd)
    d = y - mean
    var = _exact_right_dot(d * d, ones_bd) * (1.0 / hd)
    yn = d * lax.rsqrt(var + RWKV_LN_EPS) * lnw_ref[...] + lnb_ref[...]
    bonus = fb_ref[:, :, FB_BONUS * w:(FB_BONUS + 1) * w].reshape(n, w).astype(F32)
    gate = fb_ref[:, :, FB_GATE * w:(FB_GATE + 1) * w].reshape(n, w).astype(F32)
    o_ref[...] = ((yn + bonus) * gate).astype(o_ref.dtype).reshape(o_ref.shape)


def _rwkv(l, fa, fb, pc, batch, seq, ln_w, ln_b):
    chunk = RWKV_CHUNK
    w = RWKV_WIDTH
    assert RWKV_HEADS * chunk == w
    fa = fa.reshape(batch, seq, 5 * w)
    fb = fb.reshape(batch, seq, 4 * w)
    pc = pc.reshape(batch, seq // chunk, w)
    cps = RWKV_PREP_CHUNKS
    seg = cps * chunk
    ft = pl.pallas_call(
        functools.partial(_rwkv_chunk_kernel, nb=batch, chunk=chunk, cps=cps),
        out_shape=jax.ShapeDtypeStruct((batch, seq, 4 * w), BF16),
        grid=(seq // seg,),
        in_specs=[pl.BlockSpec((batch, seg, 5 * w), lambda c: (0, c, 0))],
        out_specs=pl.BlockSpec((batch, seg, 4 * w), lambda c: (0, c, 0)),
        compiler_params=_cparams(("parallel",)),
        name="rwkv7_chunk",
    )(fa)
    cps = RWKV_SCAN_CHUNKS
    seg = cps * chunk
    out = pl.pallas_call(
        functools.partial(_rwkv_scan_kernel, layer=l, nb=batch, chunk=chunk, cps=cps),
        out_shape=jax.ShapeDtypeStruct((batch, seq, w), BF16),
        grid=(seq // seg,),
        in_specs=[pl.BlockSpec((batch, seg, 2 * w), lambda c: (0, c, 0)),
                  pl.BlockSpec((batch, seg, 4 * w), lambda c: (0, c, 0)),
                  pl.BlockSpec((batch, seg, 4 * w), lambda c: (0, c, 0)),
                  _resident(pc),
                  _resident(ln_w), _resident(ln_b)],
        out_specs=pl.BlockSpec((batch, seg, w), lambda c: (0, c, 0)),
        scratch_shapes=[pltpu.VMEM((batch, w, w), F32)],
        compiler_params=_cparams(("arbitrary",)),
        name="rwkv7_scan",
    )(fa, fb, ft, pc, ln_w, ln_b)
    return out.reshape(batch * seq, w)


ML_QK = 0
ML_V = 2 * MLSTM_HEADS * MLSTM_QK
ML_O = ML_V + MLSTM_WIDTH
ML_I = ML_O + MLSTM_WIDTH
ML_F = ML_I + LANES


def _cummax_rows(x):
    n = x.shape[0]
    row = lax.broadcasted_iota(jnp.int32, x.shape, 0)
    sh = 1
    while sh < n:
        x = jnp.maximum(x, jnp.where(row >= sh, pltpu.roll(x, sh, 0), -jnp.inf))
        sh *= 2
    return x


def _mlstm_kernel(x_ref, cw_ref, cb_ref, ib_ref, fb_ref, on_ref, o_ref,
                  prev_sc, c_sc, n_sc, m_sc, *, layer, nb, chunk):
    n = nb * chunk
    nh = MLSTM_HEADS
    dk = MLSTM_QK
    dv = MLSTM_V
    qkw = nh * dk
    vw = MLSTM_WIDTH
    cb_ref, ib_ref, fb_ref, on_ref = _rows(layer, cb_ref, ib_ref, fb_ref, on_ref)

    @pl.when(pl.program_id(0) == 0)
    def _():
        prev_sc[...] = jnp.zeros(prev_sc.shape, F32)
        c_sc[...] = jnp.zeros(c_sc.shape, F32)
        n_sc[...] = jnp.zeros(n_sc.shape, F32)
        m_sc[...] = jnp.zeros(m_sc.shape, F32)

    x = x_ref[...].reshape(n, MLSTM_IN)
    qk_raw = x[:, ML_QK:ML_V]
    prev = prev_sc[...]
    conv = cb_ref[...] + qk_raw * cw_ref[MLSTM_CONV - 1:MLSTM_CONV, :]
    for s in range(1, MLSTM_CONV):
        conv = conv + _shift_rows(qk_raw, prev, s, chunk) * cw_ref[MLSTM_CONV - 1 - s:MLSTM_CONV - s, :]
    prev_sc[...] = qk_raw
    qk = conv * _sigmoid(conv)
    q_all = qk[:, 0:qkw] * (dk ** -0.5)
    k_all = qk[:, qkw:]
    v_all = x[:, ML_V:ML_O]
    o_pre = x[:, ML_O:ML_I]
    li_all = x[:, ML_I:ML_F] + ib_ref[...]
    lf_all = _log_sigmoid(x[:, ML_F:ML_F + LANES] + fb_ref[...])

    ri = lax.broadcasted_iota(jnp.int32, (chunk, chunk), 0)
    ci = lax.broadcasted_iota(jnp.int32, (chunk, chunk), 1)
    causal = ci <= ri
    tri = jnp.where(causal, 1.0, 0.0)
    lane_k = lax.broadcasted_iota(jnp.int32, (chunk, qkw), 1)
    lane_v = lax.broadcasted_iota(jnp.int32, (chunk, vw), 1)
    rc = lax.broadcasted_iota(jnp.int32, (qkw, vw), 0)
    cc = lax.broadcasted_iota(jnp.int32, (qkw, vw), 1)
    cmask = _div(rc, dk) == _div(cc, dv)
    expand_v = jnp.where(rc == _div(cc, dv), 1.0, 0.0).astype(BF16)
    rk = lax.broadcasted_iota(jnp.int32, (qkw, qkw), 0)
    ck = lax.broadcasted_iota(jnp.int32, (qkw, qkw), 1)
    expand_k = jnp.where(rk == _div(ck, dk), 1.0, 0.0).astype(BF16)
    gather_k = jnp.where(_div(rk, dk) == ck, 1.0, 0.0).astype(BF16)

    bs = range(nb)
    sls = [slice(b * chunk, (b + 1) * chunk) for b in bs]
    q = [q_all[sl] for sl in sls]
    k = [k_all[sl] for sl in sls]
    k16 = [z.astype(BF16) for z in k]
    v = [v_all[sl] for sl in sls]
    li = [li_all[sl] for sl in sls]
    c_old = [c_sc[b] for b in bs]
    n_old = [n_sc[b] for b in bs]
    m_prev = [m_sc[b] for b in bs]
    g = [_exact_left_dot(tri, lf_all[sl]) for sl in sls]
    lig = [li[b] - g[b] for b in bs]
    inter_log = [g[b] + m_prev[b] for b in bs]
    m_t = [jnp.maximum(inter_log[b], g[b] + _cummax_rows(lig[b])) for b in bs]
    inter_w = [jnp.exp(inter_log[b] - m_t[b]) for b in bs]
    log2e = math.log2(math.e)
    gm = [(g[b] - m_t[b]) * log2e for b in bs]
    lig_t = [jnp.transpose(z * log2e) for z in lig]
    qn = [_exact_right_dot(q[b] * n_old[b], gather_k, parts=2) for b in bs]
    q_c = [_bdot(q[b], c_old[b]) for b in bs]
    ssum = [jnp.zeros((chunk, LANES), F32) for _ in bs]
    num = [jnp.zeros((chunk, vw), F32) for _ in bs]
    for h in range(nh):
        mk = (lane_k >= h * dk) & (lane_k < (h + 1) * dk)
        mv = (lane_v >= h * dv) & (lane_v < (h + 1) * dv)
        qk_h = [lax.dot_general(jnp.where(mk, q[b], 0.0).astype(BF16), k16[b], (((1,), (1,)), ((), ())),
                                preferred_element_type=F32) for b in bs]
        d = [jnp.broadcast_to(gm[b][:, h:h + 1], (chunk, chunk)) + lig_t[b][h:h + 1, :] for b in bs]
        s = [qk_h[b] * jnp.exp2(jnp.where(causal, d[b], -jnp.inf)) for b in bs]
        ssum = [jnp.where(lane_k == h, jnp.sum(s[b], axis=-1, keepdims=True), ssum[b]) for b in bs]
        num = [num[b] + _bdot(s[b], jnp.where(mv, v[b], 0.0)) for b in bs]
    den = [inter_w[b] * qn[b] + ssum[b] for b in bs]
    rden = [1.0 / jnp.maximum(jnp.abs(den[b]), jnp.exp(-m_t[b])) for b in bs]
    g_last = [g[b][chunk - 1:chunk, :] for b in bs]
    a_all = [g_last[b] - g[b] + li[b] for b in bs]
    m_new = [jnp.maximum(g_last[b] + m_prev[b], jnp.max(a_all[b], axis=0, keepdims=True)) for b in bs]
    dec = [jnp.exp(g_last[b] + m_prev[b] - m_new[b]) for b in bs]
    wts = [jnp.exp(a_all[b] - m_new[b]) for b in bs]
    per_head = [jnp.concatenate([inter_w[b], rden[b], wts[b], jnp.broadcast_to(dec[b], (8, LANES))], axis=0)
                for b in bs]
    on_v = [_exact_right_dot(per_head[b], expand_v, parts=2) for b in bs]
    on_k = [_exact_right_dot(per_head[b][2 * chunk:], expand_k, parts=2) for b in bs]
    hs = [(on_v[b][0:chunk] * q_c[b] + num[b]) * on_v[b][chunk:2 * chunk] for b in bs]
    for b in bs:
        c_sc[b] = c_old[b] * on_v[b][3 * chunk:3 * chunk + 1] + jnp.where(
            cmask, _bdot_tn(k16[b], on_v[b][2 * chunk:3 * chunk] * v[b]), 0.0)
        n_sc[b] = n_old[b] * on_k[b][chunk:chunk + 1] + jnp.sum(on_k[b][0:chunk] * k[b], axis=0, keepdims=True)
        m_sc[b] = m_new[b]

    hh = jnp.concatenate(hs, axis=0)
    rv = lax.broadcasted_iota(jnp.int32, (vw, vw), 0)
    cv = lax.broadcasted_iota(jnp.int32, (vw, vw), 1)
    head_ones = jnp.where(_div(rv, dv) == _div(cv, dv), 1.0, 0.0).astype(BF16)
    ms = _exact_right_dot(hh * hh, head_ones, parts=2) * (1.0 / dv)
    out = hh * lax.rsqrt(ms + NORM_EPS) * on_ref[...] * _sigmoid(o_pre)
    o_ref[...] = out.astype(o_ref.dtype).reshape(o_ref.shape)


def _mlstm(l, x, batch, seq, cw, cb, ib, fb, on):
    chunk = MLSTM_CHUNK
    x = x.reshape(batch, seq, MLSTM_IN)
    out = pl.pallas_call(
        functools.partial(_mlstm_kernel, layer=l, nb=batch, chunk=chunk),
        out_shape=jax.ShapeDtypeStruct((batch, seq, MLSTM_WIDTH), BF16),
        grid=(seq // chunk,),
        in_specs=[pl.BlockSpec((batch, chunk, MLSTM_IN), lambda c: (0, c, 0)),
                  _resident(cw, l), _resident(cb), _resident(ib), _resident(fb), _resident(on)],
        out_specs=pl.BlockSpec((batch, chunk, MLSTM_WIDTH), lambda c: (0, c, 0)),
        scratch_shapes=[pltpu.VMEM((batch * chunk, 2 * MLSTM_HEADS * MLSTM_QK), F32),
                        pltpu.VMEM((batch, MLSTM_HEADS * MLSTM_QK, MLSTM_WIDTH), F32),
                        pltpu.VMEM((batch, 1, MLSTM_HEADS * MLSTM_QK), F32),
                        pltpu.VMEM((batch, 1, LANES), F32)],
        compiler_params=_cparams(("arbitrary",)),
        name="mlstm",
    )(x, cw, cb, ib, fb, on)
    return out.reshape(batch * seq, MLSTM_WIDTH)


def _ffn_kernel(x_ref, ya_ref, yb_ref, yc_ref, wo_ref, g_ref, wg_ref, wu_ref, wd_ref, fg_ref,
                o_ref, act_sc, *, layer, final_norm, tf):
    (g_ref,) = _rows(layer, g_ref)
    y = jnp.concatenate([ya_ref[...], yb_ref[...], yc_ref[...]], axis=-1)
    x1 = x_ref[...] + jnp.dot(y, wo_ref[...], preferred_element_type=F32)
    h = _rms(x1, g_ref[...]).astype(BF16)
    for c in range(D_FF // tf):
        gate = jnp.dot(h, wg_ref[:, c * tf:(c + 1) * tf], preferred_element_type=F32)
        up = jnp.dot(h, wu_ref[:, c * tf:(c + 1) * tf], preferred_element_type=F32)
        act_sc[:, c * tf:(c + 1) * tf] = (gate * _sigmoid(gate) * up).astype(BF16)
    out = x1 + jnp.dot(act_sc[...], wd_ref[...], preferred_element_type=F32)
    if final_norm:
        out = _rms(out, fg_ref[...])
    o_ref[...] = out


def _out_ffn(l, x, ya, yb, yc, wo, g, wg, wu, wd, fg, final_norm):
    t = x.shape[0]
    tm, tf = TM_FFN, TF_FFN
    row = lambda w: pl.BlockSpec((tm, w), lambda i: (i, 0))
    return pl.pallas_call(
        functools.partial(_ffn_kernel, layer=l, final_norm=final_norm, tf=tf),
        out_shape=jax.ShapeDtypeStruct((t, D_MODEL), F32),
        grid=(t // tm,),
        in_specs=[row(D_MODEL), row(MLA_WIDTH), row(RWKV_WIDTH), row(MLSTM_WIDTH), _resident(wo, l),
                  _resident(g), _resident(wg, l), _resident(wu, l), _resident(wd, l), _resident(fg)],
        out_specs=row(D_MODEL),
        scratch_shapes=[pltpu.VMEM((tm, D_FF), BF16)],
        compiler_params=_cparams(("parallel",)),
        name="out_ffn",
    )(x, ya, yb, yc, wo, g, wg, wu, wd, fg)


def _pad_cols(w, width):
    return jnp.pad(w, [(0, 0)] * (w.ndim - 1) + [(0, width - w.shape[-1])])


def _pad_rows(w, height):
    return jnp.pad(w, [(0, 0)] * (w.ndim - 2) + [(0, height - w.shape[-2]), (0, 0)])


def _rot_half_cols(w):
    half = w.shape[-1] // 2
    return jnp.concatenate([-w[..., half:], w[..., :half]], axis=-1)


def _stacked_weights(w_in, mla_w_uq, mla_w_ukv, rwkv_w2, rwkv_a2, rwkv_g2):
    depth = w_in.shape[0]
    wt = jnp.swapaxes(w_in, 1, 2)
    c_q, c_kv, k_pe = wt[:, 0:256], wt[:, 256:512], wt[:, 512:576]
    rw = wt[:, 576:1472]
    ml = wt[:, 1472:2248]
    k_pe_rot = jnp.swapaxes(_rot_half_cols(jnp.swapaxes(k_pe, 1, 2)), 1, 2)
    w_mla = jnp.concatenate([c_q, c_kv, _pad_rows(k_pe, LANES), _pad_rows(k_pe_rot, LANES)], axis=1)
    w_mlstm = jnp.concatenate([ml[:, 0:256], ml[:, 256:512], ml[:, 520:776],
                               _pad_rows(ml[:, 512:516], LANES), _pad_rows(ml[:, 516:520], LANES)], axis=1)
    w_all = jnp.concatenate([w_mla, rw, w_mlstm], axis=1).astype(BF16)

    uq = mla_w_uq.reshape(depth, MLA_Q_LORA, MLA_HEADS, MLA_NOPE + MLA_ROPE)
    nope = uq[..., :MLA_NOPE].reshape(depth, MLA_Q_LORA, MLA_HEADS * MLA_NOPE)
    pe = _pad_cols(uq[..., MLA_NOPE:], LANES).reshape(depth, MLA_Q_LORA, MLA_HEADS * LANES)
    per = _pad_cols(_rot_half_cols(uq[..., MLA_NOPE:]), LANES).reshape(depth, MLA_Q_LORA, MLA_HEADS * LANES)
    wq = jnp.concatenate([nope, pe, per], axis=-1).astype(BF16)
    ukv = mla_w_ukv.reshape(depth, MLA_KV_LORA, MLA_HEADS, MLA_NOPE + MLA_VDIM)
    wkv = jnp.concatenate([ukv[..., :MLA_NOPE].reshape(depth, MLA_KV_LORA, -1),
                           ukv[..., MLA_NOPE:].reshape(depth, MLA_KV_LORA, -1)], axis=-1).astype(BF16)

    rows = lambda before, wt: jnp.pad(wt, ((0, 0), (before, LANES - before - wt.shape[1]), (0, 0))).astype(BF16)
    w2p = rows(0, rwkv_w2)
    a2p = rows(RWKV_DECAY_LORA, rwkv_a2)
    g2p = rows(RWKV_DECAY_LORA + RWKV_AAA_LORA, rwkv_g2)
    return w_all, wq, wkv, w2p, a2p, g2p


def kernel(x, positions, mix_norm, w_in, mla_q_norm, mla_w_uq, mla_kv_norm, mla_w_ukv, mla_out_norm, rwkv_mu, rwkv_w0, rwkv_w2, rwkv_a0, rwkv_a2, rwkv_g2, rwkv_k_k, rwkv_k_a, rwkv_r_k, rwkv_ln_w, rwkv_ln_b, mlstm_conv_w, mlstm_conv_b, mlstm_i_bias, mlstm_f_bias, mlstm_out_norm, w_out, ffn_norm, w_gate, w_up, w_down, final_norm):
    batch, seq, _ = x.shape
    depth = w_in.shape[0]
    xt = x.reshape(batch * seq, D_MODEL)
    cos, sin = _rope_tables(positions)
    w_all, wq, wkv, w2p, a2p, g2p = _stacked_weights(w_in, mla_w_uq, mla_w_ukv, rwkv_w2, rwkv_a2, rwkv_g2)
    wo, wg, wu, wd = (w.astype(BF16) for w in (w_out, w_gate, w_up, w_down))
    ml_ib = _pad_cols(mlstm_i_bias, LANES)
    ml_fb = _pad_cols(mlstm_f_bias, LANES)
    for l in range(depth):
        q, k, v, fa, fb, pc, mlstm_in = _inproj(
            l, xt, seq, mix_norm, w_all, cos, sin, mla_q_norm, mla_kv_norm, wq, wkv,
            rwkv_mu, rwkv_w0, rwkv_a0, rwkv_k_k, rwkv_k_a, rwkv_r_k,
            w2p, a2p, g2p)
        y_mla = _mla_attention(l, q, k, v, mla_out_norm, batch, seq)
        y_rwkv = _rwkv(l, fa, fb, pc, batch, seq, rwkv_ln_w, rwkv_ln_b)
        y_mlstm = _mlstm(l, mlstm_in, batch, seq, mlstm_conv_w, mlstm_conv_b, ml_ib, ml_fb,
                         mlstm_out_norm)
        xt = _out_ffn(l, xt, y_mla, y_rwkv, y_mlstm, wo, ffn_norm, wg, wu, wd,
                      final_norm.reshape(1, -1), final_norm=(l == depth - 1))
    return xt.reshape(batch, seq, D_MODEL)
```

```python
import functools
import math

import jax
import jax.numpy as jnp
from jax import lax
from jax.experimental import pallas as pl
from jax.experimental.pallas import tpu as pltpu

F32 = jnp.float32
BF16 = jnp.bfloat16

D_MODEL = 1024
DEPTH = 2
MLA_HEADS = 4
MLA_NOPE = 128
MLA_ROPE = 64
MLA_VDIM = 128
MLA_Q_LORA = 256
MLA_KV_LORA = 256
MLA_WIDTH = MLA_HEADS * MLA_VDIM
MLA_QK = 256
ROPE_THETA = 10000.0
RWKV_HEADS = 4
RWKV_HEAD = 64
RWKV_WIDTH = 256
RWKV_DECAY_LORA = 32
RWKV_AAA_LORA = 32
RWKV_GATE_LORA = 64
RWKV_IN = 3 * RWKV_WIDTH + 128
RWKV_LN_EPS = 64e-5
MLSTM_HEADS = 4
MLSTM_QK = 32
MLSTM_V = 64
MLSTM_WIDTH = 256
MLSTM_CONV = 4
MLSTM_IN = 1024
D_FF = 2816
NORM_EPS = 1e-6
LANES = 128
SUBLANES = 8

MLA_CQ = 0
MLA_CKV = MLA_CQ + MLA_Q_LORA
MLA_KPE = MLA_CKV + MLA_KV_LORA
MLA_KPER = MLA_KPE + LANES
MLA_IN = MLA_KPER + LANES

TM_INPROJ = 512
TQ_ATTN = 512
TK_ATTN = 512
HP_ATTN = 4
RWKV_CHUNK = 64
RWKV_PREP_CHUNKS = 2
MLSTM_CHUNK = 256
TM_FFN = 512
TF_FFN = 256
VMEM_LIMIT = 56 * 1024 * 1024


def _cparams(sem):
    return pltpu.CompilerParams(dimension_semantics=sem, vmem_limit_bytes=VMEM_LIMIT)


def _resident(a, layer=None):
    if layer is None:
        nd = a.ndim
        return pl.BlockSpec(a.shape, lambda *_: (0,) * nd, pipeline_mode=pl.Buffered(1))
    nd = a.ndim - 1
    return pl.BlockSpec((None,) + a.shape[1:], lambda *_: (layer,) + (0,) * nd, pipeline_mode=pl.Buffered(1))


def _rows(layer, *refs):
    return [r.at[layer:layer + 1] for r in refs]


def _bdot(a, b):
    return jnp.dot(a.astype(BF16), b.astype(BF16), preferred_element_type=F32)


def _dot_nt(a, b):
    return lax.dot_general(a, b, (((1,), (1,)), ((), ())), preferred_element_type=F32)


def _bdot_nt(a, b):
    return lax.dot_general(a.astype(BF16), b.astype(BF16), (((1,), (1,)), ((), ())),
                           preferred_element_type=F32)


def _bdot_tn(a, b):
    return lax.dot_general(a.astype(BF16), b.astype(BF16), (((0,), (0,)), ((), ())),
                           preferred_element_type=F32)


def _split3(x):
    h = x.astype(BF16)
    r1 = x - h.astype(F32)
    m = r1.astype(BF16)
    lo = (r1 - m.astype(F32)).astype(BF16)
    return h, m, lo


def _exact_left_dot(sel, x):
    h, m, lo = _split3(x)
    s = sel.astype(BF16)
    return (jnp.dot(s, h, preferred_element_type=F32) + jnp.dot(s, m, preferred_element_type=F32)
            + jnp.dot(s, lo, preferred_element_type=F32))


def _exact_right_dot(x, sel, parts=3):
    pieces = _split3(x)[:parts]
    s = sel.astype(BF16)
    out = jnp.dot(pieces[0], s, preferred_element_type=F32)
    for p in pieces[1:]:
        out = out + jnp.dot(p, s, preferred_element_type=F32)
    return out


def _rms(x, g):
    return x * lax.rsqrt(jnp.mean(x * x, axis=-1, keepdims=True) + NORM_EPS) * g


def _sigmoid(x):
    return 1.0 / (1.0 + jnp.exp(-x))


def _log_sigmoid(x):
    return jnp.minimum(x, 0.0) - jnp.log1p(jnp.exp(-jnp.abs(x)))


def _div(x, d):
    assert d & (d - 1) == 0
    return lax.shift_right_logical(x, d.bit_length() - 1)


def _mod(x, d):
    assert d & (d - 1) == 0
    return lax.bitwise_and(x, d - 1)


def _shift_rows(x, prev, s, chunk):
    n = x.shape[0]
    row = lax.broadcasted_iota(jnp.int32, x.shape, 0)
    return jnp.where(_mod(row, chunk) >= s, pltpu.roll(x, s, 0), pltpu.roll(prev, n - chunk + s, 0))


def _rope_kernel(pos_ref, invf_ref, cos_ref, sin_ref):
    ang = pos_ref[...].astype(F32) * invf_ref[...]
    cos_ref[...] = jnp.cos(ang)
    sin_ref[...] = jnp.sin(ang)


def _rope_tables(positions):
    t = positions.size
    tm = min(1024, t)
    inv_freq = ROPE_THETA ** (-jnp.arange(0, MLA_ROPE, 2, dtype=F32) / MLA_ROPE)
    invf = jnp.tile(inv_freq, LANES // (MLA_ROPE // 2))[None, :]
    return pl.pallas_call(
        _rope_kernel,
        out_shape=(jax.ShapeDtypeStruct((t, LANES), F32), jax.ShapeDtypeStruct((t, LANES), F32)),
        grid=(t // tm,),
        in_specs=[pl.BlockSpec((tm, 1), lambda i: (i, 0)), pl.BlockSpec((1, LANES), lambda i: (0, 0))],
        out_specs=(pl.BlockSpec((tm, LANES), lambda i: (i, 0)), pl.BlockSpec((tm, LANES), lambda i: (i, 0))),
        compiler_params=_cparams(("parallel",)),
        name="rope_tables",
    )(positions.reshape(t, 1), invf)


(FA_RT, FA_V, FA_AT, FA_KT, FA_BT) = range(5)
(FB_KH, FB_BH, FB_BONUS, FB_GATE) = range(4)
(FT_TA, FT_UV, FT_YV, FT_ARB) = range(4)
RWKV_GROUP = 256


def _segsum(x, seg):
    lane = lax.broadcasted_iota(jnp.int32, x.shape, 1)
    out = jnp.zeros_like(x)
    for h in range(x.shape[1] // seg):
        m = (lane >= h * seg) & (lane < (h + 1) * seg)
        s = jnp.sum(jnp.where(m, x, 0.0), axis=-1, keepdims=True)
        out = jnp.where(m, s, out)
    return out


def _rwkv_masks(chunk):
    w = RWKV_WIDTH
    r = lax.broadcasted_iota(jnp.int32, (w, w), 0)
    c = lax.broadcasted_iota(jnp.int32, (w, w), 1)
    return r, c, _div(r, chunk) == _div(c, chunk)


def _rwkv_features(xs, r0, prm, fa_ref, fb_ref, pc_ref, chunk):
    w0_ref, a0_ref, kk_ref, ka_ref, rk_ref, w2_ref, a2_ref, g2_ref = prm
    w = RWKV_WIDTH
    hd = RWKV_HEAD
    n = xs.shape[0]
    r = xs[:, 0:w]
    k = xs[:, w:2 * w]
    v = xs[:, 2 * w:3 * w]
    lor = xs[:, 3 * w:]
    ld = -math.exp(-0.5) * _sigmoid(w0_ref[...] + _bdot(jnp.tanh(lor), w2_ref[...]))
    a = _sigmoid(a0_ref[...] + _bdot(lor, a2_ref[...]))
    g = _bdot(_sigmoid(lor), g2_ref[...])
    kk = k * kk_ref[...]
    kk = kk / jnp.maximum(jnp.sqrt(_segsum(kk * kk, hd)), 1e-12)
    k2 = k * (1.0 + (a - 1.0) * ka_ref[...])
    kb = kk * a
    bonus = _segsum(r * k2 * rk_ref[...], hd) * v

    assert n <= RWKV_WIDTH
    ri, ci, same_chunk = _rwkv_masks(chunk)
    tri = jnp.where(same_chunk & (ci <= ri), 1.0, 0.0)[0:n, 0:n]
    cl = _exact_left_dot(tri, ld)
    units = n // chunk
    cl_last = jnp.concatenate(
        [jnp.broadcast_to(cl[(u + 1) * chunk - 1:(u + 1) * chunk, :], (chunk, w)) for u in range(units)], axis=0)
    e_neg = jnp.exp(-cl)
    e_end = jnp.exp(cl_last - cl)
    rows = slice(r0, r0 + n)

    def put(ref, sec, val):
        ref[rows, sec * w:(sec + 1) * w] = val.astype(ref.dtype)

    put(fa_ref, FA_RT, r * jnp.exp(cl))
    put(fa_ref, FA_V, v)
    put(fa_ref, FA_AT, kk * jnp.exp(cl - ld))
    put(fa_ref, FA_KT, k2 * e_neg)
    put(fa_ref, FA_BT, kb * e_neg)
    put(fb_ref, FB_KH, k2 * e_end)
    put(fb_ref, FB_BH, kb * e_end)
    put(fb_ref, FB_BONUS, bonus)
    put(fb_ref, FB_GATE, g)
    for u in range(units):
        c = r0 // chunk + u
        pc_ref[c:c + 1, :] = jnp.exp(cl_last[u * chunk:u * chunk + 1, :])


def _inproj_kernel(x_ref, g_ref, w_ref, cos_ref, sin_ref, qn_ref, kvn_ref, wq_ref, wkv_ref,
                   mu_ref, w0_ref, a0_ref, kk_ref, ka_ref, rk_ref, w2_ref, a2_ref, g2_ref,
                   q_ref, k_ref, v_ref, fa_ref, fb_ref, pc_ref, mlstm_ref, prev_sc, *, layer, chunk, tiles_per_seq):
    tm = x_ref.shape[0]
    g_ref, qn_ref, kvn_ref, mu_ref, w0_ref, a0_ref, kk_ref, ka_ref, rk_ref = _rows(
        layer, g_ref, qn_ref, kvn_ref, mu_ref, w0_ref, a0_ref, kk_ref, ka_ref, rk_ref)

    @pl.when(pl.program_id(0) % tiles_per_seq == 0)
    def _():
        prev_sc[...] = jnp.zeros(prev_sc.shape, F32)

    hb = _rms(x_ref[...], g_ref[...]).astype(BF16)
    rw = _dot_nt(hb, w_ref[MLA_IN:MLA_IN + RWKV_IN, :])
    row = lax.broadcasted_iota(jnp.int32, rw.shape, 0)
    shifted = jnp.where(row >= 1, pltpu.roll(rw, 1, 0), prev_sc[0:1, :])
    prev_sc[0:1, :] = rw[tm - 1:tm, :]
    xs = rw + (shifted - rw) * mu_ref[...]
    prm = (w0_ref, a0_ref, kk_ref, ka_ref, rk_ref, w2_ref, a2_ref, g2_ref)

    mla = _dot_nt(hb, w_ref[0:MLA_IN, :])
    groups = tm // RWKV_GROUP
    for gi in range(groups // 2):
        _rwkv_features(xs[gi * RWKV_GROUP:(gi + 1) * RWKV_GROUP], gi * RWKV_GROUP, prm, fa_ref, fb_ref, pc_ref, chunk)
    mlstm_ref[...] = _dot_nt(hb, w_ref[MLA_IN + RWKV_IN:, :])
    for gi in range(groups // 2, groups):
        _rwkv_features(xs[gi * RWKV_GROUP:(gi + 1) * RWKV_GROUP], gi * RWKV_GROUP, prm, fa_ref, fb_ref, pc_ref, chunk)

    cos = cos_ref[...]
    sin = sin_ref[...]
    scale = (MLA_NOPE + MLA_ROPE) ** -0.5 * math.log2(math.e)
    hw = MLA_HEADS * LANES
    cqn = _rms(mla[:, MLA_CQ:MLA_CKV], qn_ref[...]).astype(BF16)
    q = jnp.dot(cqn, wq_ref[...], preferred_element_type=F32)
    ckvn = _rms(mla[:, MLA_CKV:MLA_KPE], kvn_ref[...]).astype(BF16)
    kv = jnp.dot(ckvn, wkv_ref[...], preferred_element_type=F32)
    kp = (mla[:, MLA_KPE:MLA_KPER] * cos + mla[:, MLA_KPER:MLA_IN] * sin).astype(BF16)
    for h in range(MLA_HEADS):
        c0 = h * LANES
        pe = q[:, hw + c0:hw + c0 + LANES] * cos + q[:, 2 * hw + c0:2 * hw + c0 + LANES] * sin
        q_ref[:, h * MLA_QK:h * MLA_QK + LANES] = (q[:, c0:c0 + LANES] * scale).astype(BF16)
        q_ref[:, h * MLA_QK + LANES:(h + 1) * MLA_QK] = (pe * scale).astype(BF16)
        k_ref[:, h * MLA_QK:h * MLA_QK + LANES] = kv[:, c0:c0 + LANES].astype(BF16)
        k_ref[:, h * MLA_QK + LANES:(h + 1) * MLA_QK] = kp
    v_ref[...] = kv[:, hw:].astype(BF16)


def _inproj(l, x, seq, g, w, cos, sin, qn, kvn, wq, wkv, mu, w0, a0, k_k, k_a, r_k, w2p, a2p, g2p):
    t = x.shape[0]
    tm = TM_INPROJ
    chunk = RWKV_CHUNK
    assert seq % tm == 0 and tm % RWKV_GROUP == 0 and RWKV_GROUP % chunk == 0
    row = lambda width: pl.BlockSpec((tm, width), lambda i: (i, 0))
    rw = RWKV_WIDTH
    return pl.pallas_call(
        functools.partial(_inproj_kernel, layer=l, chunk=chunk, tiles_per_seq=seq // tm),
        out_shape=(jax.ShapeDtypeStruct((t, MLA_HEADS * MLA_QK), BF16),
                   jax.ShapeDtypeStruct((t, MLA_HEADS * MLA_QK), BF16),
                   jax.ShapeDtypeStruct((t, MLA_WIDTH), BF16),
                   jax.ShapeDtypeStruct((t, 5 * rw), BF16),
                   jax.ShapeDtypeStruct((t, 4 * rw), BF16),
                   jax.ShapeDtypeStruct((t // chunk, rw), F32),
                   jax.ShapeDtypeStruct((t, MLSTM_IN), F32)),
        grid=(t // tm,),
        in_specs=[row(D_MODEL), _resident(g), _resident(w, l), row(LANES), row(LANES), _resident(qn),
                  _resident(kvn), _resident(wq, l), _resident(wkv, l),
                  _resident(mu), _resident(w0), _resident(a0), _resident(k_k), _resident(k_a),
                  _resident(r_k), _resident(w2p, l), _resident(a2p, l), _resident(g2p, l)],
        out_specs=(row(MLA_HEADS * MLA_QK), row(MLA_HEADS * MLA_QK), row(MLA_WIDTH), row(5 * rw), row(4 * rw),
                   pl.BlockSpec((tm // chunk, rw), lambda i: (i, 0)), row(MLSTM_IN)),
        scratch_shapes=[pltpu.VMEM((SUBLANES, RWKV_IN), F32)],
        compiler_params=_cparams(("arbitrary",)),
        name="inproj",
    )(x, g, w, cos, sin, qn, kvn, wq, wkv, mu, w0, a0, k_k, k_a, r_k, w2p, a2p, g2p)


def _attn_kernel(q_ref, k_ref, v_ref, g_ref, o_ref, m_sc, l_sc, acc_sc, sa_sc, sb_sc, *, layer, tq, tk, hp):
    i = pl.program_id(2)
    m_sc[...] = jnp.full(m_sc.shape, -jnp.inf, F32)
    l_sc[...] = jnp.zeros(l_sc.shape, F32)
    acc_sc[...] = jnp.zeros(acc_sc.shape, F32)
    sub = m_sc.shape[1]
    hs = range(hp)
    (g_ref,) = _rows(layer, g_ref)

    def produce(j, s_ref):
        off = pl.multiple_of(j * tk, tk)
        for h in hs:
            s_ref[h] = lax.dot_general(k_ref[0, pl.ds(off, tk), h * MLA_QK:(h + 1) * MLA_QK],
                                       q_ref[0, :, h * MLA_QK:(h + 1) * MLA_QK],
                                       (((1,), (1,)), ((), ())), preferred_element_type=F32)

    def consume(j, s_ref, masked):
        off = pl.multiple_of(j * tk, tk)
        s = [s_ref[h] for h in hs]
        if masked:
            keys = lax.broadcasted_iota(jnp.int32, (tk, tq), 0)
            queries = lax.broadcasted_iota(jnp.int32, (tk, tq), 1)
            s = [jnp.where(keys <= queries, s[h], -jnp.inf) for h in hs]
        m_old = [m_sc[h][0:1] for h in hs]
        m_new = [jnp.maximum(m_old[h], jnp.max(s[h], axis=0, keepdims=True)) for h in hs]
        p = [jnp.exp2(s[h] - m_new[h]) for h in hs]
        alpha = [jnp.exp2(m_old[h] - m_new[h]) for h in hs]
        for h in hs:
            l_new = alpha[h] * l_sc[h][0:1] + jnp.sum(p[h], axis=0, keepdims=True)
            l_sc[h] = jnp.broadcast_to(l_new, (sub, tq))
            m_sc[h] = jnp.broadcast_to(m_new[h], (sub, tq))
        pv = [lax.dot_general(v_ref[0, pl.ds(off, tk), h * MLA_VDIM:(h + 1) * MLA_VDIM], p[h].astype(BF16),
                              (((0,), (0,)), ((), ())), preferred_element_type=F32) for h in hs]
        for h in hs:
            acc_sc[h] = alpha[h] * acc_sc[h] + pv[h]

    def pair(jj, c):
        j = 2 * jj
        produce(j + 1, sb_sc)
        consume(j, sa_sc, False)
        produce(j + 2, sa_sc)
        consume(j + 1, sb_sc, False)
        return c

    produce(0, sa_sc)
    lax.fori_loop(0, i // 2, pair, 0)

    @pl.when(i % 2 == 0)
    def _():
        consume(i, sa_sc, True)

    @pl.when(i % 2 == 1)
    def _():
        produce(i, sb_sc)
        consume(i - 1, sa_sc, False)
        consume(i, sb_sc, True)

    for h in hs:
        o = acc_sc[h] / l_sc[h][0:1]
        o = o * lax.rsqrt(jnp.mean(o * o, axis=0, keepdims=True) + NORM_EPS)
        o_ref[0, :, h * MLA_VDIM:(h + 1) * MLA_VDIM] = (
            jnp.transpose(o) * g_ref[:, h * MLA_VDIM:(h + 1) * MLA_VDIM]).astype(o_ref.dtype)


def _mla_attention(l, q, k, v, g, batch, seq):
    tq, tk, hp = TQ_ATTN, TK_ATTN, HP_ATTN
    assert tq == tk and MLA_VDIM == LANES and hp == MLA_HEADS
    q = q.reshape(batch, seq, MLA_HEADS * MLA_QK)
    k = k.reshape(batch, seq, MLA_HEADS * MLA_QK)
    v = v.reshape(batch, seq, MLA_WIDTH)
    out = pl.pallas_call(
        functools.partial(_attn_kernel, layer=l, tq=tq, tk=tk, hp=hp),
        out_shape=jax.ShapeDtypeStruct((batch, seq, MLA_WIDTH), BF16),
        grid=(batch, MLA_HEADS // hp, seq // tq),
        in_specs=[pl.BlockSpec((1, tq, hp * MLA_QK), lambda b, h, i: (b, i, h)),
                  pl.BlockSpec((1, seq, hp * MLA_QK), lambda b, h, i: (b, 0, h)),
                  pl.BlockSpec((1, seq, hp * MLA_VDIM), lambda b, h, i: (b, 0, h)),
                  _resident(g)],
        out_specs=pl.BlockSpec((1, tq, hp * MLA_VDIM), lambda b, h, i: (b, i, h)),
        scratch_shapes=[pltpu.VMEM((hp, SUBLANES, tq), F32), pltpu.VMEM((hp, SUBLANES, tq), F32),
                        pltpu.VMEM((hp, MLA_VDIM, tq), F32),
                        pltpu.VMEM((hp, tk, tq), F32), pltpu.VMEM((hp, tk, tq), F32)],
        compiler_params=_cparams(("parallel", "parallel", "arbitrary")),
        name="mla_attention",
    )(q, k, v, g)
    return out.reshape(batch * seq, MLA_WIDTH)


def _tile_heads(z):
    return jnp.concatenate([z] * RWKV_HEADS, axis=0)


def _rwkv_chunk_kernel(fa_ref, ft_ref, *, nb, chunk, cps):
    seg = cps * chunk
    n = nb * seg
    w = RWKV_WIDTH
    hd = RWKV_HEAD
    x = fa_ref[...].reshape(n, fa_ref.shape[-1])
    r16, v16, a16, k16, b16 = (x[:, s * w:(s + 1) * w] for s in (FA_RT, FA_V, FA_AT, FA_KT, FA_BT))

    ri, ci, same_chunk = _rwkv_masks(chunk)
    bd = _div(ri, chunk) == _div(ci, hd)
    rt = lax.broadcasted_iota(jnp.int32, (chunk, w), 0)
    cs = _mod(lax.broadcasted_iota(jnp.int32, (chunk, w), 1), chunk)
    strict = cs < rt
    incl = cs <= rt
    c16 = _div(rt, 16) == _div(cs, 16)
    c32 = _div(rt, 32) == _div(cs, 32)
    eye = jnp.where(rt == cs, 1.0, 0.0)
    units = nb * cps

    def block_diag(z):
        return jnp.where(same_chunk, _tile_heads(z), 0.0)

    def put(u, sec, val):
        b, j = divmod(u, cps)
        ft_ref[b, j * chunk:(j + 1) * chunk, sec * w:(sec + 1) * w] = val.astype(ft_ref.dtype)

    def mm(x, y):
        return jnp.dot(x, y, preferred_element_type=F32)

    us = range(units)
    sls = [slice(u * chunk, (u + 1) * chunk) for u in us]
    a_st = [jnp.where(bd, _tile_heads(a16[sl]), 0.0) for sl in sls]
    v_st = [jnp.where(bd, _tile_heads(v16[sl]), 0.0) for sl in sls]
    kb_t = [jnp.concatenate([jnp.transpose(jnp.where(bd, _tile_heads(k16[sl]), 0.0)),
                             jnp.transpose(jnp.where(bd, _tile_heads(b16[sl]), 0.0))], axis=1) for sl in sls]
    sc = [mm(jnp.concatenate([a16[sls[u]], r16[sls[u]]], axis=0), kb_t[u]) for u in us]
    l_ab = [jnp.where(strict, sc[u][0:chunk, w:], 0.0) for u in us]
    l_ak = [jnp.where(strict, sc[u][0:chunk, 0:w], 0.0).astype(BF16) for u in us]
    a_rk = [jnp.where(incl, sc[u][chunk:, 0:w], 0.0).astype(BF16) for u in us]
    for u in us:
        put(u, FT_ARB, jnp.where(incl, sc[u][chunk:, w:], 0.0))
    xm = [-jnp.where(c16, l_ab[u], 0.0) for u in us]
    xm16 = [z.astype(BF16) for z in xm]
    off32 = [block_diag(jnp.where(c32 & jnp.logical_not(c16), l_ab[u], 0.0).astype(BF16)) for u in us]
    off64 = [block_diag(jnp.where(jnp.logical_not(c32), l_ab[u], 0.0).astype(BF16)) for u in us]
    x2 = [mm(xm16[u], block_diag(xm16[u])).astype(BF16) for u in us]
    x2_bd = [block_diag(z) for z in x2]
    lv = [mm(jnp.concatenate([l_ak[u], a_rk[u]], axis=0), v_st[u]) for u in us]
    wv = [block_diag(lv[u][0:chunk].astype(BF16)) for u in us]
    t_lo = [eye + xm[u] for u in us]
    tx = [mm(jnp.concatenate([t_lo[u].astype(BF16), x2[u]], axis=0), x2_bd[u]) for u in us]
    t_lo = [(t_lo[u] + tx[u][0:chunk]).astype(BF16) for u in us]
    x4 = [tx[u][chunk:] for u in us]
    x4b = [z.astype(BF16) for z in x4]
    x4_bd = [block_diag(z) for z in x4b]
    for u in us:
        put(u, FT_YV, lv[u][chunk:])
    x8_bd = [block_diag(mm(x4b[u], x4_bd[u]).astype(BF16)) for u in us]
    t_hi = [eye + x4[u] for u in us]
    t_hi = [block_diag((t_hi[u] + mm(t_hi[u].astype(BF16), x8_bd[u])).astype(BF16)) for u in us]
    t_inv = [mm(t_lo[u], t_hi[u]) for u in us]
    for off in (off32, off64):
        tb = [z.astype(BF16) for z in t_inv]
        mid = [mm(tb[u], off[u]).astype(BF16) for u in us]
        t_inv = [t_inv[u] - mm(mid[u], block_diag(tb[u])) for u in us]
    tb = [z.astype(BF16) for z in t_inv]
    for u in us:
        put(u, FT_TA, mm(tb[u], a_st[u]))
    for u in us:
        put(u, FT_UV, mm(tb[u], wv[u]))


def _rwkv_scan_stages(fa_ref, fb_ref, ft_ref, pc_ref, lnw_ref, lnb_ref, o_ref, state_sc, *, layer, nb, chunk, cps):
    seg = cps * chunk
    n = nb * seg
    w = RWKV_WIDTH
    hd = RWKV_HEAD
    lnw_ref, lnb_ref = _rows(layer, lnw_ref, lnb_ref)
    ri, ci, _ = _rwkv_masks(chunk)
    bd = _div(ri, chunk) == _div(ci, hd)
    bdv = _div(ri, hd) == _div(ci, hd)
    ones_bd = jnp.where(bdv, 1.0, 0.0).astype(BF16)

    def sec(ref, b, j, s):
        return ref[b, j * chunk:(j + 1) * chunk, s * w:(s + 1) * w]

    ys = [[None] * cps for _ in range(nb)]
    bs = range(nb)
    for j in range(cps):
        gs = [state_sc[b] for b in bs]
        p1 = [lax.dot_general(jnp.concatenate([sec(ft_ref, b, j, FT_TA), sec(fa_ref, b, j, FA_RT)], axis=0),
                              gs[b].astype(BF16), (((1,), (1,)), ((), ())), preferred_element_type=F32)
              for b in bs]
        yield
        u = [(p1[b][0:chunk] + sec(ft_ref, b, j, FT_UV).astype(F32)).astype(BF16) for b in bs]
        upd = [lax.dot_general(jnp.concatenate([sec(fa_ref, b, j, FA_V), -u[b]], axis=0),
                               jnp.concatenate([sec(fb_ref, b, j, FB_KH), sec(fb_ref, b, j, FB_BH)], axis=0),
                               (((0,), (0,)), ((), ())), preferred_element_type=F32) for b in bs]
        for b in bs:
            state_sc[b] = gs[b] * pc_ref[b, pl.ds(pl.program_id(0) * cps + j, 1), :] + jnp.where(bdv, upd[b], 0.0)
        for b in bs:
            u_st = jnp.where(bd, _tile_heads(u[b]), 0.0)
            ys[b][j] = (p1[b][chunk:] + sec(ft_ref, b, j, FT_YV).astype(F32)
                        - jnp.dot(sec(ft_ref, b, j, FT_ARB), u_st, preferred_element_type=F32))
        yield

    y = jnp.concatenate([ys[b][j] for b in range(nb) for j in range(cps)], axis=0)
    mean = _exact_right_dot(y, ones_bd) * (1.0 / hd)
    d = y - mean
    var = _exact_right_dot(d * d, ones_bd) * (1.0 / hd)
    yn = d * lax.rsqrt(var + RWKV_LN_EPS) * lnw_ref[...] + lnb_ref[...]
    bonus = fb_ref[:, :, FB_BONUS * w:(FB_BONUS + 1) * w].reshape(n, w).astype(F32)
    gate = fb_ref[:, :, FB_GATE * w:(FB_GATE + 1) * w].reshape(n, w).astype(F32)
    o_ref[...] = ((yn + bonus) * gate).astype(o_ref.dtype).reshape(o_ref.shape)


def _rwkv_chunk(fa, batch, seq):
    chunk = RWKV_CHUNK
    w = RWKV_WIDTH
    assert RWKV_HEADS * chunk == w
    cps = RWKV_PREP_CHUNKS
    seg = cps * chunk
    return pl.pallas_call(
        functools.partial(_rwkv_chunk_kernel, nb=batch, chunk=chunk, cps=cps),
        out_shape=jax.ShapeDtypeStruct((batch, seq, 4 * w), BF16),
        grid=(seq // seg,),
        in_specs=[pl.BlockSpec((batch, seg, 5 * w), lambda c: (0, c, 0))],
        out_specs=pl.BlockSpec((batch, seg, 4 * w), lambda c: (0, c, 0)),
        compiler_params=_cparams(("parallel",)),
        name="rwkv7_chunk",
    )(fa.reshape(batch, seq, 5 * w))


ML_QK = 0
ML_V = 2 * MLSTM_HEADS * MLSTM_QK
ML_O = ML_V + MLSTM_WIDTH
ML_I = ML_O + MLSTM_WIDTH
ML_F = ML_I + LANES


def _cummax_rows(x):
    n = x.shape[0]
    row = lax.broadcasted_iota(jnp.int32, x.shape, 0)
    sh = 1
    while sh < n:
        x = jnp.maximum(x, jnp.where(row >= sh, pltpu.roll(x, sh, 0), -jnp.inf))
        sh *= 2
    return x


def _mlstm_stages(x_ref, cw_ref, cb_ref, ib_ref, fb_ref, on_ref, o_ref,
                  prev_sc, c_sc, n_sc, m_sc, *, layer, nb, chunk):
    n = nb * chunk
    nh = MLSTM_HEADS
    dk = MLSTM_QK
    dv = MLSTM_V
    qkw = nh * dk
    vw = MLSTM_WIDTH
    cb_ref, ib_ref, fb_ref, on_ref = _rows(layer, cb_ref, ib_ref, fb_ref, on_ref)
    x = x_ref[...].reshape(n, MLSTM_IN)
    qk_raw = x[:, ML_QK:ML_V]
    prev = prev_sc[...]
    conv = cb_ref[...] + qk_raw * cw_ref[MLSTM_CONV - 1:MLSTM_CONV, :]
    for s in range(1, MLSTM_CONV):
        conv = conv + _shift_rows(qk_raw, prev, s, chunk) * cw_ref[MLSTM_CONV - 1 - s:MLSTM_CONV - s, :]
    prev_sc[...] = qk_raw
    qk = conv * _sigmoid(conv)
    q_all = qk[:, 0:qkw] * (dk ** -0.5)
    k_all = qk[:, qkw:]
    v_all = x[:, ML_V:ML_O]
    o_pre = x[:, ML_O:ML_I]
    li_all = x[:, ML_I:ML_F] + ib_ref[...]
    lf_all = _log_sigmoid(x[:, ML_F:ML_F + LANES] + fb_ref[...])

    ri = lax.broadcasted_iota(jnp.int32, (chunk, chunk), 0)
    ci = lax.broadcasted_iota(jnp.int32, (chunk, chunk), 1)
    causal = ci <= ri
    tri = jnp.where(causal, 1.0, 0.0)
    lane_k = lax.broadcasted_iota(jnp.int32, (chunk, qkw), 1)
    lane_v = lax.broadcasted_iota(jnp.int32, (chunk, vw), 1)
    rc = lax.broadcasted_iota(jnp.int32, (qkw, vw), 0)
    cc = lax.broadcasted_iota(jnp.int32, (qkw, vw), 1)
    cmask = _div(rc, dk) == _div(cc, dv)
    expand_v = jnp.where(rc == _div(cc, dv), 1.0, 0.0).astype(BF16)
    rk = lax.broadcasted_iota(jnp.int32, (qkw, qkw), 0)
    ck = lax.broadcasted_iota(jnp.int32, (qkw, qkw), 1)
    expand_k = jnp.where(rk == _div(ck, dk), 1.0, 0.0).astype(BF16)
    gather_k = jnp.where(_div(rk, dk) == ck, 1.0, 0.0).astype(BF16)

    bs = range(nb)
    sls = [slice(b * chunk, (b + 1) * chunk) for b in bs]
    q = [q_all[sl] for sl in sls]
    k = [k_all[sl] for sl in sls]
    k16 = [z.astype(BF16) for z in k]
    v = [v_all[sl] for sl in sls]
    li = [li_all[sl] for sl in sls]
    c_old = [c_sc[b] for b in bs]
    n_old = [n_sc[b] for b in bs]
    m_prev = [m_sc[b] for b in bs]
    g = [_exact_left_dot(tri, lf_all[sl]) for sl in sls]
    lig = [li[b] - g[b] for b in bs]
    inter_log = [g[b] + m_prev[b] for b in bs]
    m_t = [jnp.maximum(inter_log[b], g[b] + _cummax_rows(lig[b])) for b in bs]
    inter_w = [jnp.exp(inter_log[b] - m_t[b]) for b in bs]
    log2e = math.log2(math.e)
    gm = [(g[b] - m_t[b]) * log2e for b in bs]
    lig_t = [jnp.transpose(z * log2e) for z in lig]
    qn = [_exact_right_dot(q[b] * n_old[b], gather_k, parts=2) for b in bs]
    q_c = [_bdot(q[b], c_old[b]) for b in bs]
    ssum = [jnp.zeros((chunk, LANES), F32) for _ in bs]
    num = [jnp.zeros((chunk, vw), F32) for _ in bs]
    yield
    for h in range(nh):
        mk = (lane_k >= h * dk) & (lane_k < (h + 1) * dk)
        mv = (lane_v >= h * dv) & (lane_v < (h + 1) * dv)
        qk_h = [lax.dot_general(jnp.where(mk, q[b], 0.0).astype(BF16), k16[b], (((1,), (1,)), ((), ())),
                                preferred_element_type=F32) for b in bs]
        d = [jnp.broadcast_to(gm[b][:, h:h + 1], (chunk, chunk)) + lig_t[b][h:h + 1, :] for b in bs]
        s = [qk_h[b] * jnp.exp2(jnp.where(causal, d[b], -jnp.inf)) for b in bs]
        ssum = [jnp.where(lane_k == h, jnp.sum(s[b], axis=-1, keepdims=True), ssum[b]) for b in bs]
        num = [num[b] + _bdot(s[b], jnp.where(mv, v[b], 0.0)) for b in bs]
        yield
    den = [inter_w[b] * qn[b] + ssum[b] for b in bs]
    rden = [1.0 / jnp.maximum(jnp.abs(den[b]), jnp.exp(-m_t[b])) for b in bs]
    g_last = [g[b][chunk - 1:chunk, :] for b in bs]
    a_all = [g_last[b] - g[b] + li[b] for b in bs]
    m_new = [jnp.maximum(g_last[b] + m_prev[b], jnp.max(a_all[b], axis=0, keepdims=True)) for b in bs]
    dec = [jnp.exp(g_last[b] + m_prev[b] - m_new[b]) for b in bs]
    wts = [jnp.exp(a_all[b] - m_new[b]) for b in bs]
    per_head = [jnp.concatenate([inter_w[b], rden[b], wts[b], jnp.broadcast_to(dec[b], (8, LANES))], axis=0)
                for b in bs]
    on_v = [_exact_right_dot(per_head[b], expand_v, parts=2) for b in bs]
    on_k = [_exact_right_dot(per_head[b][2 * chunk:], expand_k, parts=2) for b in bs]
    hs = [(on_v[b][0:chunk] * q_c[b] + num[b]) * on_v[b][chunk:2 * chunk] for b in bs]
    for b in bs:
        c_sc[b] = c_old[b] * on_v[b][3 * chunk:3 * chunk + 1] + jnp.where(
            cmask, _bdot_tn(k16[b], on_v[b][2 * chunk:3 * chunk] * v[b]), 0.0)
        n_sc[b] = n_old[b] * on_k[b][chunk:chunk + 1] + jnp.sum(on_k[b][0:chunk] * k[b], axis=0, keepdims=True)
        m_sc[b] = m_new[b]

    hh = jnp.concatenate(hs, axis=0)
    rv = lax.broadcasted_iota(jnp.int32, (vw, vw), 0)
    cv = lax.broadcasted_iota(jnp.int32, (vw, vw), 1)
    head_ones = jnp.where(_div(rv, dv) == _div(cv, dv), 1.0, 0.0).astype(BF16)
    ms = _exact_right_dot(hh * hh, head_ones, parts=2) * (1.0 / dv)
    out = hh * lax.rsqrt(ms + NORM_EPS) * on_ref[...] * _sigmoid(o_pre)
    o_ref[...] = out.astype(o_ref.dtype).reshape(o_ref.shape)


def _recurrent_kernel(x_ref, cw_ref, cb_ref, ib_ref, fb_ref, on_ref, fa_ref, fbk_ref, ft_ref, pc_ref, lnw_ref, lnb_ref,
                      om_ref, or_ref, prev_sc, c_sc, n_sc, m_sc, state_sc, *, layer, nb, chunk, rwkv_chunk, cps):
    @pl.when(pl.program_id(0) == 0)
    def _():
        for sc in (prev_sc, c_sc, n_sc, m_sc, state_sc):
            sc[...] = jnp.zeros(sc.shape, F32)

    scan = _rwkv_scan_stages(fa_ref, fbk_ref, ft_ref, pc_ref, lnw_ref, lnb_ref, or_ref, state_sc,
                             layer=layer, nb=nb, chunk=rwkv_chunk, cps=cps)
    mlstm = _mlstm_stages(x_ref, cw_ref, cb_ref, ib_ref, fb_ref, on_ref, om_ref, prev_sc, c_sc, n_sc, m_sc,
                          layer=layer, nb=nb, chunk=chunk)
    order = [scan, mlstm] + [scan, scan, mlstm] * (MLSTM_HEADS - 1) + [scan, mlstm]
    pending = {id(scan): 2 * cps + 1, id(mlstm): MLSTM_HEADS + 2}
    for g in order:
        next(g, None)
        pending[id(g)] -= 1
    for g in (scan, mlstm):
        for _ in range(pending[id(g)]):
            next(g, None)


def _recurrent(l, mlstm_in, fa, fb, ft, pc, batch, seq, cw, cb, ib, fbias, on, ln_w, ln_b):
    chunk = MLSTM_CHUNK
    rchunk = RWKV_CHUNK
    assert chunk % rchunk == 0
    w = RWKV_WIDTH
    x = mlstm_in.reshape(batch, seq, MLSTM_IN)
    fa = fa.reshape(batch, seq, 5 * w)
    fb = fb.reshape(batch, seq, 4 * w)
    pc = pc.reshape(batch, seq // rchunk, w)
    blk = lambda width: pl.BlockSpec((batch, chunk, width), lambda c: (0, c, 0))
    om, orw = pl.pallas_call(
        functools.partial(_recurrent_kernel, layer=l, nb=batch, chunk=chunk, rwkv_chunk=rchunk, cps=chunk // rchunk),
        out_shape=(jax.ShapeDtypeStruct((batch, seq, MLSTM_WIDTH), BF16),
                   jax.ShapeDtypeStruct((batch, seq, w), BF16)),
        grid=(seq // chunk,),
        in_specs=[blk(MLSTM_IN), _resident(cw, l), _resident(cb), _resident(ib), _resident(fbias), _resident(on),
                  blk(2 * w), blk(4 * w), blk(4 * w), _resident(pc), _resident(ln_w), _resident(ln_b)],
        out_specs=(blk(MLSTM_WIDTH), blk(w)),
        scratch_shapes=[pltpu.VMEM((batch * chunk, 2 * MLSTM_HEADS * MLSTM_QK), F32),
                        pltpu.VMEM((batch, MLSTM_HEADS * MLSTM_QK, MLSTM_WIDTH), F32),
                        pltpu.VMEM((batch, 1, MLSTM_HEADS * MLSTM_QK), F32),
                        pltpu.VMEM((batch, 1, LANES), F32),
                        pltpu.VMEM((batch, w, w), F32)],
        compiler_params=_cparams(("arbitrary",)),
        name="mlstm_rwkv_scan",
    )(x, cw, cb, ib, fbias, on, fa, fb, ft, pc, ln_w, ln_b)
    return om.reshape(batch * seq, MLSTM_WIDTH), orw.reshape(batch * seq, w)


def _ffn_kernel(x_ref, ya_ref, yb_ref, yc_ref, wo_ref, g_ref, wg_ref, wu_ref, wd_ref, fg_ref,
                o_ref, act_sc, *, layer, final_norm, tf):
    (g_ref,) = _rows(layer, g_ref)
    y = jnp.concatenate([ya_ref[...], yb_ref[...], yc_ref[...]], axis=-1)
    x1 = x_ref[...] + jnp.dot(y, wo_ref[...], preferred_element_type=F32)
    h = _rms(x1, g_ref[...]).astype(BF16)
    for c in range(D_FF // tf):
        gate = jnp.dot(h, wg_ref[:, c * tf:(c + 1) * tf], preferred_element_type=F32)
        up = jnp.dot(h, wu_ref[:, c * tf:(c + 1) * tf], preferred_element_type=F32)
        act_sc[:, c * tf:(c + 1) * tf] = (gate * _sigmoid(gate) * up).astype(BF16)
    out = x1 + jnp.dot(act_sc[...], wd_ref[...], preferred_element_type=F32)
    if final_norm:
        out = _rms(out, fg_ref[...])
    o_ref[...] = out


def _out_ffn(l, x, ya, yb, yc, wo, g, wg, wu, wd, fg, final_norm):
    t = x.shape[0]
    tm, tf = TM_FFN, TF_FFN
    row = lambda w: pl.BlockSpec((tm, w), lambda i: (i, 0))
    return pl.pallas_call(
        functools.partial(_ffn_kernel, layer=l, final_norm=final_norm, tf=tf),
        out_shape=jax.ShapeDtypeStruct((t, D_MODEL), F32),
        grid=(t // tm,),
        in_specs=[row(D_MODEL), row(MLA_WIDTH), row(RWKV_WIDTH), row(MLSTM_WIDTH), _resident(wo, l),
                  _resident(g), _resident(wg, l), _resident(wu, l), _resident(wd, l), _resident(fg)],
        out_specs=row(D_MODEL),
        scratch_shapes=[pltpu.VMEM((tm, D_FF), BF16)],
        compiler_params=_cparams(("parallel",)),
        name="out_ffn",
    )(x, ya, yb, yc, wo, g, wg, wu, wd, fg)


def _pad_cols(w, width):
    return jnp.pad(w, [(0, 0)] * (w.ndim - 1) + [(0, width - w.shape[-1])])


def _pad_rows(w, height):
    return jnp.pad(w, [(0, 0)] * (w.ndim - 2) + [(0, height - w.shape[-2]), (0, 0)])


def _rot_half_cols(w):
    half = w.shape[-1] // 2
    return jnp.concatenate([-w[..., half:], w[..., :half]], axis=-1)


def _stacked_weights(w_in, mla_w_uq, mla_w_ukv, rwkv_w2, rwkv_a2, rwkv_g2):
    depth = w_in.shape[0]
    wt = jnp.swapaxes(w_in, 1, 2)
    c_q, c_kv, k_pe = wt[:, 0:256], wt[:, 256:512], wt[:, 512:576]
    rw = wt[:, 576:1472]
    ml = wt[:, 1472:2248]
    k_pe_rot = jnp.swapaxes(_rot_half_cols(jnp.swapaxes(k_pe, 1, 2)), 1, 2)
    w_mla = jnp.concatenate([c_q, c_kv, _pad_rows(k_pe, LANES), _pad_rows(k_pe_rot, LANES)], axis=1)
    w_mlstm = jnp.concatenate([ml[:, 0:256], ml[:, 256:512], ml[:, 520:776],
                               _pad_rows(ml[:, 512:516], LANES), _pad_rows(ml[:, 516:520], LANES)], axis=1)
    w_all = jnp.concatenate([w_mla, rw, w_mlstm], axis=1).astype(BF16)

    uq = mla_w_uq.reshape(depth, MLA_Q_LORA, MLA_HEADS, MLA_NOPE + MLA_ROPE)
    nope = uq[..., :MLA_NOPE].reshape(depth, MLA_Q_LORA, MLA_HEADS * MLA_NOPE)
    pe = _pad_cols(uq[..., MLA_NOPE:], LANES).reshape(depth, MLA_Q_LORA, MLA_HEADS * LANES)
    per = _pad_cols(_rot_half_cols(uq[..., MLA_NOPE:]), LANES).reshape(depth, MLA_Q_LORA, MLA_HEADS * LANES)
    wq = jnp.concatenate([nope, pe, per], axis=-1).astype(BF16)
    ukv = mla_w_ukv.reshape(depth, MLA_KV_LORA, MLA_HEADS, MLA_NOPE + MLA_VDIM)
    wkv = jnp.concatenate([ukv[..., :MLA_NOPE].reshape(depth, MLA_KV_LORA, -1),
                           ukv[..., MLA_NOPE:].reshape(depth, MLA_KV_LORA, -1)], axis=-1).astype(BF16)

    rows = lambda before, wt: jnp.pad(wt, ((0, 0), (before, LANES - before - wt.shape[1]), (0, 0))).astype(BF16)
    w2p = rows(0, rwkv_w2)
    a2p = rows(RWKV_DECAY_LORA, rwkv_a2)
    g2p = rows(RWKV_DECAY_LORA + RWKV_AAA_LORA, rwkv_g2)
    return w_all, wq, wkv, w2p, a2p, g2p


def kernel(x, positions, mix_norm, w_in, mla_q_norm, mla_w_uq, mla_kv_norm, mla_w_ukv, mla_out_norm, rwkv_mu, rwkv_w0, rwkv_w2, rwkv_a0, rwkv_a2, rwkv_g2, rwkv_k_k, rwkv_k_a, rwkv_r_k, rwkv_ln_w, rwkv_ln_b, mlstm_conv_w, mlstm_conv_b, mlstm_i_bias, mlstm_f_bias, mlstm_out_norm, w_out, ffn_norm, w_gate, w_up, w_down, final_norm):
    batch, seq, _ = x.shape
    depth = w_in.shape[0]
    xt = x.reshape(batch * seq, D_MODEL)
    cos, sin = _rope_tables(positions)
    w_all, wq, wkv, w2p, a2p, g2p = _stacked_weights(w_in, mla_w_uq, mla_w_ukv, rwkv_w2, rwkv_a2, rwkv_g2)
    wo, wg, wu, wd = (w.astype(BF16) for w in (w_out, w_gate, w_up, w_down))
    ml_ib = _pad_cols(mlstm_i_bias, LANES)
    ml_fb = _pad_cols(mlstm_f_bias, LANES)
    for l in range(depth):
        q, k, v, fa, fb, pc, mlstm_in = _inproj(
            l, xt, seq, mix_norm, w_all, cos, sin, mla_q_norm, mla_kv_norm, wq, wkv,
            rwkv_mu, rwkv_w0, rwkv_a0, rwkv_k_k, rwkv_k_a, rwkv_r_k,
            w2p, a2p, g2p)
        y_mla = _mla_attention(l, q, k, v, mla_out_norm, batch, seq)
        ft = _rwkv_chunk(fa, batch, seq)
        y_mlstm, y_rwkv = _recurrent(l, mlstm_in, fa, fb, ft, pc, batch, seq, mlstm_conv_w, mlstm_conv_b, ml_ib, ml_fb,
                                     mlstm_out_norm, rwkv_ln_w, rwkv_ln_b)
        xt = _out_ffn(l, xt, y_mla, y_rwkv, y_mlstm, wo, ffn_norm, wg, wu, wd,
                      final_norm.reshape(1, -1), final_norm=(l == depth - 1))
    return xt.reshape(batch, seq, D_MODEL)
```

```python
import functools
import math

import jax
import jax.numpy as jnp
from jax import lax
from jax.experimental import pallas as pl
from jax.experimental.pallas import tpu as pltpu

F32 = jnp.float32
BF16 = jnp.bfloat16

D_MODEL = 1024
DEPTH = 2
MLA_HEADS = 4
MLA_NOPE = 128
MLA_ROPE = 64
MLA_VDIM = 128
MLA_Q_LORA = 256
MLA_KV_LORA = 256
MLA_WIDTH = MLA_HEADS * MLA_VDIM
MLA_QK = 256
ROPE_THETA = 10000.0
RWKV_HEADS = 4
RWKV_HEAD = 64
RWKV_WIDTH = 256
RWKV_DECAY_LORA = 32
RWKV_AAA_LORA = 32
RWKV_GATE_LORA = 64
RWKV_IN = 3 * RWKV_WIDTH + 128
RWKV_LN_EPS = 64e-5
MLSTM_HEADS = 4
MLSTM_QK = 32
MLSTM_V = 64
MLSTM_WIDTH = 256
MLSTM_CONV = 4
MLSTM_IN = 1024
D_FF = 2816
NORM_EPS = 1e-6
LANES = 128
SUBLANES = 8

MLA_CQ = 0
MLA_CKV = MLA_CQ + MLA_Q_LORA
MLA_KPE = MLA_CKV + MLA_KV_LORA
MLA_KPER = MLA_KPE + LANES
MLA_IN = MLA_KPER + LANES

TM_INPROJ = 512
TQ_ATTN = 512
TK_ATTN = 512
HP_ATTN = 4
RWKV_CHUNK = 64
RWKV_PREP_CHUNKS = 4
MLSTM_CHUNK = 256
TM_FFN = 512
TF_FFN = 256
VMEM_LIMIT = 56 * 1024 * 1024


def _cparams(sem):
    return pltpu.CompilerParams(dimension_semantics=sem, vmem_limit_bytes=VMEM_LIMIT)


def _resident(a, layer=None):
    if layer is None:
        nd = a.ndim
        return pl.BlockSpec(a.shape, lambda *_: (0,) * nd, pipeline_mode=pl.Buffered(1))
    nd = a.ndim - 1
    return pl.BlockSpec((None,) + a.shape[1:], lambda *_: (layer,) + (0,) * nd, pipeline_mode=pl.Buffered(1))


def _rows(layer, *refs):
    return [r.at[layer:layer + 1] for r in refs]


def _bdot(a, b):
    return jnp.dot(a.astype(BF16), b.astype(BF16), preferred_element_type=F32)


def _dot_nt(a, b):
    return lax.dot_general(a, b, (((1,), (1,)), ((), ())), preferred_element_type=F32)


def _bdot_nt(a, b):
    return lax.dot_general(a.astype(BF16), b.astype(BF16), (((1,), (1,)), ((), ())),
                           preferred_element_type=F32)


def _bdot_tn(a, b):
    return lax.dot_general(a.astype(BF16), b.astype(BF16), (((0,), (0,)), ((), ())),
                           preferred_element_type=F32)


def _split3(x):
    h = x.astype(BF16)
    r1 = x - h.astype(F32)
    m = r1.astype(BF16)
    lo = (r1 - m.astype(F32)).astype(BF16)
    return h, m, lo


def _exact_left_dot(sel, x):
    h, m, lo = _split3(x)
    s = sel.astype(BF16)
    return (jnp.dot(s, h, preferred_element_type=F32) + jnp.dot(s, m, preferred_element_type=F32)
            + jnp.dot(s, lo, preferred_element_type=F32))


def _exact_right_dot(x, sel, parts=3):
    pieces = _split3(x)[:parts]
    s = sel.astype(BF16)
    out = jnp.dot(pieces[0], s, preferred_element_type=F32)
    for p in pieces[1:]:
        out = out + jnp.dot(p, s, preferred_element_type=F32)
    return out


def _rms(x, g):
    return x * lax.rsqrt(jnp.mean(x * x, axis=-1, keepdims=True) + NORM_EPS) * g


def _sigmoid(x):
    return 1.0 / (1.0 + jnp.exp(-x))


def _log_sigmoid(x):
    return jnp.minimum(x, 0.0) - jnp.log1p(jnp.exp(-jnp.abs(x)))


def _div(x, d):
    assert d & (d - 1) == 0
    return lax.shift_right_logical(x, d.bit_length() - 1)


def _mod(x, d):
    assert d & (d - 1) == 0
    return lax.bitwise_and(x, d - 1)


def _shift_rows(x, prev, s, chunk):
    n = x.shape[0]
    row = lax.broadcasted_iota(jnp.int32, x.shape, 0)
    return jnp.where(_mod(row, chunk) >= s, pltpu.roll(x, s, 0), pltpu.roll(prev, n - chunk + s, 0))


def _rope_cast_kernel(pos_ref, invf_ref, *refs):
    n = (len(refs) - 2) // 2
    w_refs, cos_ref, sin_ref, o_refs = refs[:n], refs[n], refs[n + 1], refs[n + 2:]
    ang = pos_ref[...].astype(F32) * invf_ref[...]
    cos_ref[...] = jnp.cos(ang)
    sin_ref[...] = jnp.sin(ang)
    for w_ref, o_ref in zip(w_refs, o_refs):
        o_ref[...] = w_ref[...].astype(BF16)


def _rope_tables_and_casts(positions, weights):
    t = positions.size
    tm = min(1024, t)
    steps = t // tm
    inv_freq = ROPE_THETA ** (-jnp.arange(0, MLA_ROPE, 2, dtype=F32) / MLA_ROPE)
    invf = jnp.tile(inv_freq, LANES // (MLA_ROPE // 2))[None, :]
    flat = [w.reshape(-1, w.shape[-1]) for w in weights]
    rows = [f.shape[0] // steps for f in flat]
    assert all(f.shape[0] % steps == 0 and r % 16 == 0 for f, r in zip(flat, rows))
    slab = lambda r, c: pl.BlockSpec((r, c), lambda i: (i, 0))
    outs = pl.pallas_call(
        _rope_cast_kernel,
        out_shape=(jax.ShapeDtypeStruct((t, LANES), F32), jax.ShapeDtypeStruct((t, LANES), F32))
        + tuple(jax.ShapeDtypeStruct(f.shape, BF16) for f in flat),
        grid=(steps,),
        in_specs=[slab(tm, 1), pl.BlockSpec((1, LANES), lambda i: (0, 0))]
        + [slab(r, f.shape[1]) for f, r in zip(flat, rows)],
        out_specs=(slab(tm, LANES), slab(tm, LANES)) + tuple(slab(r, f.shape[1]) for f, r in zip(flat, rows)),
        compiler_params=_cparams(("parallel",)),
        name="rope_tables_casts",
    )(positions.reshape(t, 1), invf, *flat)
    return outs[0], outs[1], [o.reshape(w.shape) for o, w in zip(outs[2:], weights)]


(FA_RT, FA_V, FA_AT, FA_KT, FA_BT) = range(5)
(FB_KH, FB_BH, FB_BONUS, FB_GATE) = range(4)
(FT_TA, FT_UV, FT_YV, FT_ARB) = range(4)
RWKV_GROUP = 256


def _segsum(x, seg):
    lane = lax.broadcasted_iota(jnp.int32, x.shape, 1)
    out = jnp.zeros_like(x)
    for h in range(x.shape[1] // seg):
        m = (lane >= h * seg) & (lane < (h + 1) * seg)
        s = jnp.sum(jnp.where(m, x, 0.0), axis=-1, keepdims=True)
        out = jnp.where(m, s, out)
    return out


def _rwkv_masks(chunk):
    w = RWKV_WIDTH
    r = lax.broadcasted_iota(jnp.int32, (w, w), 0)
    c = lax.broadcasted_iota(jnp.int32, (w, w), 1)
    return r, c, _div(r, chunk) == _div(c, chunk)


def _rwkv_features(xs, r0, prm, fa_ref, fb_ref, pc_ref, chunk):
    w0_ref, a0_ref, kk_ref, ka_ref, rk_ref, w2_ref, a2_ref, g2_ref = prm
    w = RWKV_WIDTH
    hd = RWKV_HEAD
    n = xs.shape[0]
    r = xs[:, 0:w]
    k = xs[:, w:2 * w]
    v = xs[:, 2 * w:3 * w]
    lor = xs[:, 3 * w:]
    ld = -math.exp(-0.5) * _sigmoid(w0_ref[...] + _bdot(jnp.tanh(lor), w2_ref[...]))
    a = _sigmoid(a0_ref[...] + _bdot(lor, a2_ref[...]))
    g = _bdot(_sigmoid(lor), g2_ref[...])
    kk = k * kk_ref[...]
    kk = kk / jnp.maximum(jnp.sqrt(_segsum(kk * kk, hd)), 1e-12)
    k2 = k * (1.0 + (a - 1.0) * ka_ref[...])
    kb = kk * a
    bonus = _segsum(r * k2 * rk_ref[...], hd) * v

    assert n <= RWKV_WIDTH
    ri, ci, same_chunk = _rwkv_masks(chunk)
    tri = jnp.where(same_chunk & (ci <= ri), 1.0, 0.0)[0:n, 0:n]
    cl = _exact_left_dot(tri, ld)
    units = n // chunk
    cl_last = jnp.concatenate(
        [jnp.broadcast_to(cl[(u + 1) * chunk - 1:(u + 1) * chunk, :], (chunk, w)) for u in range(units)], axis=0)
    e_neg = jnp.exp(-cl)
    e_end = jnp.exp(cl_last - cl)
    rows = slice(r0, r0 + n)

    def put(ref, sec, val):
        ref[rows, sec * w:(sec + 1) * w] = val.astype(ref.dtype)

    put(fa_ref, FA_RT, r * jnp.exp(cl))
    put(fa_ref, FA_V, v)
    put(fa_ref, FA_AT, kk * jnp.exp(cl - ld))
    put(fa_ref, FA_KT, k2 * e_neg)
    put(fa_ref, FA_BT, kb * e_neg)
    put(fb_ref, FB_KH, k2 * e_end)
    put(fb_ref, FB_BH, kb * e_end)
    put(fb_ref, FB_BONUS, bonus)
    put(fb_ref, FB_GATE, g)
    for u in range(units):
        c = r0 // chunk + u
        pc_ref[c:c + 1, :] = jnp.exp(cl_last[u * chunk:u * chunk + 1, :])


def _inproj_kernel(x_ref, g_ref, w_ref, cos_ref, sin_ref, qn_ref, kvn_ref, wq_ref, wkv_ref,
                   mu_ref, w0_ref, a0_ref, kk_ref, ka_ref, rk_ref, w2_ref, a2_ref, g2_ref,
                   q_ref, k_ref, v_ref, fa_ref, fb_ref, pc_ref, mlstm_ref, prev_sc, *, layer, chunk, tiles_per_seq):
    tm = x_ref.shape[0]
    g_ref, qn_ref, kvn_ref, mu_ref, w0_ref, a0_ref, kk_ref, ka_ref, rk_ref = _rows(
        layer, g_ref, qn_ref, kvn_ref, mu_ref, w0_ref, a0_ref, kk_ref, ka_ref, rk_ref)

    @pl.when(pl.program_id(0) % tiles_per_seq == 0)
    def _():
        prev_sc[...] = jnp.zeros(prev_sc.shape, F32)

    hb = _rms(x_ref[...], g_ref[...]).astype(BF16)
    rw = _dot_nt(hb, w_ref[MLA_IN:MLA_IN + RWKV_IN, :])
    row = lax.broadcasted_iota(jnp.int32, rw.shape, 0)
    shifted = jnp.where(row >= 1, pltpu.roll(rw, 1, 0), prev_sc[0:1, :])
    prev_sc[0:1, :] = rw[tm - 1:tm, :]
    xs = rw + (shifted - rw) * mu_ref[...]
    prm = (w0_ref, a0_ref, kk_ref, ka_ref, rk_ref, w2_ref, a2_ref, g2_ref)

    mla = _dot_nt(hb, w_ref[0:MLA_IN, :])
    groups = tm // RWKV_GROUP
    for gi in range(groups // 2):
        _rwkv_features(xs[gi * RWKV_GROUP:(gi + 1) * RWKV_GROUP], gi * RWKV_GROUP, prm, fa_ref, fb_ref, pc_ref, chunk)
    mlstm_ref[...] = _dot_nt(hb, w_ref[MLA_IN + RWKV_IN:, :])
    for gi in range(groups // 2, groups):
        _rwkv_features(xs[gi * RWKV_GROUP:(gi + 1) * RWKV_GROUP], gi * RWKV_GROUP, prm, fa_ref, fb_ref, pc_ref, chunk)

    cos = cos_ref[...]
    sin = sin_ref[...]
    scale = (MLA_NOPE + MLA_ROPE) ** -0.5 * math.log2(math.e)
    hw = MLA_HEADS * LANES
    cqn = _rms(mla[:, MLA_CQ:MLA_CKV], qn_ref[...]).astype(BF16)
    q = jnp.dot(cqn, wq_ref[...], preferred_element_type=F32)
    ckvn = _rms(mla[:, MLA_CKV:MLA_KPE], kvn_ref[...]).astype(BF16)
    kv = jnp.dot(ckvn, wkv_ref[...], preferred_element_type=F32)
    kp = (mla[:, MLA_KPE:MLA_KPER] * cos + mla[:, MLA_KPER:MLA_IN] * sin).astype(BF16)
    for h in range(MLA_HEADS):
        c0 = h * LANES
        pe = q[:, hw + c0:hw + c0 + LANES] * cos + q[:, 2 * hw + c0:2 * hw + c0 + LANES] * sin
        q_ref[:, h * MLA_QK:h * MLA_QK + LANES] = (q[:, c0:c0 + LANES] * scale).astype(BF16)
        q_ref[:, h * MLA_QK + LANES:(h + 1) * MLA_QK] = (pe * scale).astype(BF16)
        k_ref[:, h * MLA_QK:h * MLA_QK + LANES] = kv[:, c0:c0 + LANES].astype(BF16)
        k_ref[:, h * MLA_QK + LANES:(h + 1) * MLA_QK] = kp
    v_ref[...] = kv[:, hw:].astype(BF16)


def _inproj(l, x, seq, g, w, cos, sin, qn, kvn, wq, wkv, mu, w0, a0, k_k, k_a, r_k, w2p, a2p, g2p):
    t = x.shape[0]
    tm = TM_INPROJ
    chunk = RWKV_CHUNK
    assert seq % tm == 0 and tm % RWKV_GROUP == 0 and RWKV_GROUP % chunk == 0
    row = lambda width: pl.BlockSpec((tm, width), lambda i: (i, 0))
    rw = RWKV_WIDTH
    return pl.pallas_call(
        functools.partial(_inproj_kernel, layer=l, chunk=chunk, tiles_per_seq=seq // tm),
        out_shape=(jax.ShapeDtypeStruct((t, MLA_HEADS * MLA_QK), BF16),
                   jax.ShapeDtypeStruct((t, MLA_HEADS * MLA_QK), BF16),
                   jax.ShapeDtypeStruct((t, MLA_WIDTH), BF16),
                   jax.ShapeDtypeStruct((t, 5 * rw), BF16),
                   jax.ShapeDtypeStruct((t, 4 * rw), BF16),
                   jax.ShapeDtypeStruct((t // chunk, rw), F32),
                   jax.ShapeDtypeStruct((t, MLSTM_IN), F32)),
        grid=(t // tm,),
        in_specs=[row(D_MODEL), _resident(g), _resident(w, l), row(LANES), row(LANES), _resident(qn),
                  _resident(kvn), _resident(wq, l), _resident(wkv, l),
                  _resident(mu), _resident(w0), _resident(a0), _resident(k_k), _resident(k_a),
                  _resident(r_k), _resident(w2p, l), _resident(a2p, l), _resident(g2p, l)],
        out_specs=(row(MLA_HEADS * MLA_QK), row(MLA_HEADS * MLA_QK), row(MLA_WIDTH), row(5 * rw), row(4 * rw),
                   pl.BlockSpec((tm // chunk, rw), lambda i: (i, 0)), row(MLSTM_IN)),
        scratch_shapes=[pltpu.VMEM((SUBLANES, RWKV_IN), F32)],
        compiler_params=_cparams(("arbitrary",)),
        name="inproj",
    )(x, g, w, cos, sin, qn, kvn, wq, wkv, mu, w0, a0, k_k, k_a, r_k, w2p, a2p, g2p)


def _attn_kernel(q_ref, k_ref, v_ref, g_ref, o_ref, m_sc, l_sc, acc_sc, sa_sc, sb_sc, *, layer, tq, tk, hp):
    i = pl.program_id(2)
    m_sc[...] = jnp.full(m_sc.shape, -jnp.inf, F32)
    l_sc[...] = jnp.zeros(l_sc.shape, F32)
    acc_sc[...] = jnp.zeros(acc_sc.shape, F32)
    sub = m_sc.shape[1]
    hs = range(hp)
    (g_ref,) = _rows(layer, g_ref)

    def produce(j, s_ref):
        off = pl.multiple_of(j * tk, tk)
        for h in hs:
            s_ref[h] = lax.dot_general(k_ref[0, pl.ds(off, tk), h * MLA_QK:(h + 1) * MLA_QK],
                                       q_ref[0, :, h * MLA_QK:(h + 1) * MLA_QK],
                                       (((1,), (1,)), ((), ())), preferred_element_type=F32)

    def consume(j, s_ref, masked):
        off = pl.multiple_of(j * tk, tk)
        s = [s_ref[h] for h in hs]
        if masked:
            keys = lax.broadcasted_iota(jnp.int32, (tk, tq), 0)
            queries = lax.broadcasted_iota(jnp.int32, (tk, tq), 1)
            s = [jnp.where(keys <= queries, s[h], -jnp.inf) for h in hs]
        m_old = [m_sc[h][0:1] for h in hs]
        m_new = [jnp.maximum(m_old[h], jnp.max(s[h], axis=0, keepdims=True)) for h in hs]
        p = [jnp.exp2(s[h] - m_new[h]) for h in hs]
        alpha = [jnp.exp2(m_old[h] - m_new[h]) for h in hs]
        for h in hs:
            l_new = alpha[h] * l_sc[h][0:1] + jnp.sum(p[h], axis=0, keepdims=True)
            l_sc[h] = jnp.broadcast_to(l_new, (sub, tq))
            m_sc[h] = jnp.broadcast_to(m_new[h], (sub, tq))
        pv = [lax.dot_general(v_ref[0, pl.ds(off, tk), h * MLA_VDIM:(h + 1) * MLA_VDIM], p[h].astype(BF16),
                              (((0,), (0,)), ((), ())), preferred_element_type=F32) for h in hs]
        for h in hs:
            acc_sc[h] = alpha[h] * acc_sc[h] + pv[h]

    def pair(jj, c):
        j = 2 * jj
        produce(j + 1, sb_sc)
        consume(j, sa_sc, False)
        produce(j + 2, sa_sc)
        consume(j + 1, sb_sc, False)
        return c

    produce(0, sa_sc)
    lax.fori_loop(0, i // 2, pair, 0)

    @pl.when(i % 2 == 0)
    def _():
        consume(i, sa_sc, True)

    @pl.when(i % 2 == 1)
    def _():
        produce(i, sb_sc)
        consume(i - 1, sa_sc, False)
        consume(i, sb_sc, True)

    for h in hs:
        o = acc_sc[h] / l_sc[h][0:1]
        o = o * lax.rsqrt(jnp.mean(o * o, axis=0, keepdims=True) + NORM_EPS)
        o_ref[0, :, h * MLA_VDIM:(h + 1) * MLA_VDIM] = (
            jnp.transpose(o) * g_ref[:, h * MLA_VDIM:(h + 1) * MLA_VDIM]).astype(o_ref.dtype)


def _mla_attention(l, q, k, v, g, batch, seq):
    tq, tk, hp = TQ_ATTN, TK_ATTN, HP_ATTN
    assert tq == tk and MLA_VDIM == LANES and hp == MLA_HEADS
    q = q.reshape(batch, seq, MLA_HEADS * MLA_QK)
    k = k.reshape(batch, seq, MLA_HEADS * MLA_QK)
    v = v.reshape(batch, seq, MLA_WIDTH)
    out = pl.pallas_call(
        functools.partial(_attn_kernel, layer=l, tq=tq, tk=tk, hp=hp),
        out_shape=jax.ShapeDtypeStruct((batch, seq, MLA_WIDTH), BF16),
        grid=(batch, MLA_HEADS // hp, seq // tq),
        in_specs=[pl.BlockSpec((1, tq, hp * MLA_QK), lambda b, h, i: (b, i, h)),
                  pl.BlockSpec((1, seq, hp * MLA_QK), lambda b, h, i: (b, 0, h)),
                  pl.BlockSpec((1, seq, hp * MLA_VDIM), lambda b, h, i: (b, 0, h)),
                  _resident(g)],
        out_specs=pl.BlockSpec((1, tq, hp * MLA_VDIM), lambda b, h, i: (b, i, h)),
        scratch_shapes=[pltpu.VMEM((hp, SUBLANES, tq), F32), pltpu.VMEM((hp, SUBLANES, tq), F32),
                        pltpu.VMEM((hp, MLA_VDIM, tq), F32),
                        pltpu.VMEM((hp, tk, tq), F32), pltpu.VMEM((hp, tk, tq), F32)],
        compiler_params=_cparams(("parallel", "parallel", "arbitrary")),
        name="mla_attention",
    )(q, k, v, g)
    return out.reshape(batch * seq, MLA_WIDTH)


def _tile_heads(z):
    return jnp.concatenate([z] * RWKV_HEADS, axis=0)


def _rwkv_chunk_kernel(fa_ref, ft_ref, *, nb, chunk, cps):
    seg = cps * chunk
    n = nb * seg
    w = RWKV_WIDTH
    hd = RWKV_HEAD
    x = fa_ref[...].reshape(n, fa_ref.shape[-1])
    r16, v16, a16, k16, b16 = (x[:, s * w:(s + 1) * w] for s in (FA_RT, FA_V, FA_AT, FA_KT, FA_BT))

    ri, ci, same_chunk = _rwkv_masks(chunk)
    bd = _div(ri, chunk) == _div(ci, hd)
    rt = lax.broadcasted_iota(jnp.int32, (chunk, w), 0)
    cs = _mod(lax.broadcasted_iota(jnp.int32, (chunk, w), 1), chunk)
    strict = cs < rt
    incl = cs <= rt
    c16 = _div(rt, 16) == _div(cs, 16)
    c32 = _div(rt, 32) == _div(cs, 32)
    eye = jnp.where(rt == cs, 1.0, 0.0)
    units = nb * cps

    def block_diag(z):
        return jnp.where(same_chunk, _tile_heads(z), 0.0)

    def put(u, sec, val):
        b, j = divmod(u, cps)
        ft_ref[b, j * chunk:(j + 1) * chunk, sec * w:(sec + 1) * w] = val.astype(ft_ref.dtype)

    def mm(x, y):
        return jnp.dot(x, y, preferred_element_type=F32)

    us = range(units)
    sls = [slice(u * chunk, (u + 1) * chunk) for u in us]
    a_st = [jnp.where(bd, _tile_heads(a16[sl]), 0.0) for sl in sls]
    v_st = [jnp.where(bd, _tile_heads(v16[sl]), 0.0) for sl in sls]
    kb_t = [jnp.concatenate([jnp.transpose(jnp.where(bd, _tile_heads(k16[sl]), 0.0)),
                             jnp.transpose(jnp.where(bd, _tile_heads(b16[sl]), 0.0))], axis=1) for sl in sls]
    sc = [mm(jnp.concatenate([a16[sls[u]], r16[sls[u]]], axis=0), kb_t[u]) for u in us]
    l_ab = [jnp.where(strict, sc[u][0:chunk, w:], 0.0) for u in us]
    l_ak = [jnp.where(strict, sc[u][0:chunk, 0:w], 0.0).astype(BF16) for u in us]
    a_rk = [jnp.where(incl, sc[u][chunk:, 0:w], 0.0).astype(BF16) for u in us]
    for u in us:
        put(u, FT_ARB, jnp.where(incl, sc[u][chunk:, w:], 0.0))
    xm = [-jnp.where(c16, l_ab[u], 0.0) for u in us]
    xm16 = [z.astype(BF16) for z in xm]
    off32 = [block_diag(jnp.where(c32 & jnp.logical_not(c16), l_ab[u], 0.0).astype(BF16)) for u in us]
    off64 = [block_diag(jnp.where(jnp.logical_not(c32), l_ab[u], 0.0).astype(BF16)) for u in us]
    x2 = [mm(xm16[u], block_diag(xm16[u])).astype(BF16) for u in us]
    x2_bd = [block_diag(z) for z in x2]
    lv = [mm(jnp.concatenate([l_ak[u], a_rk[u]], axis=0), v_st[u]) for u in us]
    wv = [block_diag(lv[u][0:chunk].astype(BF16)) for u in us]
    t_lo = [eye + xm[u] for u in us]
    tx = [mm(jnp.concatenate([t_lo[u].astype(BF16), x2[u]], axis=0), x2_bd[u]) for u in us]
    t_lo = [(t_lo[u] + tx[u][0:chunk]).astype(BF16) for u in us]
    x4 = [tx[u][chunk:] for u in us]
    x4b = [z.astype(BF16) for z in x4]
    x4_bd = [block_diag(z) for z in x4b]
    for u in us:
        put(u, FT_YV, lv[u][chunk:])
    x8_bd = [block_diag(mm(x4b[u], x4_bd[u]).astype(BF16)) for u in us]
    t_hi = [eye + x4[u] for u in us]
    t_hi = [block_diag((t_hi[u] + mm(t_hi[u].astype(BF16), x8_bd[u])).astype(BF16)) for u in us]
    t_inv = [mm(t_lo[u], t_hi[u]) for u in us]
    for off in (off32, off64):
        tb = [z.astype(BF16) for z in t_inv]
        mid = [mm(tb[u], off[u]).astype(BF16) for u in us]
        t_inv = [t_inv[u] - mm(mid[u], block_diag(tb[u])) for u in us]
    tb = [z.astype(BF16) for z in t_inv]
    for u in us:
        put(u, FT_TA, mm(tb[u], a_st[u]))
    for u in us:
        put(u, FT_UV, mm(tb[u], wv[u]))


def _rwkv_scan_stages(fa_ref, fb_ref, ft_ref, pc_ref, lnw_ref, lnb_ref, o_ref, state_sc, *, layer, nb, chunk, cps):
    seg = cps * chunk
    n = nb * seg
    w = RWKV_WIDTH
    hd = RWKV_HEAD
    lnw_ref, lnb_ref = _rows(layer, lnw_ref, lnb_ref)
    ri, ci, _ = _rwkv_masks(chunk)
    bd = _div(ri, chunk) == _div(ci, hd)
    bdv = _div(ri, hd) == _div(ci, hd)
    ones_bd = jnp.where(bdv, 1.0, 0.0).astype(BF16)

    def sec(ref, b, j, s):
        return ref[b, j * chunk:(j + 1) * chunk, s * w:(s + 1) * w]

    ys = [[None] * cps for _ in range(nb)]
    bs = range(nb)
    for j in range(cps):
        gs = [state_sc[b] for b in bs]
        p1 = [lax.dot_general(jnp.concatenate([sec(ft_ref, b, j, FT_TA), sec(fa_ref, b, j, FA_RT)], axis=0),
                              gs[b].astype(BF16), (((1,), (1,)), ((), ())), preferred_element_type=F32)
              for b in bs]
        yield
        u = [(p1[b][0:chunk] + sec(ft_ref, b, j, FT_UV).astype(F32)).astype(BF16) for b in bs]
        upd = [lax.dot_general(jnp.concatenate([sec(fa_ref, b, j, FA_V), -u[b]], axis=0),
                               jnp.concatenate([sec(fb_ref, b, j, FB_KH), sec(fb_ref, b, j, FB_BH)], axis=0),
                               (((0,), (0,)), ((), ())), preferred_element_type=F32) for b in bs]
        for b in bs:
            state_sc[b] = gs[b] * pc_ref[b, pl.ds(pl.program_id(0) * cps + j, 1), :] + jnp.where(bdv, upd[b], 0.0)
        for b in bs:
            u_st = jnp.where(bd, _tile_heads(u[b]), 0.0)
            ys[b][j] = (p1[b][chunk:] + sec(ft_ref, b, j, FT_YV).astype(F32)
                        - jnp.dot(sec(ft_ref, b, j, FT_ARB), u_st, preferred_element_type=F32))
        yield

    y = jnp.concatenate([ys[b][j] for b in range(nb) for j in range(cps)], axis=0)
    mean = _exact_right_dot(y, ones_bd, parts=2) * (1.0 / hd)
    d = y - mean
    var = _exact_right_dot(d * d, ones_bd, parts=2) * (1.0 / hd)
    yn = d * lax.rsqrt(var + RWKV_LN_EPS) * lnw_ref[...] + lnb_ref[...]
    bonus = fb_ref[:, :, FB_BONUS * w:(FB_BONUS + 1) * w].reshape(n, w).astype(F32)
    gate = fb_ref[:, :, FB_GATE * w:(FB_GATE + 1) * w].reshape(n, w).astype(F32)
    o_ref[...] = ((yn + bonus) * gate).astype(o_ref.dtype).reshape(o_ref.shape)


def _rwkv_chunk(fa, batch, seq):
    chunk = RWKV_CHUNK
    w = RWKV_WIDTH
    assert RWKV_HEADS * chunk == w
    cps = RWKV_PREP_CHUNKS
    seg = cps * chunk
    return pl.pallas_call(
        functools.partial(_rwkv_chunk_kernel, nb=batch, chunk=chunk, cps=cps),
        out_shape=jax.ShapeDtypeStruct((batch, seq, 4 * w), BF16),
        grid=(seq // seg,),
        in_specs=[pl.BlockSpec((batch, seg, 5 * w), lambda c: (0, c, 0))],
        out_specs=pl.BlockSpec((batch, seg, 4 * w), lambda c: (0, c, 0)),
        compiler_params=_cparams(("parallel",)),
        name="rwkv7_chunk",
    )(fa.reshape(batch, seq, 5 * w))


ML_QK = 0
ML_V = 2 * MLSTM_HEADS * MLSTM_QK
ML_O = ML_V + MLSTM_WIDTH
ML_I = ML_O + MLSTM_WIDTH
ML_F = ML_I + LANES


def _cummax_rows(x):
    n = x.shape[0]
    row = lax.broadcasted_iota(jnp.int32, x.shape, 0)
    sh = 1
    while sh < n:
        x = jnp.maximum(x, jnp.where(row >= sh, pltpu.roll(x, sh, 0), -jnp.inf))
        sh *= 2
    return x


def _mlstm_stages(x_ref, cw_ref, cb_ref, ib_ref, fb_ref, on_ref, o_ref,
                  prev_sc, c_sc, n_sc, m_sc, *, layer, nb, chunk):
    n = nb * chunk
    nh = MLSTM_HEADS
    dk = MLSTM_QK
    dv = MLSTM_V
    qkw = nh * dk
    vw = MLSTM_WIDTH
    cb_ref, ib_ref, fb_ref, on_ref = _rows(layer, cb_ref, ib_ref, fb_ref, on_ref)
    x = x_ref[...].reshape(n, MLSTM_IN)
    qk_raw = x[:, ML_QK:ML_V]
    prev = prev_sc[...]
    conv = cb_ref[...] + qk_raw * cw_ref[MLSTM_CONV - 1:MLSTM_CONV, :]
    for s in range(1, MLSTM_CONV):
        conv = conv + _shift_rows(qk_raw, prev, s, chunk) * cw_ref[MLSTM_CONV - 1 - s:MLSTM_CONV - s, :]
    prev_sc[...] = qk_raw
    qk = conv * _sigmoid(conv)
    q_all = qk[:, 0:qkw] * (dk ** -0.5)
    k_all = qk[:, qkw:]
    v_all = x[:, ML_V:ML_O]
    o_pre = x[:, ML_O:ML_I]
    li_all = x[:, ML_I:ML_F] + ib_ref[...]
    lf_all = _log_sigmoid(x[:, ML_F:ML_F + LANES] + fb_ref[...])

    ri = lax.broadcasted_iota(jnp.int32, (chunk, chunk), 0)
    ci = lax.broadcasted_iota(jnp.int32, (chunk, chunk), 1)
    causal = ci <= ri
    tri = jnp.where(causal, 1.0, 0.0)
    lane_k = lax.broadcasted_iota(jnp.int32, (chunk, qkw), 1)
    lane_v = lax.broadcasted_iota(jnp.int32, (chunk, vw), 1)
    rc = lax.broadcasted_iota(jnp.int32, (qkw, vw), 0)
    cc = lax.broadcasted_iota(jnp.int32, (qkw, vw), 1)
    cmask = _div(rc, dk) == _div(cc, dv)
    expand_v = jnp.where(rc == _div(cc, dv), 1.0, 0.0).astype(BF16)
    rk = lax.broadcasted_iota(jnp.int32, (qkw, qkw), 0)
    ck = lax.broadcasted_iota(jnp.int32, (qkw, qkw), 1)
    expand_k = jnp.where(rk == _div(ck, dk), 1.0, 0.0).astype(BF16)
    gather_k = jnp.where(_div(rk, dk) == ck, 1.0, 0.0).astype(BF16)

    bs = range(nb)
    sls = [slice(b * chunk, (b + 1) * chunk) for b in bs]
    q = [q_all[sl] for sl in sls]
    k = [k_all[sl] for sl in sls]
    k16 = [z.astype(BF16) for z in k]
    v = [v_all[sl] for sl in sls]
    li = [li_all[sl] for sl in sls]
    c_old = [c_sc[b] for b in bs]
    n_old = [n_sc[b] for b in bs]
    m_prev = [m_sc[b] for b in bs]
    g = [_exact_left_dot(tri, lf_all[sl]) for sl in sls]
    lig = [li[b] - g[b] for b in bs]
    inter_log = [g[b] + m_prev[b] for b in bs]
    m_t = [jnp.maximum(inter_log[b], g[b] + _cummax_rows(lig[b])) for b in bs]
    inter_w = [jnp.exp(inter_log[b] - m_t[b]) for b in bs]
    log2e = math.log2(math.e)
    gm = [(g[b] - m_t[b]) * log2e for b in bs]
    lig_t = [jnp.transpose(z * log2e) for z in lig]
    qn = [_exact_right_dot(q[b] * n_old[b], gather_k, parts=2) for b in bs]
    q_c = [_bdot(q[b], c_old[b]) for b in bs]
    ssum = [jnp.zeros((chunk, LANES), F32) for _ in bs]
    num = [jnp.zeros((chunk, vw), F32) for _ in bs]
    yield
    for h in range(nh):
        mk = (lane_k >= h * dk) & (lane_k < (h + 1) * dk)
        mv = (lane_v >= h * dv) & (lane_v < (h + 1) * dv)
        qk_h = [lax.dot_general(jnp.where(mk, q[b], 0.0).astype(BF16), k16[b], (((1,), (1,)), ((), ())),
                                preferred_element_type=F32) for b in bs]
        d = [jnp.broadcast_to(gm[b][:, h:h + 1], (chunk, chunk)) + lig_t[b][h:h + 1, :] for b in bs]
        s = [qk_h[b] * jnp.exp2(jnp.where(causal, d[b], -jnp.inf)) for b in bs]
        ssum = [jnp.where(lane_k == h, jnp.sum(s[b], axis=-1, keepdims=True), ssum[b]) for b in bs]
        num = [num[b] + _bdot(s[b], jnp.where(mv, v[b], 0.0)) for b in bs]
        yield
    den = [inter_w[b] * qn[b] + ssum[b] for b in bs]
    rden = [1.0 / jnp.maximum(jnp.abs(den[b]), jnp.exp(-m_t[b])) for b in bs]
    g_last = [g[b][chunk - 1:chunk, :] for b in bs]
    a_all = [g_last[b] - g[b] + li[b] for b in bs]
    m_new = [jnp.maximum(g_last[b] + m_prev[b], jnp.max(a_all[b], axis=0, keepdims=True)) for b in bs]
    dec = [jnp.exp(g_last[b] + m_prev[b] - m_new[b]) for b in bs]
    wts = [jnp.exp(a_all[b] - m_new[b]) for b in bs]
    per_head = [jnp.concatenate([inter_w[b], rden[b], wts[b], jnp.broadcast_to(dec[b], (8, LANES))], axis=0)
                for b in bs]
    on_v = [_exact_right_dot(per_head[b], expand_v, parts=2) for b in bs]
    on_k = [_exact_right_dot(per_head[b][2 * chunk:], expand_k, parts=2) for b in bs]
    hs = [(on_v[b][0:chunk] * q_c[b] + num[b]) * on_v[b][chunk:2 * chunk] for b in bs]
    for b in bs:
        c_sc[b] = c_old[b] * on_v[b][3 * chunk:3 * chunk + 1] + jnp.where(
            cmask, _bdot_tn(k16[b], on_v[b][2 * chunk:3 * chunk] * v[b]), 0.0)
        n_sc[b] = n_old[b] * on_k[b][chunk:chunk + 1] + jnp.sum(on_k[b][0:chunk] * k[b], axis=0, keepdims=True)
        m_sc[b] = m_new[b]

    hh = jnp.concatenate(hs, axis=0)
    rv = lax.broadcasted_iota(jnp.int32, (vw, vw), 0)
    cv = lax.broadcasted_iota(jnp.int32, (vw, vw), 1)
    head_ones = jnp.where(_div(rv, dv) == _div(cv, dv), 1.0, 0.0).astype(BF16)
    ms = _exact_right_dot(hh * hh, head_ones, parts=2) * (1.0 / dv)
    out = hh * lax.rsqrt(ms + NORM_EPS) * on_ref[...] * _sigmoid(o_pre)
    o_ref[...] = out.astype(o_ref.dtype).reshape(o_ref.shape)


def _recurrent_kernel(x_ref, cw_ref, cb_ref, ib_ref, fb_ref, on_ref, fa_ref, fbk_ref, ft_ref, pc_ref, lnw_ref, lnb_ref,
                      om_ref, or_ref, prev_sc, c_sc, n_sc, m_sc, state_sc, *, layer, nb, chunk, rwkv_chunk, cps):
    @pl.when(pl.program_id(0) == 0)
    def _():
        for sc in (prev_sc, c_sc, n_sc, m_sc, state_sc):
            sc[...] = jnp.zeros(sc.shape, F32)

    scan = _rwkv_scan_stages(fa_ref, fbk_ref, ft_ref, pc_ref, lnw_ref, lnb_ref, or_ref, state_sc,
                             layer=layer, nb=nb, chunk=rwkv_chunk, cps=cps)
    mlstm = _mlstm_stages(x_ref, cw_ref, cb_ref, ib_ref, fb_ref, on_ref, om_ref, prev_sc, c_sc, n_sc, m_sc,
                          layer=layer, nb=nb, chunk=chunk)
    order = [scan, mlstm] + [scan, scan, mlstm] * (MLSTM_HEADS - 1) + [scan, mlstm]
    pending = {id(scan): 2 * cps + 1, id(mlstm): MLSTM_HEADS + 2}
    for g in order:
        next(g, None)
        pending[id(g)] -= 1
    for g in (scan, mlstm):
        for _ in range(pending[id(g)]):
            next(g, None)


def _recurrent(l, mlstm_in, fa, fb, ft, pc, batch, seq, cw, cb, ib, fbias, on, ln_w, ln_b):
    chunk = MLSTM_CHUNK
    rchunk = RWKV_CHUNK
    assert chunk % rchunk == 0
    w = RWKV_WIDTH
    x = mlstm_in.reshape(batch, seq, MLSTM_IN)
    fa = fa.reshape(batch, seq, 5 * w)
    fb = fb.reshape(batch, seq, 4 * w)
    pc = pc.reshape(batch, seq // rchunk, w)
    blk = lambda width: pl.BlockSpec((batch, chunk, width), lambda c: (0, c, 0))
    om, orw = pl.pallas_call(
        functools.partial(_recurrent_kernel, layer=l, nb=batch, chunk=chunk, rwkv_chunk=rchunk, cps=chunk // rchunk),
        out_shape=(jax.ShapeDtypeStruct((batch, seq, MLSTM_WIDTH), BF16),
                   jax.ShapeDtypeStruct((batch, seq, w), BF16)),
        grid=(seq // chunk,),
        in_specs=[blk(MLSTM_IN), _resident(cw, l), _resident(cb), _resident(ib), _resident(fbias), _resident(on),
                  blk(2 * w), blk(4 * w), blk(4 * w), _resident(pc), _resident(ln_w), _resident(ln_b)],
        out_specs=(blk(MLSTM_WIDTH), blk(w)),
        scratch_shapes=[pltpu.VMEM((batch * chunk, 2 * MLSTM_HEADS * MLSTM_QK), F32),
                        pltpu.VMEM((batch, MLSTM_HEADS * MLSTM_QK, MLSTM_WIDTH), F32),
                        pltpu.VMEM((batch, 1, MLSTM_HEADS * MLSTM_QK), F32),
                        pltpu.VMEM((batch, 1, LANES), F32),
                        pltpu.VMEM((batch, w, w), F32)],
        compiler_params=_cparams(("arbitrary",)),
        name="mlstm_rwkv_scan",
    )(x, cw, cb, ib, fbias, on, fa, fb, ft, pc, ln_w, ln_b)
    return om.reshape(batch * seq, MLSTM_WIDTH), orw.reshape(batch * seq, w)


def _ffn_kernel(x_ref, ya_ref, yb_ref, yc_ref, wo_ref, g_ref, wg_ref, wu_ref, wd_ref, fg_ref,
                o_ref, act_sc, *, layer, final_norm, tf):
    (g_ref,) = _rows(layer, g_ref)
    y = jnp.concatenate([ya_ref[...], yb_ref[...], yc_ref[...]], axis=-1)
    x1 = x_ref[...] + jnp.dot(y, wo_ref[...], preferred_element_type=F32)
    h = _rms(x1, g_ref[...]).astype(BF16)
    for c in range(D_FF // tf):
        gate = jnp.dot(h, wg_ref[:, c * tf:(c + 1) * tf], preferred_element_type=F32)
        up = jnp.dot(h, wu_ref[:, c * tf:(c + 1) * tf], preferred_element_type=F32)
        act_sc[:, c * tf:(c + 1) * tf] = (gate * _sigmoid(gate) * up).astype(BF16)
    out = x1 + jnp.dot(act_sc[...], wd_ref[...], preferred_element_type=F32)
    if final_norm:
        out = _rms(out, fg_ref[...])
    o_ref[...] = out


def _out_ffn(l, x, ya, yb, yc, wo, g, wg, wu, wd, fg, final_norm):
    t = x.shape[0]
    tm, tf = TM_FFN, TF_FFN
    row = lambda w: pl.BlockSpec((tm, w), lambda i: (i, 0))
    return pl.pallas_call(
        functools.partial(_ffn_kernel, layer=l, final_norm=final_norm, tf=tf),
        out_shape=jax.ShapeDtypeStruct((t, D_MODEL), F32),
        grid=(t // tm,),
        in_specs=[row(D_MODEL), row(MLA_WIDTH), row(RWKV_WIDTH), row(MLSTM_WIDTH), _resident(wo, l),
                  _resident(g), _resident(wg, l), _resident(wu, l), _resident(wd, l), _resident(fg)],
        out_specs=row(D_MODEL),
        scratch_shapes=[pltpu.VMEM((tm, D_FF), BF16)],
        compiler_params=_cparams(("parallel",)),
        name="out_ffn",
    )(x, ya, yb, yc, wo, g, wg, wu, wd, fg)


def _pad_cols(w, width):
    return jnp.pad(w, [(0, 0)] * (w.ndim - 1) + [(0, width - w.shape[-1])])


def _pad_rows(w, height):
    return jnp.pad(w, [(0, 0)] * (w.ndim - 2) + [(0, height - w.shape[-2]), (0, 0)])


def _rot_half_cols(w):
    half = w.shape[-1] // 2
    return jnp.concatenate([-w[..., half:], w[..., :half]], axis=-1)


def _stacked_weights(w_in, mla_w_uq, mla_w_ukv, rwkv_w2, rwkv_a2, rwkv_g2):
    depth = w_in.shape[0]
    wt = jnp.swapaxes(w_in, 1, 2)
    c_q, c_kv, k_pe = wt[:, 0:256], wt[:, 256:512], wt[:, 512:576]
    rw = wt[:, 576:1472]
    ml = wt[:, 1472:2248]
    k_pe_rot = jnp.swapaxes(_rot_half_cols(jnp.swapaxes(k_pe, 1, 2)), 1, 2)
    w_mla = jnp.concatenate([c_q, c_kv, _pad_rows(k_pe, LANES), _pad_rows(k_pe_rot, LANES)], axis=1)
    w_mlstm = jnp.concatenate([ml[:, 0:256], ml[:, 256:512], ml[:, 520:776],
                               _pad_rows(ml[:, 512:516], LANES), _pad_rows(ml[:, 516:520], LANES)], axis=1)
    w_all = jnp.concatenate([w_mla, rw, w_mlstm], axis=1).astype(BF16)

    uq = mla_w_uq.reshape(depth, MLA_Q_LORA, MLA_HEADS, MLA_NOPE + MLA_ROPE)
    nope = uq[..., :MLA_NOPE].reshape(depth, MLA_Q_LORA, MLA_HEADS * MLA_NOPE)
    pe = _pad_cols(uq[..., MLA_NOPE:], LANES).reshape(depth, MLA_Q_LORA, MLA_HEADS * LANES)
    per = _pad_cols(_rot_half_cols(uq[..., MLA_NOPE:]), LANES).reshape(depth, MLA_Q_LORA, MLA_HEADS * LANES)
    wq = jnp.concatenate([nope, pe, per], axis=-1).astype(BF16)
    ukv = mla_w_ukv.reshape(depth, MLA_KV_LORA, MLA_HEADS, MLA_NOPE + MLA_VDIM)
    wkv = jnp.concatenate([ukv[..., :MLA_NOPE].reshape(depth, MLA_KV_LORA, -1),
                           ukv[..., MLA_NOPE:].reshape(depth, MLA_KV_LORA, -1)], axis=-1).astype(BF16)

    rows = lambda before, wt: jnp.pad(wt, ((0, 0), (before, LANES - before - wt.shape[1]), (0, 0))).astype(BF16)
    w2p = rows(0, rwkv_w2)
    a2p = rows(RWKV_DECAY_LORA, rwkv_a2)
    g2p = rows(RWKV_DECAY_LORA + RWKV_AAA_LORA, rwkv_g2)
    return w_all, wq, wkv, w2p, a2p, g2p


def kernel(x, positions, mix_norm, w_in, mla_q_norm, mla_w_uq, mla_kv_norm, mla_w_ukv, mla_out_norm, rwkv_mu, rwkv_w0, rwkv_w2, rwkv_a0, rwkv_a2, rwkv_g2, rwkv_k_k, rwkv_k_a, rwkv_r_k, rwkv_ln_w, rwkv_ln_b, mlstm_conv_w, mlstm_conv_b, mlstm_i_bias, mlstm_f_bias, mlstm_out_norm, w_out, ffn_norm, w_gate, w_up, w_down, final_norm):
    batch, seq, _ = x.shape
    depth = w_in.shape[0]
    xt = x.reshape(batch * seq, D_MODEL)
    cos, sin, (wo, wg, wu, wd) = _rope_tables_and_casts(positions, (w_out, w_gate, w_up, w_down))
    w_all, wq, wkv, w2p, a2p, g2p = _stacked_weights(w_in, mla_w_uq, mla_w_ukv, rwkv_w2, rwkv_a2, rwkv_g2)
    ml_ib = _pad_cols(mlstm_i_bias, LANES)
    ml_fb = _pad_cols(mlstm_f_bias, LANES)
    for l in range(depth):
        q, k, v, fa, fb, pc, mlstm_in = _inproj(
            l, xt, seq, mix_norm, w_all, cos, sin, mla_q_norm, mla_kv_norm, wq, wkv,
            rwkv_mu, rwkv_w0, rwkv_a0, rwkv_k_k, rwkv_k_a, rwkv_r_k,
            w2p, a2p, g2p)
        y_mla = _mla_attention(l, q, k, v, mla_out_norm, batch, seq)
        ft = _rwkv_chunk(fa, batch, seq)
        y_mlstm, y_rwkv = _recurrent(l, mlstm_in, fa, fb, ft, pc, batch, seq, mlstm_conv_w, mlstm_conv_b, ml_ib, ml_fb,
                                     mlstm_out_norm, rwkv_ln_w, rwkv_ln_b)
        xt = _out_ffn(l, xt, y_mla, y_rwkv, y_mlstm, wo, ffn_norm, wg, wu, wd,
                      final_norm.reshape(1, -1), final_norm=(l == depth - 1))
    return xt.reshape(batch, seq, D_MODEL)
```

```python
import functools
import math

import jax
import jax.numpy as jnp
from jax import lax
from jax.experimental import pallas as pl
from jax.experimental.pallas import tpu as pltpu

F32 = jnp.float32
BF16 = jnp.bfloat16

D_MODEL = 1024
DEPTH = 2
MLA_HEADS = 4
MLA_NOPE = 128
MLA_ROPE = 64
MLA_VDIM = 128
MLA_Q_LORA = 256
MLA_KV_LORA = 256
MLA_WIDTH = MLA_HEADS * MLA_VDIM
MLA_QK = 256
ROPE_THETA = 10000.0
RWKV_HEADS = 4
RWKV_HEAD = 64
RWKV_WIDTH = 256
RWKV_DECAY_LORA = 32
RWKV_AAA_LORA = 32
RWKV_GATE_LORA = 64
RWKV_IN = 3 * RWKV_WIDTH + 128
RWKV_LN_EPS = 64e-5
MLSTM_HEADS = 4
MLSTM_QK = 32
MLSTM_V = 64
MLSTM_WIDTH = 256
MLSTM_CONV = 4
MLSTM_IN = 1024
D_FF = 2816
NORM_EPS = 1e-6
LANES = 128
SUBLANES = 8

MLA_CQ = 0
MLA_CKV = MLA_CQ + MLA_Q_LORA
MLA_KPE = MLA_CKV + MLA_KV_LORA
MLA_KPER = MLA_KPE + LANES
MLA_IN = MLA_KPER + LANES

TM_INPROJ = 512
TQ_ATTN = 512
TK_ATTN = 512
HP_ATTN = 4
RWKV_CHUNK = 64
RWKV_PREP_CHUNKS = 2
MLSTM_CHUNK = 256
TM_FFN = 512
TF_FFN = 256
VMEM_LIMIT = 56 * 1024 * 1024


def _cparams(sem):
    return pltpu.CompilerParams(dimension_semantics=sem, vmem_limit_bytes=VMEM_LIMIT)


def _resident(a, layer=None):
    if layer is None:
        nd = a.ndim
        return pl.BlockSpec(a.shape, lambda *_: (0,) * nd, pipeline_mode=pl.Buffered(1))
    nd = a.ndim - 1
    return pl.BlockSpec((None,) + a.shape[1:], lambda *_: (layer,) + (0,) * nd, pipeline_mode=pl.Buffered(1))


def _rows(layer, *refs):
    return [r.at[layer:layer + 1] for r in refs]


def _bdot(a, b):
    return jnp.dot(a.astype(BF16), b.astype(BF16), preferred_element_type=F32)


def _dot_nt(a, b):
    return lax.dot_general(a, b, (((1,), (1,)), ((), ())), preferred_element_type=F32)


def _bdot_nt(a, b):
    return lax.dot_general(a.astype(BF16), b.astype(BF16), (((1,), (1,)), ((), ())),
                           preferred_element_type=F32)


def _bdot_tn(a, b):
    return lax.dot_general(a.astype(BF16), b.astype(BF16), (((0,), (0,)), ((), ())),
                           preferred_element_type=F32)


def _split3(x):
    h = x.astype(BF16)
    r1 = x - h.astype(F32)
    m = r1.astype(BF16)
    lo = (r1 - m.astype(F32)).astype(BF16)
    return h, m, lo


def _exact_left_dot(sel, x):
    h, m, lo = _split3(x)
    s = sel.astype(BF16)
    return (jnp.dot(s, h, preferred_element_type=F32) + jnp.dot(s, m, preferred_element_type=F32)
            + jnp.dot(s, lo, preferred_element_type=F32))


def _exact_right_dot(x, sel, parts=3):
    pieces = _split3(x)[:parts]
    s = sel.astype(BF16)
    out = jnp.dot(pieces[0], s, preferred_element_type=F32)
    for p in pieces[1:]:
        out = out + jnp.dot(p, s, preferred_element_type=F32)
    return out


def _rms(x, g):
    return x * lax.rsqrt(jnp.mean(x * x, axis=-1, keepdims=True) + NORM_EPS) * g


def _sigmoid(x):
    return 1.0 / (1.0 + jnp.exp(-x))


def _log_sigmoid(x):
    return jnp.minimum(x, 0.0) - jnp.log1p(jnp.exp(-jnp.abs(x)))


def _div(x, d):
    assert d & (d - 1) == 0
    return lax.shift_right_logical(x, d.bit_length() - 1)


def _mod(x, d):
    assert d & (d - 1) == 0
    return lax.bitwise_and(x, d - 1)


def _shift_rows(x, prev, s, chunk):
    n = x.shape[0]
    row = lax.broadcasted_iota(jnp.int32, x.shape, 0)
    return jnp.where(_mod(row, chunk) >= s, pltpu.roll(x, s, 0), pltpu.roll(prev, n - chunk + s, 0))


def _rope_cast_kernel(pos_ref, invf_ref, *refs):
    n = (len(refs) - 2) // 2
    w_refs, cos_ref, sin_ref, o_refs = refs[:n], refs[n], refs[n + 1], refs[n + 2:]
    ang = pos_ref[...].astype(F32) * invf_ref[...]
    cos_ref[...] = jnp.cos(ang)
    sin_ref[...] = jnp.sin(ang)
    for w_ref, o_ref in zip(w_refs, o_refs):
        o_ref[...] = w_ref[...].astype(BF16)


def _rope_tables_and_casts(positions, weights):
    t = positions.size
    tm = min(1024, t)
    steps = t // tm
    inv_freq = ROPE_THETA ** (-jnp.arange(0, MLA_ROPE, 2, dtype=F32) / MLA_ROPE)
    invf = jnp.tile(inv_freq, LANES // (MLA_ROPE // 2))[None, :]
    flat = [w.reshape(-1, w.shape[-1]) for w in weights]
    rows = [f.shape[0] // steps for f in flat]
    assert all(f.shape[0] % steps == 0 and r % 16 == 0 for f, r in zip(flat, rows))
    slab = lambda r, c: pl.BlockSpec((r, c), lambda i: (i, 0))
    outs = pl.pallas_call(
        _rope_cast_kernel,
        out_shape=(jax.ShapeDtypeStruct((t, LANES), F32), jax.ShapeDtypeStruct((t, LANES), F32))
        + tuple(jax.ShapeDtypeStruct(f.shape, BF16) for f in flat),
        grid=(steps,),
        in_specs=[slab(tm, 1), pl.BlockSpec((1, LANES), lambda i: (0, 0))]
        + [slab(r, f.shape[1]) for f, r in zip(flat, rows)],
        out_specs=(slab(tm, LANES), slab(tm, LANES)) + tuple(slab(r, f.shape[1]) for f, r in zip(flat, rows)),
        compiler_params=_cparams(("parallel",)),
        name="rope_tables_casts",
    )(positions.reshape(t, 1), invf, *flat)
    return outs[0], outs[1], [o.reshape(w.shape) for o, w in zip(outs[2:], weights)]


(FA_RT, FA_V, FA_AT, FA_KT, FA_BT) = range(5)
(FB_KH, FB_BH, FB_BONUS, FB_GATE) = range(4)
(FT_TA, FT_UV, FT_YV, FT_ARB) = range(4)
RWKV_GROUP = 256


def _segsum(x, seg):
    assert 2 * seg == LANES and x.shape[1] % LANES == 0
    out = []
    for blk in range(x.shape[1] // LANES):
        xb = x[:, blk * LANES:(blk + 1) * LANES]
        low = lax.broadcasted_iota(jnp.int32, xb.shape, 1) < seg
        s_low = jnp.sum(jnp.where(low, xb, 0.0), axis=-1, keepdims=True)
        s_high = jnp.sum(jnp.where(low, 0.0, xb), axis=-1, keepdims=True)
        out.append(jnp.where(low, s_low, s_high))
    return jnp.concatenate(out, axis=-1)


def _rwkv_masks(chunk):
    w = RWKV_WIDTH
    r = lax.broadcasted_iota(jnp.int32, (w, w), 0)
    c = lax.broadcasted_iota(jnp.int32, (w, w), 1)
    return r, c, _div(r, chunk) == _div(c, chunk)


def _rwkv_features(xs, r0, prm, fa_ref, fb_ref, pc_ref, chunk):
    w0_ref, a0_ref, kk_ref, ka_ref, rk_ref, w2_ref, a2_ref, g2_ref = prm
    w = RWKV_WIDTH
    hd = RWKV_HEAD
    n = xs.shape[0]
    r = xs[:, 0:w]
    k = xs[:, w:2 * w]
    v = xs[:, 2 * w:3 * w]
    lor = xs[:, 3 * w:]
    ld = -math.exp(-0.5) * _sigmoid(w0_ref[...] + _bdot(jnp.tanh(lor), w2_ref[...]))
    a = _sigmoid(a0_ref[...] + _bdot(lor, a2_ref[...]))
    g = _bdot(_sigmoid(lor), g2_ref[...])
    kk = k * kk_ref[...]
    kk = kk / jnp.maximum(jnp.sqrt(_segsum(kk * kk, hd)), 1e-12)
    k2 = k * (1.0 + (a - 1.0) * ka_ref[...])
    kb = kk * a
    bonus = _segsum(r * k2 * rk_ref[...], hd) * v

    assert n <= RWKV_WIDTH
    ri, ci, same_chunk = _rwkv_masks(chunk)
    tri = jnp.where(same_chunk & (ci <= ri), 1.0, 0.0)[0:n, 0:n]
    cl = _exact_left_dot(tri, ld)
    units = n // chunk
    cl_last = jnp.concatenate(
        [jnp.broadcast_to(cl[(u + 1) * chunk - 1:(u + 1) * chunk, :], (chunk, w)) for u in range(units)], axis=0)
    e_neg = jnp.exp(-cl)
    e_end = jnp.exp(cl_last - cl)
    rows = slice(r0, r0 + n)

    def put(ref, sec, val):
        ref[rows, sec * w:(sec + 1) * w] = val.astype(ref.dtype)

    put(fa_ref, FA_RT, r * jnp.exp(cl))
    put(fa_ref, FA_V, v)
    put(fa_ref, FA_AT, kk * jnp.exp(cl - ld))
    put(fa_ref, FA_KT, k2 * e_neg)
    put(fa_ref, FA_BT, kb * e_neg)
    put(fb_ref, FB_KH, k2 * e_end)
    put(fb_ref, FB_BH, kb * e_end)
    put(fb_ref, FB_BONUS, bonus)
    put(fb_ref, FB_GATE, g)
    for u in range(units):
        c = r0 // chunk + u
        pc_ref[c:c + 1, :] = jnp.exp(cl_last[u * chunk:u * chunk + 1, :])


def _inproj_kernel(x_ref, g_ref, w_ref, cos_ref, sin_ref, qn_ref, kvn_ref, wq_ref, wkv_ref,
                   mu_ref, w0_ref, a0_ref, kk_ref, ka_ref, rk_ref, w2_ref, a2_ref, g2_ref,
                   q_ref, k_ref, v_ref, fa_ref, fb_ref, pc_ref, mlstm_ref, prev_sc, *, layer, chunk, tiles_per_seq):
    tm = x_ref.shape[0]
    g_ref, qn_ref, kvn_ref, mu_ref, w0_ref, a0_ref, kk_ref, ka_ref, rk_ref = _rows(
        layer, g_ref, qn_ref, kvn_ref, mu_ref, w0_ref, a0_ref, kk_ref, ka_ref, rk_ref)

    @pl.when(pl.program_id(0) % tiles_per_seq == 0)
    def _():
        prev_sc[...] = jnp.zeros(prev_sc.shape, F32)

    hb = _rms(x_ref[...], g_ref[...]).astype(BF16)
    rw = _dot_nt(hb, w_ref[MLA_IN:MLA_IN + RWKV_IN, :])
    row = lax.broadcasted_iota(jnp.int32, rw.shape, 0)
    shifted = jnp.where(row >= 1, pltpu.roll(rw, 1, 0), prev_sc[0:1, :])
    prev_sc[0:1, :] = rw[tm - 1:tm, :]
    xs = rw + (shifted - rw) * mu_ref[...]
    prm = (w0_ref, a0_ref, kk_ref, ka_ref, rk_ref, w2_ref, a2_ref, g2_ref)

    mla = _dot_nt(hb, w_ref[0:MLA_IN, :])
    groups = tm // RWKV_GROUP
    for gi in range(groups // 2):
        _rwkv_features(xs[gi * RWKV_GROUP:(gi + 1) * RWKV_GROUP], gi * RWKV_GROUP, prm, fa_ref, fb_ref, pc_ref, chunk)
    mlstm_ref[...] = _dot_nt(hb, w_ref[MLA_IN + RWKV_IN:, :])
    for gi in range(groups // 2, groups):
        _rwkv_features(xs[gi * RWKV_GROUP:(gi + 1) * RWKV_GROUP], gi * RWKV_GROUP, prm, fa_ref, fb_ref, pc_ref, chunk)

    cos = cos_ref[...]
    sin = sin_ref[...]
    scale = (MLA_NOPE + MLA_ROPE) ** -0.5 * math.log2(math.e)
    hw = MLA_HEADS * LANES
    cqn = _rms(mla[:, MLA_CQ:MLA_CKV], qn_ref[...]).astype(BF16)
    q = jnp.dot(cqn, wq_ref[...], preferred_element_type=F32)
    ckvn = _rms(mla[:, MLA_CKV:MLA_KPE], kvn_ref[...]).astype(BF16)
    kv = jnp.dot(ckvn, wkv_ref[...], preferred_element_type=F32)
    kp = (mla[:, MLA_KPE:MLA_KPER] * cos + mla[:, MLA_KPER:MLA_IN] * sin).astype(BF16)
    for h in range(MLA_HEADS):
        c0 = h * LANES
        pe = q[:, hw + c0:hw + c0 + LANES] * cos + q[:, 2 * hw + c0:2 * hw + c0 + LANES] * sin
        q_ref[:, h * MLA_QK:h * MLA_QK + LANES] = (q[:, c0:c0 + LANES] * scale).astype(BF16)
        q_ref[:, h * MLA_QK + LANES:(h + 1) * MLA_QK] = (pe * scale).astype(BF16)
        k_ref[:, h * MLA_QK:h * MLA_QK + LANES] = kv[:, c0:c0 + LANES].astype(BF16)
        k_ref[:, h * MLA_QK + LANES:(h + 1) * MLA_QK] = kp
    v_ref[...] = kv[:, hw:].astype(BF16)


def _inproj(l, x, seq, g, w, cos, sin, qn, kvn, wq, wkv, mu, w0, a0, k_k, k_a, r_k, w2p, a2p, g2p):
    t = x.shape[0]
    tm = TM_INPROJ
    chunk = RWKV_CHUNK
    assert seq % tm == 0 and tm % RWKV_GROUP == 0 and RWKV_GROUP % chunk == 0
    row = lambda width: pl.BlockSpec((tm, width), lambda i: (i, 0))
    rw = RWKV_WIDTH
    return pl.pallas_call(
        functools.partial(_inproj_kernel, layer=l, chunk=chunk, tiles_per_seq=seq // tm),
        out_shape=(jax.ShapeDtypeStruct((t, MLA_HEADS * MLA_QK), BF16),
                   jax.ShapeDtypeStruct((t, MLA_HEADS * MLA_QK), BF16),
                   jax.ShapeDtypeStruct((t, MLA_WIDTH), BF16),
                   jax.ShapeDtypeStruct((t, 5 * rw), BF16),
                   jax.ShapeDtypeStruct((t, 4 * rw), BF16),
                   jax.ShapeDtypeStruct((t // chunk, rw), F32),
                   jax.ShapeDtypeStruct((t, MLSTM_IN), F32)),
        grid=(t // tm,),
        in_specs=[row(D_MODEL), _resident(g), _resident(w, l), row(LANES), row(LANES), _resident(qn),
                  _resident(kvn), _resident(wq, l), _resident(wkv, l),
                  _resident(mu), _resident(w0), _resident(a0), _resident(k_k), _resident(k_a),
                  _resident(r_k), _resident(w2p, l), _resident(a2p, l), _resident(g2p, l)],
        out_specs=(row(MLA_HEADS * MLA_QK), row(MLA_HEADS * MLA_QK), row(MLA_WIDTH), row(5 * rw), row(4 * rw),
                   pl.BlockSpec((tm // chunk, rw), lambda i: (i, 0)), row(MLSTM_IN)),
        scratch_shapes=[pltpu.VMEM((SUBLANES, RWKV_IN), F32)],
        compiler_params=_cparams(("arbitrary",)),
        name="inproj",
    )(x, g, w, cos, sin, qn, kvn, wq, wkv, mu, w0, a0, k_k, k_a, r_k, w2p, a2p, g2p)


def _attn_kernel(q_ref, k_ref, v_ref, g_ref, o_ref, m_sc, l_sc, acc_sc, sa_sc, sb_sc, *, layer, tq, tk, hp):
    i = pl.program_id(2)
    m_sc[...] = jnp.full(m_sc.shape, -jnp.inf, F32)
    l_sc[...] = jnp.zeros(l_sc.shape, F32)
    acc_sc[...] = jnp.zeros(acc_sc.shape, F32)
    sub = m_sc.shape[1]
    hs = range(hp)
    (g_ref,) = _rows(layer, g_ref)

    def produce(j, s_ref):
        off = pl.multiple_of(j * tk, tk)
        for h in hs:
            s_ref[h] = lax.dot_general(k_ref[0, pl.ds(off, tk), h * MLA_QK:(h + 1) * MLA_QK],
                                       q_ref[0, :, h * MLA_QK:(h + 1) * MLA_QK],
                                       (((1,), (1,)), ((), ())), preferred_element_type=F32)

    def consume(j, s_ref, masked):
        off = pl.multiple_of(j * tk, tk)
        s = [s_ref[h] for h in hs]
        if masked:
            keys = lax.broadcasted_iota(jnp.int32, (tk, tq), 0)
            queries = lax.broadcasted_iota(jnp.int32, (tk, tq), 1)
            s = [jnp.where(keys <= queries, s[h], -jnp.inf) for h in hs]
        m_old = [m_sc[h][0:1] for h in hs]
        m_new = [jnp.maximum(m_old[h], jnp.max(s[h], axis=0, keepdims=True)) for h in hs]
        p = [jnp.exp2(s[h] - m_new[h]) for h in hs]
        alpha = [jnp.exp2(m_old[h] - m_new[h]) for h in hs]
        for h in hs:
            l_new = alpha[h] * l_sc[h][0:1] + jnp.sum(p[h], axis=0, keepdims=True)
            l_sc[h] = jnp.broadcast_to(l_new, (sub, tq))
            m_sc[h] = jnp.broadcast_to(m_new[h], (sub, tq))
        pv = [lax.dot_general(v_ref[0, pl.ds(off, tk), h * MLA_VDIM:(h + 1) * MLA_VDIM], p[h].astype(BF16),
                              (((0,), (0,)), ((), ())), preferred_element_type=F32) for h in hs]
        for h in hs:
            acc_sc[h] = alpha[h] * acc_sc[h] + pv[h]

    def pair(jj, c):
        j = 2 * jj
        produce(j + 1, sb_sc)
        consume(j, sa_sc, False)
        produce(j + 2, sa_sc)
        consume(j + 1, sb_sc, False)
        return c

    produce(0, sa_sc)
    lax.fori_loop(0, i // 2, pair, 0)

    @pl.when(i % 2 == 0)
    def _():
        consume(i, sa_sc, True)

    @pl.when(i % 2 == 1)
    def _():
        produce(i, sb_sc)
        consume(i - 1, sa_sc, False)
        consume(i, sb_sc, True)

    for h in hs:
        o = acc_sc[h] / l_sc[h][0:1]
        o = o * lax.rsqrt(jnp.mean(o * o, axis=0, keepdims=True) + NORM_EPS)
        o_ref[0, :, h * MLA_VDIM:(h + 1) * MLA_VDIM] = (
            jnp.transpose(o) * g_ref[:, h * MLA_VDIM:(h + 1) * MLA_VDIM]).astype(o_ref.dtype)


def _mla_attention(l, q, k, v, g, batch, seq):
    tq, tk, hp = TQ_ATTN, TK_ATTN, HP_ATTN
    assert tq == tk and MLA_VDIM == LANES and hp == MLA_HEADS
    q = q.reshape(batch, seq, MLA_HEADS * MLA_QK)
    k = k.reshape(batch, seq, MLA_HEADS * MLA_QK)
    v = v.reshape(batch, seq, MLA_WIDTH)
    out = pl.pallas_call(
        functools.partial(_attn_kernel, layer=l, tq=tq, tk=tk, hp=hp),
        out_shape=jax.ShapeDtypeStruct((batch, seq, MLA_WIDTH), BF16),
        grid=(batch, MLA_HEADS // hp, seq // tq),
        in_specs=[pl.BlockSpec((1, tq, hp * MLA_QK), lambda b, h, i: (b, i, h)),
                  pl.BlockSpec((1, seq, hp * MLA_QK), lambda b, h, i: (b, 0, h)),
                  pl.BlockSpec((1, seq, hp * MLA_VDIM), lambda b, h, i: (b, 0, h)),
                  _resident(g)],
        out_specs=pl.BlockSpec((1, tq, hp * MLA_VDIM), lambda b, h, i: (b, i, h)),
        scratch_shapes=[pltpu.VMEM((hp, SUBLANES, tq), F32), pltpu.VMEM((hp, SUBLANES, tq), F32),
                        pltpu.VMEM((hp, MLA_VDIM, tq), F32),
                        pltpu.VMEM((hp, tk, tq), F32), pltpu.VMEM((hp, tk, tq), F32)],
        compiler_params=_cparams(("parallel", "parallel", "arbitrary")),
        name="mla_attention",
    )(q, k, v, g)
    return out.reshape(batch * seq, MLA_WIDTH)


def _tile_heads(z):
    return jnp.concatenate([z] * RWKV_HEADS, axis=0)


def _rwkv_chunk_kernel(fa_ref, ft_ref, *, nb, chunk, cps):
    seg = cps * chunk
    n = nb * seg
    w = RWKV_WIDTH
    hd = RWKV_HEAD
    x = fa_ref[...].reshape(n, fa_ref.shape[-1])
    r16, v16, a16, k16, b16 = (x[:, s * w:(s + 1) * w] for s in (FA_RT, FA_V, FA_AT, FA_KT, FA_BT))

    ri, ci, same_chunk = _rwkv_masks(chunk)
    bd = _div(ri, chunk) == _div(ci, hd)
    rt = lax.broadcasted_iota(jnp.int32, (chunk, w), 0)
    cs = _mod(lax.broadcasted_iota(jnp.int32, (chunk, w), 1), chunk)
    strict = cs < rt
    incl = cs <= rt
    c16 = _div(rt, 16) == _div(cs, 16)
    c32 = _div(rt, 32) == _div(cs, 32)
    eye = jnp.where(rt == cs, 1.0, 0.0)
    units = nb * cps

    def block_diag(z):
        return jnp.where(same_chunk, _tile_heads(z), 0.0)

    def put(u, sec, val):
        b, j = divmod(u, cps)
        ft_ref[b, j * chunk:(j + 1) * chunk, sec * w:(sec + 1) * w] = val.astype(ft_ref.dtype)

    def mm(x, y):
        return jnp.dot(x, y, preferred_element_type=F32)

    us = range(units)
    sls = [slice(u * chunk, (u + 1) * chunk) for u in us]
    a_st = [jnp.where(bd, _tile_heads(a16[sl]), 0.0) for sl in sls]
    v_st = [jnp.where(bd, _tile_heads(v16[sl]), 0.0) for sl in sls]
    kb_t = [jnp.concatenate([jnp.transpose(jnp.where(bd, _tile_heads(k16[sl]), 0.0)),
                             jnp.transpose(jnp.where(bd, _tile_heads(b16[sl]), 0.0))], axis=1) for sl in sls]
    sc = [mm(jnp.concatenate([a16[sls[u]], r16[sls[u]]], axis=0), kb_t[u]) for u in us]
    l_ab = [jnp.where(strict, sc[u][0:chunk, w:], 0.0) for u in us]
    l_ak = [jnp.where(strict, sc[u][0:chunk, 0:w], 0.0).astype(BF16) for u in us]
    a_rk = [jnp.where(incl, sc[u][chunk:, 0:w], 0.0).astype(BF16) for u in us]
    for u in us:
        put(u, FT_ARB, jnp.where(incl, sc[u][chunk:, w:], 0.0))
    xm = [-jnp.where(c16, l_ab[u], 0.0) for u in us]
    xm16 = [z.astype(BF16) for z in xm]
    off32 = [block_diag(jnp.where(c32 & jnp.logical_not(c16), l_ab[u], 0.0).astype(BF16)) for u in us]
    off64 = [block_diag(jnp.where(jnp.logical_not(c32), l_ab[u], 0.0).astype(BF16)) for u in us]
    x2 = [mm(xm16[u], block_diag(xm16[u])).astype(BF16) for u in us]
    x2_bd = [block_diag(z) for z in x2]
    lv = [mm(jnp.concatenate([l_ak[u], a_rk[u]], axis=0), v_st[u]) for u in us]
    wv = [block_diag(lv[u][0:chunk].astype(BF16)) for u in us]
    t_lo = [eye + xm[u] for u in us]
    tx = [mm(jnp.concatenate([t_lo[u].astype(BF16), x2[u]], axis=0), x2_bd[u]) for u in us]
    t_lo = [(t_lo[u] + tx[u][0:chunk]).astype(BF16) for u in us]
    x4 = [tx[u][chunk:] for u in us]
    x4b = [z.astype(BF16) for z in x4]
    x4_bd = [block_diag(z) for z in x4b]
    for u in us:
        put(u, FT_YV, lv[u][chunk:])
    x8_bd = [block_diag(mm(x4b[u], x4_bd[u]).astype(BF16)) for u in us]
    t_hi = [eye + x4[u] for u in us]
    t_hi = [block_diag((t_hi[u] + mm(t_hi[u].astype(BF16), x8_bd[u])).astype(BF16)) for u in us]
    t_inv = [mm(t_lo[u], t_hi[u]) for u in us]
    for off in (off32, off64):
        tb = [z.astype(BF16) for z in t_inv]
        mid = [mm(tb[u], off[u]).astype(BF16) for u in us]
        t_inv = [t_inv[u] - mm(mid[u], block_diag(tb[u])) for u in us]
    tb = [z.astype(BF16) for z in t_inv]
    for u in us:
        put(u, FT_TA, mm(tb[u], a_st[u]))
    for u in us:
        put(u, FT_UV, mm(tb[u], wv[u]))


def _rwkv_scan_stages(fa_ref, fb_ref, ft_ref, pc_ref, lnw_ref, lnb_ref, o_ref, state_sc, *, layer, nb, chunk, cps):
    seg = cps * chunk
    n = nb * seg
    w = RWKV_WIDTH
    hd = RWKV_HEAD
    lnw_ref, lnb_ref = _rows(layer, lnw_ref, lnb_ref)
    ri, ci, _ = _rwkv_masks(chunk)
    bd = _div(ri, chunk) == _div(ci, hd)
    bdv = _div(ri, hd) == _div(ci, hd)
    ones_bd = jnp.where(bdv, 1.0, 0.0).astype(BF16)

    def sec(ref, b, j, s):
        return ref[b, j * chunk:(j + 1) * chunk, s * w:(s + 1) * w]

    ys = [[None] * cps for _ in range(nb)]
    bs = range(nb)
    for j in range(cps):
        gs = [state_sc[b] for b in bs]
        p1 = [lax.dot_general(jnp.concatenate([sec(ft_ref, b, j, FT_TA), sec(fa_ref, b, j, FA_RT)], axis=0),
                              gs[b].astype(BF16), (((1,), (1,)), ((), ())), preferred_element_type=F32)
              for b in bs]
        yield
        u = [(p1[b][0:chunk] + sec(ft_ref, b, j, FT_UV).astype(F32)).astype(BF16) for b in bs]
        upd = [lax.dot_general(jnp.concatenate([sec(fa_ref, b, j, FA_V), -u[b]], axis=0),
                               jnp.concatenate([sec(fb_ref, b, j, FB_KH), sec(fb_ref, b, j, FB_BH)], axis=0),
                               (((0,), (0,)), ((), ())), preferred_element_type=F32) for b in bs]
        for b in bs:
            state_sc[b] = gs[b] * pc_ref[b, pl.ds(pl.program_id(0) * cps + j, 1), :] + jnp.where(bdv, upd[b], 0.0)
        for b in bs:
            u_st = jnp.where(bd, _tile_heads(u[b]), 0.0)
            ys[b][j] = (p1[b][chunk:] + sec(ft_ref, b, j, FT_YV).astype(F32)
                        - jnp.dot(sec(ft_ref, b, j, FT_ARB), u_st, preferred_element_type=F32))
        yield

    y = jnp.concatenate([ys[b][j] for b in range(nb) for j in range(cps)], axis=0)
    mean = _exact_right_dot(y, ones_bd, parts=2) * (1.0 / hd)
    d = y - mean
    var = _exact_right_dot(d * d, ones_bd, parts=2) * (1.0 / hd)
    yn = d * lax.rsqrt(var + RWKV_LN_EPS) * lnw_ref[...] + lnb_ref[...]
    bonus = fb_ref[:, :, FB_BONUS * w:(FB_BONUS + 1) * w].reshape(n, w).astype(F32)
    gate = fb_ref[:, :, FB_GATE * w:(FB_GATE + 1) * w].reshape(n, w).astype(F32)
    o_ref[...] = ((yn + bonus) * gate).astype(o_ref.dtype).reshape(o_ref.shape)


def _rwkv_chunk(fa, batch, seq):
    chunk = RWKV_CHUNK
    w = RWKV_WIDTH
    assert RWKV_HEADS * chunk == w
    cps = RWKV_PREP_CHUNKS
    seg = cps * chunk
    return pl.pallas_call(
        functools.partial(_rwkv_chunk_kernel, nb=batch, chunk=chunk, cps=cps),
        out_shape=jax.ShapeDtypeStruct((batch, seq, 4 * w), BF16),
        grid=(seq // seg,),
        in_specs=[pl.BlockSpec((batch, seg, 5 * w), lambda c: (0, c, 0))],
        out_specs=pl.BlockSpec((batch, seg, 4 * w), lambda c: (0, c, 0)),
        compiler_params=_cparams(("parallel",)),
        name="rwkv7_chunk",
    )(fa.reshape(batch, seq, 5 * w))


ML_QK = 0
ML_V = 2 * MLSTM_HEADS * MLSTM_QK
ML_O = ML_V + MLSTM_WIDTH
ML_I = ML_O + MLSTM_WIDTH
ML_F = ML_I + LANES


def _cummax_rows(x):
    n = x.shape[0]
    row = lax.broadcasted_iota(jnp.int32, x.shape, 0)
    sh = 1
    while sh < n:
        x = jnp.maximum(x, jnp.where(row >= sh, pltpu.roll(x, sh, 0), -jnp.inf))
        sh *= 2
    return x


def _mlstm_stages(x_ref, cw_ref, cb_ref, ib_ref, fb_ref, on_ref, o_ref,
                  prev_sc, c_sc, n_sc, m_sc, *, layer, nb, chunk):
    n = nb * chunk
    nh = MLSTM_HEADS
    dk = MLSTM_QK
    dv = MLSTM_V
    qkw = nh * dk
    vw = MLSTM_WIDTH
    cb_ref, ib_ref, fb_ref, on_ref = _rows(layer, cb_ref, ib_ref, fb_ref, on_ref)
    x = x_ref[...].reshape(n, MLSTM_IN)
    qk_raw = x[:, ML_QK:ML_V]
    prev = prev_sc[...]
    conv = cb_ref[...] + qk_raw * cw_ref[MLSTM_CONV - 1:MLSTM_CONV, :]
    for s in range(1, MLSTM_CONV):
        conv = conv + _shift_rows(qk_raw, prev, s, chunk) * cw_ref[MLSTM_CONV - 1 - s:MLSTM_CONV - s, :]
    prev_sc[...] = qk_raw
    qk = conv * _sigmoid(conv)
    q_all = qk[:, 0:qkw] * (dk ** -0.5)
    k_all = qk[:, qkw:]
    v_all = x[:, ML_V:ML_O]
    o_pre = x[:, ML_O:ML_I]
    li_all = x[:, ML_I:ML_F] + ib_ref[...]
    lf_all = _log_sigmoid(x[:, ML_F:ML_F + LANES] + fb_ref[...])

    ri = lax.broadcasted_iota(jnp.int32, (chunk, chunk), 0)
    ci = lax.broadcasted_iota(jnp.int32, (chunk, chunk), 1)
    causal = ci <= ri
    tri = jnp.where(causal, 1.0, 0.0)
    lane_k = lax.broadcasted_iota(jnp.int32, (chunk, qkw), 1)
    lane_v = lax.broadcasted_iota(jnp.int32, (chunk, vw), 1)
    rc = lax.broadcasted_iota(jnp.int32, (qkw, vw), 0)
    cc = lax.broadcasted_iota(jnp.int32, (qkw, vw), 1)
    cmask = _div(rc, dk) == _div(cc, dv)
    expand_v = jnp.where(rc == _div(cc, dv), 1.0, 0.0).astype(BF16)
    rk = lax.broadcasted_iota(jnp.int32, (qkw, qkw), 0)
    ck = lax.broadcasted_iota(jnp.int32, (qkw, qkw), 1)
    expand_k = jnp.where(rk == _div(ck, dk), 1.0, 0.0).astype(BF16)
    gather_k = jnp.where(_div(rk, dk) == ck, 1.0, 0.0).astype(BF16)

    bs = range(nb)
    sls = [slice(b * chunk, (b + 1) * chunk) for b in bs]
    q = [q_all[sl] for sl in sls]
    k = [k_all[sl] for sl in sls]
    k16 = [z.astype(BF16) for z in k]
    v = [v_all[sl] for sl in sls]
    li = [li_all[sl] for sl in sls]
    c_old = [c_sc[b] for b in bs]
    n_old = [n_sc[b] for b in bs]
    m_prev = [m_sc[b] for b in bs]
    g = [_exact_left_dot(tri, lf_all[sl]) for sl in sls]
    lig = [li[b] - g[b] for b in bs]
    inter_log = [g[b] + m_prev[b] for b in bs]
    m_t = [jnp.maximum(inter_log[b], g[b] + _cummax_rows(lig[b])) for b in bs]
    inter_w = [jnp.exp(inter_log[b] - m_t[b]) for b in bs]
    log2e = math.log2(math.e)
    gm = [(g[b] - m_t[b]) * log2e for b in bs]
    lig_t = [jnp.transpose(z * log2e) for z in lig]
    qn = [_exact_right_dot(q[b] * n_old[b], gather_k, parts=2) for b in bs]
    q_c = [_bdot(q[b], c_old[b]) for b in bs]
    ssum = [jnp.zeros((chunk, LANES), F32) for _ in bs]
    num = [jnp.zeros((chunk, vw), F32) for _ in bs]
    yield
    for h in range(nh):
        mk = (lane_k >= h * dk) & (lane_k < (h + 1) * dk)
        mv = (lane_v >= h * dv) & (lane_v < (h + 1) * dv)
        qk_h = [lax.dot_general(jnp.where(mk, q[b], 0.0).astype(BF16), k16[b], (((1,), (1,)), ((), ())),
                                preferred_element_type=F32) for b in bs]
        d = [jnp.broadcast_to(gm[b][:, h:h + 1], (chunk, chunk)) + lig_t[b][h:h + 1, :] for b in bs]
        s = [qk_h[b] * jnp.exp2(jnp.where(causal, d[b], -jnp.inf)) for b in bs]
        ssum = [jnp.where(lane_k == h, jnp.sum(s[b], axis=-1, keepdims=True), ssum[b]) for b in bs]
        num = [num[b] + _bdot(s[b], jnp.where(mv, v[b], 0.0)) for b in bs]
        yield
    den = [inter_w[b] * qn[b] + ssum[b] for b in bs]
    rden = [1.0 / jnp.maximum(jnp.abs(den[b]), jnp.exp(-m_t[b])) for b in bs]
    g_last = [g[b][chunk - 1:chunk, :] for b in bs]
    a_all = [g_last[b] - g[b] + li[b] for b in bs]
    m_new = [jnp.maximum(g_last[b] + m_prev[b], jnp.max(a_all[b], axis=0, keepdims=True)) for b in bs]
    dec = [jnp.exp(g_last[b] + m_prev[b] - m_new[b]) for b in bs]
    wts = [jnp.exp(a_all[b] - m_new[b]) for b in bs]
    per_head = [jnp.concatenate([inter_w[b], rden[b], wts[b], jnp.broadcast_to(dec[b], (8, LANES))], axis=0)
                for b in bs]
    on_v = [_exact_right_dot(per_head[b], expand_v, parts=2) for b in bs]
    on_k = [_exact_right_dot(per_head[b][2 * chunk:], expand_k, parts=2) for b in bs]
    hs = [(on_v[b][0:chunk] * q_c[b] + num[b]) * on_v[b][chunk:2 * chunk] for b in bs]
    for b in bs:
        c_sc[b] = c_old[b] * on_v[b][3 * chunk:3 * chunk + 1] + jnp.where(
            cmask, _bdot_tn(k16[b], on_v[b][2 * chunk:3 * chunk] * v[b]), 0.0)
        n_sc[b] = n_old[b] * on_k[b][chunk:chunk + 1] + jnp.sum(on_k[b][0:chunk] * k[b], axis=0, keepdims=True)
        m_sc[b] = m_new[b]

    hh = jnp.concatenate(hs, axis=0)
    rv = lax.broadcasted_iota(jnp.int32, (vw, vw), 0)
    cv = lax.broadcasted_iota(jnp.int32, (vw, vw), 1)
    head_ones = jnp.where(_div(rv, dv) == _div(cv, dv), 1.0, 0.0).astype(BF16)
    ms = _exact_right_dot(hh * hh, head_ones, parts=2) * (1.0 / dv)
    out = hh * lax.rsqrt(ms + NORM_EPS) * on_ref[...] * _sigmoid(o_pre)
    o_ref[...] = out.astype(o_ref.dtype).reshape(o_ref.shape)


def _recurrent_kernel(x_ref, cw_ref, cb_ref, ib_ref, fb_ref, on_ref, fa_ref, fbk_ref, ft_ref, pc_ref, lnw_ref, lnb_ref,
                      om_ref, or_ref, prev_sc, c_sc, n_sc, m_sc, state_sc, *, layer, nb, chunk, rwkv_chunk, cps):
    @pl.when(pl.program_id(0) == 0)
    def _():
        for sc in (prev_sc, c_sc, n_sc, m_sc, state_sc):
            sc[...] = jnp.zeros(sc.shape, F32)

    scan = _rwkv_scan_stages(fa_ref, fbk_ref, ft_ref, pc_ref, lnw_ref, lnb_ref, or_ref, state_sc,
                             layer=layer, nb=nb, chunk=rwkv_chunk, cps=cps)
    mlstm = _mlstm_stages(x_ref, cw_ref, cb_ref, ib_ref, fb_ref, on_ref, om_ref, prev_sc, c_sc, n_sc, m_sc,
                          layer=layer, nb=nb, chunk=chunk)
    order = [scan, mlstm] + [scan, scan, mlstm] * (MLSTM_HEADS - 1) + [scan, mlstm]
    pending = {id(scan): 2 * cps + 1, id(mlstm): MLSTM_HEADS + 2}
    for g in order:
        next(g, None)
        pending[id(g)] -= 1
    for g in (scan, mlstm):
        for _ in range(pending[id(g)]):
            next(g, None)


def _recurrent(l, mlstm_in, fa, fb, ft, pc, batch, seq, cw, cb, ib, fbias, on, ln_w, ln_b):
    chunk = MLSTM_CHUNK
    rchunk = RWKV_CHUNK
    assert chunk % rchunk == 0
    w = RWKV_WIDTH
    x = mlstm_in.reshape(batch, seq, MLSTM_IN)
    fa = fa.reshape(batch, seq, 5 * w)
    fb = fb.reshape(batch, seq, 4 * w)
    pc = pc.reshape(batch, seq // rchunk, w)
    blk = lambda width: pl.BlockSpec((batch, chunk, width), lambda c: (0, c, 0))
    om, orw = pl.pallas_call(
        functools.partial(_recurrent_kernel, layer=l, nb=batch, chunk=chunk, rwkv_chunk=rchunk, cps=chunk // rchunk),
        out_shape=(jax.ShapeDtypeStruct((batch, seq, MLSTM_WIDTH), BF16),
                   jax.ShapeDtypeStruct((batch, seq, w), BF16)),
        grid=(seq // chunk,),
        in_specs=[blk(MLSTM_IN), _resident(cw, l), _resident(cb), _resident(ib), _resident(fbias), _resident(on),
                  blk(2 * w), blk(4 * w), blk(4 * w), _resident(pc), _resident(ln_w), _resident(ln_b)],
        out_specs=(blk(MLSTM_WIDTH), blk(w)),
        scratch_shapes=[pltpu.VMEM((batch * chunk, 2 * MLSTM_HEADS * MLSTM_QK), F32),
                        pltpu.VMEM((batch, MLSTM_HEADS * MLSTM_QK, MLSTM_WIDTH), F32),
                        pltpu.VMEM((batch, 1, MLSTM_HEADS * MLSTM_QK), F32),
                        pltpu.VMEM((batch, 1, LANES), F32),
                        pltpu.VMEM((batch, w, w), F32)],
        compiler_params=_cparams(("arbitrary",)),
        name="mlstm_rwkv_scan",
    )(x, cw, cb, ib, fbias, on, fa, fb, ft, pc, ln_w, ln_b)
    return om.reshape(batch * seq, MLSTM_WIDTH), orw.reshape(batch * seq, w)


def _ffn_kernel(x_ref, ya_ref, yb_ref, yc_ref, wo_ref, g_ref, wg_ref, wu_ref, wd_ref, fg_ref,
                o_ref, act_sc, *, layer, final_norm, tf):
    (g_ref,) = _rows(layer, g_ref)
    y = jnp.concatenate([ya_ref[...], yb_ref[...], yc_ref[...]], axis=-1)
    x1 = x_ref[...] + jnp.dot(y, wo_ref[...], preferred_element_type=F32)
    h = _rms(x1, g_ref[...]).astype(BF16)
    for c in range(D_FF // tf):
        gate = jnp.dot(h, wg_ref[:, c * tf:(c + 1) * tf], preferred_element_type=F32)
        up = jnp.dot(h, wu_ref[:, c * tf:(c + 1) * tf], preferred_element_type=F32)
        act_sc[:, c * tf:(c + 1) * tf] = (gate * _sigmoid(gate) * up).astype(BF16)
    out = x1 + jnp.dot(act_sc[...], wd_ref[...], preferred_element_type=F32)
    if final_norm:
        out = _rms(out, fg_ref[...])
    o_ref[...] = out


def _out_ffn(l, x, ya, yb, yc, wo, g, wg, wu, wd, fg, final_norm):
    t = x.shape[0]
    tm, tf = TM_FFN, TF_FFN
    row = lambda w: pl.BlockSpec((tm, w), lambda i: (i, 0))
    return pl.pallas_call(
        functools.partial(_ffn_kernel, layer=l, final_norm=final_norm, tf=tf),
        out_shape=jax.ShapeDtypeStruct((t, D_MODEL), F32),
        grid=(t // tm,),
        in_specs=[row(D_MODEL), row(MLA_WIDTH), row(RWKV_WIDTH), row(MLSTM_WIDTH), _resident(wo, l),
                  _resident(g), _resident(wg, l), _resident(wu, l), _resident(wd, l), _resident(fg)],
        out_specs=row(D_MODEL),
        scratch_shapes=[pltpu.VMEM((tm, D_FF), BF16)],
        compiler_params=_cparams(("parallel",)),
        name="out_ffn",
    )(x, ya, yb, yc, wo, g, wg, wu, wd, fg)


def _pad_cols(w, width):
    return jnp.pad(w, [(0, 0)] * (w.ndim - 1) + [(0, width - w.shape[-1])])


def _pad_rows(w, height):
    return jnp.pad(w, [(0, 0)] * (w.ndim - 2) + [(0, height - w.shape[-2]), (0, 0)])


def _rot_half_cols(w):
    half = w.shape[-1] // 2
    return jnp.concatenate([-w[..., half:], w[..., :half]], axis=-1)


def _stacked_weights(w_in, mla_w_uq, mla_w_ukv, rwkv_w2, rwkv_a2, rwkv_g2):
    depth = w_in.shape[0]
    wt = jnp.swapaxes(w_in, 1, 2)
    c_q, c_kv, k_pe = wt[:, 0:256], wt[:, 256:512], wt[:, 512:576]
    rw = wt[:, 576:1472]
    ml = wt[:, 1472:2248]
    k_pe_rot = jnp.swapaxes(_rot_half_cols(jnp.swapaxes(k_pe, 1, 2)), 1, 2)
    w_mla = jnp.concatenate([c_q, c_kv, _pad_rows(k_pe, LANES), _pad_rows(k_pe_rot, LANES)], axis=1)
    w_mlstm = jnp.concatenate([ml[:, 0:256], ml[:, 256:512], ml[:, 520:776],
                               _pad_rows(ml[:, 512:516], LANES), _pad_rows(ml[:, 516:520], LANES)], axis=1)
    w_all = jnp.concatenate([w_mla, rw, w_mlstm], axis=1).astype(BF16)

    uq = mla_w_uq.reshape(depth, MLA_Q_LORA, MLA_HEADS, MLA_NOPE + MLA_ROPE)
    nope = uq[..., :MLA_NOPE].reshape(depth, MLA_Q_LORA, MLA_HEADS * MLA_NOPE)
    pe = _pad_cols(uq[..., MLA_NOPE:], LANES).reshape(depth, MLA_Q_LORA, MLA_HEADS * LANES)
    per = _pad_cols(_rot_half_cols(uq[..., MLA_NOPE:]), LANES).reshape(depth, MLA_Q_LORA, MLA_HEADS * LANES)
    wq = jnp.concatenate([nope, pe, per], axis=-1).astype(BF16)
    ukv = mla_w_ukv.reshape(depth, MLA_KV_LORA, MLA_HEADS, MLA_NOPE + MLA_VDIM)
    wkv = jnp.concatenate([ukv[..., :MLA_NOPE].reshape(depth, MLA_KV_LORA, -1),
                           ukv[..., MLA_NOPE:].reshape(depth, MLA_KV_LORA, -1)], axis=-1).astype(BF16)

    rows = lambda before, wt: jnp.pad(wt, ((0, 0), (before, LANES - before - wt.shape[1]), (0, 0))).astype(BF16)
    w2p = rows(0, rwkv_w2)
    a2p = rows(RWKV_DECAY_LORA, rwkv_a2)
    g2p = rows(RWKV_DECAY_LORA + RWKV_AAA_LORA, rwkv_g2)
    return w_all, wq, wkv, w2p, a2p, g2p


def kernel(x, positions, mix_norm, w_in, mla_q_norm, mla_w_uq, mla_kv_norm, mla_w_ukv, mla_out_norm, rwkv_mu, rwkv_w0, rwkv_w2, rwkv_a0, rwkv_a2, rwkv_g2, rwkv_k_k, rwkv_k_a, rwkv_r_k, rwkv_ln_w, rwkv_ln_b, mlstm_conv_w, mlstm_conv_b, mlstm_i_bias, mlstm_f_bias, mlstm_out_norm, w_out, ffn_norm, w_gate, w_up, w_down, final_norm):
    batch, seq, _ = x.shape
    depth = w_in.shape[0]
    xt = x.reshape(batch * seq, D_MODEL)
    cos, sin, (wo, wg, wu, wd) = _rope_tables_and_casts(positions, (w_out, w_gate, w_up, w_down))
    w_all, wq, wkv, w2p, a2p, g2p = _stacked_weights(w_in, mla_w_uq, mla_w_ukv, rwkv_w2, rwkv_a2, rwkv_g2)
    ml_ib = _pad_cols(mlstm_i_bias, LANES)
    ml_fb = _pad_cols(mlstm_f_bias, LANES)
    for l in range(depth):
        q, k, v, fa, fb, pc, mlstm_in = _inproj(
            l, xt, seq, mix_norm, w_all, cos, sin, mla_q_norm, mla_kv_norm, wq, wkv,
            rwkv_mu, rwkv_w0, rwkv_a0, rwkv_k_k, rwkv_k_a, rwkv_r_k,
            w2p, a2p, g2p)
        y_mla = _mla_attention(l, q, k, v, mla_out_norm, batch, seq)
        ft = _rwkv_chunk(fa, batch, seq)
        y_mlstm, y_rwkv = _recurrent(l, mlstm_in, fa, fb, ft, pc, batch, seq, mlstm_conv_w, mlstm_conv_b, ml_ib, ml_fb,
                                     mlstm_out_norm, rwkv_ln_w, rwkv_ln_b)
        xt = _out_ffn(l, xt, y_mla, y_rwkv, y_mlstm, wo, ffn_norm, wg, wu, wd,
                      final_norm.reshape(1, -1), final_norm=(l == depth - 1))
    return xt.reshape(batch, seq, D_MODEL)
```

```python
import functools
import math

import jax
import jax.numpy as jnp
from jax import lax
from jax.experimental import pallas as pl
from jax.experimental.pallas import tpu as pltpu

F32 = jnp.float32
BF16 = jnp.bfloat16

D_MODEL = 1024
DEPTH = 2
MLA_HEADS = 4
MLA_NOPE = 128
MLA_ROPE = 64
MLA_VDIM = 128
MLA_Q_LORA = 256
MLA_KV_LORA = 256
MLA_WIDTH = MLA_HEADS * MLA_VDIM
MLA_QK = 256
ROPE_THETA = 10000.0
RWKV_HEADS = 4
RWKV_HEAD = 64
RWKV_WIDTH = 256
RWKV_DECAY_LORA = 32
RWKV_AAA_LORA = 32
RWKV_GATE_LORA = 64
RWKV_IN = 3 * RWKV_WIDTH + 128
RWKV_LN_EPS = 64e-5
MLSTM_HEADS = 4
MLSTM_QK = 32
MLSTM_V = 64
MLSTM_WIDTH = 256
MLSTM_CONV = 4
MLSTM_IN = 1024
D_FF = 2816
NORM_EPS = 1e-6
LANES = 128
SUBLANES = 8

MLA_CQ = 0
MLA_CKV = MLA_CQ + MLA_Q_LORA
MLA_KPE = MLA_CKV + MLA_KV_LORA
MLA_KPER = MLA_KPE + LANES
MLA_IN = MLA_KPER + LANES

TM_INPROJ = 512
TQ_ATTN = 512
TK_ATTN = 512
HP_ATTN = 4
RWKV_CHUNK = 64
RWKV_PREP_CHUNKS = 2
MLSTM_CHUNK = 256
TM_FFN = 512
TF_FFN = 256
VMEM_LIMIT = 56 * 1024 * 1024


def _cparams(sem):
    return pltpu.CompilerParams(dimension_semantics=sem, vmem_limit_bytes=VMEM_LIMIT)


def _resident(a, layer=None):
    if layer is None:
        nd = a.ndim
        return pl.BlockSpec(a.shape, lambda *_: (0,) * nd, pipeline_mode=pl.Buffered(1))
    nd = a.ndim - 1
    return pl.BlockSpec((None,) + a.shape[1:], lambda *_: (layer,) + (0,) * nd, pipeline_mode=pl.Buffered(1))


def _rows(layer, *refs):
    return [r.at[layer:layer + 1] for r in refs]


def _bdot(a, b):
    return jnp.dot(a.astype(BF16), b.astype(BF16), preferred_element_type=F32)


def _dot_nt(a, b):
    return lax.dot_general(a, b, (((1,), (1,)), ((), ())), preferred_element_type=F32)


def _bdot_nt(a, b):
    return lax.dot_general(a.astype(BF16), b.astype(BF16), (((1,), (1,)), ((), ())),
                           preferred_element_type=F32)


def _bdot_tn(a, b):
    return lax.dot_general(a.astype(BF16), b.astype(BF16), (((0,), (0,)), ((), ())),
                           preferred_element_type=F32)


def _split3(x):
    h = x.astype(BF16)
    r1 = x - h.astype(F32)
    m = r1.astype(BF16)
    lo = (r1 - m.astype(F32)).astype(BF16)
    return h, m, lo


def _exact_left_dot(sel, x):
    h, m, lo = _split3(x)
    s = sel.astype(BF16)
    return (jnp.dot(s, h, preferred_element_type=F32) + jnp.dot(s, m, preferred_element_type=F32)
            + jnp.dot(s, lo, preferred_element_type=F32))


def _exact_right_dot(x, sel, parts=3):
    pieces = _split3(x)[:parts]
    s = sel.astype(BF16)
    out = jnp.dot(pieces[0], s, preferred_element_type=F32)
    for p in pieces[1:]:
        out = out + jnp.dot(p, s, preferred_element_type=F32)
    return out


def _rms(x, g):
    return x * lax.rsqrt(jnp.mean(x * x, axis=-1, keepdims=True) + NORM_EPS) * g


def _sigmoid(x):
    return 1.0 / (1.0 + jnp.exp(-x))


def _log_sigmoid(x):
    return jnp.minimum(x, 0.0) - jnp.log1p(jnp.exp(-jnp.abs(x)))


def _div(x, d):
    assert d & (d - 1) == 0
    return lax.shift_right_logical(x, d.bit_length() - 1)


def _mod(x, d):
    assert d & (d - 1) == 0
    return lax.bitwise_and(x, d - 1)


def _shift_rows(x, prev, s, chunk):
    n = x.shape[0]
    row = lax.broadcasted_iota(jnp.int32, x.shape, 0)
    return jnp.where(_mod(row, chunk) >= s, pltpu.roll(x, s, 0), pltpu.roll(prev, n - chunk + s, 0))


def _rope_cast_kernel(pos_ref, invf_ref, *refs):
    n = (len(refs) - 2) // 2
    w_refs, cos_ref, sin_ref, o_refs = refs[:n], refs[n], refs[n + 1], refs[n + 2:]
    ang = pos_ref[...].astype(F32) * invf_ref[...]
    cos_ref[...] = jnp.cos(ang)
    sin_ref[...] = jnp.sin(ang)
    for w_ref, o_ref in zip(w_refs, o_refs):
        o_ref[...] = w_ref[...].astype(BF16)


def _rope_tables_and_casts(positions, weights):
    t = positions.size
    tm = min(1024, t)
    steps = t // tm
    inv_freq = ROPE_THETA ** (-jnp.arange(0, MLA_ROPE, 2, dtype=F32) / MLA_ROPE)
    invf = jnp.tile(inv_freq, LANES // (MLA_ROPE // 2))[None, :]
    flat = [w.reshape(-1, w.shape[-1]) for w in weights]
    rows = [f.shape[0] // steps for f in flat]
    assert all(f.shape[0] % steps == 0 and r % 16 == 0 for f, r in zip(flat, rows))
    slab = lambda r, c: pl.BlockSpec((r, c), lambda i: (i, 0))
    outs = pl.pallas_call(
        _rope_cast_kernel,
        out_shape=(jax.ShapeDtypeStruct((t, LANES), F32), jax.ShapeDtypeStruct((t, LANES), F32))
        + tuple(jax.ShapeDtypeStruct(f.shape, BF16) for f in flat),
        grid=(steps,),
        in_specs=[slab(tm, 1), pl.BlockSpec((1, LANES), lambda i: (0, 0))]
        + [slab(r, f.shape[1]) for f, r in zip(flat, rows)],
        out_specs=(slab(tm, LANES), slab(tm, LANES)) + tuple(slab(r, f.shape[1]) for f, r in zip(flat, rows)),
        compiler_params=_cparams(("parallel",)),
        name="rope_tables_casts",
    )(positions.reshape(t, 1), invf, *flat)
    return outs[0], outs[1], [o.reshape(w.shape) for o, w in zip(outs[2:], weights)]


(FA_RT, FA_V, FA_AT, FA_KT, FA_BT) = range(5)
(FB_KH, FB_BH, FB_BONUS, FB_GATE) = range(4)
(FT_TA, FT_UV, FT_YV, FT_ARB) = range(4)
RWKV_GROUP = 256


def _segsum(x, seg):
    assert 2 * seg == LANES and x.shape[1] % LANES == 0
    out = []
    for blk in range(x.shape[1] // LANES):
        xb = x[:, blk * LANES:(blk + 1) * LANES]
        low = lax.broadcasted_iota(jnp.int32, xb.shape, 1) < seg
        s_low = jnp.sum(jnp.where(low, xb, 0.0), axis=-1, keepdims=True)
        s_high = jnp.sum(jnp.where(low, 0.0, xb), axis=-1, keepdims=True)
        out.append(jnp.where(low, s_low, s_high))
    return jnp.concatenate(out, axis=-1)


def _rwkv_masks(chunk):
    w = RWKV_WIDTH
    r = lax.broadcasted_iota(jnp.int32, (w, w), 0)
    c = lax.broadcasted_iota(jnp.int32, (w, w), 1)
    return r, c, _div(r, chunk) == _div(c, chunk)


def _rwkv_features(xs, r0, prm, fa_ref, fb_ref, pc_ref, chunk):
    w0_ref, a0_ref, kk_ref, ka_ref, rk_ref, w2_ref, a2_ref, g2_ref = prm
    w = RWKV_WIDTH
    hd = RWKV_HEAD
    n = xs.shape[0]
    r = xs[:, 0:w]
    k = xs[:, w:2 * w]
    v = xs[:, 2 * w:3 * w]
    lor = xs[:, 3 * w:]
    ld = -math.exp(-0.5) * _sigmoid(w0_ref[...] + _bdot(jnp.tanh(lor), w2_ref[...]))
    a = _sigmoid(a0_ref[...] + _bdot(lor, a2_ref[...]))
    g = _bdot(_sigmoid(lor), g2_ref[...])
    kk = k * kk_ref[...]
    kk = kk / jnp.maximum(jnp.sqrt(_segsum(kk * kk, hd)), 1e-12)
    k2 = k * (1.0 + (a - 1.0) * ka_ref[...])
    kb = kk * a
    bonus = _segsum(r * k2 * rk_ref[...], hd) * v

    assert n <= RWKV_WIDTH
    ri, ci, same_chunk = _rwkv_masks(chunk)
    tri = jnp.where(same_chunk & (ci <= ri), 1.0, 0.0)[0:n, 0:n]
    cl = _exact_left_dot(tri, ld)
    units = n // chunk
    cl_last = jnp.concatenate(
        [jnp.broadcast_to(cl[(u + 1) * chunk - 1:(u + 1) * chunk, :], (chunk, w)) for u in range(units)], axis=0)
    e_neg = jnp.exp(-cl)
    e_end = jnp.exp(cl_last - cl)
    rows = slice(r0, r0 + n)

    def put(ref, sec, val):
        ref[rows, sec * w:(sec + 1) * w] = val.astype(ref.dtype)

    put(fa_ref, FA_RT, r * jnp.exp(cl))
    put(fa_ref, FA_V, v)
    put(fa_ref, FA_AT, kk * jnp.exp(cl - ld))
    put(fa_ref, FA_KT, k2 * e_neg)
    put(fa_ref, FA_BT, kb * e_neg)
    put(fb_ref, FB_KH, k2 * e_end)
    put(fb_ref, FB_BH, kb * e_end)
    put(fb_ref, FB_BONUS, bonus)
    put(fb_ref, FB_GATE, g)
    for u in range(units):
        c = r0 // chunk + u
        pc_ref[c:c + 1, :] = jnp.exp(cl_last[u * chunk:u * chunk + 1, :])


def _inproj_kernel(x_ref, g_ref, w_ref, cos_ref, sin_ref, qn_ref, kvn_ref, wq_ref, wkv_ref,
                   mu_ref, w0_ref, a0_ref, kk_ref, ka_ref, rk_ref, w2_ref, a2_ref, g2_ref,
                   q_ref, k_ref, v_ref, fa_ref, fb_ref, pc_ref, mlstm_ref, prev_sc, *, layer, chunk, tiles_per_seq):
    tm = x_ref.shape[0]
    g_ref, qn_ref, kvn_ref, mu_ref, w0_ref, a0_ref, kk_ref, ka_ref, rk_ref = _rows(
        layer, g_ref, qn_ref, kvn_ref, mu_ref, w0_ref, a0_ref, kk_ref, ka_ref, rk_ref)

    @pl.when(pl.program_id(0) % tiles_per_seq == 0)
    def _():
        prev_sc[...] = jnp.zeros(prev_sc.shape, F32)

    hb = _rms(x_ref[...], g_ref[...]).astype(BF16)
    rw = _dot_nt(hb, w_ref[MLA_IN:MLA_IN + RWKV_IN, :])
    row = lax.broadcasted_iota(jnp.int32, rw.shape, 0)
    shifted = jnp.where(row >= 1, pltpu.roll(rw, 1, 0), prev_sc[0:1, :])
    prev_sc[0:1, :] = rw[tm - 1:tm, :]
    xs = rw + (shifted - rw) * mu_ref[...]
    prm = (w0_ref, a0_ref, kk_ref, ka_ref, rk_ref, w2_ref, a2_ref, g2_ref)

    mla = _dot_nt(hb, w_ref[0:MLA_IN, :])
    groups = tm // RWKV_GROUP
    for gi in range(groups // 2):
        _rwkv_features(xs[gi * RWKV_GROUP:(gi + 1) * RWKV_GROUP], gi * RWKV_GROUP, prm, fa_ref, fb_ref, pc_ref, chunk)
    mlstm_ref[...] = _dot_nt(hb, w_ref[MLA_IN + RWKV_IN:, :])
    for gi in range(groups // 2, groups):
        _rwkv_features(xs[gi * RWKV_GROUP:(gi + 1) * RWKV_GROUP], gi * RWKV_GROUP, prm, fa_ref, fb_ref, pc_ref, chunk)

    cos = cos_ref[...]
    sin = sin_ref[...]
    scale = (MLA_NOPE + MLA_ROPE) ** -0.5 * math.log2(math.e)
    hw = MLA_HEADS * LANES
    cqn = _rms(mla[:, MLA_CQ:MLA_CKV], qn_ref[...]).astype(BF16)
    q = jnp.dot(cqn, wq_ref[...], preferred_element_type=F32)
    ckvn = _rms(mla[:, MLA_CKV:MLA_KPE], kvn_ref[...]).astype(BF16)
    kv = jnp.dot(ckvn, wkv_ref[...], preferred_element_type=F32)
    kp = (mla[:, MLA_KPE:MLA_KPER] * cos + mla[:, MLA_KPER:MLA_IN] * sin).astype(BF16)
    for h in range(MLA_HEADS):
        c0 = h * LANES
        pe = q[:, hw + c0:hw + c0 + LANES] * cos + q[:, 2 * hw + c0:2 * hw + c0 + LANES] * sin
        q_ref[:, h * MLA_QK:h * MLA_QK + LANES] = (q[:, c0:c0 + LANES] * scale).astype(BF16)
        q_ref[:, h * MLA_QK + LANES:(h + 1) * MLA_QK] = (pe * scale).astype(BF16)
        k_ref[:, h * MLA_QK:h * MLA_QK + LANES] = kv[:, c0:c0 + LANES].astype(BF16)
        k_ref[:, h * MLA_QK + LANES:(h + 1) * MLA_QK] = kp
    v_ref[...] = kv[:, hw:].astype(BF16)


def _inproj(l, x, seq, g, w, cos, sin, qn, kvn, wq, wkv, mu, w0, a0, k_k, k_a, r_k, w2p, a2p, g2p):
    t = x.shape[0]
    tm = TM_INPROJ
    chunk = RWKV_CHUNK
    assert seq % tm == 0 and tm % RWKV_GROUP == 0 and RWKV_GROUP % chunk == 0
    row = lambda width: pl.BlockSpec((tm, width), lambda i: (i, 0))
    rw = RWKV_WIDTH
    return pl.pallas_call(
        functools.partial(_inproj_kernel, layer=l, chunk=chunk, tiles_per_seq=seq // tm),
        out_shape=(jax.ShapeDtypeStruct((t, MLA_HEADS * MLA_QK), BF16),
                   jax.ShapeDtypeStruct((t, MLA_HEADS * MLA_QK), BF16),
                   jax.ShapeDtypeStruct((t, MLA_WIDTH), BF16),
                   jax.ShapeDtypeStruct((t, 5 * rw), BF16),
                   jax.ShapeDtypeStruct((t, 4 * rw), BF16),
                   jax.ShapeDtypeStruct((t // chunk, rw), F32),
                   jax.ShapeDtypeStruct((t, MLSTM_IN), F32)),
        grid=(t // tm,),
        in_specs=[row(D_MODEL), _resident(g), _resident(w, l), row(LANES), row(LANES), _resident(qn),
                  _resident(kvn), _resident(wq, l), _resident(wkv, l),
                  _resident(mu), _resident(w0), _resident(a0), _resident(k_k), _resident(k_a),
                  _resident(r_k), _resident(w2p, l), _resident(a2p, l), _resident(g2p, l)],
        out_specs=(row(MLA_HEADS * MLA_QK), row(MLA_HEADS * MLA_QK), row(MLA_WIDTH), row(5 * rw), row(4 * rw),
                   pl.BlockSpec((tm // chunk, rw), lambda i: (i, 0)), row(MLSTM_IN)),
        scratch_shapes=[pltpu.VMEM((SUBLANES, RWKV_IN), F32)],
        compiler_params=_cparams(("arbitrary",)),
        name="inproj",
    )(x, g, w, cos, sin, qn, kvn, wq, wkv, mu, w0, a0, k_k, k_a, r_k, w2p, a2p, g2p)


def _attn_kernel(q_ref, k_ref, v_ref, g_ref, o_ref, m_sc, acc_sc, sa_sc, sb_sc, *, layer, tq, tk, hp):
    i = pl.program_id(2)
    m_sc[...] = jnp.full(m_sc.shape, -jnp.inf, F32)
    acc_sc[...] = jnp.zeros(acc_sc.shape, F32)
    sub = m_sc.shape[1]
    hs = range(hp)
    ones = jnp.ones((acc_sc.shape[1] - MLA_VDIM, tk), BF16)
    (g_ref,) = _rows(layer, g_ref)

    def produce(j, s_ref):
        off = pl.multiple_of(j * tk, tk)
        for h in hs:
            s_ref[h] = lax.dot_general(k_ref[0, pl.ds(off, tk), h * MLA_QK:(h + 1) * MLA_QK],
                                       q_ref[0, :, h * MLA_QK:(h + 1) * MLA_QK],
                                       (((1,), (1,)), ((), ())), preferred_element_type=F32)

    def consume(j, s_ref, masked):
        off = pl.multiple_of(j * tk, tk)
        s = [s_ref[h] for h in hs]
        if masked:
            keys = lax.broadcasted_iota(jnp.int32, (tk, tq), 0)
            queries = lax.broadcasted_iota(jnp.int32, (tk, tq), 1)
            s = [jnp.where(keys <= queries, s[h], -jnp.inf) for h in hs]
        m_old = [m_sc[h][0:1] for h in hs]
        m_new = [jnp.maximum(m_old[h], jnp.max(s[h], axis=0, keepdims=True)) for h in hs]
        p = [jnp.exp2(s[h] - m_new[h]) for h in hs]
        alpha = [jnp.exp2(m_old[h] - m_new[h]) for h in hs]
        for h in hs:
            m_sc[h] = jnp.broadcast_to(m_new[h], (sub, tq))
        v1 = [jnp.concatenate([jnp.transpose(v_ref[0, pl.ds(off, tk), h * MLA_VDIM:(h + 1) * MLA_VDIM]), ones],
                              axis=0) for h in hs]
        pv = [jnp.dot(v1[h], p[h].astype(BF16), preferred_element_type=F32) for h in hs]
        for h in hs:
            acc_sc[h] = alpha[h] * acc_sc[h] + pv[h]

    def pair(jj, c):
        j = 2 * jj
        produce(j + 1, sb_sc)
        consume(j, sa_sc, False)
        produce(j + 2, sa_sc)
        consume(j + 1, sb_sc, False)
        return c

    produce(0, sa_sc)
    lax.fori_loop(0, i // 2, pair, 0)

    @pl.when(i % 2 == 0)
    def _():
        consume(i, sa_sc, True)

    @pl.when(i % 2 == 1)
    def _():
        produce(i, sb_sc)
        consume(i - 1, sa_sc, False)
        consume(i, sb_sc, True)

    for h in hs:
        o = acc_sc[h, 0:MLA_VDIM] / acc_sc[h, MLA_VDIM:MLA_VDIM + 1]
        o = o * lax.rsqrt(jnp.mean(o * o, axis=0, keepdims=True) + NORM_EPS)
        o_ref[0, :, h * MLA_VDIM:(h + 1) * MLA_VDIM] = (
            jnp.transpose(o) * g_ref[:, h * MLA_VDIM:(h + 1) * MLA_VDIM]).astype(o_ref.dtype)


def _mla_attention(l, q, k, v, g, batch, seq):
    tq, tk, hp = TQ_ATTN, TK_ATTN, HP_ATTN
    assert tq == tk and MLA_VDIM == LANES and hp == MLA_HEADS
    q = q.reshape(batch, seq, MLA_HEADS * MLA_QK)
    k = k.reshape(batch, seq, MLA_HEADS * MLA_QK)
    v = v.reshape(batch, seq, MLA_WIDTH)
    out = pl.pallas_call(
        functools.partial(_attn_kernel, layer=l, tq=tq, tk=tk, hp=hp),
        out_shape=jax.ShapeDtypeStruct((batch, seq, MLA_WIDTH), BF16),
        grid=(batch, MLA_HEADS // hp, seq // tq),
        in_specs=[pl.BlockSpec((1, tq, hp * MLA_QK), lambda b, h, i: (b, i, h)),
                  pl.BlockSpec((1, seq, hp * MLA_QK), lambda b, h, i: (b, 0, h)),
                  pl.BlockSpec((1, seq, hp * MLA_VDIM), lambda b, h, i: (b, 0, h)),
                  _resident(g)],
        out_specs=pl.BlockSpec((1, tq, hp * MLA_VDIM), lambda b, h, i: (b, i, h)),
        scratch_shapes=[pltpu.VMEM((hp, SUBLANES, tq), F32),
                        pltpu.VMEM((hp, MLA_VDIM + 2 * SUBLANES, tq), F32),
                        pltpu.VMEM((hp, tk, tq), F32), pltpu.VMEM((hp, tk, tq), F32)],
        compiler_params=_cparams(("parallel", "parallel", "arbitrary")),
        name="mla_attention",
    )(q, k, v, g)
    return out.reshape(batch * seq, MLA_WIDTH)


def _tile_heads(z):
    return jnp.concatenate([z] * RWKV_HEADS, axis=0)


def _rwkv_chunk_kernel(fa_ref, ft_ref, *, nb, chunk, cps):
    seg = cps * chunk
    n = nb * seg
    w = RWKV_WIDTH
    hd = RWKV_HEAD
    x = fa_ref[...].reshape(n, fa_ref.shape[-1])
    r16, v16, a16, k16, b16 = (x[:, s * w:(s + 1) * w] for s in (FA_RT, FA_V, FA_AT, FA_KT, FA_BT))

    ri, ci, same_chunk = _rwkv_masks(chunk)
    bd = _div(ri, chunk) == _div(ci, hd)
    rt = lax.broadcasted_iota(jnp.int32, (chunk, w), 0)
    cs = _mod(lax.broadcasted_iota(jnp.int32, (chunk, w), 1), chunk)
    strict = cs < rt
    incl = cs <= rt
    c16 = _div(rt, 16) == _div(cs, 16)
    c32 = _div(rt, 32) == _div(cs, 32)
    eye = jnp.where(rt == cs, 1.0, 0.0)
    units = nb * cps

    def block_diag(z):
        return jnp.where(same_chunk, _tile_heads(z), 0.0)

    def put(u, sec, val):
        b, j = divmod(u, cps)
        ft_ref[b, j * chunk:(j + 1) * chunk, sec * w:(sec + 1) * w] = val.astype(ft_ref.dtype)

    def mm(x, y):
        return jnp.dot(x, y, preferred_element_type=F32)

    us = range(units)
    sls = [slice(u * chunk, (u + 1) * chunk) for u in us]
    a_st = [jnp.where(bd, _tile_heads(a16[sl]), 0.0) for sl in sls]
    v_st = [jnp.where(bd, _tile_heads(v16[sl]), 0.0) for sl in sls]
    kb_t = [jnp.concatenate([jnp.transpose(jnp.where(bd, _tile_heads(k16[sl]), 0.0)),
                             jnp.transpose(jnp.where(bd, _tile_heads(b16[sl]), 0.0))], axis=1) for sl in sls]
    sc = [mm(jnp.concatenate([a16[sls[u]], r16[sls[u]]], axis=0), kb_t[u]) for u in us]
    l_ab = [jnp.where(strict, sc[u][0:chunk, w:], 0.0) for u in us]
    l_ak = [jnp.where(strict, sc[u][0:chunk, 0:w], 0.0).astype(BF16) for u in us]
    a_rk = [jnp.where(incl, sc[u][chunk:, 0:w], 0.0).astype(BF16) for u in us]
    for u in us:
        put(u, FT_ARB, jnp.where(incl, sc[u][chunk:, w:], 0.0))
    xm = [-jnp.where(c16, l_ab[u], 0.0) for u in us]
    xm16 = [z.astype(BF16) for z in xm]
    off32 = [block_diag(jnp.where(c32 & jnp.logical_not(c16), l_ab[u], 0.0).astype(BF16)) for u in us]
    off64 = [block_diag(jnp.where(jnp.logical_not(c32), l_ab[u], 0.0).astype(BF16)) for u in us]
    x2 = [mm(xm16[u], block_diag(xm16[u])).astype(BF16) for u in us]
    x2_bd = [block_diag(z) for z in x2]
    lv = [mm(jnp.concatenate([l_ak[u], a_rk[u]], axis=0), v_st[u]) for u in us]
    wv = [block_diag(lv[u][0:chunk].astype(BF16)) for u in us]
    t_lo = [eye + xm[u] for u in us]
    tx = [mm(jnp.concatenate([t_lo[u].astype(BF16), x2[u]], axis=0), x2_bd[u]) for u in us]
    t_lo = [(t_lo[u] + tx[u][0:chunk]).astype(BF16) for u in us]
    x4 = [tx[u][chunk:] for u in us]
    x4b = [z.astype(BF16) for z in x4]
    x4_bd = [block_diag(z) for z in x4b]
    for u in us:
        put(u, FT_YV, lv[u][chunk:])
    x8_bd = [block_diag(mm(x4b[u], x4_bd[u]).astype(BF16)) for u in us]
    t_hi = [eye + x4[u] for u in us]
    t_hi = [block_diag((t_hi[u] + mm(t_hi[u].astype(BF16), x8_bd[u])).astype(BF16)) for u in us]
    t_inv = [mm(t_lo[u], t_hi[u]) for u in us]
    for off in (off32, off64):
        tb = [z.astype(BF16) for z in t_inv]
        mid = [mm(tb[u], off[u]).astype(BF16) for u in us]
        t_inv = [t_inv[u] - mm(mid[u], block_diag(tb[u])) for u in us]
    tb = [z.astype(BF16) for z in t_inv]
    for u in us:
        put(u, FT_TA, mm(tb[u], a_st[u]))
    for u in us:
        put(u, FT_UV, mm(tb[u], wv[u]))


def _rwkv_scan_stages(fa_ref, fb_ref, ft_ref, pc_ref, lnw_ref, lnb_ref, o_ref, state_sc, *, layer, nb, chunk, cps):
    seg = cps * chunk
    n = nb * seg
    w = RWKV_WIDTH
    hd = RWKV_HEAD
    lnw_ref, lnb_ref = _rows(layer, lnw_ref, lnb_ref)
    ri, ci, _ = _rwkv_masks(chunk)
    bd = _div(ri, chunk) == _div(ci, hd)
    bdv = _div(ri, hd) == _div(ci, hd)
    ones_bd = jnp.where(bdv, 1.0, 0.0).astype(BF16)

    def sec(ref, b, j, s):
        return ref[b, j * chunk:(j + 1) * chunk, s * w:(s + 1) * w]

    ys = [[None] * cps for _ in range(nb)]
    bs = range(nb)
    for j in range(cps):
        gs = [state_sc[b] for b in bs]
        p1 = [lax.dot_general(jnp.concatenate([sec(ft_ref, b, j, FT_TA), sec(fa_ref, b, j, FA_RT)], axis=0),
                              gs[b].astype(BF16), (((1,), (1,)), ((), ())), preferred_element_type=F32)
              for b in bs]
        yield
        u = [(p1[b][0:chunk] + sec(ft_ref, b, j, FT_UV).astype(F32)).astype(BF16) for b in bs]
        upd = [lax.dot_general(jnp.concatenate([sec(fa_ref, b, j, FA_V), -u[b]], axis=0),
                               jnp.concatenate([sec(fb_ref, b, j, FB_KH), sec(fb_ref, b, j, FB_BH)], axis=0),
                               (((0,), (0,)), ((), ())), preferred_element_type=F32) for b in bs]
        for b in bs:
            state_sc[b] = gs[b] * pc_ref[b, pl.ds(pl.program_id(0) * cps + j, 1), :] + jnp.where(bdv, upd[b], 0.0)
        for b in bs:
            u_st = jnp.where(bd, _tile_heads(u[b]), 0.0)
            ys[b][j] = (p1[b][chunk:] + sec(ft_ref, b, j, FT_YV).astype(F32)
                        - jnp.dot(sec(ft_ref, b, j, FT_ARB), u_st, preferred_element_type=F32))
        yield

    y = jnp.concatenate([ys[b][j] for b in range(nb) for j in range(cps)], axis=0)
    mean = _exact_right_dot(y, ones_bd, parts=2) * (1.0 / hd)
    d = y - mean
    var = _exact_right_dot(d * d, ones_bd, parts=2) * (1.0 / hd)
    yn = d * lax.rsqrt(var + RWKV_LN_EPS) * lnw_ref[...] + lnb_ref[...]
    bonus = fb_ref[:, :, FB_BONUS * w:(FB_BONUS + 1) * w].reshape(n, w).astype(F32)
    gate = fb_ref[:, :, FB_GATE * w:(FB_GATE + 1) * w].reshape(n, w).astype(F32)
    o_ref[...] = ((yn + bonus) * gate).astype(o_ref.dtype).reshape(o_ref.shape)


def _rwkv_chunk(fa, batch, seq):
    chunk = RWKV_CHUNK
    w = RWKV_WIDTH
    assert RWKV_HEADS * chunk == w
    cps = RWKV_PREP_CHUNKS
    seg = cps * chunk
    return pl.pallas_call(
        functools.partial(_rwkv_chunk_kernel, nb=batch, chunk=chunk, cps=cps),
        out_shape=jax.ShapeDtypeStruct((batch, seq, 4 * w), BF16),
        grid=(seq // seg,),
        in_specs=[pl.BlockSpec((batch, seg, 5 * w), lambda c: (0, c, 0))],
        out_specs=pl.BlockSpec((batch, seg, 4 * w), lambda c: (0, c, 0)),
        compiler_params=_cparams(("parallel",)),
        name="rwkv7_chunk",
    )(fa.reshape(batch, seq, 5 * w))


ML_QK = 0
ML_V = 2 * MLSTM_HEADS * MLSTM_QK
ML_O = ML_V + MLSTM_WIDTH
ML_I = ML_O + MLSTM_WIDTH
ML_F = ML_I + LANES


def _cummax_rows(x):
    n = x.shape[0]
    row = lax.broadcasted_iota(jnp.int32, x.shape, 0)
    sh = 1
    while sh < n:
        x = jnp.maximum(x, jnp.where(row >= sh, pltpu.roll(x, sh, 0), -jnp.inf))
        sh *= 2
    return x


def _mlstm_stages(x_ref, cw_ref, cb_ref, ib_ref, fb_ref, on_ref, o_ref,
                  prev_sc, c_sc, n_sc, m_sc, *, layer, nb, chunk):
    n = nb * chunk
    nh = MLSTM_HEADS
    dk = MLSTM_QK
    dv = MLSTM_V
    qkw = nh * dk
    vw = MLSTM_WIDTH
    cb_ref, ib_ref, fb_ref, on_ref = _rows(layer, cb_ref, ib_ref, fb_ref, on_ref)
    x = x_ref[...].reshape(n, MLSTM_IN)
    qk_raw = x[:, ML_QK:ML_V]
    prev = prev_sc[...]
    conv = cb_ref[...] + qk_raw * cw_ref[MLSTM_CONV - 1:MLSTM_CONV, :]
    for s in range(1, MLSTM_CONV):
        conv = conv + _shift_rows(qk_raw, prev, s, chunk) * cw_ref[MLSTM_CONV - 1 - s:MLSTM_CONV - s, :]
    prev_sc[...] = qk_raw
    qk = conv * _sigmoid(conv)
    q_all = qk[:, 0:qkw] * (dk ** -0.5)
    k_all = qk[:, qkw:]
    v_all = x[:, ML_V:ML_O]
    o_pre = x[:, ML_O:ML_I]
    li_all = x[:, ML_I:ML_F] + ib_ref[...]
    lf_all = _log_sigmoid(x[:, ML_F:ML_F + LANES] + fb_ref[...])

    ri = lax.broadcasted_iota(jnp.int32, (chunk, chunk), 0)
    ci = lax.broadcasted_iota(jnp.int32, (chunk, chunk), 1)
    causal = ci <= ri
    tri = jnp.where(causal, 1.0, 0.0)
    lane_k = lax.broadcasted_iota(jnp.int32, (chunk, qkw), 1)
    lane_v = lax.broadcasted_iota(jnp.int32, (chunk, vw), 1)
    rc = lax.broadcasted_iota(jnp.int32, (qkw, vw), 0)
    cc = lax.broadcasted_iota(jnp.int32, (qkw, vw), 1)
    cmask = _div(rc, dk) == _div(cc, dv)
    expand_v = jnp.where(rc == _div(cc, dv), 1.0, 0.0).astype(BF16)
    rk = lax.broadcasted_iota(jnp.int32, (qkw, qkw), 0)
    ck = lax.broadcasted_iota(jnp.int32, (qkw, qkw), 1)
    expand_k = jnp.where(rk == _div(ck, dk), 1.0, 0.0).astype(BF16)
    gather_k = jnp.where(_div(rk, dk) == ck, 1.0, 0.0).astype(BF16)

    bs = range(nb)
    sls = [slice(b * chunk, (b + 1) * chunk) for b in bs]
    q = [q_all[sl] for sl in sls]
    k = [k_all[sl] for sl in sls]
    k16 = [z.astype(BF16) for z in k]
    v = [v_all[sl] for sl in sls]
    li = [li_all[sl] for sl in sls]
    c_old = [c_sc[b] for b in bs]
    n_old = [n_sc[b] for b in bs]
    m_prev = [m_sc[b] for b in bs]
    g = [_exact_left_dot(tri, lf_all[sl]) for sl in sls]
    lig = [li[b] - g[b] for b in bs]
    inter_log = [g[b] + m_prev[b] for b in bs]
    m_t = [jnp.maximum(inter_log[b], g[b] + _cummax_rows(lig[b])) for b in bs]
    inter_w = [jnp.exp(inter_log[b] - m_t[b]) for b in bs]
    log2e = math.log2(math.e)
    gm = [(g[b] - m_t[b]) * log2e for b in bs]
    lig_t = [jnp.transpose(z * log2e) for z in lig]
    qn = [_exact_right_dot(q[b] * n_old[b], gather_k, parts=2) for b in bs]
    q_c = [_bdot(q[b], c_old[b]) for b in bs]
    ssum = [jnp.zeros((chunk, LANES), F32) for _ in bs]
    num = [jnp.zeros((chunk, vw), F32) for _ in bs]
    yield
    for h in range(nh):
        mk = (lane_k >= h * dk) & (lane_k < (h + 1) * dk)
        mv = (lane_v >= h * dv) & (lane_v < (h + 1) * dv)
        qk_h = [lax.dot_general(jnp.where(mk, q[b], 0.0).astype(BF16), k16[b], (((1,), (1,)), ((), ())),
                                preferred_element_type=F32) for b in bs]
        d = [jnp.broadcast_to(gm[b][:, h:h + 1], (chunk, chunk)) + lig_t[b][h:h + 1, :] for b in bs]
        s = [qk_h[b] * jnp.exp2(jnp.where(causal, d[b], -jnp.inf)) for b in bs]
        ssum = [jnp.where(lane_k == h, jnp.sum(s[b], axis=-1, keepdims=True), ssum[b]) for b in bs]
        num = [num[b] + _bdot(s[b], jnp.where(mv, v[b], 0.0)) for b in bs]
        yield
    den = [inter_w[b] * qn[b] + ssum[b] for b in bs]
    rden = [1.0 / jnp.maximum(jnp.abs(den[b]), jnp.exp(-m_t[b])) for b in bs]
    g_last = [g[b][chunk - 1:chunk, :] for b in bs]
    a_all = [g_last[b] - g[b] + li[b] for b in bs]
    m_new = [jnp.maximum(g_last[b] + m_prev[b], jnp.max(a_all[b], axis=0, keepdims=True)) for b in bs]
    dec = [jnp.exp(g_last[b] + m_prev[b] - m_new[b]) for b in bs]
    wts = [jnp.exp(a_all[b] - m_new[b]) for b in bs]
    per_head = [jnp.concatenate([inter_w[b], rden[b], wts[b], jnp.broadcast_to(dec[b], (8, LANES))], axis=0)
                for b in bs]
    on_v = [_exact_right_dot(per_head[b], expand_v, parts=2) for b in bs]
    on_k = [_exact_right_dot(per_head[b][2 * chunk:], expand_k, parts=2) for b in bs]
    hs = [(on_v[b][0:chunk] * q_c[b] + num[b]) * on_v[b][chunk:2 * chunk] for b in bs]
    for b in bs:
        c_sc[b] = c_old[b] * on_v[b][3 * chunk:3 * chunk + 1] + jnp.where(
            cmask, _bdot_tn(k16[b], on_v[b][2 * chunk:3 * chunk] * v[b]), 0.0)
        n_sc[b] = n_old[b] * on_k[b][chunk:chunk + 1] + jnp.sum(on_k[b][0:chunk] * k[b], axis=0, keepdims=True)
        m_sc[b] = m_new[b]

    hh = jnp.concatenate(hs, axis=0)
    rv = lax.broadcasted_iota(jnp.int32, (vw, vw), 0)
    cv = lax.broadcasted_iota(jnp.int32, (vw, vw), 1)
    head_ones = jnp.where(_div(rv, dv) == _div(cv, dv), 1.0, 0.0).astype(BF16)
    ms = _exact_right_dot(hh * hh, head_ones, parts=2) * (1.0 / dv)
    out = hh * lax.rsqrt(ms + NORM_EPS) * on_ref[...] * _sigmoid(o_pre)
    o_ref[...] = out.astype(o_ref.dtype).reshape(o_ref.shape)


def _recurrent_kernel(x_ref, cw_ref, cb_ref, ib_ref, fb_ref, on_ref, fa_ref, fbk_ref, ft_ref, pc_ref, lnw_ref, lnb_ref,
                      om_ref, or_ref, prev_sc, c_sc, n_sc, m_sc, state_sc, *, layer, nb, chunk, rwkv_chunk, cps):
    @pl.when(pl.program_id(0) == 0)
    def _():
        for sc in (prev_sc, c_sc, n_sc, m_sc, state_sc):
            sc[...] = jnp.zeros(sc.shape, F32)

    scan = _rwkv_scan_stages(fa_ref, fbk_ref, ft_ref, pc_ref, lnw_ref, lnb_ref, or_ref, state_sc,
                             layer=layer, nb=nb, chunk=rwkv_chunk, cps=cps)
    mlstm = _mlstm_stages(x_ref, cw_ref, cb_ref, ib_ref, fb_ref, on_ref, om_ref, prev_sc, c_sc, n_sc, m_sc,
                          layer=layer, nb=nb, chunk=chunk)
    order = [scan, mlstm] + [scan, scan, mlstm] * (MLSTM_HEADS - 1) + [scan, mlstm]
    pending = {id(scan): 2 * cps + 1, id(mlstm): MLSTM_HEADS + 2}
    for g in order:
        next(g, None)
        pending[id(g)] -= 1
    for g in (scan, mlstm):
        for _ in range(pending[id(g)]):
            next(g, None)


def _recurrent(l, mlstm_in, fa, fb, ft, pc, batch, seq, cw, cb, ib, fbias, on, ln_w, ln_b):
    chunk = MLSTM_CHUNK
    rchunk = RWKV_CHUNK
    assert chunk % rchunk == 0
    w = RWKV_WIDTH
    x = mlstm_in.reshape(batch, seq, MLSTM_IN)
    fa = fa.reshape(batch, seq, 5 * w)
    fb = fb.reshape(batch, seq, 4 * w)
    pc = pc.reshape(batch, seq // rchunk, w)
    blk = lambda width: pl.BlockSpec((batch, chunk, width), lambda c: (0, c, 0))
    om, orw = pl.pallas_call(
        functools.partial(_recurrent_kernel, layer=l, nb=batch, chunk=chunk, rwkv_chunk=rchunk, cps=chunk // rchunk),
        out_shape=(jax.ShapeDtypeStruct((batch, seq, MLSTM_WIDTH), BF16),
                   jax.ShapeDtypeStruct((batch, seq, w), BF16)),
        grid=(seq // chunk,),
        in_specs=[blk(MLSTM_IN), _resident(cw, l), _resident(cb), _resident(ib), _resident(fbias), _resident(on),
                  blk(2 * w), blk(4 * w), blk(4 * w), _resident(pc), _resident(ln_w), _resident(ln_b)],
        out_specs=(blk(MLSTM_WIDTH), blk(w)),
        scratch_shapes=[pltpu.VMEM((batch * chunk, 2 * MLSTM_HEADS * MLSTM_QK), F32),
                        pltpu.VMEM((batch, MLSTM_HEADS * MLSTM_QK, MLSTM_WIDTH), F32),
                        pltpu.VMEM((batch, 1, MLSTM_HEADS * MLSTM_QK), F32),
                        pltpu.VMEM((batch, 1, LANES), F32),
                        pltpu.VMEM((batch, w, w), F32)],
        compiler_params=_cparams(("arbitrary",)),
        name="mlstm_rwkv_scan",
    )(x, cw, cb, ib, fbias, on, fa, fb, ft, pc, ln_w, ln_b)
    return om.reshape(batch * seq, MLSTM_WIDTH), orw.reshape(batch * seq, w)


def _ffn_kernel(x_ref, ya_ref, yb_ref, yc_ref, wo_ref, g_ref, wg_ref, wu_ref, wd_ref, fg_ref,
                o_ref, act_sc, *, layer, final_norm, tf):
    (g_ref,) = _rows(layer, g_ref)
    y = jnp.concatenate([ya_ref[...], yb_ref[...], yc_ref[...]], axis=-1)
    x1 = x_ref[...] + jnp.dot(y, wo_ref[...], preferred_element_type=F32)
    h = _rms(x1, g_ref[...]).astype(BF16)
    for c in range(D_FF // tf):
        gate = jnp.dot(h, wg_ref[:, c * tf:(c + 1) * tf], preferred_element_type=F32)
        up = jnp.dot(h, wu_ref[:, c * tf:(c + 1) * tf], preferred_element_type=F32)
        act_sc[:, c * tf:(c + 1) * tf] = (gate * _sigmoid(gate) * up).astype(BF16)
    out = x1 + jnp.dot(act_sc[...], wd_ref[...], preferred_element_type=F32)
    if final_norm:
        out = _rms(out, fg_ref[...])
    o_ref[...] = out


def _out_ffn(l, x, ya, yb, yc, wo, g, wg, wu, wd, fg, final_norm):
    t = x.shape[0]
    tm, tf = TM_FFN, TF_FFN
    row = lambda w: pl.BlockSpec((tm, w), lambda i: (i, 0))
    return pl.pallas_call(
        functools.partial(_ffn_kernel, layer=l, final_norm=final_norm, tf=tf),
        out_shape=jax.ShapeDtypeStruct((t, D_MODEL), F32),
        grid=(t // tm,),
        in_specs=[row(D_MODEL), row(MLA_WIDTH), row(RWKV_WIDTH), row(MLSTM_WIDTH), _resident(wo, l),
                  _resident(g), _resident(wg, l), _resident(wu, l), _resident(wd, l), _resident(fg)],
        out_specs=row(D_MODEL),
        scratch_shapes=[pltpu.VMEM((tm, D_FF), BF16)],
        compiler_params=_cparams(("parallel",)),
        name="out_ffn",
    )(x, ya, yb, yc, wo, g, wg, wu, wd, fg)


def _pad_cols(w, width):
    return jnp.pad(w, [(0, 0)] * (w.ndim - 1) + [(0, width - w.shape[-1])])


def _pad_rows(w, height):
    return jnp.pad(w, [(0, 0)] * (w.ndim - 2) + [(0, height - w.shape[-2]), (0, 0)])


def _rot_half_cols(w):
    half = w.shape[-1] // 2
    return jnp.concatenate([-w[..., half:], w[..., :half]], axis=-1)


def _stacked_weights(w_in, mla_w_uq, mla_w_ukv, rwkv_w2, rwkv_a2, rwkv_g2):
    depth = w_in.shape[0]
    wt = jnp.swapaxes(w_in, 1, 2)
    c_q, c_kv, k_pe = wt[:, 0:256], wt[:, 256:512], wt[:, 512:576]
    rw = wt[:, 576:1472]
    ml = wt[:, 1472:2248]
    k_pe_rot = jnp.swapaxes(_rot_half_cols(jnp.swapaxes(k_pe, 1, 2)), 1, 2)
    w_mla = jnp.concatenate([c_q, c_kv, _pad_rows(k_pe, LANES), _pad_rows(k_pe_rot, LANES)], axis=1)
    w_mlstm = jnp.concatenate([ml[:, 0:256], ml[:, 256:512], ml[:, 520:776],
                               _pad_rows(ml[:, 512:516], LANES), _pad_rows(ml[:, 516:520], LANES)], axis=1)
    w_all = jnp.concatenate([w_mla, rw, w_mlstm], axis=1).astype(BF16)

    uq = mla_w_uq.reshape(depth, MLA_Q_LORA, MLA_HEADS, MLA_NOPE + MLA_ROPE)
    nope = uq[..., :MLA_NOPE].reshape(depth, MLA_Q_LORA, MLA_HEADS * MLA_NOPE)
    pe = _pad_cols(uq[..., MLA_NOPE:], LANES).reshape(depth, MLA_Q_LORA, MLA_HEADS * LANES)
    per = _pad_cols(_rot_half_cols(uq[..., MLA_NOPE:]), LANES).reshape(depth, MLA_Q_LORA, MLA_HEADS * LANES)
    wq = jnp.concatenate([nope, pe, per], axis=-1).astype(BF16)
    ukv = mla_w_ukv.reshape(depth, MLA_KV_LORA, MLA_HEADS, MLA_NOPE + MLA_VDIM)
    wkv = jnp.concatenate([ukv[..., :MLA_NOPE].reshape(depth, MLA_KV_LORA, -1),
                           ukv[..., MLA_NOPE:].reshape(depth, MLA_KV_LORA, -1)], axis=-1).astype(BF16)

    rows = lambda before, wt: jnp.pad(wt, ((0, 0), (before, LANES - before - wt.shape[1]), (0, 0))).astype(BF16)
    w2p = rows(0, rwkv_w2)
    a2p = rows(RWKV_DECAY_LORA, rwkv_a2)
    g2p = rows(RWKV_DECAY_LORA + RWKV_AAA_LORA, rwkv_g2)
    return w_all, wq, wkv, w2p, a2p, g2p


def kernel(x, positions, mix_norm, w_in, mla_q_norm, mla_w_uq, mla_kv_norm, mla_w_ukv, mla_out_norm, rwkv_mu, rwkv_w0, rwkv_w2, rwkv_a0, rwkv_a2, rwkv_g2, rwkv_k_k, rwkv_k_a, rwkv_r_k, rwkv_ln_w, rwkv_ln_b, mlstm_conv_w, mlstm_conv_b, mlstm_i_bias, mlstm_f_bias, mlstm_out_norm, w_out, ffn_norm, w_gate, w_up, w_down, final_norm):
    batch, seq, _ = x.shape
    depth = w_in.shape[0]
    xt = x.reshape(batch * seq, D_MODEL)
    cos, sin, (wo, wg, wu, wd) = _rope_tables_and_casts(positions, (w_out, w_gate, w_up, w_down))
    w_all, wq, wkv, w2p, a2p, g2p = _stacked_weights(w_in, mla_w_uq, mla_w_ukv, rwkv_w2, rwkv_a2, rwkv_g2)
    ml_ib = _pad_cols(mlstm_i_bias, LANES)
    ml_fb = _pad_cols(mlstm_f_bias, LANES)
    for l in range(depth):
        q, k, v, fa, fb, pc, mlstm_in = _inproj(
            l, xt, seq, mix_norm, w_all, cos, sin, mla_q_norm, mla_kv_norm, wq, wkv,
            rwkv_mu, rwkv_w0, rwkv_a0, rwkv_k_k, rwkv_k_a, rwkv_r_k,
            w2p, a2p, g2p)
        y_mla = _mla_attention(l, q, k, v, mla_out_norm, batch, seq)
        ft = _rwkv_chunk(fa, batch, seq)
        y_mlstm, y_rwkv = _recurrent(l, mlstm_in, fa, fb, ft, pc, batch, seq, mlstm_conv_w, mlstm_conv_b, ml_ib, ml_fb,
                                     mlstm_out_norm, rwkv_ln_w, rwkv_ln_b)
        xt = _out_ffn(l, xt, y_mla, y_rwkv, y_mlstm, wo, ffn_norm, wg, wu, wd,
                      final_norm.reshape(1, -1), final_norm=(l == depth - 1))
    return xt.reshape(batch, seq, D_MODEL)
```

```python
import functools
import math

import jax
import jax.numpy as jnp
from jax import lax
from jax.experimental import pallas as pl
from jax.experimental.pallas import tpu as pltpu

F32 = jnp.float32
BF16 = jnp.bfloat16

D_MODEL = 1024
DEPTH = 2
MLA_HEADS = 4
MLA_NOPE = 128
MLA_ROPE = 64
MLA_VDIM = 128
MLA_Q_LORA = 256
MLA_KV_LORA = 256
MLA_WIDTH = MLA_HEADS * MLA_VDIM
MLA_QK = 256
ROPE_THETA = 10000.0
RWKV_HEADS = 4
RWKV_HEAD = 64
RWKV_WIDTH = 256
RWKV_DECAY_LORA = 32
RWKV_AAA_LORA = 32
RWKV_GATE_LORA = 64
RWKV_IN = 3 * RWKV_WIDTH + 128
RWKV_LN_EPS = 64e-5
MLSTM_HEADS = 4
MLSTM_QK = 32
MLSTM_V = 64
MLSTM_WIDTH = 256
MLSTM_CONV = 4
MLSTM_IN = 1024
D_FF = 2816
NORM_EPS = 1e-6
LANES = 128
SUBLANES = 8

MLA_CQ = 0
MLA_CKV = MLA_CQ + MLA_Q_LORA
MLA_KPE = MLA_CKV + MLA_KV_LORA
MLA_KPER = MLA_KPE + LANES
MLA_IN = MLA_KPER + LANES

TM_INPROJ = 512
TQ_ATTN = 512
TK_ATTN = 512
HP_ATTN = 4
RWKV_CHUNK = 64
RWKV_PREP_CHUNKS = 2
MLSTM_CHUNK = 256
TM_FFN = 512
TF_FFN = 256
VMEM_LIMIT = 56 * 1024 * 1024


def _cparams(sem):
    return pltpu.CompilerParams(dimension_semantics=sem, vmem_limit_bytes=VMEM_LIMIT)


def _resident(a, layer=None):
    if layer is None:
        nd = a.ndim
        return pl.BlockSpec(a.shape, lambda *_: (0,) * nd, pipeline_mode=pl.Buffered(1))
    nd = a.ndim - 1
    return pl.BlockSpec((None,) + a.shape[1:], lambda *_: (layer,) + (0,) * nd, pipeline_mode=pl.Buffered(1))


def _rows(layer, *refs):
    return [r.at[layer:layer + 1] for r in refs]


def _bdot(a, b):
    return jnp.dot(a.astype(BF16), b.astype(BF16), preferred_element_type=F32)


def _dot_nt(a, b):
    return lax.dot_general(a, b, (((1,), (1,)), ((), ())), preferred_element_type=F32)


def _bdot_nt(a, b):
    return lax.dot_general(a.astype(BF16), b.astype(BF16), (((1,), (1,)), ((), ())),
                           preferred_element_type=F32)


def _bdot_tn(a, b):
    return lax.dot_general(a.astype(BF16), b.astype(BF16), (((0,), (0,)), ((), ())),
                           preferred_element_type=F32)


def _split3(x):
    h = x.astype(BF16)
    r1 = x - h.astype(F32)
    m = r1.astype(BF16)
    lo = (r1 - m.astype(F32)).astype(BF16)
    return h, m, lo


def _exact_left_dot(sel, x):
    h, m, lo = _split3(x)
    s = sel.astype(BF16)
    return (jnp.dot(s, h, preferred_element_type=F32) + jnp.dot(s, m, preferred_element_type=F32)
            + jnp.dot(s, lo, preferred_element_type=F32))


def _exact_right_dot(x, sel, parts=3):
    pieces = _split3(x)[:parts]
    s = sel.astype(BF16)
    out = jnp.dot(pieces[0], s, preferred_element_type=F32)
    for p in pieces[1:]:
        out = out + jnp.dot(p, s, preferred_element_type=F32)
    return out


def _rms(x, g):
    return x * lax.rsqrt(jnp.mean(x * x, axis=-1, keepdims=True) + NORM_EPS) * g


def _sigmoid(x):
    return 1.0 / (1.0 + jnp.exp(-x))


def _log_sigmoid(x):
    return jnp.minimum(x, 0.0) - jnp.log1p(jnp.exp(-jnp.abs(x)))


def _div(x, d):
    assert d & (d - 1) == 0
    return lax.shift_right_logical(x, d.bit_length() - 1)


def _mod(x, d):
    assert d & (d - 1) == 0
    return lax.bitwise_and(x, d - 1)


def _shift_rows(x, prev, s, chunk):
    n = x.shape[0]
    row = lax.broadcasted_iota(jnp.int32, x.shape, 0)
    return jnp.where(_mod(row, chunk) >= s, pltpu.roll(x, s, 0), pltpu.roll(prev, n - chunk + s, 0))


def _rope_cast_kernel(pos_ref, invf_ref, *refs):
    n = (len(refs) - 2) // 2
    w_refs, cos_ref, sin_ref, o_refs = refs[:n], refs[n], refs[n + 1], refs[n + 2:]
    ang = pos_ref[...].astype(F32) * invf_ref[...]
    cos_ref[...] = jnp.cos(ang)
    sin_ref[...] = jnp.sin(ang)
    for w_ref, o_ref in zip(w_refs, o_refs):
        o_ref[...] = w_ref[...].astype(BF16)


def _rope_tables_and_casts(positions, weights):
    t = positions.size
    tm = min(1024, t)
    steps = t // tm
    inv_freq = ROPE_THETA ** (-jnp.arange(0, MLA_ROPE, 2, dtype=F32) / MLA_ROPE)
    invf = jnp.tile(inv_freq, LANES // (MLA_ROPE // 2))[None, :]
    flat = [w.reshape(-1, w.shape[-1]) for w in weights]
    rows = [f.shape[0] // steps for f in flat]
    assert all(f.shape[0] % steps == 0 and r % 16 == 0 for f, r in zip(flat, rows))
    slab = lambda r, c: pl.BlockSpec((r, c), lambda i: (i, 0))
    outs = pl.pallas_call(
        _rope_cast_kernel,
        out_shape=(jax.ShapeDtypeStruct((t, LANES), F32), jax.ShapeDtypeStruct((t, LANES), F32))
        + tuple(jax.ShapeDtypeStruct(f.shape, BF16) for f in flat),
        grid=(steps,),
        in_specs=[slab(tm, 1), pl.BlockSpec((1, LANES), lambda i: (0, 0))]
        + [slab(r, f.shape[1]) for f, r in zip(flat, rows)],
        out_specs=(slab(tm, LANES), slab(tm, LANES)) + tuple(slab(r, f.shape[1]) for f, r in zip(flat, rows)),
        compiler_params=_cparams(("parallel",)),
        name="rope_tables_casts",
    )(positions.reshape(t, 1), invf, *flat)
    return outs[0], outs[1], [o.reshape(w.shape) for o, w in zip(outs[2:], weights)]


(FA_RT, FA_V, FA_AT, FA_KT, FA_BT) = range(5)
(FB_KH, FB_BH, FB_BONUS, FB_GATE) = range(4)
(FT_TA, FT_UV, FT_YV, FT_ARB) = range(4)
RWKV_GROUP = 256


def _segsum(x, seg):
    assert 2 * seg == LANES and x.shape[1] % LANES == 0
    out = []
    for blk in range(x.shape[1] // LANES):
        xb = x[:, blk * LANES:(blk + 1) * LANES]
        low = lax.broadcasted_iota(jnp.int32, xb.shape, 1) < seg
        s_low = jnp.sum(jnp.where(low, xb, 0.0), axis=-1, keepdims=True)
        s_high = jnp.sum(jnp.where(low, 0.0, xb), axis=-1, keepdims=True)
        out.append(jnp.where(low, s_low, s_high))
    return jnp.concatenate(out, axis=-1)


def _rwkv_masks(chunk):
    w = RWKV_WIDTH
    r = lax.broadcasted_iota(jnp.int32, (w, w), 0)
    c = lax.broadcasted_iota(jnp.int32, (w, w), 1)
    return r, c, _div(r, chunk) == _div(c, chunk)


def _rwkv_features(xs, r0, prm, fa_ref, fb_ref, pc_ref, chunk):
    w0_ref, a0_ref, kk_ref, ka_ref, rk_ref, w2_ref, a2_ref, g2_ref = prm
    w = RWKV_WIDTH
    hd = RWKV_HEAD
    n = xs.shape[0]
    r = xs[:, 0:w]
    k = xs[:, w:2 * w]
    v = xs[:, 2 * w:3 * w]
    lor = xs[:, 3 * w:]
    ld = -math.exp(-0.5) * _sigmoid(w0_ref[...] + _bdot(jnp.tanh(lor), w2_ref[...]))
    a = _sigmoid(a0_ref[...] + _bdot(lor, a2_ref[...]))
    g = _bdot(_sigmoid(lor), g2_ref[...])
    kk = k * kk_ref[...]
    kk = kk / jnp.maximum(jnp.sqrt(_segsum(kk * kk, hd)), 1e-12)
    k2 = k * (1.0 + (a - 1.0) * ka_ref[...])
    kb = kk * a
    bonus = _segsum(r * k2 * rk_ref[...], hd) * v

    assert n <= RWKV_WIDTH
    ri, ci, same_chunk = _rwkv_masks(chunk)
    tri = jnp.where(same_chunk & (ci <= ri), 1.0, 0.0)[0:n, 0:n]
    cl = _exact_left_dot(tri, ld)
    units = n // chunk
    cl_last = jnp.concatenate(
        [jnp.broadcast_to(cl[(u + 1) * chunk - 1:(u + 1) * chunk, :], (chunk, w)) for u in range(units)], axis=0)
    e_neg = jnp.exp(-cl)
    e_end = jnp.exp(cl_last - cl)
    rows = slice(r0, r0 + n)

    def put(ref, sec, val):
        ref[rows, sec * w:(sec + 1) * w] = val.astype(ref.dtype)

    put(fa_ref, FA_RT, r * jnp.exp(cl))
    put(fa_ref, FA_V, v)
    put(fa_ref, FA_AT, kk * jnp.exp(cl - ld))
    put(fa_ref, FA_KT, k2 * e_neg)
    put(fa_ref, FA_BT, kb * e_neg)
    put(fb_ref, FB_KH, k2 * e_end)
    put(fb_ref, FB_BH, kb * e_end)
    put(fb_ref, FB_BONUS, bonus)
    put(fb_ref, FB_GATE, g)
    for u in range(units):
        c = r0 // chunk + u
        pc_ref[c:c + 1, :] = jnp.exp(cl_last[u * chunk:u * chunk + 1, :])


def _inproj_kernel(x_ref, g_ref, w_ref, cos_ref, sin_ref, qn_ref, kvn_ref, wq_ref, wkv_ref,
                   mu_ref, w0_ref, a0_ref, kk_ref, ka_ref, rk_ref, w2_ref, a2_ref, g2_ref,
                   q_ref, k_ref, v_ref, fa_ref, fb_ref, pc_ref, mlstm_ref, prev_sc, *, layer, chunk, tiles_per_seq):
    tm = x_ref.shape[0]
    g_ref, qn_ref, kvn_ref, mu_ref, w0_ref, a0_ref, kk_ref, ka_ref, rk_ref = _rows(
        layer, g_ref, qn_ref, kvn_ref, mu_ref, w0_ref, a0_ref, kk_ref, ka_ref, rk_ref)

    @pl.when(pl.program_id(0) % tiles_per_seq == 0)
    def _():
        prev_sc[...] = jnp.zeros(prev_sc.shape, F32)

    hb = _rms(x_ref[...], g_ref[...]).astype(BF16)
    rw = _dot_nt(hb, w_ref[MLA_IN:MLA_IN + RWKV_IN, :])
    row = lax.broadcasted_iota(jnp.int32, rw.shape, 0)
    shifted = jnp.where(row >= 1, pltpu.roll(rw, 1, 0), prev_sc[0:1, :])
    prev_sc[0:1, :] = rw[tm - 1:tm, :]
    xs = rw + (shifted - rw) * mu_ref[...]
    prm = (w0_ref, a0_ref, kk_ref, ka_ref, rk_ref, w2_ref, a2_ref, g2_ref)

    mla = _dot_nt(hb, w_ref[0:MLA_IN, :])
    groups = tm // RWKV_GROUP
    for gi in range(groups // 2):
        _rwkv_features(xs[gi * RWKV_GROUP:(gi + 1) * RWKV_GROUP], gi * RWKV_GROUP, prm, fa_ref, fb_ref, pc_ref, chunk)
    mlstm_ref[...] = _dot_nt(hb, w_ref[MLA_IN + RWKV_IN:, :])
    for gi in range(groups // 2, groups):
        _rwkv_features(xs[gi * RWKV_GROUP:(gi + 1) * RWKV_GROUP], gi * RWKV_GROUP, prm, fa_ref, fb_ref, pc_ref, chunk)

    cos = cos_ref[...]
    sin = sin_ref[...]
    scale = (MLA_NOPE + MLA_ROPE) ** -0.5 * math.log2(math.e)
    hw = MLA_HEADS * LANES
    cqn = _rms(mla[:, MLA_CQ:MLA_CKV], qn_ref[...]).astype(BF16)
    q = jnp.dot(cqn, wq_ref[...], preferred_element_type=F32)
    ckvn = _rms(mla[:, MLA_CKV:MLA_KPE], kvn_ref[...]).astype(BF16)
    kv = jnp.dot(ckvn, wkv_ref[...], preferred_element_type=F32)
    kp = (mla[:, MLA_KPE:MLA_KPER] * cos + mla[:, MLA_KPER:MLA_IN] * sin).astype(BF16)
    for h in range(MLA_HEADS):
        c0 = h * LANES
        pe = q[:, hw + c0:hw + c0 + LANES] * cos + q[:, 2 * hw + c0:2 * hw + c0 + LANES] * sin
        q_ref[:, h * MLA_QK:h * MLA_QK + LANES] = (q[:, c0:c0 + LANES] * scale).astype(BF16)
        q_ref[:, h * MLA_QK + LANES:(h + 1) * MLA_QK] = (pe * scale).astype(BF16)
        k_ref[:, h * MLA_QK:h * MLA_QK + LANES] = kv[:, c0:c0 + LANES].astype(BF16)
        k_ref[:, h * MLA_QK + LANES:(h + 1) * MLA_QK] = kp
    v_ref[...] = kv[:, hw:].astype(BF16)


def _inproj(l, x, seq, g, w, cos, sin, qn, kvn, wq, wkv, mu, w0, a0, k_k, k_a, r_k, w2p, a2p, g2p):
    t = x.shape[0]
    tm = TM_INPROJ
    chunk = RWKV_CHUNK
    assert seq % tm == 0 and tm % RWKV_GROUP == 0 and RWKV_GROUP % chunk == 0
    row = lambda width: pl.BlockSpec((tm, width), lambda i: (i, 0))
    rw = RWKV_WIDTH
    return pl.pallas_call(
        functools.partial(_inproj_kernel, layer=l, chunk=chunk, tiles_per_seq=seq // tm),
        out_shape=(jax.ShapeDtypeStruct((t, MLA_HEADS * MLA_QK), BF16),
                   jax.ShapeDtypeStruct((t, MLA_HEADS * MLA_QK), BF16),
                   jax.ShapeDtypeStruct((t, MLA_WIDTH), BF16),
                   jax.ShapeDtypeStruct((t, 5 * rw), BF16),
                   jax.ShapeDtypeStruct((t, 4 * rw), BF16),
                   jax.ShapeDtypeStruct((t // chunk, rw), F32),
                   jax.ShapeDtypeStruct((t, MLSTM_IN), F32)),
        grid=(t // tm,),
        in_specs=[row(D_MODEL), _resident(g), _resident(w, l), row(LANES), row(LANES), _resident(qn),
                  _resident(kvn), _resident(wq, l), _resident(wkv, l),
                  _resident(mu), _resident(w0), _resident(a0), _resident(k_k), _resident(k_a),
                  _resident(r_k), _resident(w2p, l), _resident(a2p, l), _resident(g2p, l)],
        out_specs=(row(MLA_HEADS * MLA_QK), row(MLA_HEADS * MLA_QK), row(MLA_WIDTH), row(5 * rw), row(4 * rw),
                   pl.BlockSpec((tm // chunk, rw), lambda i: (i, 0)), row(MLSTM_IN)),
        scratch_shapes=[pltpu.VMEM((SUBLANES, RWKV_IN), F32)],
        compiler_params=_cparams(("arbitrary",)),
        name="inproj",
    )(x, g, w, cos, sin, qn, kvn, wq, wkv, mu, w0, a0, k_k, k_a, r_k, w2p, a2p, g2p)


def _attn_kernel(q_ref, k_ref, v_ref, g_ref, o_ref, m_sc, acc_sc, sa_sc, sb_sc, *, layer, tq, tk, hp):
    i = pl.program_id(2)
    m_sc[...] = jnp.full(m_sc.shape, -jnp.inf, F32)
    acc_sc[...] = jnp.zeros(acc_sc.shape, F32)
    sub = m_sc.shape[1]
    hs = range(hp)
    ones = jnp.ones((acc_sc.shape[1] - MLA_VDIM, tk), BF16)
    (g_ref,) = _rows(layer, g_ref)

    def produce(j, s_ref):
        off = pl.multiple_of(j * tk, tk)
        for h in hs:
            s_ref[h] = lax.dot_general(k_ref[0, pl.ds(off, tk), h * MLA_QK:(h + 1) * MLA_QK],
                                       q_ref[0, :, h * MLA_QK:(h + 1) * MLA_QK],
                                       (((1,), (1,)), ((), ())), preferred_element_type=F32)

    def consume(j, s_ref, masked):
        off = pl.multiple_of(j * tk, tk)
        s = [s_ref[h] for h in hs]
        if masked:
            keys = lax.broadcasted_iota(jnp.int32, (tk, tq), 0)
            queries = lax.broadcasted_iota(jnp.int32, (tk, tq), 1)
            s = [jnp.where(keys <= queries, s[h], -jnp.inf) for h in hs]
        m_old = [m_sc[h][0:1] for h in hs]
        m_new = [jnp.maximum(m_old[h], jnp.max(s[h], axis=0, keepdims=True)) for h in hs]
        p = [jnp.exp2(s[h] - m_new[h]) for h in hs]
        alpha = [jnp.exp2(m_old[h] - m_new[h]) for h in hs]
        for h in hs:
            m_sc[h] = jnp.broadcast_to(m_new[h], (sub, tq))
        v1 = [jnp.concatenate([jnp.transpose(v_ref[0, pl.ds(off, tk), h * MLA_VDIM:(h + 1) * MLA_VDIM]), ones],
                              axis=0) for h in hs]
        pv = [jnp.dot(v1[h], p[h].astype(BF16), preferred_element_type=F32) for h in hs]
        for h in hs:
            acc_sc[h] = alpha[h] * acc_sc[h] + pv[h]

    def pair(jj, c):
        j = 2 * jj
        produce(j + 1, sb_sc)
        consume(j, sa_sc, False)
        produce(j + 2, sa_sc)
        consume(j + 1, sb_sc, False)
        return c

    produce(0, sa_sc)
    lax.fori_loop(0, i // 2, pair, 0)

    @pl.when(i % 2 == 0)
    def _():
        consume(i, sa_sc, True)

    @pl.when(i % 2 == 1)
    def _():
        produce(i, sb_sc)
        consume(i - 1, sa_sc, False)
        consume(i, sb_sc, True)

    for h in hs:
        o = acc_sc[h, 0:MLA_VDIM] / acc_sc[h, MLA_VDIM:MLA_VDIM + 1]
        o = o * lax.rsqrt(jnp.mean(o * o, axis=0, keepdims=True) + NORM_EPS)
        o_ref[0, :, h * MLA_VDIM:(h + 1) * MLA_VDIM] = (
            jnp.transpose(o) * g_ref[:, h * MLA_VDIM:(h + 1) * MLA_VDIM]).astype(o_ref.dtype)


def _mla_attention(l, q, k, v, g, batch, seq):
    tq, tk, hp = TQ_ATTN, TK_ATTN, HP_ATTN
    assert tq == tk and MLA_VDIM == LANES and hp == MLA_HEADS
    q = q.reshape(batch, seq, MLA_HEADS * MLA_QK)
    k = k.reshape(batch, seq, MLA_HEADS * MLA_QK)
    v = v.reshape(batch, seq, MLA_WIDTH)
    out = pl.pallas_call(
        functools.partial(_attn_kernel, layer=l, tq=tq, tk=tk, hp=hp),
        out_shape=jax.ShapeDtypeStruct((batch, seq, MLA_WIDTH), BF16),
        grid=(batch, MLA_HEADS // hp, seq // tq),
        in_specs=[pl.BlockSpec((1, tq, hp * MLA_QK), lambda b, h, i: (b, i, h)),
                  pl.BlockSpec((1, seq, hp * MLA_QK), lambda b, h, i: (b, 0, h)),
                  pl.BlockSpec((1, seq, hp * MLA_VDIM), lambda b, h, i: (b, 0, h)),
                  _resident(g)],
        out_specs=pl.BlockSpec((1, tq, hp * MLA_VDIM), lambda b, h, i: (b, i, h)),
        scratch_shapes=[pltpu.VMEM((hp, SUBLANES, tq), F32),
                        pltpu.VMEM((hp, MLA_VDIM + 2 * SUBLANES, tq), F32),
                        pltpu.VMEM((hp, tk, tq), F32), pltpu.VMEM((hp, tk, tq), F32)],
        compiler_params=_cparams(("parallel", "parallel", "arbitrary")),
        name="mla_attention",
    )(q, k, v, g)
    return out.reshape(batch * seq, MLA_WIDTH)


def _tile_heads(z):
    return jnp.concatenate([z] * RWKV_HEADS, axis=0)


def _rwkv_chunk_kernel(fa_ref, ft_ref, *, nb, chunk, cps):
    seg = cps * chunk
    n = nb * seg
    w = RWKV_WIDTH
    hd = RWKV_HEAD
    x = fa_ref[...].reshape(n, fa_ref.shape[-1])
    r16, v16, a16, k16, b16 = (x[:, s * w:(s + 1) * w] for s in (FA_RT, FA_V, FA_AT, FA_KT, FA_BT))

    ri, ci, same_chunk = _rwkv_masks(chunk)
    bd = _div(ri, chunk) == _div(ci, hd)
    rt = lax.broadcasted_iota(jnp.int32, (chunk, w), 0)
    cs = _mod(lax.broadcasted_iota(jnp.int32, (chunk, w), 1), chunk)
    strict = cs < rt
    incl = cs <= rt
    c16 = _div(rt, 16) == _div(cs, 16)
    c32 = _div(rt, 32) == _div(cs, 32)
    eye = jnp.where(rt == cs, 1.0, 0.0)
    units = nb * cps

    def block_diag(z):
        return jnp.where(same_chunk, _tile_heads(z), 0.0)

    def put(u, sec, val):
        b, j = divmod(u, cps)
        ft_ref[b, j * chunk:(j + 1) * chunk, sec * w:(sec + 1) * w] = val.astype(ft_ref.dtype)

    def mm(x, y):
        return jnp.dot(x, y, preferred_element_type=F32)

    us = range(units)
    sls = [slice(u * chunk, (u + 1) * chunk) for u in us]
    a_st = [jnp.where(bd, _tile_heads(a16[sl]), 0.0) for sl in sls]
    v_st = [jnp.where(bd, _tile_heads(v16[sl]), 0.0) for sl in sls]
    kb_t = [jnp.concatenate([jnp.transpose(jnp.where(bd, _tile_heads(k16[sl]), 0.0)),
                             jnp.transpose(jnp.where(bd, _tile_heads(b16[sl]), 0.0))], axis=1) for sl in sls]
    sc = [mm(jnp.concatenate([a16[sls[u]], r16[sls[u]]], axis=0), kb_t[u]) for u in us]
    l_ab = [jnp.where(strict, sc[u][0:chunk, w:], 0.0) for u in us]
    l_ak = [jnp.where(strict, sc[u][0:chunk, 0:w], 0.0).astype(BF16) for u in us]
    a_rk = [jnp.where(incl, sc[u][chunk:, 0:w], 0.0).astype(BF16) for u in us]
    for u in us:
        put(u, FT_ARB, jnp.where(incl, sc[u][chunk:, w:], 0.0))
    xm = [-jnp.where(c16, l_ab[u], 0.0) for u in us]
    xm16 = [z.astype(BF16) for z in xm]
    off32 = [block_diag(jnp.where(c32 & jnp.logical_not(c16), l_ab[u], 0.0).astype(BF16)) for u in us]
    off64 = [block_diag(jnp.where(jnp.logical_not(c32), l_ab[u], 0.0).astype(BF16)) for u in us]
    x2 = [mm(xm16[u], block_diag(xm16[u])).astype(BF16) for u in us]
    x2_bd = [block_diag(z) for z in x2]
    lv = [mm(jnp.concatenate([l_ak[u], a_rk[u]], axis=0), v_st[u]) for u in us]
    wv = [block_diag(lv[u][0:chunk].astype(BF16)) for u in us]
    t_lo = [eye + xm[u] for u in us]
    tx = [mm(jnp.concatenate([t_lo[u].astype(BF16), x2[u]], axis=0), x2_bd[u]) for u in us]
    t_lo = [(t_lo[u] + tx[u][0:chunk]).astype(BF16) for u in us]
    x4 = [tx[u][chunk:] for u in us]
    x4b = [z.astype(BF16) for z in x4]
    x4_bd = [block_diag(z) for z in x4b]
    for u in us:
        put(u, FT_YV, lv[u][chunk:])
    x8_bd = [block_diag(mm(x4b[u], x4_bd[u]).astype(BF16)) for u in us]
    t_hi = [eye + x4[u] for u in us]
    t_hi = [block_diag((t_hi[u] + mm(t_hi[u].astype(BF16), x8_bd[u])).astype(BF16)) for u in us]
    t_inv = [mm(t_lo[u], t_hi[u]) for u in us]
    for off in (off32, off64):
        tb = [z.astype(BF16) for z in t_inv]
        mid = [mm(tb[u], off[u]).astype(BF16) for u in us]
        t_inv = [t_inv[u] - mm(mid[u], block_diag(tb[u])) for u in us]
    tb = [z.astype(BF16) for z in t_inv]
    for u in us:
        put(u, FT_TA, mm(tb[u], a_st[u]))
    for u in us:
        put(u, FT_UV, mm(tb[u], wv[u]))


def _rwkv_scan_stages(fa_ref, fb_ref, ft_ref, pc_ref, lnw_ref, lnb_ref, o_ref, state_sc, *, layer, nb, chunk, cps):
    seg = cps * chunk
    n = nb * seg
    w = RWKV_WIDTH
    hd = RWKV_HEAD
    lnw_ref, lnb_ref = _rows(layer, lnw_ref, lnb_ref)
    ri, ci, _ = _rwkv_masks(chunk)
    bd = _div(ri, chunk) == _div(ci, hd)
    bdv = _div(ri, hd) == _div(ci, hd)

    def sec(ref, b, j, s):
        return ref[b, j * chunk:(j + 1) * chunk, s * w:(s + 1) * w]

    ys = [[None] * cps for _ in range(nb)]
    bs = range(nb)
    for j in range(cps):
        gs = [state_sc[b] for b in bs]
        p1 = [lax.dot_general(jnp.concatenate([sec(ft_ref, b, j, FT_TA), sec(fa_ref, b, j, FA_RT)], axis=0),
                              gs[b].astype(BF16), (((1,), (1,)), ((), ())), preferred_element_type=F32)
              for b in bs]
        yield
        u = [(p1[b][0:chunk] + sec(ft_ref, b, j, FT_UV).astype(F32)).astype(BF16) for b in bs]
        upd = [lax.dot_general(jnp.concatenate([sec(fa_ref, b, j, FA_V), -u[b]], axis=0),
                               jnp.concatenate([sec(fb_ref, b, j, FB_KH), sec(fb_ref, b, j, FB_BH)], axis=0),
                               (((0,), (0,)), ((), ())), preferred_element_type=F32) for b in bs]
        for b in bs:
            state_sc[b] = gs[b] * pc_ref[b, pl.ds(pl.program_id(0) * cps + j, 1), :] + jnp.where(bdv, upd[b], 0.0)
        for b in bs:
            u_st = jnp.where(bd, _tile_heads(u[b]), 0.0)
            ys[b][j] = (p1[b][chunk:] + sec(ft_ref, b, j, FT_YV).astype(F32)
                        - jnp.dot(sec(ft_ref, b, j, FT_ARB), u_st, preferred_element_type=F32))
        yield

    y = jnp.concatenate([ys[b][j] for b in range(nb) for j in range(cps)], axis=0)
    mean = _segsum(y, hd) * (1.0 / hd)
    d = y - mean
    var = _segsum(d * d, hd) * (1.0 / hd)
    yn = d * lax.rsqrt(var + RWKV_LN_EPS) * lnw_ref[...] + lnb_ref[...]
    bonus = fb_ref[:, :, FB_BONUS * w:(FB_BONUS + 1) * w].reshape(n, w).astype(F32)
    gate = fb_ref[:, :, FB_GATE * w:(FB_GATE + 1) * w].reshape(n, w).astype(F32)
    o_ref[...] = ((yn + bonus) * gate).astype(o_ref.dtype).reshape(o_ref.shape)


def _rwkv_chunk(fa, batch, seq):
    chunk = RWKV_CHUNK
    w = RWKV_WIDTH
    assert RWKV_HEADS * chunk == w
    cps = RWKV_PREP_CHUNKS
    seg = cps * chunk
    return pl.pallas_call(
        functools.partial(_rwkv_chunk_kernel, nb=batch, chunk=chunk, cps=cps),
        out_shape=jax.ShapeDtypeStruct((batch, seq, 4 * w), BF16),
        grid=(seq // seg,),
        in_specs=[pl.BlockSpec((batch, seg, 5 * w), lambda c: (0, c, 0))],
        out_specs=pl.BlockSpec((batch, seg, 4 * w), lambda c: (0, c, 0)),
        compiler_params=_cparams(("parallel",)),
        name="rwkv7_chunk",
    )(fa.reshape(batch, seq, 5 * w))


ML_QK = 0
ML_V = 2 * MLSTM_HEADS * MLSTM_QK
ML_O = ML_V + MLSTM_WIDTH
ML_I = ML_O + MLSTM_WIDTH
ML_F = ML_I + LANES


def _cummax_rows(x):
    n = x.shape[0]
    row = lax.broadcasted_iota(jnp.int32, x.shape, 0)
    sh = 1
    while sh < n:
        x = jnp.maximum(x, jnp.where(row >= sh, pltpu.roll(x, sh, 0), -jnp.inf))
        sh *= 2
    return x


def _mlstm_stages(x_ref, cw_ref, cb_ref, ib_ref, fb_ref, on_ref, o_ref,
                  prev_sc, c_sc, n_sc, m_sc, *, layer, nb, chunk):
    n = nb * chunk
    nh = MLSTM_HEADS
    dk = MLSTM_QK
    dv = MLSTM_V
    qkw = nh * dk
    vw = MLSTM_WIDTH
    cb_ref, ib_ref, fb_ref, on_ref = _rows(layer, cb_ref, ib_ref, fb_ref, on_ref)
    x = x_ref[...].reshape(n, MLSTM_IN)
    qk_raw = x[:, ML_QK:ML_V]
    prev = prev_sc[...]
    conv = cb_ref[...] + qk_raw * cw_ref[MLSTM_CONV - 1:MLSTM_CONV, :]
    for s in range(1, MLSTM_CONV):
        conv = conv + _shift_rows(qk_raw, prev, s, chunk) * cw_ref[MLSTM_CONV - 1 - s:MLSTM_CONV - s, :]
    prev_sc[...] = qk_raw
    qk = conv * _sigmoid(conv)
    q_all = qk[:, 0:qkw] * (dk ** -0.5)
    k_all = qk[:, qkw:]
    v_all = x[:, ML_V:ML_O]
    o_pre = x[:, ML_O:ML_I]
    li_all = x[:, ML_I:ML_F] + ib_ref[...]
    lf_all = _log_sigmoid(x[:, ML_F:ML_F + LANES] + fb_ref[...])

    ri = lax.broadcasted_iota(jnp.int32, (chunk, chunk), 0)
    ci = lax.broadcasted_iota(jnp.int32, (chunk, chunk), 1)
    causal = ci <= ri
    tri = jnp.where(causal, 1.0, 0.0)
    lane_k = lax.broadcasted_iota(jnp.int32, (chunk, qkw), 1)
    lane_v = lax.broadcasted_iota(jnp.int32, (chunk, vw), 1)
    rc = lax.broadcasted_iota(jnp.int32, (qkw, vw), 0)
    cc = lax.broadcasted_iota(jnp.int32, (qkw, vw), 1)
    cmask = _div(rc, dk) == _div(cc, dv)
    expand_v = jnp.where(rc == _div(cc, dv), 1.0, 0.0).astype(BF16)
    rk = lax.broadcasted_iota(jnp.int32, (qkw, qkw), 0)
    ck = lax.broadcasted_iota(jnp.int32, (qkw, qkw), 1)
    expand_k = jnp.where(rk == _div(ck, dk), 1.0, 0.0).astype(BF16)
    gather_k = jnp.where(_div(rk, dk) == ck, 1.0, 0.0).astype(BF16)

    bs = range(nb)
    sls = [slice(b * chunk, (b + 1) * chunk) for b in bs]
    q = [q_all[sl] for sl in sls]
    k = [k_all[sl] for sl in sls]
    k16 = [z.astype(BF16) for z in k]
    v = [v_all[sl] for sl in sls]
    li = [li_all[sl] for sl in sls]
    c_old = [c_sc[b] for b in bs]
    n_old = [n_sc[b] for b in bs]
    m_prev = [m_sc[b] for b in bs]
    g = [_exact_left_dot(tri, lf_all[sl]) for sl in sls]
    lig = [li[b] - g[b] for b in bs]
    inter_log = [g[b] + m_prev[b] for b in bs]
    m_t = [jnp.maximum(inter_log[b], g[b] + _cummax_rows(lig[b])) for b in bs]
    inter_w = [jnp.exp(inter_log[b] - m_t[b]) for b in bs]
    log2e = math.log2(math.e)
    gm = [(g[b] - m_t[b]) * log2e for b in bs]
    lig_t = [jnp.transpose(z * log2e) for z in lig]
    qn = [_exact_right_dot(q[b] * n_old[b], gather_k, parts=2) for b in bs]
    q_c = [_bdot(q[b], c_old[b]) for b in bs]
    ssum = [jnp.zeros((chunk, LANES), F32) for _ in bs]
    num = [jnp.zeros((chunk, vw), F32) for _ in bs]
    yield
    for h in range(nh):
        mk = (lane_k >= h * dk) & (lane_k < (h + 1) * dk)
        mv = (lane_v >= h * dv) & (lane_v < (h + 1) * dv)
        qk_h = [lax.dot_general(jnp.where(mk, q[b], 0.0).astype(BF16), k16[b], (((1,), (1,)), ((), ())),
                                preferred_element_type=F32) for b in bs]
        d = [jnp.broadcast_to(gm[b][:, h:h + 1], (chunk, chunk)) + lig_t[b][h:h + 1, :] for b in bs]
        s = [qk_h[b] * jnp.exp2(jnp.where(causal, d[b], -jnp.inf)) for b in bs]
        ssum = [jnp.where(lane_k == h, jnp.sum(s[b], axis=-1, keepdims=True), ssum[b]) for b in bs]
        num = [num[b] + _bdot(s[b], jnp.where(mv, v[b], 0.0)) for b in bs]
        yield
    den = [inter_w[b] * qn[b] + ssum[b] for b in bs]
    rden = [1.0 / jnp.maximum(jnp.abs(den[b]), jnp.exp(-m_t[b])) for b in bs]
    g_last = [g[b][chunk - 1:chunk, :] for b in bs]
    a_all = [g_last[b] - g[b] + li[b] for b in bs]
    m_new = [jnp.maximum(g_last[b] + m_prev[b], jnp.max(a_all[b], axis=0, keepdims=True)) for b in bs]
    dec = [jnp.exp(g_last[b] + m_prev[b] - m_new[b]) for b in bs]
    wts = [jnp.exp(a_all[b] - m_new[b]) for b in bs]
    per_head = [jnp.concatenate([inter_w[b], rden[b], wts[b], jnp.broadcast_to(dec[b], (8, LANES))], axis=0)
                for b in bs]
    pieces = [_split3(per_head[b])[:2] for b in bs]
    on_v = [sum(jnp.dot(p, expand_v, preferred_element_type=F32) for p in pieces[b]) for b in bs]
    on_k = [sum(jnp.dot(p[2 * chunk:], expand_k, preferred_element_type=F32) for p in pieces[b]) for b in bs]
    hs = [(on_v[b][0:chunk] * q_c[b] + num[b]) * on_v[b][chunk:2 * chunk] for b in bs]
    for b in bs:
        c_sc[b] = c_old[b] * on_v[b][3 * chunk:3 * chunk + 1] + jnp.where(
            cmask, _bdot_tn(k16[b], on_v[b][2 * chunk:3 * chunk] * v[b]), 0.0)
        n_sc[b] = n_old[b] * on_k[b][chunk:chunk + 1] + jnp.sum(on_k[b][0:chunk] * k[b], axis=0, keepdims=True)
        m_sc[b] = m_new[b]

    hh = jnp.concatenate(hs, axis=0)
    ms = _segsum(hh * hh, dv) * (1.0 / dv)
    out = hh * lax.rsqrt(ms + NORM_EPS) * on_ref[...] * _sigmoid(o_pre)
    o_ref[...] = out.astype(o_ref.dtype).reshape(o_ref.shape)


def _recurrent_kernel(x_ref, cw_ref, cb_ref, ib_ref, fb_ref, on_ref, fa_ref, fbk_ref, ft_ref, pc_ref, lnw_ref, lnb_ref,
                      om_ref, or_ref, prev_sc, c_sc, n_sc, m_sc, state_sc, *, layer, nb, chunk, rwkv_chunk, cps):
    @pl.when(pl.program_id(0) == 0)
    def _():
        for sc in (prev_sc, c_sc, n_sc, m_sc, state_sc):
            sc[...] = jnp.zeros(sc.shape, F32)

    scan = _rwkv_scan_stages(fa_ref, fbk_ref, ft_ref, pc_ref, lnw_ref, lnb_ref, or_ref, state_sc,
                             layer=layer, nb=nb, chunk=rwkv_chunk, cps=cps)
    mlstm = _mlstm_stages(x_ref, cw_ref, cb_ref, ib_ref, fb_ref, on_ref, om_ref, prev_sc, c_sc, n_sc, m_sc,
                          layer=layer, nb=nb, chunk=chunk)
    order = [scan, mlstm] + [scan, scan, mlstm] * (MLSTM_HEADS - 1) + [scan, mlstm]
    pending = {id(scan): 2 * cps + 1, id(mlstm): MLSTM_HEADS + 2}
    for g in order:
        next(g, None)
        pending[id(g)] -= 1
    for g in (scan, mlstm):
        for _ in range(pending[id(g)]):
            next(g, None)


def _recurrent(l, mlstm_in, fa, fb, ft, pc, batch, seq, cw, cb, ib, fbias, on, ln_w, ln_b):
    chunk = MLSTM_CHUNK
    rchunk = RWKV_CHUNK
    assert chunk % rchunk == 0
    w = RWKV_WIDTH
    x = mlstm_in.reshape(batch, seq, MLSTM_IN)
    fa = fa.reshape(batch, seq, 5 * w)
    fb = fb.reshape(batch, seq, 4 * w)
    pc = pc.reshape(batch, seq // rchunk, w)
    blk = lambda width: pl.BlockSpec((batch, chunk, width), lambda c: (0, c, 0))
    om, orw = pl.pallas_call(
        functools.partial(_recurrent_kernel, layer=l, nb=batch, chunk=chunk, rwkv_chunk=rchunk, cps=chunk // rchunk),
        out_shape=(jax.ShapeDtypeStruct((batch, seq, MLSTM_WIDTH), BF16),
                   jax.ShapeDtypeStruct((batch, seq, w), BF16)),
        grid=(seq // chunk,),
        in_specs=[blk(MLSTM_IN), _resident(cw, l), _resident(cb), _resident(ib), _resident(fbias), _resident(on),
                  blk(2 * w), blk(4 * w), blk(4 * w), _resident(pc), _resident(ln_w), _resident(ln_b)],
        out_specs=(blk(MLSTM_WIDTH), blk(w)),
        scratch_shapes=[pltpu.VMEM((batch * chunk, 2 * MLSTM_HEADS * MLSTM_QK), F32),
                        pltpu.VMEM((batch, MLSTM_HEADS * MLSTM_QK, MLSTM_WIDTH), F32),
                        pltpu.VMEM((batch, 1, MLSTM_HEADS * MLSTM_QK), F32),
                        pltpu.VMEM((batch, 1, LANES), F32),
                        pltpu.VMEM((batch, w, w), F32)],
        compiler_params=_cparams(("arbitrary",)),
        name="mlstm_rwkv_scan",
    )(x, cw, cb, ib, fbias, on, fa, fb, ft, pc, ln_w, ln_b)
    return om.reshape(batch * seq, MLSTM_WIDTH), orw.reshape(batch * seq, w)


def _ffn_kernel(x_ref, ya_ref, yb_ref, yc_ref, wo_ref, g_ref, wg_ref, wu_ref, wd_ref, fg_ref,
                o_ref, act_sc, *, layer, final_norm, tf):
    (g_ref,) = _rows(layer, g_ref)
    y = jnp.concatenate([ya_ref[...], yb_ref[...], yc_ref[...]], axis=-1)
    x1 = x_ref[...] + jnp.dot(y, wo_ref[...], preferred_element_type=F32)
    h = _rms(x1, g_ref[...]).astype(BF16)
    for c in range(D_FF // tf):
        gate = jnp.dot(h, wg_ref[:, c * tf:(c + 1) * tf], preferred_element_type=F32)
        up = jnp.dot(h, wu_ref[:, c * tf:(c + 1) * tf], preferred_element_type=F32)
        act_sc[:, c * tf:(c + 1) * tf] = (gate * _sigmoid(gate) * up).astype(BF16)
    out = x1 + jnp.dot(act_sc[...], wd_ref[...], preferred_element_type=F32)
    if final_norm:
        out = _rms(out, fg_ref[...])
    o_ref[...] = out


def _out_ffn(l, x, ya, yb, yc, wo, g, wg, wu, wd, fg, final_norm):
    t = x.shape[0]
    tm, tf = TM_FFN, TF_FFN
    row = lambda w: pl.BlockSpec((tm, w), lambda i: (i, 0))
    return pl.pallas_call(
        functools.partial(_ffn_kernel, layer=l, final_norm=final_norm, tf=tf),
        out_shape=jax.ShapeDtypeStruct((t, D_MODEL), F32),
        grid=(t // tm,),
        in_specs=[row(D_MODEL), row(MLA_WIDTH), row(RWKV_WIDTH), row(MLSTM_WIDTH), _resident(wo, l),
                  _resident(g), _resident(wg, l), _resident(wu, l), _resident(wd, l), _resident(fg)],
        out_specs=row(D_MODEL),
        scratch_shapes=[pltpu.VMEM((tm, D_FF), BF16)],
        compiler_params=_cparams(("parallel",)),
        name="out_ffn",
    )(x, ya, yb, yc, wo, g, wg, wu, wd, fg)


def _pad_cols(w, width):
    return jnp.pad(w, [(0, 0)] * (w.ndim - 1) + [(0, width - w.shape[-1])])


def _pad_rows(w, height):
    return jnp.pad(w, [(0, 0)] * (w.ndim - 2) + [(0, height - w.shape[-2]), (0, 0)])


def _rot_half_cols(w):
    half = w.shape[-1] // 2
    return jnp.concatenate([-w[..., half:], w[..., :half]], axis=-1)


def _stacked_weights(w_in, mla_w_uq, mla_w_ukv, rwkv_w2, rwkv_a2, rwkv_g2):
    depth = w_in.shape[0]
    wt = jnp.swapaxes(w_in, 1, 2)
    c_q, c_kv, k_pe = wt[:, 0:256], wt[:, 256:512], wt[:, 512:576]
    rw = wt[:, 576:1472]
    ml = wt[:, 1472:2248]
    k_pe_rot = jnp.swapaxes(_rot_half_cols(jnp.swapaxes(k_pe, 1, 2)), 1, 2)
    w_mla = jnp.concatenate([c_q, c_kv, _pad_rows(k_pe, LANES), _pad_rows(k_pe_rot, LANES)], axis=1)
    w_mlstm = jnp.concatenate([ml[:, 0:256], ml[:, 256:512], ml[:, 520:776],
                               _pad_rows(ml[:, 512:516], LANES), _pad_rows(ml[:, 516:520], LANES)], axis=1)
    w_all = jnp.concatenate([w_mla, rw, w_mlstm], axis=1).astype(BF16)

    uq = mla_w_uq.reshape(depth, MLA_Q_LORA, MLA_HEADS, MLA_NOPE + MLA_ROPE)
    nope = uq[..., :MLA_NOPE].reshape(depth, MLA_Q_LORA, MLA_HEADS * MLA_NOPE)
    pe = _pad_cols(uq[..., MLA_NOPE:], LANES).reshape(depth, MLA_Q_LORA, MLA_HEADS * LANES)
    per = _pad_cols(_rot_half_cols(uq[..., MLA_NOPE:]), LANES).reshape(depth, MLA_Q_LORA, MLA_HEADS * LANES)
    wq = jnp.concatenate([nope, pe, per], axis=-1).astype(BF16)
    ukv = mla_w_ukv.reshape(depth, MLA_KV_LORA, MLA_HEADS, MLA_NOPE + MLA_VDIM)
    wkv = jnp.concatenate([ukv[..., :MLA_NOPE].reshape(depth, MLA_KV_LORA, -1),
                           ukv[..., MLA_NOPE:].reshape(depth, MLA_KV_LORA, -1)], axis=-1).astype(BF16)

    rows = lambda before, wt: jnp.pad(wt, ((0, 0), (before, LANES - before - wt.shape[1]), (0, 0))).astype(BF16)
    w2p = rows(0, rwkv_w2)
    a2p = rows(RWKV_DECAY_LORA, rwkv_a2)
    g2p = rows(RWKV_DECAY_LORA + RWKV_AAA_LORA, rwkv_g2)
    return w_all, wq, wkv, w2p, a2p, g2p


def kernel(x, positions, mix_norm, w_in, mla_q_norm, mla_w_uq, mla_kv_norm, mla_w_ukv, mla_out_norm, rwkv_mu, rwkv_w0, rwkv_w2, rwkv_a0, rwkv_a2, rwkv_g2, rwkv_k_k, rwkv_k_a, rwkv_r_k, rwkv_ln_w, rwkv_ln_b, mlstm_conv_w, mlstm_conv_b, mlstm_i_bias, mlstm_f_bias, mlstm_out_norm, w_out, ffn_norm, w_gate, w_up, w_down, final_norm):
    batch, seq, _ = x.shape
    depth = w_in.shape[0]
    xt = x.reshape(batch * seq, D_MODEL)
    cos, sin, (wo, wg, wu, wd) = _rope_tables_and_casts(positions, (w_out, w_gate, w_up, w_down))
    w_all, wq, wkv, w2p, a2p, g2p = _stacked_weights(w_in, mla_w_uq, mla_w_ukv, rwkv_w2, rwkv_a2, rwkv_g2)
    ml_ib = _pad_cols(mlstm_i_bias, LANES)
    ml_fb = _pad_cols(mlstm_f_bias, LANES)
    for l in range(depth):
        q, k, v, fa, fb, pc, mlstm_in = _inproj(
            l, xt, seq, mix_norm, w_all, cos, sin, mla_q_norm, mla_kv_norm, wq, wkv,
            rwkv_mu, rwkv_w0, rwkv_a0, rwkv_k_k, rwkv_k_a, rwkv_r_k,
            w2p, a2p, g2p)
        y_mla = _mla_attention(l, q, k, v, mla_out_norm, batch, seq)
        ft = _rwkv_chunk(fa, batch, seq)
        y_mlstm, y_rwkv = _recurrent(l, mlstm_in, fa, fb, ft, pc, batch, seq, mlstm_conv_w, mlstm_conv_b, ml_ib, ml_fb,
                                     mlstm_out_norm, rwkv_ln_w, rwkv_ln_b)
        xt = _out_ffn(l, xt, y_mla, y_rwkv, y_mlstm, wo, ffn_norm, wg, wu, wd,
                      final_norm.reshape(1, -1), final_norm=(l == depth - 1))
    return xt.reshape(batch, seq, D_MODEL)
```

```python
import functools
import math

import jax
import jax.numpy as jnp
from jax import lax
from jax.experimental import pallas as pl
from jax.experimental.pallas import tpu as pltpu

F32 = jnp.float32
BF16 = jnp.bfloat16

D_MODEL = 1024
MLA_HEADS = 4
MLA_NOPE = 128
MLA_ROPE = 64
MLA_VDIM = 128
MLA_Q_LORA = 256
MLA_KV_LORA = 256
MLA_WIDTH = MLA_HEADS * MLA_VDIM
MLA_QK = 256
ROPE_THETA = 10000.0
RWKV_HEADS = 4
RWKV_HEAD = 64
RWKV_WIDTH = 256
RWKV_DECAY_LORA = 32
RWKV_AAA_LORA = 32
RWKV_GATE_LORA = 64
RWKV_IN = 3 * RWKV_WIDTH + 128
RWKV_LN_EPS = 64e-5
MLSTM_HEADS = 4
MLSTM_QK = 32
MLSTM_V = 64
MLSTM_WIDTH = 256
MLSTM_CONV = 4
MLSTM_IN = 1024
D_FF = 2816
NORM_EPS = 1e-6
LANES = 128
SUBLANES = 8

MLA_CQ = 0
MLA_CKV = MLA_CQ + MLA_Q_LORA
MLA_KPE = MLA_CKV + MLA_KV_LORA
MLA_KPER = MLA_KPE + LANES
MLA_IN = MLA_KPER + LANES

TM_INPROJ = 512
TQ_ATTN = 512
TK_ATTN = 512
HP_ATTN = 4
RWKV_CHUNK = 64
RWKV_PREP_CHUNKS = 2
MLSTM_CHUNK = 256
TM_FFN = 512
TF_FFN = 256
VMEM_LIMIT = 56 * 1024 * 1024


def _cparams(sem):
    return pltpu.CompilerParams(dimension_semantics=sem, vmem_limit_bytes=VMEM_LIMIT)


def _resident(a, layer=None):
    if layer is None:
        nd = a.ndim
        return pl.BlockSpec(a.shape, lambda *_: (0,) * nd, pipeline_mode=pl.Buffered(1))
    nd = a.ndim - 1
    return pl.BlockSpec((None,) + a.shape[1:], lambda *_: (layer,) + (0,) * nd, pipeline_mode=pl.Buffered(1))


def _rows(layer, *refs):
    return [r.at[layer:layer + 1] for r in refs]


def _bdot(a, b):
    return jnp.dot(a.astype(BF16), b.astype(BF16), preferred_element_type=F32)


def _dot_nt(a, b):
    return lax.dot_general(a, b, (((1,), (1,)), ((), ())), preferred_element_type=F32)


def _bdot_tn(a, b):
    return lax.dot_general(a.astype(BF16), b.astype(BF16), (((0,), (0,)), ((), ())),
                           preferred_element_type=F32)


def _split3(x):
    h = x.astype(BF16)
    r1 = x - h.astype(F32)
    m = r1.astype(BF16)
    lo = (r1 - m.astype(F32)).astype(BF16)
    return h, m, lo


def _exact_left_dot(sel, x):
    h, m, lo = _split3(x)
    s = sel.astype(BF16)
    return (jnp.dot(s, h, preferred_element_type=F32) + jnp.dot(s, m, preferred_element_type=F32)
            + jnp.dot(s, lo, preferred_element_type=F32))


def _exact_right_dot(x, sel, parts=3):
    pieces = _split3(x)[:parts]
    s = sel.astype(BF16)
    out = jnp.dot(pieces[0], s, preferred_element_type=F32)
    for p in pieces[1:]:
        out = out + jnp.dot(p, s, preferred_element_type=F32)
    return out


def _rms(x, g):
    return x * lax.rsqrt(jnp.mean(x * x, axis=-1, keepdims=True) + NORM_EPS) * g


def _sigmoid(x):
    return 1.0 / (1.0 + jnp.exp(-x))


def _log_sigmoid(x):
    return jnp.minimum(x, 0.0) - jnp.log1p(jnp.exp(-jnp.abs(x)))


def _div(x, d):
    assert d & (d - 1) == 0
    return lax.shift_right_logical(x, d.bit_length() - 1)


def _mod(x, d):
    assert d & (d - 1) == 0
    return lax.bitwise_and(x, d - 1)


def _shift_rows(x, prev, s, chunk):
    n = x.shape[0]
    row = lax.broadcasted_iota(jnp.int32, x.shape, 0)
    return jnp.where(_mod(row, chunk) >= s, pltpu.roll(x, s, 0), pltpu.roll(prev, n - chunk + s, 0))


def _rope_cast_kernel(pos_ref, invf_ref, *refs):
    n = (len(refs) - 2) // 2
    w_refs, cos_ref, sin_ref, o_refs = refs[:n], refs[n], refs[n + 1], refs[n + 2:]
    ang = pos_ref[...].astype(F32) * invf_ref[...]
    cos_ref[...] = jnp.cos(ang)
    sin_ref[...] = jnp.sin(ang)
    for w_ref, o_ref in zip(w_refs, o_refs):
        o_ref[...] = w_ref[...].astype(BF16)


def _rope_tables_and_casts(positions, weights):
    t = positions.size
    tm = min(1024, t)
    steps = t // tm
    inv_freq = ROPE_THETA ** (-jnp.arange(0, MLA_ROPE, 2, dtype=F32) / MLA_ROPE)
    invf = jnp.tile(inv_freq, LANES // (MLA_ROPE // 2))[None, :]
    flat = [w.reshape(-1, w.shape[-1]) for w in weights]
    rows = [f.shape[0] // steps for f in flat]
    assert all(f.shape[0] % steps == 0 and r % 16 == 0 for f, r in zip(flat, rows))
    slab = lambda r, c: pl.BlockSpec((r, c), lambda i: (i, 0))
    outs = pl.pallas_call(
        _rope_cast_kernel,
        out_shape=(jax.ShapeDtypeStruct((t, LANES), F32), jax.ShapeDtypeStruct((t, LANES), F32))
        + tuple(jax.ShapeDtypeStruct(f.shape, BF16) for f in flat),
        grid=(steps,),
        in_specs=[slab(tm, 1), pl.BlockSpec((1, LANES), lambda i: (0, 0))]
        + [slab(r, f.shape[1]) for f, r in zip(flat, rows)],
        out_specs=(slab(tm, LANES), slab(tm, LANES)) + tuple(slab(r, f.shape[1]) for f, r in zip(flat, rows)),
        compiler_params=_cparams(("parallel",)),
        name="rope_tables_casts",
    )(positions.reshape(t, 1), invf, *flat)
    return outs[0], outs[1], [o.reshape(w.shape) for o, w in zip(outs[2:], weights)]


(FA_RT, FA_V, FA_AT, FA_KT, FA_BT) = range(5)
(FB_KH, FB_BH, FB_BONUS, FB_GATE) = range(4)
(FT_TA, FT_UV, FT_YV, FT_ARB) = range(4)
RWKV_GROUP = 256


def _segsum(x, seg):
    assert 2 * seg == LANES and x.shape[1] % LANES == 0
    out = []
    for blk in range(x.shape[1] // LANES):
        xb = x[:, blk * LANES:(blk + 1) * LANES]
        low = lax.broadcasted_iota(jnp.int32, xb.shape, 1) < seg
        s_low = jnp.sum(jnp.where(low, xb, 0.0), axis=-1, keepdims=True)
        s_high = jnp.sum(jnp.where(low, 0.0, xb), axis=-1, keepdims=True)
        out.append(jnp.where(low, s_low, s_high))
    return jnp.concatenate(out, axis=-1)


def _rwkv_masks(chunk):
    w = RWKV_WIDTH
    r = lax.broadcasted_iota(jnp.int32, (w, w), 0)
    c = lax.broadcasted_iota(jnp.int32, (w, w), 1)
    return r, c, _div(r, chunk) == _div(c, chunk)


def _rwkv_features(xs, r0, prm, fa_ref, fb_ref, pc_ref, chunk):
    w0_ref, a0_ref, kk_ref, ka_ref, rk_ref, w2_ref, a2_ref, g2_ref = prm
    w = RWKV_WIDTH
    hd = RWKV_HEAD
    n = xs.shape[0]
    r = xs[:, 0:w]
    k = xs[:, w:2 * w]
    v = xs[:, 2 * w:3 * w]
    lor = xs[:, 3 * w:]
    ld = -math.exp(-0.5) * _sigmoid(w0_ref[...] + _bdot(jnp.tanh(lor), w2_ref[...]))
    a = _sigmoid(a0_ref[...] + _bdot(lor, a2_ref[...]))
    g = _bdot(_sigmoid(lor), g2_ref[...])
    kk = k * kk_ref[...]
    kk = kk / jnp.maximum(jnp.sqrt(_segsum(kk * kk, hd)), 1e-12)
    k2 = k * (1.0 + (a - 1.0) * ka_ref[...])
    kb = kk * a
    bonus = _segsum(r * k2 * rk_ref[...], hd) * v

    assert n <= RWKV_WIDTH
    ri, ci, same_chunk = _rwkv_masks(chunk)
    tri = jnp.where(same_chunk & (ci <= ri), 1.0, 0.0)[0:n, 0:n]
    cl = _exact_left_dot(tri, ld)
    units = n // chunk
    cl_last = jnp.concatenate(
        [jnp.broadcast_to(cl[(u + 1) * chunk - 1:(u + 1) * chunk, :], (chunk, w)) for u in range(units)], axis=0)
    e_neg = jnp.exp(-cl)
    e_end = jnp.exp(cl_last - cl)
    rows = slice(r0, r0 + n)

    def put(ref, sec, val):
        ref[rows, sec * w:(sec + 1) * w] = val.astype(ref.dtype)

    put(fa_ref, FA_RT, r * jnp.exp(cl))
    put(fa_ref, FA_V, v)
    put(fa_ref, FA_AT, kk * jnp.exp(cl - ld))
    put(fa_ref, FA_KT, k2 * e_neg)
    put(fa_ref, FA_BT, kb * e_neg)
    put(fb_ref, FB_KH, k2 * e_end)
    put(fb_ref, FB_BH, kb * e_end)
    put(fb_ref, FB_BONUS, bonus)
    put(fb_ref, FB_GATE, g)
    for u in range(units):
        c = r0 // chunk + u
        pc_ref[c:c + 1, :] = jnp.exp(cl_last[u * chunk:u * chunk + 1, :])


def _inproj_kernel(x_ref, g_ref, w_ref, cos_ref, sin_ref, qn_ref, kvn_ref, wq_ref, wkv_ref,
                   mu_ref, w0_ref, a0_ref, kk_ref, ka_ref, rk_ref, w2_ref, a2_ref, g2_ref,
                   q_ref, k_ref, v_ref, fa_ref, fb_ref, pc_ref, mlstm_ref, prev_sc, *, layer, chunk, tiles_per_seq):
    tm = x_ref.shape[0]
    g_ref, qn_ref, kvn_ref, mu_ref, w0_ref, a0_ref, kk_ref, ka_ref, rk_ref = _rows(
        layer, g_ref, qn_ref, kvn_ref, mu_ref, w0_ref, a0_ref, kk_ref, ka_ref, rk_ref)

    @pl.when(pl.program_id(0) % tiles_per_seq == 0)
    def _():
        prev_sc[...] = jnp.zeros(prev_sc.shape, F32)

    hb = _rms(x_ref[...], g_ref[...]).astype(BF16)
    rw = _dot_nt(hb, w_ref[MLA_IN:MLA_IN + RWKV_IN, :])
    row = lax.broadcasted_iota(jnp.int32, rw.shape, 0)
    shifted = jnp.where(row >= 1, pltpu.roll(rw, 1, 0), prev_sc[0:1, :])
    prev_sc[0:1, :] = rw[tm - 1:tm, :]
    xs = rw + (shifted - rw) * mu_ref[...]
    prm = (w0_ref, a0_ref, kk_ref, ka_ref, rk_ref, w2_ref, a2_ref, g2_ref)

    mla = _dot_nt(hb, w_ref[0:MLA_IN, :])
    groups = tm // RWKV_GROUP
    for gi in range(groups // 2):
        _rwkv_features(xs[gi * RWKV_GROUP:(gi + 1) * RWKV_GROUP], gi * RWKV_GROUP, prm, fa_ref, fb_ref, pc_ref, chunk)
    mlstm_ref[...] = _dot_nt(hb, w_ref[MLA_IN + RWKV_IN:, :])
    for gi in range(groups // 2, groups):
        _rwkv_features(xs[gi * RWKV_GROUP:(gi + 1) * RWKV_GROUP], gi * RWKV_GROUP, prm, fa_ref, fb_ref, pc_ref, chunk)

    cos = cos_ref[...]
    sin = sin_ref[...]
    scale = (MLA_NOPE + MLA_ROPE) ** -0.5 * math.log2(math.e)
    hw = MLA_HEADS * LANES
    cqn = _rms(mla[:, MLA_CQ:MLA_CKV], qn_ref[...]).astype(BF16)
    q = jnp.dot(cqn, wq_ref[...], preferred_element_type=F32)
    ckvn = _rms(mla[:, MLA_CKV:MLA_KPE], kvn_ref[...]).astype(BF16)
    kv = jnp.dot(ckvn, wkv_ref[...], preferred_element_type=F32)
    kp = (mla[:, MLA_KPE:MLA_KPER] * cos + mla[:, MLA_KPER:MLA_IN] * sin).astype(BF16)
    for h in range(MLA_HEADS):
        c0 = h * LANES
        pe = q[:, hw + c0:hw + c0 + LANES] * cos + q[:, 2 * hw + c0:2 * hw + c0 + LANES] * sin
        q_ref[:, h * MLA_QK:h * MLA_QK + LANES] = (q[:, c0:c0 + LANES] * scale).astype(BF16)
        q_ref[:, h * MLA_QK + LANES:(h + 1) * MLA_QK] = (pe * scale).astype(BF16)
        k_ref[:, h * MLA_QK:h * MLA_QK + LANES] = kv[:, c0:c0 + LANES].astype(BF16)
        k_ref[:, h * MLA_QK + LANES:(h + 1) * MLA_QK] = kp
    v_ref[...] = kv[:, hw:].astype(BF16)


def _inproj(l, x, seq, g, w, cos, sin, qn, kvn, wq, wkv, mu, w0, a0, k_k, k_a, r_k, w2p, a2p, g2p):
    t = x.shape[0]
    tm = TM_INPROJ
    chunk = RWKV_CHUNK
    assert seq % tm == 0 and tm % RWKV_GROUP == 0 and RWKV_GROUP % chunk == 0
    row = lambda width: pl.BlockSpec((tm, width), lambda i: (i, 0))
    rw = RWKV_WIDTH
    return pl.pallas_call(
        functools.partial(_inproj_kernel, layer=l, chunk=chunk, tiles_per_seq=seq // tm),
        out_shape=(jax.ShapeDtypeStruct((t, MLA_HEADS * MLA_QK), BF16),
                   jax.ShapeDtypeStruct((t, MLA_HEADS * MLA_QK), BF16),
                   jax.ShapeDtypeStruct((t, MLA_WIDTH), BF16),
                   jax.ShapeDtypeStruct((t, 5 * rw), BF16),
                   jax.ShapeDtypeStruct((t, 4 * rw), BF16),
                   jax.ShapeDtypeStruct((t // chunk, rw), F32),
                   jax.ShapeDtypeStruct((t, MLSTM_IN), F32)),
        grid=(t // tm,),
        in_specs=[row(D_MODEL), _resident(g), _resident(w, l), row(LANES), row(LANES), _resident(qn),
                  _resident(kvn), _resident(wq, l), _resident(wkv, l),
                  _resident(mu), _resident(w0), _resident(a0), _resident(k_k), _resident(k_a),
                  _resident(r_k), _resident(w2p, l), _resident(a2p, l), _resident(g2p, l)],
        out_specs=(row(MLA_HEADS * MLA_QK), row(MLA_HEADS * MLA_QK), row(MLA_WIDTH), row(5 * rw), row(4 * rw),
                   pl.BlockSpec((tm // chunk, rw), lambda i: (i, 0)), row(MLSTM_IN)),
        scratch_shapes=[pltpu.VMEM((SUBLANES, RWKV_IN), F32)],
        compiler_params=_cparams(("arbitrary",)),
        name="inproj",
    )(x, g, w, cos, sin, qn, kvn, wq, wkv, mu, w0, a0, k_k, k_a, r_k, w2p, a2p, g2p)


def _attn_kernel(q_ref, k_ref, v_ref, g_ref, o_ref, m_sc, acc_sc, sa_sc, sb_sc, *, layer, tq, tk, hp):
    i = pl.program_id(2)
    m_sc[...] = jnp.full(m_sc.shape, -jnp.inf, F32)
    acc_sc[...] = jnp.zeros(acc_sc.shape, F32)
    sub = m_sc.shape[1]
    hs = range(hp)
    ones = jnp.ones((acc_sc.shape[1] - MLA_VDIM, tk), BF16)
    (g_ref,) = _rows(layer, g_ref)

    def produce(j, s_ref):
        off = pl.multiple_of(j * tk, tk)
        for h in hs:
            s_ref[h] = lax.dot_general(k_ref[0, pl.ds(off, tk), h * MLA_QK:(h + 1) * MLA_QK],
                                       q_ref[0, :, h * MLA_QK:(h + 1) * MLA_QK],
                                       (((1,), (1,)), ((), ())), preferred_element_type=F32)

    def consume(j, s_ref, masked):
        off = pl.multiple_of(j * tk, tk)
        s = [s_ref[h] for h in hs]
        if masked:
            keys = lax.broadcasted_iota(jnp.int32, (tk, tq), 0)
            queries = lax.broadcasted_iota(jnp.int32, (tk, tq), 1)
            s = [jnp.where(keys <= queries, s[h], -jnp.inf) for h in hs]
        m_old = [m_sc[h][0:1] for h in hs]
        m_new = [jnp.maximum(m_old[h], jnp.max(s[h], axis=0, keepdims=True)) for h in hs]
        p = [jnp.exp2(s[h] - m_new[h]) for h in hs]
        alpha = [jnp.exp2(m_old[h] - m_new[h]) for h in hs]
        for h in hs:
            m_sc[h] = jnp.broadcast_to(m_new[h], (sub, tq))
        v1 = [jnp.concatenate([jnp.transpose(v_ref[0, pl.ds(off, tk), h * MLA_VDIM:(h + 1) * MLA_VDIM]), ones],
                              axis=0) for h in hs]
        pv = [jnp.dot(v1[h], p[h].astype(BF16), preferred_element_type=F32) for h in hs]
        for h in hs:
            acc_sc[h] = alpha[h] * acc_sc[h] + pv[h]

    def pair(jj, c):
        j = 2 * jj
        produce(j + 1, sb_sc)
        consume(j, sa_sc, False)
        produce(j + 2, sa_sc)
        consume(j + 1, sb_sc, False)
        return c

    produce(0, sa_sc)
    lax.fori_loop(0, i // 2, pair, 0)

    @pl.when(i % 2 == 0)
    def _():
        consume(i, sa_sc, True)

    @pl.when(i % 2 == 1)
    def _():
        produce(i, sb_sc)
        consume(i - 1, sa_sc, False)
        consume(i, sb_sc, True)

    for h in hs:
        o = acc_sc[h, 0:MLA_VDIM] / acc_sc[h, MLA_VDIM:MLA_VDIM + 1]
        o = o * lax.rsqrt(jnp.mean(o * o, axis=0, keepdims=True) + NORM_EPS)
        o_ref[0, :, h * MLA_VDIM:(h + 1) * MLA_VDIM] = (
            jnp.transpose(o) * g_ref[:, h * MLA_VDIM:(h + 1) * MLA_VDIM]).astype(o_ref.dtype)


def _mla_attention(l, q, k, v, g, batch, seq):
    tq, tk, hp = TQ_ATTN, TK_ATTN, HP_ATTN
    assert tq == tk and MLA_VDIM == LANES and hp == MLA_HEADS
    q = q.reshape(batch, seq, MLA_HEADS * MLA_QK)
    k = k.reshape(batch, seq, MLA_HEADS * MLA_QK)
    v = v.reshape(batch, seq, MLA_WIDTH)
    out = pl.pallas_call(
        functools.partial(_attn_kernel, layer=l, tq=tq, tk=tk, hp=hp),
        out_shape=jax.ShapeDtypeStruct((batch, seq, MLA_WIDTH), BF16),
        grid=(batch, MLA_HEADS // hp, seq // tq),
        in_specs=[pl.BlockSpec((1, tq, hp * MLA_QK), lambda b, h, i: (b, i, h)),
                  pl.BlockSpec((1, seq, hp * MLA_QK), lambda b, h, i: (b, 0, h)),
                  pl.BlockSpec((1, seq, hp * MLA_VDIM), lambda b, h, i: (b, 0, h)),
                  _resident(g)],
        out_specs=pl.BlockSpec((1, tq, hp * MLA_VDIM), lambda b, h, i: (b, i, h)),
        scratch_shapes=[pltpu.VMEM((hp, SUBLANES, tq), F32),
                        pltpu.VMEM((hp, MLA_VDIM + 2 * SUBLANES, tq), F32),
                        pltpu.VMEM((hp, tk, tq), F32), pltpu.VMEM((hp, tk, tq), F32)],
        compiler_params=_cparams(("parallel", "parallel", "arbitrary")),
        name="mla_attention",
    )(q, k, v, g)
    return out.reshape(batch * seq, MLA_WIDTH)


def _tile_heads(z):
    return jnp.concatenate([z] * RWKV_HEADS, axis=0)


def _rwkv_chunk_kernel(fa_ref, ft_ref, *, nb, chunk, cps):
    seg = cps * chunk
    n = nb * seg
    w = RWKV_WIDTH
    hd = RWKV_HEAD
    x = fa_ref[...].reshape(n, fa_ref.shape[-1])
    r16, v16, a16, k16, b16 = (x[:, s * w:(s + 1) * w] for s in (FA_RT, FA_V, FA_AT, FA_KT, FA_BT))

    ri, ci, same_chunk = _rwkv_masks(chunk)
    bd = _div(ri, chunk) == _div(ci, hd)
    rt = lax.broadcasted_iota(jnp.int32, (chunk, w), 0)
    cs = _mod(lax.broadcasted_iota(jnp.int32, (chunk, w), 1), chunk)
    strict = cs < rt
    incl = cs <= rt
    c16 = _div(rt, 16) == _div(cs, 16)
    c32 = _div(rt, 32) == _div(cs, 32)
    eye = jnp.where(rt == cs, 1.0, 0.0)
    units = nb * cps

    def block_diag(z):
        return jnp.where(same_chunk, _tile_heads(z), 0.0)

    def put(u, sec, val):
        b, j = divmod(u, cps)
        ft_ref[b, j * chunk:(j + 1) * chunk, sec * w:(sec + 1) * w] = val.astype(ft_ref.dtype)

    def mm(x, y):
        return jnp.dot(x, y, preferred_element_type=F32)

    us = range(units)
    sls = [slice(u * chunk, (u + 1) * chunk) for u in us]
    a_st = [jnp.where(bd, _tile_heads(a16[sl]), 0.0) for sl in sls]
    v_st = [jnp.where(bd, _tile_heads(v16[sl]), 0.0) for sl in sls]
    kb_t = [jnp.concatenate([jnp.transpose(jnp.where(bd, _tile_heads(k16[sl]), 0.0)),
                             jnp.transpose(jnp.where(bd, _tile_heads(b16[sl]), 0.0))], axis=1) for sl in sls]
    sc = [mm(jnp.concatenate([a16[sls[u]], r16[sls[u]]], axis=0), kb_t[u]) for u in us]
    l_ab = [jnp.where(strict, sc[u][0:chunk, w:], 0.0) for u in us]
    l_ak = [jnp.where(strict, sc[u][0:chunk, 0:w], 0.0).astype(BF16) for u in us]
    a_rk = [jnp.where(incl, sc[u][chunk:, 0:w], 0.0).astype(BF16) for u in us]
    for u in us:
        put(u, FT_ARB, jnp.where(incl, sc[u][chunk:, w:], 0.0))
    xm = [-jnp.where(c16, l_ab[u], 0.0) for u in us]
    xm16 = [z.astype(BF16) for z in xm]
    off32 = [block_diag(jnp.where(c32 & jnp.logical_not(c16), l_ab[u], 0.0).astype(BF16)) for u in us]
    off64 = [block_diag(jnp.where(jnp.logical_not(c32), l_ab[u], 0.0).astype(BF16)) for u in us]
    x2 = [mm(xm16[u], block_diag(xm16[u])).astype(BF16) for u in us]
    x2_bd = [block_diag(z) for z in x2]
    lv = [mm(jnp.concatenate([l_ak[u], a_rk[u]], axis=0), v_st[u]) for u in us]
    wv = [block_diag(lv[u][0:chunk].astype(BF16)) for u in us]
    t_lo = [eye + xm[u] for u in us]
    tx = [mm(jnp.concatenate([t_lo[u].astype(BF16), x2[u]], axis=0), x2_bd[u]) for u in us]
    t_lo = [(t_lo[u] + tx[u][0:chunk]).astype(BF16) for u in us]
    x4 = [tx[u][chunk:] for u in us]
    x4b = [z.astype(BF16) for z in x4]
    x4_bd = [block_diag(z) for z in x4b]
    for u in us:
        put(u, FT_YV, lv[u][chunk:])
    x8_bd = [block_diag(mm(x4b[u], x4_bd[u]).astype(BF16)) for u in us]
    t_hi = [eye + x4[u] for u in us]
    t_hi = [block_diag((t_hi[u] + mm(t_hi[u].astype(BF16), x8_bd[u])).astype(BF16)) for u in us]
    t_inv = [mm(t_lo[u], t_hi[u]) for u in us]
    for off in (off32, off64):
        tb = [z.astype(BF16) for z in t_inv]
        mid = [mm(tb[u], off[u]).astype(BF16) for u in us]
        t_inv = [t_inv[u] - mm(mid[u], block_diag(tb[u])) for u in us]
    tb = [z.astype(BF16) for z in t_inv]
    for u in us:
        put(u, FT_TA, mm(tb[u], a_st[u]))
    for u in us:
        put(u, FT_UV, mm(tb[u], wv[u]))


def _rwkv_scan_stages(fa_ref, fb_ref, ft_ref, pc_ref, lnw_ref, lnb_ref, o_ref, state_sc, *, layer, nb, chunk, cps):
    seg = cps * chunk
    n = nb * seg
    w = RWKV_WIDTH
    hd = RWKV_HEAD
    lnw_ref, lnb_ref = _rows(layer, lnw_ref, lnb_ref)
    ri, ci, _ = _rwkv_masks(chunk)
    bd = _div(ri, chunk) == _div(ci, hd)
    bdv = _div(ri, hd) == _div(ci, hd)

    def sec(ref, b, j, s):
        return ref[b, j * chunk:(j + 1) * chunk, s * w:(s + 1) * w]

    ys = [[None] * cps for _ in range(nb)]
    bs = range(nb)
    for j in range(cps):
        gs = [state_sc[b] for b in bs]
        p1 = [lax.dot_general(jnp.concatenate([sec(ft_ref, b, j, FT_TA), sec(fa_ref, b, j, FA_RT)], axis=0),
                              gs[b].astype(BF16), (((1,), (1,)), ((), ())), preferred_element_type=F32)
              for b in bs]
        yield
        u = [(p1[b][0:chunk] + sec(ft_ref, b, j, FT_UV).astype(F32)).astype(BF16) for b in bs]
        upd = [lax.dot_general(jnp.concatenate([sec(fa_ref, b, j, FA_V), -u[b]], axis=0),
                               jnp.concatenate([sec(fb_ref, b, j, FB_KH), sec(fb_ref, b, j, FB_BH)], axis=0),
                               (((0,), (0,)), ((), ())), preferred_element_type=F32) for b in bs]
        for b in bs:
            state_sc[b] = gs[b] * pc_ref[b, pl.ds(pl.program_id(0) * cps + j, 1), :] + jnp.where(bdv, upd[b], 0.0)
        for b in bs:
            u_st = jnp.where(bd, _tile_heads(u[b]), 0.0)
            ys[b][j] = (p1[b][chunk:] + sec(ft_ref, b, j, FT_YV).astype(F32)
                        - jnp.dot(sec(ft_ref, b, j, FT_ARB), u_st, preferred_element_type=F32))
        yield

    y = jnp.concatenate([ys[b][j] for b in range(nb) for j in range(cps)], axis=0)
    mean = _segsum(y, hd) * (1.0 / hd)
    d = y - mean
    var = _segsum(d * d, hd) * (1.0 / hd)
    yn = d * lax.rsqrt(var + RWKV_LN_EPS) * lnw_ref[...] + lnb_ref[...]
    bonus = fb_ref[:, :, FB_BONUS * w:(FB_BONUS + 1) * w].reshape(n, w).astype(F32)
    gate = fb_ref[:, :, FB_GATE * w:(FB_GATE + 1) * w].reshape(n, w).astype(F32)
    o_ref[...] = ((yn + bonus) * gate).astype(o_ref.dtype).reshape(o_ref.shape)


def _rwkv_chunk(fa, batch, seq):
    chunk = RWKV_CHUNK
    w = RWKV_WIDTH
    assert RWKV_HEADS * chunk == w
    cps = RWKV_PREP_CHUNKS
    seg = cps * chunk
    return pl.pallas_call(
        functools.partial(_rwkv_chunk_kernel, nb=batch, chunk=chunk, cps=cps),
        out_shape=jax.ShapeDtypeStruct((batch, seq, 4 * w), BF16),
        grid=(seq // seg,),
        in_specs=[pl.BlockSpec((batch, seg, 5 * w), lambda c: (0, c, 0))],
        out_specs=pl.BlockSpec((batch, seg, 4 * w), lambda c: (0, c, 0)),
        compiler_params=_cparams(("parallel",)),
        name="rwkv7_chunk",
    )(fa.reshape(batch, seq, 5 * w))


ML_QK = 0
ML_V = 2 * MLSTM_HEADS * MLSTM_QK
ML_O = ML_V + MLSTM_WIDTH
ML_I = ML_O + MLSTM_WIDTH
ML_F = ML_I + LANES


def _cummax_rows(x):
    n = x.shape[0]
    row = lax.broadcasted_iota(jnp.int32, x.shape, 0)
    sh = 1
    while sh < n:
        x = jnp.maximum(x, jnp.where(row >= sh, pltpu.roll(x, sh, 0), -jnp.inf))
        sh *= 2
    return x


def _mlstm_stages(x_ref, cw_ref, cb_ref, ib_ref, fb_ref, on_ref, o_ref,
                  prev_sc, c_sc, n_sc, m_sc, *, layer, nb, chunk):
    n = nb * chunk
    nh = MLSTM_HEADS
    dk = MLSTM_QK
    dv = MLSTM_V
    qkw = nh * dk
    vw = MLSTM_WIDTH
    cb_ref, ib_ref, fb_ref, on_ref = _rows(layer, cb_ref, ib_ref, fb_ref, on_ref)
    x = x_ref[...].reshape(n, MLSTM_IN)
    qk_raw = x[:, ML_QK:ML_V]
    prev = prev_sc[...]
    conv = cb_ref[...] + qk_raw * cw_ref[MLSTM_CONV - 1:MLSTM_CONV, :]
    for s in range(1, MLSTM_CONV):
        conv = conv + _shift_rows(qk_raw, prev, s, chunk) * cw_ref[MLSTM_CONV - 1 - s:MLSTM_CONV - s, :]
    prev_sc[...] = qk_raw
    qk = conv * _sigmoid(conv)
    q_all = qk[:, 0:qkw] * (dk ** -0.5)
    k_all = qk[:, qkw:]
    v_all = x[:, ML_V:ML_O]
    o_pre = x[:, ML_O:ML_I]
    li_all = x[:, ML_I:ML_F] + ib_ref[...]
    lf_all = _log_sigmoid(x[:, ML_F:ML_F + LANES] + fb_ref[...])

    ri = lax.broadcasted_iota(jnp.int32, (chunk, chunk), 0)
    ci = lax.broadcasted_iota(jnp.int32, (chunk, chunk), 1)
    causal = ci <= ri
    tri = jnp.where(causal, 1.0, 0.0)
    lane_k = lax.broadcasted_iota(jnp.int32, (chunk, qkw), 1)
    lane_v = lax.broadcasted_iota(jnp.int32, (chunk, vw), 1)
    rc = lax.broadcasted_iota(jnp.int32, (qkw, vw), 0)
    cc = lax.broadcasted_iota(jnp.int32, (qkw, vw), 1)
    cmask = _div(rc, dk) == _div(cc, dv)
    expand_v = jnp.where(rc == _div(cc, dv), 1.0, 0.0).astype(BF16)
    rk = lax.broadcasted_iota(jnp.int32, (qkw, qkw), 0)
    ck = lax.broadcasted_iota(jnp.int32, (qkw, qkw), 1)
    expand_k = jnp.where(rk == _div(ck, dk), 1.0, 0.0).astype(BF16)
    gather_k = jnp.where(_div(rk, dk) == ck, 1.0, 0.0).astype(BF16)

    bs = range(nb)
    sls = [slice(b * chunk, (b + 1) * chunk) for b in bs]
    q = [q_all[sl] for sl in sls]
    k = [k_all[sl] for sl in sls]
    k16 = [z.astype(BF16) for z in k]
    v = [v_all[sl] for sl in sls]
    li = [li_all[sl] for sl in sls]
    c_old = [c_sc[b] for b in bs]
    n_old = [n_sc[b] for b in bs]
    m_prev = [m_sc[b] for b in bs]
    g = [_exact_left_dot(tri, lf_all[sl]) for sl in sls]
    lig = [li[b] - g[b] for b in bs]
    inter_log = [g[b] + m_prev[b] for b in bs]
    m_t = [jnp.maximum(inter_log[b], g[b] + _cummax_rows(lig[b])) for b in bs]
    inter_w = [jnp.exp(inter_log[b] - m_t[b]) for b in bs]
    log2e = math.log2(math.e)
    gm = [(g[b] - m_t[b]) * log2e for b in bs]
    lig_t = [jnp.transpose(z * log2e) for z in lig]
    qn = [_exact_right_dot(q[b] * n_old[b], gather_k, parts=2) for b in bs]
    q_c = [_bdot(q[b], c_old[b]) for b in bs]
    ssum = [jnp.zeros((chunk, LANES), F32) for _ in bs]
    num = [jnp.zeros((chunk, vw), F32) for _ in bs]
    yield
    for h in range(nh):
        mk = (lane_k >= h * dk) & (lane_k < (h + 1) * dk)
        mv = (lane_v >= h * dv) & (lane_v < (h + 1) * dv)
        qk_h = [lax.dot_general(jnp.where(mk, q[b], 0.0).astype(BF16), k16[b], (((1,), (1,)), ((), ())),
                                preferred_element_type=F32) for b in bs]
        d = [jnp.broadcast_to(gm[b][:, h:h + 1], (chunk, chunk)) + lig_t[b][h:h + 1, :] for b in bs]
        s = [qk_h[b] * jnp.exp2(jnp.where(causal, d[b], -jnp.inf)) for b in bs]
        ssum = [jnp.where(lane_k == h, jnp.sum(s[b], axis=-1, keepdims=True), ssum[b]) for b in bs]
        num = [num[b] + _bdot(s[b], jnp.where(mv, v[b], 0.0)) for b in bs]
        yield
    den = [inter_w[b] * qn[b] + ssum[b] for b in bs]
    rden = [1.0 / jnp.maximum(jnp.abs(den[b]), jnp.exp(-m_t[b])) for b in bs]
    g_last = [g[b][chunk - 1:chunk, :] for b in bs]
    a_all = [g_last[b] - g[b] + li[b] for b in bs]
    m_new = [jnp.maximum(g_last[b] + m_prev[b], jnp.max(a_all[b], axis=0, keepdims=True)) for b in bs]
    dec = [jnp.exp(g_last[b] + m_prev[b] - m_new[b]) for b in bs]
    wts = [jnp.exp(a_all[b] - m_new[b]) for b in bs]
    per_head = [jnp.concatenate([inter_w[b], rden[b], wts[b], jnp.broadcast_to(dec[b], (8, LANES))], axis=0)
                for b in bs]
    pieces = [_split3(per_head[b])[:2] for b in bs]
    on_v = [sum(jnp.dot(p, expand_v, preferred_element_type=F32) for p in pieces[b]) for b in bs]
    on_k = [sum(jnp.dot(p[2 * chunk:], expand_k, preferred_element_type=F32) for p in pieces[b]) for b in bs]
    hs = [(on_v[b][0:chunk] * q_c[b] + num[b]) * on_v[b][chunk:2 * chunk] for b in bs]
    for b in bs:
        c_sc[b] = c_old[b] * on_v[b][3 * chunk:3 * chunk + 1] + jnp.where(
            cmask, _bdot_tn(k16[b], on_v[b][2 * chunk:3 * chunk] * v[b]), 0.0)
        n_sc[b] = n_old[b] * on_k[b][chunk:chunk + 1] + jnp.sum(on_k[b][0:chunk] * k[b], axis=0, keepdims=True)
        m_sc[b] = m_new[b]

    hh = jnp.concatenate(hs, axis=0)
    ms = _segsum(hh * hh, dv) * (1.0 / dv)
    out = hh * lax.rsqrt(ms + NORM_EPS) * on_ref[...] * _sigmoid(o_pre)
    o_ref[...] = out.astype(o_ref.dtype).reshape(o_ref.shape)


def _recurrent_kernel(x_ref, cw_ref, cb_ref, ib_ref, fb_ref, on_ref, fa_ref, fbk_ref, ft_ref, pc_ref, lnw_ref, lnb_ref,
                      om_ref, or_ref, prev_sc, c_sc, n_sc, m_sc, state_sc, *, layer, nb, chunk, rwkv_chunk, cps):
    @pl.when(pl.program_id(0) == 0)
    def _():
        for sc in (prev_sc, c_sc, n_sc, m_sc, state_sc):
            sc[...] = jnp.zeros(sc.shape, F32)

    scan = _rwkv_scan_stages(fa_ref, fbk_ref, ft_ref, pc_ref, lnw_ref, lnb_ref, or_ref, state_sc,
                             layer=layer, nb=nb, chunk=rwkv_chunk, cps=cps)
    mlstm = _mlstm_stages(x_ref, cw_ref, cb_ref, ib_ref, fb_ref, on_ref, om_ref, prev_sc, c_sc, n_sc, m_sc,
                          layer=layer, nb=nb, chunk=chunk)
    order = [scan, mlstm] + [scan, scan, mlstm] * (MLSTM_HEADS - 1) + [scan, mlstm]
    pending = {id(scan): 2 * cps + 1, id(mlstm): MLSTM_HEADS + 2}
    for g in order:
        next(g, None)
        pending[id(g)] -= 1
    for g in (scan, mlstm):
        for _ in range(pending[id(g)]):
            next(g, None)


def _recurrent(l, mlstm_in, fa, fb, ft, pc, batch, seq, cw, cb, ib, fbias, on, ln_w, ln_b):
    chunk = MLSTM_CHUNK
    rchunk = RWKV_CHUNK
    assert chunk % rchunk == 0
    w = RWKV_WIDTH
    x = mlstm_in.reshape(batch, seq, MLSTM_IN)
    fa = fa.reshape(batch, seq, 5 * w)
    fb = fb.reshape(batch, seq, 4 * w)
    pc = pc.reshape(batch, seq // rchunk, w)
    blk = lambda width: pl.BlockSpec((batch, chunk, width), lambda c: (0, c, 0))
    om, orw = pl.pallas_call(
        functools.partial(_recurrent_kernel, layer=l, nb=batch, chunk=chunk, rwkv_chunk=rchunk, cps=chunk // rchunk),
        out_shape=(jax.ShapeDtypeStruct((batch, seq, MLSTM_WIDTH), BF16),
                   jax.ShapeDtypeStruct((batch, seq, w), BF16)),
        grid=(seq // chunk,),
        in_specs=[blk(MLSTM_IN), _resident(cw, l), _resident(cb), _resident(ib), _resident(fbias), _resident(on),
                  blk(2 * w), blk(4 * w), blk(4 * w), _resident(pc), _resident(ln_w), _resident(ln_b)],
        out_specs=(blk(MLSTM_WIDTH), blk(w)),
        scratch_shapes=[pltpu.VMEM((batch * chunk, 2 * MLSTM_HEADS * MLSTM_QK), F32),
                        pltpu.VMEM((batch, MLSTM_HEADS * MLSTM_QK, MLSTM_WIDTH), F32),
                        pltpu.VMEM((batch, 1, MLSTM_HEADS * MLSTM_QK), F32),
                        pltpu.VMEM((batch, 1, LANES), F32),
                        pltpu.VMEM((batch, w, w), F32)],
        compiler_params=_cparams(("arbitrary",)),
        name="mlstm_rwkv_scan",
    )(x, cw, cb, ib, fbias, on, fa, fb, ft, pc, ln_w, ln_b)
    return om.reshape(batch * seq, MLSTM_WIDTH), orw.reshape(batch * seq, w)


def _ffn_kernel(x_ref, ya_ref, yb_ref, yc_ref, wo_ref, g_ref, wg_ref, wu_ref, wd_ref, fg_ref,
                o_ref, act_sc, *, layer, final_norm, tf):
    (g_ref,) = _rows(layer, g_ref)
    y = jnp.concatenate([ya_ref[...], yb_ref[...], yc_ref[...]], axis=-1)
    x1 = x_ref[...] + jnp.dot(y, wo_ref[...], preferred_element_type=F32)
    h = _rms(x1, g_ref[...]).astype(BF16)
    for c in range(D_FF // tf):
        gate = jnp.dot(h, wg_ref[:, c * tf:(c + 1) * tf], preferred_element_type=F32)
        up = jnp.dot(h, wu_ref[:, c * tf:(c + 1) * tf], preferred_element_type=F32)
        act_sc[:, c * tf:(c + 1) * tf] = (gate * _sigmoid(gate) * up).astype(BF16)
    out = x1 + jnp.dot(act_sc[...], wd_ref[...], preferred_element_type=F32)
    if final_norm:
        out = _rms(out, fg_ref[...])
    o_ref[...] = out


def _out_ffn(l, x, ya, yb, yc, wo, g, wg, wu, wd, fg, final_norm):
    t = x.shape[0]
    tm, tf = TM_FFN, TF_FFN
    row = lambda w: pl.BlockSpec((tm, w), lambda i: (i, 0))
    return pl.pallas_call(
        functools.partial(_ffn_kernel, layer=l, final_norm=final_norm, tf=tf),
        out_shape=jax.ShapeDtypeStruct((t, D_MODEL), F32),
        grid=(t // tm,),
        in_specs=[row(D_MODEL), row(MLA_WIDTH), row(RWKV_WIDTH), row(MLSTM_WIDTH), _resident(wo, l),
                  _resident(g), _resident(wg, l), _resident(wu, l), _resident(wd, l), _resident(fg)],
        out_specs=row(D_MODEL),
        scratch_shapes=[pltpu.VMEM((tm, D_FF), BF16)],
        compiler_params=_cparams(("parallel",)),
        name="out_ffn",
    )(x, ya, yb, yc, wo, g, wg, wu, wd, fg)


def _pad_cols(w, width):
    return jnp.pad(w, [(0, 0)] * (w.ndim - 1) + [(0, width - w.shape[-1])])


def _pad_rows(w, height):
    return jnp.pad(w, [(0, 0)] * (w.ndim - 2) + [(0, height - w.shape[-2]), (0, 0)])


def _rot_half_cols(w):
    half = w.shape[-1] // 2
    return jnp.concatenate([-w[..., half:], w[..., :half]], axis=-1)


def _stacked_weights(w_in, mla_w_uq, mla_w_ukv, rwkv_w2, rwkv_a2, rwkv_g2):
    depth = w_in.shape[0]
    wt = jnp.swapaxes(w_in, 1, 2)
    c_q, c_kv, k_pe = wt[:, 0:256], wt[:, 256:512], wt[:, 512:576]
    rw = wt[:, 576:1472]
    ml = wt[:, 1472:2248]
    k_pe_rot = jnp.swapaxes(_rot_half_cols(jnp.swapaxes(k_pe, 1, 2)), 1, 2)
    w_mla = jnp.concatenate([c_q, c_kv, _pad_rows(k_pe, LANES), _pad_rows(k_pe_rot, LANES)], axis=1)
    w_mlstm = jnp.concatenate([ml[:, 0:256], ml[:, 256:512], ml[:, 520:776],
                               _pad_rows(ml[:, 512:516], LANES), _pad_rows(ml[:, 516:520], LANES)], axis=1)
    w_all = jnp.concatenate([w_mla, rw, w_mlstm], axis=1).astype(BF16)

    uq = mla_w_uq.reshape(depth, MLA_Q_LORA, MLA_HEADS, MLA_NOPE + MLA_ROPE)
    nope = uq[..., :MLA_NOPE].reshape(depth, MLA_Q_LORA, MLA_HEADS * MLA_NOPE)
    pe = _pad_cols(uq[..., MLA_NOPE:], LANES).reshape(depth, MLA_Q_LORA, MLA_HEADS * LANES)
    per = _pad_cols(_rot_half_cols(uq[..., MLA_NOPE:]), LANES).reshape(depth, MLA_Q_LORA, MLA_HEADS * LANES)
    wq = jnp.concatenate([nope, pe, per], axis=-1).astype(BF16)
    ukv = mla_w_ukv.reshape(depth, MLA_KV_LORA, MLA_HEADS, MLA_NOPE + MLA_VDIM)
    wkv = jnp.concatenate([ukv[..., :MLA_NOPE].reshape(depth, MLA_KV_LORA, -1),
                           ukv[..., MLA_NOPE:].reshape(depth, MLA_KV_LORA, -1)], axis=-1).astype(BF16)

    assert RWKV_DECAY_LORA + RWKV_AAA_LORA + RWKV_GATE_LORA == LANES
    rows = lambda before, wt: jnp.pad(wt, ((0, 0), (before, LANES - before - wt.shape[1]), (0, 0))).astype(BF16)
    w2p = rows(0, rwkv_w2)
    a2p = rows(RWKV_DECAY_LORA, rwkv_a2)
    g2p = rows(RWKV_DECAY_LORA + RWKV_AAA_LORA, rwkv_g2)
    return w_all, wq, wkv, w2p, a2p, g2p


def kernel(x, positions, mix_norm, w_in, mla_q_norm, mla_w_uq, mla_kv_norm, mla_w_ukv, mla_out_norm, rwkv_mu, rwkv_w0, rwkv_w2, rwkv_a0, rwkv_a2, rwkv_g2, rwkv_k_k, rwkv_k_a, rwkv_r_k, rwkv_ln_w, rwkv_ln_b, mlstm_conv_w, mlstm_conv_b, mlstm_i_bias, mlstm_f_bias, mlstm_out_norm, w_out, ffn_norm, w_gate, w_up, w_down, final_norm):
    batch, seq, _ = x.shape
    depth = w_in.shape[0]
    xt = x.reshape(batch * seq, D_MODEL)
    cos, sin, (wo, wg, wu, wd) = _rope_tables_and_casts(positions, (w_out, w_gate, w_up, w_down))
    w_all, wq, wkv, w2p, a2p, g2p = _stacked_weights(w_in, mla_w_uq, mla_w_ukv, rwkv_w2, rwkv_a2, rwkv_g2)
    ml_ib = _pad_cols(mlstm_i_bias, LANES)
    ml_fb = _pad_cols(mlstm_f_bias, LANES)
    for l in range(depth):
        q, k, v, fa, fb, pc, mlstm_in = _inproj(
            l, xt, seq, mix_norm, w_all, cos, sin, mla_q_norm, mla_kv_norm, wq, wkv,
            rwkv_mu, rwkv_w0, rwkv_a0, rwkv_k_k, rwkv_k_a, rwkv_r_k,
            w2p, a2p, g2p)
        y_mla = _mla_attention(l, q, k, v, mla_out_norm, batch, seq)
        ft = _rwkv_chunk(fa, batch, seq)
        y_mlstm, y_rwkv = _recurrent(l, mlstm_in, fa, fb, ft, pc, batch, seq, mlstm_conv_w, mlstm_conv_b, ml_ib, ml_fb,
                                     mlstm_out_norm, rwkv_ln_w, rwkv_ln_b)
        xt = _out_ffn(l, xt, y_mla, y_rwkv, y_mlstm, wo, ffn_norm, wg, wu, wd,
                      final_norm.reshape(1, -1), final_norm=(l == depth - 1))
    return xt.reshape(batch, seq, D_MODEL)
```

```python
import functools
import math

import jax
import jax.numpy as jnp
from jax import lax
from jax.experimental import pallas as pl
from jax.experimental.pallas import tpu as pltpu

F32 = jnp.float32
BF16 = jnp.bfloat16

D_MODEL = 1024
MLA_HEADS = 4
MLA_NOPE = 128
MLA_ROPE = 64
MLA_VDIM = 128
MLA_Q_LORA = 256
MLA_KV_LORA = 256
MLA_WIDTH = MLA_HEADS * MLA_VDIM
MLA_QK = 256
ROPE_THETA = 10000.0
RWKV_HEADS = 4
RWKV_HEAD = 64
RWKV_WIDTH = 256
RWKV_DECAY_LORA = 32
RWKV_AAA_LORA = 32
RWKV_GATE_LORA = 64
RWKV_IN = 3 * RWKV_WIDTH + 128
RWKV_LN_EPS = 64e-5
MLSTM_HEADS = 4
MLSTM_QK = 32
MLSTM_V = 64
MLSTM_WIDTH = 256
MLSTM_CONV = 4
MLSTM_IN = 1024
D_FF = 2816
NORM_EPS = 1e-6
LANES = 128
SUBLANES = 8

MLA_CQ = 0
MLA_CKV = MLA_CQ + MLA_Q_LORA
MLA_KPE = MLA_CKV + MLA_KV_LORA
MLA_KPER = MLA_KPE + LANES
MLA_IN = MLA_KPER + LANES

TM_INPROJ = 1024
TQ_ATTN = 512
TK_ATTN = 512
HP_ATTN = 4
RWKV_CHUNK = 64
RWKV_PREP_CHUNKS = 2
MLSTM_CHUNK = 256
TM_FFN = 1024
TF_FFN = 256
VMEM_LIMIT = 56 * 1024 * 1024


def _cparams(sem):
    return pltpu.CompilerParams(dimension_semantics=sem, vmem_limit_bytes=VMEM_LIMIT)


def _resident(a, layer=None):
    if layer is None:
        nd = a.ndim
        return pl.BlockSpec(a.shape, lambda *_: (0,) * nd, pipeline_mode=pl.Buffered(1))
    nd = a.ndim - 1
    return pl.BlockSpec((None,) + a.shape[1:], lambda *_: (layer,) + (0,) * nd, pipeline_mode=pl.Buffered(1))


def _rows(layer, *refs):
    return [r.at[layer:layer + 1] for r in refs]


def _bdot(a, b):
    return jnp.dot(a.astype(BF16), b.astype(BF16), preferred_element_type=F32)


def _dot_nt(a, b):
    return lax.dot_general(a, b, (((1,), (1,)), ((), ())), preferred_element_type=F32)


def _bdot_tn(a, b):
    return lax.dot_general(a.astype(BF16), b.astype(BF16), (((0,), (0,)), ((), ())),
                           preferred_element_type=F32)


def _split3(x):
    h = x.astype(BF16)
    r1 = x - h.astype(F32)
    m = r1.astype(BF16)
    lo = (r1 - m.astype(F32)).astype(BF16)
    return h, m, lo


def _exact_left_dot(sel, x):
    h, m, lo = _split3(x)
    s = sel.astype(BF16)
    return (jnp.dot(s, h, preferred_element_type=F32) + jnp.dot(s, m, preferred_element_type=F32)
            + jnp.dot(s, lo, preferred_element_type=F32))


def _exact_right_dot(x, sel, parts=3):
    pieces = _split3(x)[:parts]
    s = sel.astype(BF16)
    out = jnp.dot(pieces[0], s, preferred_element_type=F32)
    for p in pieces[1:]:
        out = out + jnp.dot(p, s, preferred_element_type=F32)
    return out


def _rms(x, g):
    return x * lax.rsqrt(jnp.mean(x * x, axis=-1, keepdims=True) + NORM_EPS) * g


def _sigmoid(x):
    return 1.0 / (1.0 + jnp.exp(-x))


def _log_sigmoid(x):
    return jnp.minimum(x, 0.0) - jnp.log1p(jnp.exp(-jnp.abs(x)))


def _div(x, d):
    assert d & (d - 1) == 0
    return lax.shift_right_logical(x, d.bit_length() - 1)


def _mod(x, d):
    assert d & (d - 1) == 0
    return lax.bitwise_and(x, d - 1)


def _shift_rows(x, prev, s, chunk):
    n = x.shape[0]
    row = lax.broadcasted_iota(jnp.int32, x.shape, 0)
    return jnp.where(_mod(row, chunk) >= s, pltpu.roll(x, s, 0), pltpu.roll(prev, n - chunk + s, 0))


def _rope_cast_kernel(pos_ref, invf_ref, *refs):
    n = (len(refs) - 2) // 2
    w_refs, cos_ref, sin_ref, o_refs = refs[:n], refs[n], refs[n + 1], refs[n + 2:]
    ang = pos_ref[...].astype(F32) * invf_ref[...]
    cos_ref[...] = jnp.cos(ang)
    sin_ref[...] = jnp.sin(ang)
    for w_ref, o_ref in zip(w_refs, o_refs):
        o_ref[...] = w_ref[...].astype(BF16)


def _rope_tables_and_casts(positions, weights):
    t = positions.size
    tm = min(1024, t)
    steps = t // tm
    inv_freq = ROPE_THETA ** (-jnp.arange(0, MLA_ROPE, 2, dtype=F32) / MLA_ROPE)
    invf = jnp.tile(inv_freq, LANES // (MLA_ROPE // 2))[None, :]
    flat = [w.reshape(-1, w.shape[-1]) for w in weights]
    rows = [f.shape[0] // steps for f in flat]
    assert all(f.shape[0] % steps == 0 and r % 16 == 0 for f, r in zip(flat, rows))
    slab = lambda r, c: pl.BlockSpec((r, c), lambda i: (i, 0))
    outs = pl.pallas_call(
        _rope_cast_kernel,
        out_shape=(jax.ShapeDtypeStruct((t, LANES), F32), jax.ShapeDtypeStruct((t, LANES), F32))
        + tuple(jax.ShapeDtypeStruct(f.shape, BF16) for f in flat),
        grid=(steps,),
        in_specs=[slab(tm, 1), pl.BlockSpec((1, LANES), lambda i: (0, 0))]
        + [slab(r, f.shape[1]) for f, r in zip(flat, rows)],
        out_specs=(slab(tm, LANES), slab(tm, LANES)) + tuple(slab(r, f.shape[1]) for f, r in zip(flat, rows)),
        compiler_params=_cparams(("parallel",)),
        name="rope_tables_casts",
    )(positions.reshape(t, 1), invf, *flat)
    return outs[0], outs[1], [o.reshape(w.shape) for o, w in zip(outs[2:], weights)]


(FA_RT, FA_V, FA_AT, FA_KT, FA_BT) = range(5)
(FB_KH, FB_BH, FB_BONUS, FB_GATE) = range(4)
(FT_TA, FT_UV, FT_YV, FT_ARB) = range(4)
RWKV_GROUP = 256


def _segsum(x, seg):
    assert 2 * seg == LANES and x.shape[1] % LANES == 0
    out = []
    for blk in range(x.shape[1] // LANES):
        xb = x[:, blk * LANES:(blk + 1) * LANES]
        low = lax.broadcasted_iota(jnp.int32, xb.shape, 1) < seg
        s_low = jnp.sum(jnp.where(low, xb, 0.0), axis=-1, keepdims=True)
        s_high = jnp.sum(jnp.where(low, 0.0, xb), axis=-1, keepdims=True)
        out.append(jnp.where(low, s_low, s_high))
    return jnp.concatenate(out, axis=-1)


def _rwkv_masks(chunk):
    w = RWKV_WIDTH
    r = lax.broadcasted_iota(jnp.int32, (w, w), 0)
    c = lax.broadcasted_iota(jnp.int32, (w, w), 1)
    return r, c, _div(r, chunk) == _div(c, chunk)


def _rwkv_features(xs, r0, prm, fa_ref, fb_ref, pc_ref, chunk):
    w0_ref, a0_ref, kk_ref, ka_ref, rk_ref, w2_ref, a2_ref, g2_ref = prm
    w = RWKV_WIDTH
    hd = RWKV_HEAD
    n = xs.shape[0]
    r = xs[:, 0:w]
    k = xs[:, w:2 * w]
    v = xs[:, 2 * w:3 * w]
    lor = xs[:, 3 * w:]
    ld = -math.exp(-0.5) * _sigmoid(w0_ref[...] + _bdot(jnp.tanh(lor), w2_ref[...]))
    a = _sigmoid(a0_ref[...] + _bdot(lor, a2_ref[...]))
    g = _bdot(_sigmoid(lor), g2_ref[...])
    kk = k * kk_ref[...]
    kk = kk / jnp.maximum(jnp.sqrt(_segsum(kk * kk, hd)), 1e-12)
    k2 = k * (1.0 + (a - 1.0) * ka_ref[...])
    kb = kk * a
    bonus = _segsum(r * k2 * rk_ref[...], hd) * v

    assert n <= RWKV_WIDTH
    ri, ci, same_chunk = _rwkv_masks(chunk)
    tri = jnp.where(same_chunk & (ci <= ri), 1.0, 0.0)[0:n, 0:n]
    cl = _exact_left_dot(tri, ld)
    units = n // chunk
    cl_last = jnp.concatenate(
        [jnp.broadcast_to(cl[(u + 1) * chunk - 1:(u + 1) * chunk, :], (chunk, w)) for u in range(units)], axis=0)
    e_neg = jnp.exp(-cl)
    e_end = jnp.exp(cl_last - cl)
    rows = slice(r0, r0 + n)

    def put(ref, sec, val):
        ref[rows, sec * w:(sec + 1) * w] = val.astype(ref.dtype)

    put(fa_ref, FA_RT, r * jnp.exp(cl))
    put(fa_ref, FA_V, v)
    put(fa_ref, FA_AT, kk * jnp.exp(cl - ld))
    put(fa_ref, FA_KT, k2 * e_neg)
    put(fa_ref, FA_BT, kb * e_neg)
    put(fb_ref, FB_KH, k2 * e_end)
    put(fb_ref, FB_BH, kb * e_end)
    put(fb_ref, FB_BONUS, bonus)
    put(fb_ref, FB_GATE, g)
    for u in range(units):
        c = r0 // chunk + u
        pc_ref[c:c + 1, :] = jnp.exp(cl_last[u * chunk:u * chunk + 1, :])


def _inproj_kernel(x_ref, g_ref, w_ref, cos_ref, sin_ref, qn_ref, kvn_ref, wq_ref, wkv_ref,
                   mu_ref, w0_ref, a0_ref, kk_ref, ka_ref, rk_ref, w2_ref, a2_ref, g2_ref,
                   q_ref, k_ref, v_ref, fa_ref, fb_ref, pc_ref, mlstm_ref, prev_sc, *, layer, chunk, tiles_per_seq):
    tm = x_ref.shape[0]
    g_ref, qn_ref, kvn_ref, mu_ref, w0_ref, a0_ref, kk_ref, ka_ref, rk_ref = _rows(
        layer, g_ref, qn_ref, kvn_ref, mu_ref, w0_ref, a0_ref, kk_ref, ka_ref, rk_ref)

    @pl.when(pl.program_id(0) % tiles_per_seq == 0)
    def _():
        prev_sc[...] = jnp.zeros(prev_sc.shape, F32)

    hb = _rms(x_ref[...], g_ref[...]).astype(BF16)
    rw = _dot_nt(hb, w_ref[MLA_IN:MLA_IN + RWKV_IN, :])
    row = lax.broadcasted_iota(jnp.int32, rw.shape, 0)
    shifted = jnp.where(row >= 1, pltpu.roll(rw, 1, 0), prev_sc[0:1, :])
    prev_sc[0:1, :] = rw[tm - 1:tm, :]
    xs = rw + (shifted - rw) * mu_ref[...]
    prm = (w0_ref, a0_ref, kk_ref, ka_ref, rk_ref, w2_ref, a2_ref, g2_ref)

    mla = _dot_nt(hb, w_ref[0:MLA_IN, :])
    groups = tm // RWKV_GROUP
    for gi in range(groups // 2):
        _rwkv_features(xs[gi * RWKV_GROUP:(gi + 1) * RWKV_GROUP], gi * RWKV_GROUP, prm, fa_ref, fb_ref, pc_ref, chunk)
    mlstm_ref[...] = _dot_nt(hb, w_ref[MLA_IN + RWKV_IN:, :])
    for gi in range(groups // 2, groups):
        _rwkv_features(xs[gi * RWKV_GROUP:(gi + 1) * RWKV_GROUP], gi * RWKV_GROUP, prm, fa_ref, fb_ref, pc_ref, chunk)

    cos = cos_ref[...]
    sin = sin_ref[...]
    scale = (MLA_NOPE + MLA_ROPE) ** -0.5 * math.log2(math.e)
    hw = MLA_HEADS * LANES
    cqn = _rms(mla[:, MLA_CQ:MLA_CKV], qn_ref[...]).astype(BF16)
    q = jnp.dot(cqn, wq_ref[...], preferred_element_type=F32)
    ckvn = _rms(mla[:, MLA_CKV:MLA_KPE], kvn_ref[...]).astype(BF16)
    kv = jnp.dot(ckvn, wkv_ref[...], preferred_element_type=F32)
    kp = (mla[:, MLA_KPE:MLA_KPER] * cos + mla[:, MLA_KPER:MLA_IN] * sin).astype(BF16)
    for h in range(MLA_HEADS):
        c0 = h * LANES
        pe = q[:, hw + c0:hw + c0 + LANES] * cos + q[:, 2 * hw + c0:2 * hw + c0 + LANES] * sin
        q_ref[:, h * MLA_QK:h * MLA_QK + LANES] = (q[:, c0:c0 + LANES] * scale).astype(BF16)
        q_ref[:, h * MLA_QK + LANES:(h + 1) * MLA_QK] = (pe * scale).astype(BF16)
        k_ref[:, h * MLA_QK:h * MLA_QK + LANES] = kv[:, c0:c0 + LANES].astype(BF16)
        k_ref[:, h * MLA_QK + LANES:(h + 1) * MLA_QK] = kp
    v_ref[...] = kv[:, hw:].astype(BF16)


def _inproj(l, x, seq, g, w, cos, sin, qn, kvn, wq, wkv, mu, w0, a0, k_k, k_a, r_k, w2p, a2p, g2p):
    t = x.shape[0]
    tm = TM_INPROJ
    chunk = RWKV_CHUNK
    assert seq % tm == 0 and tm % RWKV_GROUP == 0 and RWKV_GROUP % chunk == 0
    row = lambda width: pl.BlockSpec((tm, width), lambda i: (i, 0))
    rw = RWKV_WIDTH
    return pl.pallas_call(
        functools.partial(_inproj_kernel, layer=l, chunk=chunk, tiles_per_seq=seq // tm),
        out_shape=(jax.ShapeDtypeStruct((t, MLA_HEADS * MLA_QK), BF16),
                   jax.ShapeDtypeStruct((t, MLA_HEADS * MLA_QK), BF16),
                   jax.ShapeDtypeStruct((t, MLA_WIDTH), BF16),
                   jax.ShapeDtypeStruct((t, 5 * rw), BF16),
                   jax.ShapeDtypeStruct((t, 4 * rw), BF16),
                   jax.ShapeDtypeStruct((t // chunk, rw), F32),
                   jax.ShapeDtypeStruct((t, MLSTM_IN), F32)),
        grid=(t // tm,),
        in_specs=[row(D_MODEL), _resident(g), _resident(w, l), row(LANES), row(LANES), _resident(qn),
                  _resident(kvn), _resident(wq, l), _resident(wkv, l),
                  _resident(mu), _resident(w0), _resident(a0), _resident(k_k), _resident(k_a),
                  _resident(r_k), _resident(w2p, l), _resident(a2p, l), _resident(g2p, l)],
        out_specs=(row(MLA_HEADS * MLA_QK), row(MLA_HEADS * MLA_QK), row(MLA_WIDTH), row(5 * rw), row(4 * rw),
                   pl.BlockSpec((tm // chunk, rw), lambda i: (i, 0)), row(MLSTM_IN)),
        scratch_shapes=[pltpu.VMEM((SUBLANES, RWKV_IN), F32)],
        compiler_params=_cparams(("arbitrary",)),
        name="inproj",
    )(x, g, w, cos, sin, qn, kvn, wq, wkv, mu, w0, a0, k_k, k_a, r_k, w2p, a2p, g2p)


def _attn_kernel(q_ref, k_ref, v_ref, g_ref, o_ref, m_sc, acc_sc, sa_sc, sb_sc, *, layer, tq, tk, hp):
    i = pl.program_id(2)
    m_sc[...] = jnp.full(m_sc.shape, -jnp.inf, F32)
    acc_sc[...] = jnp.zeros(acc_sc.shape, F32)
    sub = m_sc.shape[1]
    hs = range(hp)
    ones = jnp.ones((acc_sc.shape[1] - MLA_VDIM, tk), BF16)
    (g_ref,) = _rows(layer, g_ref)

    def produce(j, s_ref):
        off = pl.multiple_of(j * tk, tk)
        for h in hs:
            s_ref[h] = lax.dot_general(k_ref[0, pl.ds(off, tk), h * MLA_QK:(h + 1) * MLA_QK],
                                       q_ref[0, :, h * MLA_QK:(h + 1) * MLA_QK],
                                       (((1,), (1,)), ((), ())), preferred_element_type=F32)

    def consume(j, s_ref, masked):
        off = pl.multiple_of(j * tk, tk)
        s = [s_ref[h] for h in hs]
        if masked:
            keys = lax.broadcasted_iota(jnp.int32, (tk, tq), 0)
            queries = lax.broadcasted_iota(jnp.int32, (tk, tq), 1)
            s = [jnp.where(keys <= queries, s[h], -jnp.inf) for h in hs]
        m_old = [m_sc[h][0:1] for h in hs]
        m_new = [jnp.maximum(m_old[h], jnp.max(s[h], axis=0, keepdims=True)) for h in hs]
        p = [jnp.exp2(s[h] - m_new[h]) for h in hs]
        alpha = [jnp.exp2(m_old[h] - m_new[h]) for h in hs]
        for h in hs:
            m_sc[h] = jnp.broadcast_to(m_new[h], (sub, tq))
        v1 = [jnp.concatenate([jnp.transpose(v_ref[0, pl.ds(off, tk), h * MLA_VDIM:(h + 1) * MLA_VDIM]), ones],
                              axis=0) for h in hs]
        pv = [jnp.dot(v1[h], p[h].astype(BF16), preferred_element_type=F32) for h in hs]
        for h in hs:
            acc_sc[h] = alpha[h] * acc_sc[h] + pv[h]

    def pair(jj, c):
        j = 2 * jj
        produce(j + 1, sb_sc)
        consume(j, sa_sc, False)
        produce(j + 2, sa_sc)
        consume(j + 1, sb_sc, False)
        return c

    produce(0, sa_sc)
    lax.fori_loop(0, i // 2, pair, 0)

    @pl.when(i % 2 == 0)
    def _():
        consume(i, sa_sc, True)

    @pl.when(i % 2 == 1)
    def _():
        produce(i, sb_sc)
        consume(i - 1, sa_sc, False)
        consume(i, sb_sc, True)

    for h in hs:
        o = acc_sc[h, 0:MLA_VDIM] / acc_sc[h, MLA_VDIM:MLA_VDIM + 1]
        o = o * lax.rsqrt(jnp.mean(o * o, axis=0, keepdims=True) + NORM_EPS)
        o_ref[0, :, h * MLA_VDIM:(h + 1) * MLA_VDIM] = (
            jnp.transpose(o) * g_ref[:, h * MLA_VDIM:(h + 1) * MLA_VDIM]).astype(o_ref.dtype)


def _mla_attention(l, q, k, v, g, batch, seq):
    tq, tk, hp = TQ_ATTN, TK_ATTN, HP_ATTN
    assert tq == tk and MLA_VDIM == LANES and hp == MLA_HEADS
    q = q.reshape(batch, seq, MLA_HEADS * MLA_QK)
    k = k.reshape(batch, seq, MLA_HEADS * MLA_QK)
    v = v.reshape(batch, seq, MLA_WIDTH)
    out = pl.pallas_call(
        functools.partial(_attn_kernel, layer=l, tq=tq, tk=tk, hp=hp),
        out_shape=jax.ShapeDtypeStruct((batch, seq, MLA_WIDTH), BF16),
        grid=(batch, MLA_HEADS // hp, seq // tq),
        in_specs=[pl.BlockSpec((1, tq, hp * MLA_QK), lambda b, h, i: (b, i, h)),
                  pl.BlockSpec((1, seq, hp * MLA_QK), lambda b, h, i: (b, 0, h)),
                  pl.BlockSpec((1, seq, hp * MLA_VDIM), lambda b, h, i: (b, 0, h)),
                  _resident(g)],
        out_specs=pl.BlockSpec((1, tq, hp * MLA_VDIM), lambda b, h, i: (b, i, h)),
        scratch_shapes=[pltpu.VMEM((hp, SUBLANES, tq), F32),
                        pltpu.VMEM((hp, MLA_VDIM + 2 * SUBLANES, tq), F32),
                        pltpu.VMEM((hp, tk, tq), F32), pltpu.VMEM((hp, tk, tq), F32)],
        compiler_params=_cparams(("parallel", "parallel", "arbitrary")),
        name="mla_attention",
    )(q, k, v, g)
    return out.reshape(batch * seq, MLA_WIDTH)


def _tile_heads(z):
    return jnp.concatenate([z] * RWKV_HEADS, axis=0)


def _rwkv_chunk_kernel(fa_ref, ft_ref, *, nb, chunk, cps):
    seg = cps * chunk
    n = nb * seg
    w = RWKV_WIDTH
    hd = RWKV_HEAD
    x = fa_ref[...].reshape(n, fa_ref.shape[-1])
    r16, v16, a16, k16, b16 = (x[:, s * w:(s + 1) * w] for s in (FA_RT, FA_V, FA_AT, FA_KT, FA_BT))

    ri, ci, same_chunk = _rwkv_masks(chunk)
    bd = _div(ri, chunk) == _div(ci, hd)
    rt = lax.broadcasted_iota(jnp.int32, (chunk, w), 0)
    cs = _mod(lax.broadcasted_iota(jnp.int32, (chunk, w), 1), chunk)
    strict = cs < rt
    incl = cs <= rt
    c16 = _div(rt, 16) == _div(cs, 16)
    c32 = _div(rt, 32) == _div(cs, 32)
    eye = jnp.where(rt == cs, 1.0, 0.0)
    units = nb * cps

    def block_diag(z):
        return jnp.where(same_chunk, _tile_heads(z), 0.0)

    def put(u, sec, val):
        b, j = divmod(u, cps)
        ft_ref[b, j * chunk:(j + 1) * chunk, sec * w:(sec + 1) * w] = val.astype(ft_ref.dtype)

    def mm(x, y):
        return jnp.dot(x, y, preferred_element_type=F32)

    us = range(units)
    sls = [slice(u * chunk, (u + 1) * chunk) for u in us]
    a_st = [jnp.where(bd, _tile_heads(a16[sl]), 0.0) for sl in sls]
    v_st = [jnp.where(bd, _tile_heads(v16[sl]), 0.0) for sl in sls]
    kb_t = [jnp.concatenate([jnp.transpose(jnp.where(bd, _tile_heads(k16[sl]), 0.0)),
                             jnp.transpose(jnp.where(bd, _tile_heads(b16[sl]), 0.0))], axis=1) for sl in sls]
    sc = [mm(jnp.concatenate([a16[sls[u]], r16[sls[u]]], axis=0), kb_t[u]) for u in us]
    l_ab = [jnp.where(strict, sc[u][0:chunk, w:], 0.0) for u in us]
    l_ak = [jnp.where(strict, sc[u][0:chunk, 0:w], 0.0).astype(BF16) for u in us]
    a_rk = [jnp.where(incl, sc[u][chunk:, 0:w], 0.0).astype(BF16) for u in us]
    for u in us:
        put(u, FT_ARB, jnp.where(incl, sc[u][chunk:, w:], 0.0))
    xm = [-jnp.where(c16, l_ab[u], 0.0) for u in us]
    xm16 = [z.astype(BF16) for z in xm]
    off32 = [block_diag(jnp.where(c32 & jnp.logical_not(c16), l_ab[u], 0.0).astype(BF16)) for u in us]
    off64 = [block_diag(jnp.where(jnp.logical_not(c32), l_ab[u], 0.0).astype(BF16)) for u in us]
    x2 = [mm(xm16[u], block_diag(xm16[u])).astype(BF16) for u in us]
    x2_bd = [block_diag(z) for z in x2]
    lv = [mm(jnp.concatenate([l_ak[u], a_rk[u]], axis=0), v_st[u]) for u in us]
    wv = [block_diag(lv[u][0:chunk].astype(BF16)) for u in us]
    t_lo = [eye + xm[u] for u in us]
    tx = [mm(jnp.concatenate([t_lo[u].astype(BF16), x2[u]], axis=0), x2_bd[u]) for u in us]
    t_lo = [(t_lo[u] + tx[u][0:chunk]).astype(BF16) for u in us]
    x4 = [tx[u][chunk:] for u in us]
    x4b = [z.astype(BF16) for z in x4]
    x4_bd = [block_diag(z) for z in x4b]
    for u in us:
        put(u, FT_YV, lv[u][chunk:])
    x8_bd = [block_diag(mm(x4b[u], x4_bd[u]).astype(BF16)) for u in us]
    t_hi = [eye + x4[u] for u in us]
    t_hi = [block_diag((t_hi[u] + mm(t_hi[u].astype(BF16), x8_bd[u])).astype(BF16)) for u in us]
    t_inv = [mm(t_lo[u], t_hi[u]) for u in us]
    for off in (off32, off64):
        tb = [z.astype(BF16) for z in t_inv]
        mid = [mm(tb[u], off[u]).astype(BF16) for u in us]
        t_inv = [t_inv[u] - mm(mid[u], block_diag(tb[u])) for u in us]
    tb = [z.astype(BF16) for z in t_inv]
    for u in us:
        put(u, FT_TA, mm(tb[u], a_st[u]))
    for u in us:
        put(u, FT_UV, mm(tb[u], wv[u]))


def _rwkv_scan_stages(fa_ref, fb_ref, ft_ref, pc_ref, lnw_ref, lnb_ref, o_ref, state_sc, *, layer, nb, chunk, cps):
    seg = cps * chunk
    n = nb * seg
    w = RWKV_WIDTH
    hd = RWKV_HEAD
    lnw_ref, lnb_ref = _rows(layer, lnw_ref, lnb_ref)
    ri, ci, _ = _rwkv_masks(chunk)
    bd = _div(ri, chunk) == _div(ci, hd)
    bdv = _div(ri, hd) == _div(ci, hd)

    def sec(ref, b, j, s):
        return ref[b, j * chunk:(j + 1) * chunk, s * w:(s + 1) * w]

    ys = [[None] * cps for _ in range(nb)]
    bs = range(nb)
    for j in range(cps):
        gs = [state_sc[b] for b in bs]
        p1 = [lax.dot_general(jnp.concatenate([sec(ft_ref, b, j, FT_TA), sec(fa_ref, b, j, FA_RT)], axis=0),
                              gs[b].astype(BF16), (((1,), (1,)), ((), ())), preferred_element_type=F32)
              for b in bs]
        yield
        u = [(p1[b][0:chunk] + sec(ft_ref, b, j, FT_UV).astype(F32)).astype(BF16) for b in bs]
        upd = [lax.dot_general(jnp.concatenate([sec(fa_ref, b, j, FA_V), -u[b]], axis=0),
                               jnp.concatenate([sec(fb_ref, b, j, FB_KH), sec(fb_ref, b, j, FB_BH)], axis=0),
                               (((0,), (0,)), ((), ())), preferred_element_type=F32) for b in bs]
        for b in bs:
            state_sc[b] = gs[b] * pc_ref[b, pl.ds(pl.program_id(0) * cps + j, 1), :] + jnp.where(bdv, upd[b], 0.0)
        for b in bs:
            u_st = jnp.where(bd, _tile_heads(u[b]), 0.0)
            ys[b][j] = (p1[b][chunk:] + sec(ft_ref, b, j, FT_YV).astype(F32)
                        - jnp.dot(sec(ft_ref, b, j, FT_ARB), u_st, preferred_element_type=F32))
        yield

    y = jnp.concatenate([ys[b][j] for b in range(nb) for j in range(cps)], axis=0)
    mean = _segsum(y, hd) * (1.0 / hd)
    d = y - mean
    var = _segsum(d * d, hd) * (1.0 / hd)
    yn = d * lax.rsqrt(var + RWKV_LN_EPS) * lnw_ref[...] + lnb_ref[...]
    bonus = fb_ref[:, :, FB_BONUS * w:(FB_BONUS + 1) * w].reshape(n, w).astype(F32)
    gate = fb_ref[:, :, FB_GATE * w:(FB_GATE + 1) * w].reshape(n, w).astype(F32)
    o_ref[...] = ((yn + bonus) * gate).astype(o_ref.dtype).reshape(o_ref.shape)


def _rwkv_chunk(fa, batch, seq):
    chunk = RWKV_CHUNK
    w = RWKV_WIDTH
    assert RWKV_HEADS * chunk == w
    cps = RWKV_PREP_CHUNKS
    seg = cps * chunk
    return pl.pallas_call(
        functools.partial(_rwkv_chunk_kernel, nb=batch, chunk=chunk, cps=cps),
        out_shape=jax.ShapeDtypeStruct((batch, seq, 4 * w), BF16),
        grid=(seq // seg,),
        in_specs=[pl.BlockSpec((batch, seg, 5 * w), lambda c: (0, c, 0))],
        out_specs=pl.BlockSpec((batch, seg, 4 * w), lambda c: (0, c, 0)),
        compiler_params=_cparams(("parallel",)),
        name="rwkv7_chunk",
    )(fa.reshape(batch, seq, 5 * w))


ML_QK = 0
ML_V = 2 * MLSTM_HEADS * MLSTM_QK
ML_O = ML_V + MLSTM_WIDTH
ML_I = ML_O + MLSTM_WIDTH
ML_F = ML_I + LANES


def _cummax_rows(x):
    n = x.shape[0]
    row = lax.broadcasted_iota(jnp.int32, x.shape, 0)
    sh = 1
    while sh < n:
        x = jnp.maximum(x, jnp.where(row >= sh, pltpu.roll(x, sh, 0), -jnp.inf))
        sh *= 2
    return x


def _mlstm_stages(x_ref, cw_ref, cb_ref, ib_ref, fb_ref, on_ref, o_ref,
                  prev_sc, c_sc, n_sc, m_sc, *, layer, nb, chunk):
    n = nb * chunk
    nh = MLSTM_HEADS
    dk = MLSTM_QK
    dv = MLSTM_V
    qkw = nh * dk
    vw = MLSTM_WIDTH
    cb_ref, ib_ref, fb_ref, on_ref = _rows(layer, cb_ref, ib_ref, fb_ref, on_ref)
    x = x_ref[...].reshape(n, MLSTM_IN)
    qk_raw = x[:, ML_QK:ML_V]
    prev = prev_sc[...]
    conv = cb_ref[...] + qk_raw * cw_ref[MLSTM_CONV - 1:MLSTM_CONV, :]
    for s in range(1, MLSTM_CONV):
        conv = conv + _shift_rows(qk_raw, prev, s, chunk) * cw_ref[MLSTM_CONV - 1 - s:MLSTM_CONV - s, :]
    prev_sc[...] = qk_raw
    qk = conv * _sigmoid(conv)
    q_all = qk[:, 0:qkw] * (dk ** -0.5)
    k_all = qk[:, qkw:]
    v_all = x[:, ML_V:ML_O]
    o_pre = x[:, ML_O:ML_I]
    li_all = x[:, ML_I:ML_F] + ib_ref[...]
    lf_all = _log_sigmoid(x[:, ML_F:ML_F + LANES] + fb_ref[...])

    ri = lax.broadcasted_iota(jnp.int32, (chunk, chunk), 0)
    ci = lax.broadcasted_iota(jnp.int32, (chunk, chunk), 1)
    causal = ci <= ri
    tri = jnp.where(causal, 1.0, 0.0)
    lane_k = lax.broadcasted_iota(jnp.int32, (chunk, qkw), 1)
    lane_v = lax.broadcasted_iota(jnp.int32, (chunk, vw), 1)
    rc = lax.broadcasted_iota(jnp.int32, (qkw, vw), 0)
    cc = lax.broadcasted_iota(jnp.int32, (qkw, vw), 1)
    cmask = _div(rc, dk) == _div(cc, dv)
    expand_v = jnp.where(rc == _div(cc, dv), 1.0, 0.0).astype(BF16)
    rk = lax.broadcasted_iota(jnp.int32, (qkw, qkw), 0)
    ck = lax.broadcasted_iota(jnp.int32, (qkw, qkw), 1)
    expand_k = jnp.where(rk == _div(ck, dk), 1.0, 0.0).astype(BF16)
    gather_k = jnp.where(_div(rk, dk) == ck, 1.0, 0.0).astype(BF16)

    bs = range(nb)
    sls = [slice(b * chunk, (b + 1) * chunk) for b in bs]
    q = [q_all[sl] for sl in sls]
    k = [k_all[sl] for sl in sls]
    k16 = [z.astype(BF16) for z in k]
    v = [v_all[sl] for sl in sls]
    li = [li_all[sl] for sl in sls]
    c_old = [c_sc[b] for b in bs]
    n_old = [n_sc[b] for b in bs]
    m_prev = [m_sc[b] for b in bs]
    g = [_exact_left_dot(tri, lf_all[sl]) for sl in sls]
    lig = [li[b] - g[b] for b in bs]
    inter_log = [g[b] + m_prev[b] for b in bs]
    m_t = [jnp.maximum(inter_log[b], g[b] + _cummax_rows(lig[b])) for b in bs]
    inter_w = [jnp.exp(inter_log[b] - m_t[b]) for b in bs]
    log2e = math.log2(math.e)
    gm = [(g[b] - m_t[b]) * log2e for b in bs]
    lig_t = [jnp.transpose(z * log2e) for z in lig]
    qn = [_exact_right_dot(q[b] * n_old[b], gather_k, parts=2) for b in bs]
    q_c = [_bdot(q[b], c_old[b]) for b in bs]
    ssum = [jnp.zeros((chunk, LANES), F32) for _ in bs]
    num = [jnp.zeros((chunk, vw), F32) for _ in bs]
    yield
    for h in range(nh):
        mk = (lane_k >= h * dk) & (lane_k < (h + 1) * dk)
        mv = (lane_v >= h * dv) & (lane_v < (h + 1) * dv)
        qk_h = [lax.dot_general(jnp.where(mk, q[b], 0.0).astype(BF16), k16[b], (((1,), (1,)), ((), ())),
                                preferred_element_type=F32) for b in bs]
        d = [jnp.broadcast_to(gm[b][:, h:h + 1], (chunk, chunk)) + lig_t[b][h:h + 1, :] for b in bs]
        s = [qk_h[b] * jnp.exp2(jnp.where(causal, d[b], -jnp.inf)) for b in bs]
        ssum = [jnp.where(lane_k == h, jnp.sum(s[b], axis=-1, keepdims=True), ssum[b]) for b in bs]
        num = [num[b] + _bdot(s[b], jnp.where(mv, v[b], 0.0)) for b in bs]
        yield
    den = [inter_w[b] * qn[b] + ssum[b] for b in bs]
    rden = [1.0 / jnp.maximum(jnp.abs(den[b]), jnp.exp(-m_t[b])) for b in bs]
    g_last = [g[b][chunk - 1:chunk, :] for b in bs]
    a_all = [g_last[b] - g[b] + li[b] for b in bs]
    m_new = [jnp.maximum(g_last[b] + m_prev[b], jnp.max(a_all[b], axis=0, keepdims=True)) for b in bs]
    dec = [jnp.exp(g_last[b] + m_prev[b] - m_new[b]) for b in bs]
    wts = [jnp.exp(a_all[b] - m_new[b]) for b in bs]
    per_head = [jnp.concatenate([inter_w[b], rden[b], wts[b], jnp.broadcast_to(dec[b], (8, LANES))], axis=0)
                for b in bs]
    pieces = [_split3(per_head[b])[:2] for b in bs]
    on_v = [sum(jnp.dot(p, expand_v, preferred_element_type=F32) for p in pieces[b]) for b in bs]
    on_k = [sum(jnp.dot(p[2 * chunk:], expand_k, preferred_element_type=F32) for p in pieces[b]) for b in bs]
    hs = [(on_v[b][0:chunk] * q_c[b] + num[b]) * on_v[b][chunk:2 * chunk] for b in bs]
    for b in bs:
        c_sc[b] = c_old[b] * on_v[b][3 * chunk:3 * chunk + 1] + jnp.where(
            cmask, _bdot_tn(k16[b], on_v[b][2 * chunk:3 * chunk] * v[b]), 0.0)
        n_sc[b] = n_old[b] * on_k[b][chunk:chunk + 1] + jnp.sum(on_k[b][0:chunk] * k[b], axis=0, keepdims=True)
        m_sc[b] = m_new[b]

    hh = jnp.concatenate(hs, axis=0)
    ms = _segsum(hh * hh, dv) * (1.0 / dv)
    out = hh * lax.rsqrt(ms + NORM_EPS) * on_ref[...] * _sigmoid(o_pre)
    o_ref[...] = out.astype(o_ref.dtype).reshape(o_ref.shape)


def _recurrent_kernel(x_ref, cw_ref, cb_ref, ib_ref, fb_ref, on_ref, fa_ref, fbk_ref, ft_ref, pc_ref, lnw_ref, lnb_ref,
                      om_ref, or_ref, prev_sc, c_sc, n_sc, m_sc, state_sc, *, layer, nb, chunk, rwkv_chunk, cps):
    @pl.when(pl.program_id(0) == 0)
    def _():
        for sc in (prev_sc, c_sc, n_sc, m_sc, state_sc):
            sc[...] = jnp.zeros(sc.shape, F32)

    scan = _rwkv_scan_stages(fa_ref, fbk_ref, ft_ref, pc_ref, lnw_ref, lnb_ref, or_ref, state_sc,
                             layer=layer, nb=nb, chunk=rwkv_chunk, cps=cps)
    mlstm = _mlstm_stages(x_ref, cw_ref, cb_ref, ib_ref, fb_ref, on_ref, om_ref, prev_sc, c_sc, n_sc, m_sc,
                          layer=layer, nb=nb, chunk=chunk)
    order = [scan, mlstm] + [scan, scan, mlstm] * (MLSTM_HEADS - 1) + [scan, mlstm]
    pending = {id(scan): 2 * cps + 1, id(mlstm): MLSTM_HEADS + 2}
    for g in order:
        next(g, None)
        pending[id(g)] -= 1
    for g in (scan, mlstm):
        for _ in range(pending[id(g)]):
            next(g, None)


def _recurrent(l, mlstm_in, fa, fb, ft, pc, batch, seq, cw, cb, ib, fbias, on, ln_w, ln_b):
    chunk = MLSTM_CHUNK
    rchunk = RWKV_CHUNK
    assert chunk % rchunk == 0
    w = RWKV_WIDTH
    x = mlstm_in.reshape(batch, seq, MLSTM_IN)
    fa = fa.reshape(batch, seq, 5 * w)
    fb = fb.reshape(batch, seq, 4 * w)
    pc = pc.reshape(batch, seq // rchunk, w)
    blk = lambda width: pl.BlockSpec((batch, chunk, width), lambda c: (0, c, 0))
    om, orw = pl.pallas_call(
        functools.partial(_recurrent_kernel, layer=l, nb=batch, chunk=chunk, rwkv_chunk=rchunk, cps=chunk // rchunk),
        out_shape=(jax.ShapeDtypeStruct((batch, seq, MLSTM_WIDTH), BF16),
                   jax.ShapeDtypeStruct((batch, seq, w), BF16)),
        grid=(seq // chunk,),
        in_specs=[blk(MLSTM_IN), _resident(cw, l), _resident(cb), _resident(ib), _resident(fbias), _resident(on),
                  blk(2 * w), blk(4 * w), blk(4 * w), _resident(pc), _resident(ln_w), _resident(ln_b)],
        out_specs=(blk(MLSTM_WIDTH), blk(w)),
        scratch_shapes=[pltpu.VMEM((batch * chunk, 2 * MLSTM_HEADS * MLSTM_QK), F32),
                        pltpu.VMEM((batch, MLSTM_HEADS * MLSTM_QK, MLSTM_WIDTH), F32),
                        pltpu.VMEM((batch, 1, MLSTM_HEADS * MLSTM_QK), F32),
                        pltpu.VMEM((batch, 1, LANES), F32),
                        pltpu.VMEM((batch, w, w), F32)],
        compiler_params=_cparams(("arbitrary",)),
        name="mlstm_rwkv_scan",
    )(x, cw, cb, ib, fbias, on, fa, fb, ft, pc, ln_w, ln_b)
    return om.reshape(batch * seq, MLSTM_WIDTH), orw.reshape(batch * seq, w)


def _ffn_kernel(x_ref, ya_ref, yb_ref, yc_ref, wo_ref, g_ref, wg_ref, wu_ref, wd_ref, fg_ref,
                o_ref, act_sc, *, layer, final_norm, tf):
    (g_ref,) = _rows(layer, g_ref)
    y = jnp.concatenate([ya_ref[...], yb_ref[...], yc_ref[...]], axis=-1)
    x1 = x_ref[...] + jnp.dot(y, wo_ref[...], preferred_element_type=F32)
    h = _rms(x1, g_ref[...]).astype(BF16)
    for c in range(D_FF // tf):
        gate = jnp.dot(h, wg_ref[:, c * tf:(c + 1) * tf], preferred_element_type=F32)
        up = jnp.dot(h, wu_ref[:, c * tf:(c + 1) * tf], preferred_element_type=F32)
        act_sc[:, c * tf:(c + 1) * tf] = (gate * _sigmoid(gate) * up).astype(BF16)
    out = x1 + jnp.dot(act_sc[...], wd_ref[...], preferred_element_type=F32)
    if final_norm:
        out = _rms(out, fg_ref[...])
    o_ref[...] = out


def _out_ffn(l, x, ya, yb, yc, wo, g, wg, wu, wd, fg, final_norm):
    t = x.shape[0]
    tm, tf = TM_FFN, TF_FFN
    row = lambda w: pl.BlockSpec((tm, w), lambda i: (i, 0))
    return pl.pallas_call(
        functools.partial(_ffn_kernel, layer=l, final_norm=final_norm, tf=tf),
        out_shape=jax.ShapeDtypeStruct((t, D_MODEL), F32),
        grid=(t // tm,),
        in_specs=[row(D_MODEL), row(MLA_WIDTH), row(RWKV_WIDTH), row(MLSTM_WIDTH), _resident(wo, l),
                  _resident(g), _resident(wg, l), _resident(wu, l), _resident(wd, l), _resident(fg)],
        out_specs=row(D_MODEL),
        scratch_shapes=[pltpu.VMEM((tm, D_FF), BF16)],
        compiler_params=_cparams(("parallel",)),
        name="out_ffn",
    )(x, ya, yb, yc, wo, g, wg, wu, wd, fg)


def _pad_cols(w, width):
    return jnp.pad(w, [(0, 0)] * (w.ndim - 1) + [(0, width - w.shape[-1])])


def _pad_rows(w, height):
    return jnp.pad(w, [(0, 0)] * (w.ndim - 2) + [(0, height - w.shape[-2]), (0, 0)])


def _rot_half_cols(w):
    half = w.shape[-1] // 2
    return jnp.concatenate([-w[..., half:], w[..., :half]], axis=-1)


def _stacked_weights(w_in, mla_w_uq, mla_w_ukv, rwkv_w2, rwkv_a2, rwkv_g2):
    depth = w_in.shape[0]
    wt = jnp.swapaxes(w_in, 1, 2)
    c_q, c_kv, k_pe = wt[:, 0:256], wt[:, 256:512], wt[:, 512:576]
    rw = wt[:, 576:1472]
    ml = wt[:, 1472:2248]
    k_pe_rot = jnp.swapaxes(_rot_half_cols(jnp.swapaxes(k_pe, 1, 2)), 1, 2)
    w_mla = jnp.concatenate([c_q, c_kv, _pad_rows(k_pe, LANES), _pad_rows(k_pe_rot, LANES)], axis=1)
    w_mlstm = jnp.concatenate([ml[:, 0:256], ml[:, 256:512], ml[:, 520:776],
                               _pad_rows(ml[:, 512:516], LANES), _pad_rows(ml[:, 516:520], LANES)], axis=1)
    w_all = jnp.concatenate([w_mla, rw, w_mlstm], axis=1).astype(BF16)

    uq = mla_w_uq.reshape(depth, MLA_Q_LORA, MLA_HEADS, MLA_NOPE + MLA_ROPE)
    nope = uq[..., :MLA_NOPE].reshape(depth, MLA_Q_LORA, MLA_HEADS * MLA_NOPE)
    pe = _pad_cols(uq[..., MLA_NOPE:], LANES).reshape(depth, MLA_Q_LORA, MLA_HEADS * LANES)
    per = _pad_cols(_rot_half_cols(uq[..., MLA_NOPE:]), LANES).reshape(depth, MLA_Q_LORA, MLA_HEADS * LANES)
    wq = jnp.concatenate([nope, pe, per], axis=-1).astype(BF16)
    ukv = mla_w_ukv.reshape(depth, MLA_KV_LORA, MLA_HEADS, MLA_NOPE + MLA_VDIM)
    wkv = jnp.concatenate([ukv[..., :MLA_NOPE].reshape(depth, MLA_KV_LORA, -1),
                           ukv[..., MLA_NOPE:].reshape(depth, MLA_KV_LORA, -1)], axis=-1).astype(BF16)

    assert RWKV_DECAY_LORA + RWKV_AAA_LORA + RWKV_GATE_LORA == LANES
    rows = lambda before, wt: jnp.pad(wt, ((0, 0), (before, LANES - before - wt.shape[1]), (0, 0))).astype(BF16)
    w2p = rows(0, rwkv_w2)
    a2p = rows(RWKV_DECAY_LORA, rwkv_a2)
    g2p = rows(RWKV_DECAY_LORA + RWKV_AAA_LORA, rwkv_g2)
    return w_all, wq, wkv, w2p, a2p, g2p


def kernel(x, positions, mix_norm, w_in, mla_q_norm, mla_w_uq, mla_kv_norm, mla_w_ukv, mla_out_norm, rwkv_mu, rwkv_w0, rwkv_w2, rwkv_a0, rwkv_a2, rwkv_g2, rwkv_k_k, rwkv_k_a, rwkv_r_k, rwkv_ln_w, rwkv_ln_b, mlstm_conv_w, mlstm_conv_b, mlstm_i_bias, mlstm_f_bias, mlstm_out_norm, w_out, ffn_norm, w_gate, w_up, w_down, final_norm):
    batch, seq, _ = x.shape
    depth = w_in.shape[0]
    xt = x.reshape(batch * seq, D_MODEL)
    cos, sin, (wo, wg, wu, wd) = _rope_tables_and_casts(positions, (w_out, w_gate, w_up, w_down))
    w_all, wq, wkv, w2p, a2p, g2p = _stacked_weights(w_in, mla_w_uq, mla_w_ukv, rwkv_w2, rwkv_a2, rwkv_g2)
    ml_ib = _pad_cols(mlstm_i_bias, LANES)
    ml_fb = _pad_cols(mlstm_f_bias, LANES)
    for l in range(depth):
        q, k, v, fa, fb, pc, mlstm_in = _inproj(
            l, xt, seq, mix_norm, w_all, cos, sin, mla_q_norm, mla_kv_norm, wq, wkv,
            rwkv_mu, rwkv_w0, rwkv_a0, rwkv_k_k, rwkv_k_a, rwkv_r_k,
            w2p, a2p, g2p)
        y_mla = _mla_attention(l, q, k, v, mla_out_norm, batch, seq)
        ft = _rwkv_chunk(fa, batch, seq)
        y_mlstm, y_rwkv = _recurrent(l, mlstm_in, fa, fb, ft, pc, batch, seq, mlstm_conv_w, mlstm_conv_b, ml_ib, ml_fb,
                                     mlstm_out_norm, rwkv_ln_w, rwkv_ln_b)
        xt = _out_ffn(l, xt, y_mla, y_rwkv, y_mlstm, wo, ffn_norm, wg, wu, wd,
                      final_norm.reshape(1, -1), final_norm=(l == depth - 1))
    return xt.reshape(batch, seq, D_MODEL)
```

```python
import functools
import math

import jax
import jax.numpy as jnp
from jax import lax
from jax.experimental import pallas as pl
from jax.experimental.pallas import tpu as pltpu

F32 = jnp.float32
BF16 = jnp.bfloat16

D_MODEL = 1024
MLA_HEADS = 4
MLA_NOPE = 128
MLA_ROPE = 64
MLA_VDIM = 128
MLA_Q_LORA = 256
MLA_KV_LORA = 256
MLA_WIDTH = MLA_HEADS * MLA_VDIM
MLA_QK = 256
ROPE_THETA = 10000.0
RWKV_HEADS = 4
RWKV_HEAD = 64
RWKV_WIDTH = 256
RWKV_DECAY_LORA = 32
RWKV_AAA_LORA = 32
RWKV_GATE_LORA = 64
RWKV_IN = 3 * RWKV_WIDTH + 128
RWKV_LN_EPS = 64e-5
MLSTM_HEADS = 4
MLSTM_QK = 32
MLSTM_V = 64
MLSTM_WIDTH = 256
MLSTM_CONV = 4
MLSTM_IN = 1024
D_FF = 2816
NORM_EPS = 1e-6
LANES = 128
SUBLANES = 8

MLA_CQ = 0
MLA_CKV = MLA_CQ + MLA_Q_LORA
MLA_KPE = MLA_CKV + MLA_KV_LORA
MLA_KPER = MLA_KPE + LANES
MLA_IN = MLA_KPER + LANES

TM_INPROJ = 1024
TQ_ATTN = 512
TK_ATTN = 512
HP_ATTN = 4
RWKV_CHUNK = 64
RWKV_PREP_CHUNKS = 2
MLSTM_CHUNK = 256
TM_FFN = 1024
TF_FFN = 256
VMEM_LIMIT = 56 * 1024 * 1024


def _cparams(sem):
    return pltpu.CompilerParams(dimension_semantics=sem, vmem_limit_bytes=VMEM_LIMIT)


def _resident(a, layer=None):
    if layer is None:
        nd = a.ndim
        return pl.BlockSpec(a.shape, lambda *_: (0,) * nd, pipeline_mode=pl.Buffered(1))
    nd = a.ndim - 1
    return pl.BlockSpec((None,) + a.shape[1:], lambda *_: (layer,) + (0,) * nd, pipeline_mode=pl.Buffered(1))


def _rows(layer, *refs):
    return [r.at[layer:layer + 1] for r in refs]


def _bdot(a, b):
    return jnp.dot(a.astype(BF16), b.astype(BF16), preferred_element_type=F32)


def _dot_nt(a, b):
    return lax.dot_general(a, b, (((1,), (1,)), ((), ())), preferred_element_type=F32)


def _bdot_tn(a, b):
    return lax.dot_general(a.astype(BF16), b.astype(BF16), (((0,), (0,)), ((), ())),
                           preferred_element_type=F32)


def _split3(x):
    h = x.astype(BF16)
    r1 = x - h.astype(F32)
    m = r1.astype(BF16)
    lo = (r1 - m.astype(F32)).astype(BF16)
    return h, m, lo


def _exact_left_dot(sel, x):
    h, m, lo = _split3(x)
    s = sel.astype(BF16)
    return (jnp.dot(s, h, preferred_element_type=F32) + jnp.dot(s, m, preferred_element_type=F32)
            + jnp.dot(s, lo, preferred_element_type=F32))


def _exact_right_dot(x, sel, parts=3):
    pieces = _split3(x)[:parts]
    s = sel.astype(BF16)
    out = jnp.dot(pieces[0], s, preferred_element_type=F32)
    for p in pieces[1:]:
        out = out + jnp.dot(p, s, preferred_element_type=F32)
    return out


def _rms(x, g):
    return x * lax.rsqrt(jnp.mean(x * x, axis=-1, keepdims=True) + NORM_EPS) * g


def _sigmoid(x):
    return 1.0 / (1.0 + jnp.exp(-x))


def _log_sigmoid(x):
    return jnp.minimum(x, 0.0) - jnp.log1p(jnp.exp(-jnp.abs(x)))


def _div(x, d):
    assert d & (d - 1) == 0
    return lax.shift_right_logical(x, d.bit_length() - 1)


def _mod(x, d):
    assert d & (d - 1) == 0
    return lax.bitwise_and(x, d - 1)


def _shift_rows(x, prev, s, chunk):
    n = x.shape[0]
    row = lax.broadcasted_iota(jnp.int32, x.shape, 0)
    return jnp.where(_mod(row, chunk) >= s, pltpu.roll(x, s, 0), pltpu.roll(prev, n - chunk + s, 0))


def _rope_cast_kernel(pos_ref, invf_ref, *refs):
    n = (len(refs) - 2) // 2
    w_refs, cos_ref, sin_ref, o_refs = refs[:n], refs[n], refs[n + 1], refs[n + 2:]
    ang = pos_ref[...].astype(F32) * invf_ref[...]
    cos_ref[...] = jnp.cos(ang)
    sin_ref[...] = jnp.sin(ang)
    for w_ref, o_ref in zip(w_refs, o_refs):
        o_ref[...] = w_ref[...].astype(BF16)


def _rope_tables_and_casts(positions, weights):
    t = positions.size
    tm = min(1024, t)
    steps = t // tm
    inv_freq = ROPE_THETA ** (-jnp.arange(0, MLA_ROPE, 2, dtype=F32) / MLA_ROPE)
    invf = jnp.tile(inv_freq, LANES // (MLA_ROPE // 2))[None, :]
    flat = [w.reshape(-1, w.shape[-1]) for w in weights]
    rows = [f.shape[0] // steps for f in flat]
    assert all(f.shape[0] % steps == 0 and r % 16 == 0 for f, r in zip(flat, rows))
    slab = lambda r, c: pl.BlockSpec((r, c), lambda i: (i, 0))
    outs = pl.pallas_call(
        _rope_cast_kernel,
        out_shape=(jax.ShapeDtypeStruct((t, LANES), F32), jax.ShapeDtypeStruct((t, LANES), F32))
        + tuple(jax.ShapeDtypeStruct(f.shape, BF16) for f in flat),
        grid=(steps,),
        in_specs=[slab(tm, 1), pl.BlockSpec((1, LANES), lambda i: (0, 0))]
        + [slab(r, f.shape[1]) for f, r in zip(flat, rows)],
        out_specs=(slab(tm, LANES), slab(tm, LANES)) + tuple(slab(r, f.shape[1]) for f, r in zip(flat, rows)),
        compiler_params=_cparams(("parallel",)),
        name="rope_tables_casts",
    )(positions.reshape(t, 1), invf, *flat)
    return outs[0], outs[1], [o.reshape(w.shape) for o, w in zip(outs[2:], weights)]


(FA_RT, FA_V, FA_AT, FA_KT, FA_BT) = range(5)
(FB_KH, FB_BH, FB_BONUS, FB_GATE) = range(4)
(FT_TA, FT_UV, FT_YV, FT_ARB) = range(4)
RWKV_GROUP = 256


def _segsum(x, seg):
    assert 2 * seg == LANES and x.shape[1] % LANES == 0
    out = []
    for blk in range(x.shape[1] // LANES):
        xb = x[:, blk * LANES:(blk + 1) * LANES]
        low = lax.broadcasted_iota(jnp.int32, xb.shape, 1) < seg
        s_low = jnp.sum(jnp.where(low, xb, 0.0), axis=-1, keepdims=True)
        s_high = jnp.sum(jnp.where(low, 0.0, xb), axis=-1, keepdims=True)
        out.append(jnp.where(low, s_low, s_high))
    return jnp.concatenate(out, axis=-1)


def _rwkv_masks(chunk):
    w = RWKV_WIDTH
    r = lax.broadcasted_iota(jnp.int32, (w, w), 0)
    c = lax.broadcasted_iota(jnp.int32, (w, w), 1)
    return r, c, _div(r, chunk) == _div(c, chunk)


def _rwkv_features(xs, r0, prm, fa_ref, fb_ref, pc_ref, chunk):
    w0_ref, a0_ref, kk_ref, ka_ref, rk_ref, w2_ref, a2_ref, g2_ref = prm
    w = RWKV_WIDTH
    hd = RWKV_HEAD
    n = xs.shape[0]
    r = xs[:, 0:w]
    k = xs[:, w:2 * w]
    v = xs[:, 2 * w:3 * w]
    lor = xs[:, 3 * w:]
    ld = -math.exp(-0.5) * _sigmoid(w0_ref[...] + _bdot(jnp.tanh(lor), w2_ref[...]))
    a = _sigmoid(a0_ref[...] + _bdot(lor, a2_ref[...]))
    g = _bdot(_sigmoid(lor), g2_ref[...])
    kk = k * kk_ref[...]
    kk = kk / jnp.maximum(jnp.sqrt(_segsum(kk * kk, hd)), 1e-12)
    k2 = k * (1.0 + (a - 1.0) * ka_ref[...])
    kb = kk * a
    bonus = _segsum(r * k2 * rk_ref[...], hd) * v

    assert n <= RWKV_WIDTH
    ri, ci, same_chunk = _rwkv_masks(chunk)
    tri = jnp.where(same_chunk & (ci <= ri), 1.0, 0.0)[0:n, 0:n]
    cl = _exact_left_dot(tri, ld)
    units = n // chunk
    cl_last = jnp.concatenate(
        [jnp.broadcast_to(cl[(u + 1) * chunk - 1:(u + 1) * chunk, :], (chunk, w)) for u in range(units)], axis=0)
    e_neg = jnp.exp(-cl)
    e_end = jnp.exp(cl_last - cl)
    rows = slice(r0, r0 + n)

    def put(ref, sec, val):
        ref[rows, sec * w:(sec + 1) * w] = val.astype(ref.dtype)

    put(fa_ref, FA_RT, r * jnp.exp(cl))
    put(fa_ref, FA_V, v)
    put(fa_ref, FA_AT, kk * jnp.exp(cl - ld))
    put(fa_ref, FA_KT, k2 * e_neg)
    put(fa_ref, FA_BT, kb * e_neg)
    put(fb_ref, FB_KH, k2 * e_end)
    put(fb_ref, FB_BH, kb * e_end)
    put(fb_ref, FB_BONUS, bonus)
    put(fb_ref, FB_GATE, g)
    for u in range(units):
        c = r0 // chunk + u
        pc_ref[c:c + 1, :] = jnp.exp(cl_last[u * chunk:u * chunk + 1, :])


def _inproj_kernel(x_ref, g_ref, w_ref, cos_ref, sin_ref, qn_ref, kvn_ref, wq_ref, wkv_ref,
                   mu_ref, w0_ref, a0_ref, kk_ref, ka_ref, rk_ref, w2_ref, a2_ref, g2_ref,
                   q_ref, k_ref, v_ref, fa_ref, fb_ref, pc_ref, mlstm_ref, prev_sc, *, layer, chunk, tiles_per_seq):
    tm = x_ref.shape[0]
    g_ref, qn_ref, kvn_ref, mu_ref, w0_ref, a0_ref, kk_ref, ka_ref, rk_ref = _rows(
        layer, g_ref, qn_ref, kvn_ref, mu_ref, w0_ref, a0_ref, kk_ref, ka_ref, rk_ref)

    @pl.when(pl.program_id(0) % tiles_per_seq == 0)
    def _():
        prev_sc[...] = jnp.zeros(prev_sc.shape, F32)

    hb = _rms(x_ref[...], g_ref[...]).astype(BF16)
    rw = _dot_nt(hb, w_ref[MLA_IN:MLA_IN + RWKV_IN, :])
    row = lax.broadcasted_iota(jnp.int32, rw.shape, 0)
    shifted = jnp.where(row >= 1, pltpu.roll(rw, 1, 0), prev_sc[0:1, :])
    prev_sc[0:1, :] = rw[tm - 1:tm, :]
    xs = rw + (shifted - rw) * mu_ref[...]
    prm = (w0_ref, a0_ref, kk_ref, ka_ref, rk_ref, w2_ref, a2_ref, g2_ref)

    mla = _dot_nt(hb, w_ref[0:MLA_IN, :])
    groups = tm // RWKV_GROUP
    for gi in range(groups // 2):
        _rwkv_features(xs[gi * RWKV_GROUP:(gi + 1) * RWKV_GROUP], gi * RWKV_GROUP, prm, fa_ref, fb_ref, pc_ref, chunk)
    mlstm_ref[...] = _dot_nt(hb, w_ref[MLA_IN + RWKV_IN:, :])
    for gi in range(groups // 2, groups):
        _rwkv_features(xs[gi * RWKV_GROUP:(gi + 1) * RWKV_GROUP], gi * RWKV_GROUP, prm, fa_ref, fb_ref, pc_ref, chunk)

    cos = cos_ref[...]
    sin = sin_ref[...]
    scale = (MLA_NOPE + MLA_ROPE) ** -0.5 * math.log2(math.e)
    hw = MLA_HEADS * LANES
    cqn = _rms(mla[:, MLA_CQ:MLA_CKV], qn_ref[...]).astype(BF16)
    q = jnp.dot(cqn, wq_ref[...], preferred_element_type=F32)
    ckvn = _rms(mla[:, MLA_CKV:MLA_KPE], kvn_ref[...]).astype(BF16)
    kv = jnp.dot(ckvn, wkv_ref[...], preferred_element_type=F32)
    kp = (mla[:, MLA_KPE:MLA_KPER] * cos + mla[:, MLA_KPER:MLA_IN] * sin).astype(BF16)
    for h in range(MLA_HEADS):
        c0 = h * LANES
        pe = q[:, hw + c0:hw + c0 + LANES] * cos + q[:, 2 * hw + c0:2 * hw + c0 + LANES] * sin
        q_ref[:, h * MLA_QK:h * MLA_QK + LANES] = (q[:, c0:c0 + LANES] * scale).astype(BF16)
        q_ref[:, h * MLA_QK + LANES:(h + 1) * MLA_QK] = (pe * scale).astype(BF16)
        k_ref[:, h * MLA_QK:h * MLA_QK + LANES] = kv[:, c0:c0 + LANES].astype(BF16)
        k_ref[:, h * MLA_QK + LANES:(h + 1) * MLA_QK] = kp
    v_ref[...] = kv[:, hw:].astype(BF16)


def _inproj(l, x, seq, g, w, cos, sin, qn, kvn, wq, wkv, mu, w0, a0, k_k, k_a, r_k, w2p, a2p, g2p):
    t = x.shape[0]
    tm = TM_INPROJ
    chunk = RWKV_CHUNK
    assert seq % tm == 0 and tm % RWKV_GROUP == 0 and RWKV_GROUP % chunk == 0
    row = lambda width: pl.BlockSpec((tm, width), lambda i: (i, 0))
    rw = RWKV_WIDTH
    return pl.pallas_call(
        functools.partial(_inproj_kernel, layer=l, chunk=chunk, tiles_per_seq=seq // tm),
        out_shape=(jax.ShapeDtypeStruct((t, MLA_HEADS * MLA_QK), BF16),
                   jax.ShapeDtypeStruct((t, MLA_HEADS * MLA_QK), BF16),
                   jax.ShapeDtypeStruct((t, MLA_WIDTH), BF16),
                   jax.ShapeDtypeStruct((t, 5 * rw), BF16),
                   jax.ShapeDtypeStruct((t, 4 * rw), BF16),
                   jax.ShapeDtypeStruct((t // chunk, rw), F32),
                   jax.ShapeDtypeStruct((t, MLSTM_IN), F32)),
        grid=(t // tm,),
        in_specs=[row(D_MODEL), _resident(g), _resident(w, l), row(LANES), row(LANES), _resident(qn),
                  _resident(kvn), _resident(wq, l), _resident(wkv, l),
                  _resident(mu), _resident(w0), _resident(a0), _resident(k_k), _resident(k_a),
                  _resident(r_k), _resident(w2p, l), _resident(a2p, l), _resident(g2p, l)],
        out_specs=(row(MLA_HEADS * MLA_QK), row(MLA_HEADS * MLA_QK), row(MLA_WIDTH), row(5 * rw), row(4 * rw),
                   pl.BlockSpec((tm // chunk, rw), lambda i: (i, 0)), row(MLSTM_IN)),
        scratch_shapes=[pltpu.VMEM((SUBLANES, RWKV_IN), F32)],
        compiler_params=_cparams(("arbitrary",)),
        name="inproj",
    )(x, g, w, cos, sin, qn, kvn, wq, wkv, mu, w0, a0, k_k, k_a, r_k, w2p, a2p, g2p)


def _attn_kernel(q_ref, k_ref, v_ref, g_ref, *refs, layer, tq, tk, hp):
    n_cast = (len(refs) - 5) // 2
    w_refs, o_ref, wo_refs = refs[:n_cast], refs[n_cast], refs[n_cast + 1:2 * n_cast + 1]
    m_sc, acc_sc, sa_sc, sb_sc = refs[2 * n_cast + 1:]
    for w_ref, wo_ref in zip(w_refs, wo_refs):
        wo_ref[...] = w_ref[...].astype(BF16)
    i = pl.program_id(2)
    m_sc[...] = jnp.full(m_sc.shape, -jnp.inf, F32)
    acc_sc[...] = jnp.zeros(acc_sc.shape, F32)
    sub = m_sc.shape[1]
    hs = range(hp)
    ones = jnp.ones((acc_sc.shape[1] - MLA_VDIM, tk), BF16)
    (g_ref,) = _rows(layer, g_ref)

    def produce(j, s_ref):
        off = pl.multiple_of(j * tk, tk)
        for h in hs:
            s_ref[h] = lax.dot_general(k_ref[0, pl.ds(off, tk), h * MLA_QK:(h + 1) * MLA_QK],
                                       q_ref[0, :, h * MLA_QK:(h + 1) * MLA_QK],
                                       (((1,), (1,)), ((), ())), preferred_element_type=F32)

    def consume(j, s_ref, masked):
        off = pl.multiple_of(j * tk, tk)
        s = [s_ref[h] for h in hs]
        if masked:
            keys = lax.broadcasted_iota(jnp.int32, (tk, tq), 0)
            queries = lax.broadcasted_iota(jnp.int32, (tk, tq), 1)
            s = [jnp.where(keys <= queries, s[h], -jnp.inf) for h in hs]
        m_old = [m_sc[h][0:1] for h in hs]
        m_new = [jnp.maximum(m_old[h], jnp.max(s[h], axis=0, keepdims=True)) for h in hs]
        p = [jnp.exp2(s[h] - m_new[h]) for h in hs]
        alpha = [jnp.exp2(m_old[h] - m_new[h]) for h in hs]
        for h in hs:
            m_sc[h] = jnp.broadcast_to(m_new[h], (sub, tq))
        v1 = [jnp.concatenate([jnp.transpose(v_ref[0, pl.ds(off, tk), h * MLA_VDIM:(h + 1) * MLA_VDIM]), ones],
                              axis=0) for h in hs]
        pv = [jnp.dot(v1[h], p[h].astype(BF16), preferred_element_type=F32) for h in hs]
        for h in hs:
            acc_sc[h] = alpha[h] * acc_sc[h] + pv[h]

    def pair(jj, c):
        j = 2 * jj
        produce(j + 1, sb_sc)
        consume(j, sa_sc, False)
        produce(j + 2, sa_sc)
        consume(j + 1, sb_sc, False)
        return c

    produce(0, sa_sc)
    lax.fori_loop(0, i // 2, pair, 0)

    @pl.when(i % 2 == 0)
    def _():
        consume(i, sa_sc, True)

    @pl.when(i % 2 == 1)
    def _():
        produce(i, sb_sc)
        consume(i - 1, sa_sc, False)
        consume(i, sb_sc, True)

    for h in hs:
        o = acc_sc[h, 0:MLA_VDIM] / acc_sc[h, MLA_VDIM:MLA_VDIM + 1]
        o = o * lax.rsqrt(jnp.mean(o * o, axis=0, keepdims=True) + NORM_EPS)
        o_ref[0, :, h * MLA_VDIM:(h + 1) * MLA_VDIM] = (
            jnp.transpose(o) * g_ref[:, h * MLA_VDIM:(h + 1) * MLA_VDIM]).astype(o_ref.dtype)


def _mla_attention(l, q, k, v, g, batch, seq, weights=()):
    tq, tk, hp = TQ_ATTN, TK_ATTN, HP_ATTN
    assert tq == tk and MLA_VDIM == LANES and hp == MLA_HEADS
    q = q.reshape(batch, seq, MLA_HEADS * MLA_QK)
    k = k.reshape(batch, seq, MLA_HEADS * MLA_QK)
    v = v.reshape(batch, seq, MLA_WIDTH)
    nh, nq = MLA_HEADS // hp, seq // tq
    steps = batch * nh * nq
    flat = [w.reshape(-1, w.shape[-1]) for w in weights]
    rows = [f.shape[0] // steps for f in flat]
    assert all(f.shape[0] % steps == 0 and r % 16 == 0 for f, r in zip(flat, rows))
    slabs = [pl.BlockSpec((r, f.shape[1]), lambda b, h, i: ((b * nh + h) * nq + i, 0)) for f, r in zip(flat, rows)]
    outs = pl.pallas_call(
        functools.partial(_attn_kernel, layer=l, tq=tq, tk=tk, hp=hp),
        out_shape=(jax.ShapeDtypeStruct((batch, seq, MLA_WIDTH), BF16),)
        + tuple(jax.ShapeDtypeStruct(f.shape, BF16) for f in flat),
        grid=(batch, nh, nq),
        in_specs=[pl.BlockSpec((1, tq, hp * MLA_QK), lambda b, h, i: (b, i, h)),
                  pl.BlockSpec((1, seq, hp * MLA_QK), lambda b, h, i: (b, 0, h)),
                  pl.BlockSpec((1, seq, hp * MLA_VDIM), lambda b, h, i: (b, 0, h)),
                  _resident(g)] + slabs,
        out_specs=(pl.BlockSpec((1, tq, hp * MLA_VDIM), lambda b, h, i: (b, i, h)),) + tuple(slabs),
        scratch_shapes=[pltpu.VMEM((hp, SUBLANES, tq), F32),
                        pltpu.VMEM((hp, MLA_VDIM + 2 * SUBLANES, tq), F32),
                        pltpu.VMEM((hp, tk, tq), F32), pltpu.VMEM((hp, tk, tq), F32)],
        compiler_params=_cparams(("parallel", "parallel", "arbitrary")),
        name="mla_attention",
    )(q, k, v, g, *flat)
    return outs[0].reshape(batch * seq, MLA_WIDTH), [o.reshape(w.shape) for o, w in zip(outs[1:], weights)]


def _tile_heads(z):
    return jnp.concatenate([z] * RWKV_HEADS, axis=0)


def _rwkv_chunk_kernel(fa_ref, ft_ref, *, nb, chunk, cps):
    seg = cps * chunk
    n = nb * seg
    w = RWKV_WIDTH
    hd = RWKV_HEAD
    x = fa_ref[...].reshape(n, fa_ref.shape[-1])
    r16, v16, a16, k16, b16 = (x[:, s * w:(s + 1) * w] for s in (FA_RT, FA_V, FA_AT, FA_KT, FA_BT))

    ri, ci, same_chunk = _rwkv_masks(chunk)
    bd = _div(ri, chunk) == _div(ci, hd)
    rt = lax.broadcasted_iota(jnp.int32, (chunk, w), 0)
    cs = _mod(lax.broadcasted_iota(jnp.int32, (chunk, w), 1), chunk)
    strict = cs < rt
    incl = cs <= rt
    c16 = _div(rt, 16) == _div(cs, 16)
    c32 = _div(rt, 32) == _div(cs, 32)
    eye = jnp.where(rt == cs, 1.0, 0.0)
    units = nb * cps

    def block_diag(z):
        return jnp.where(same_chunk, _tile_heads(z), 0.0)

    def put(u, sec, val):
        b, j = divmod(u, cps)
        ft_ref[b, j * chunk:(j + 1) * chunk, sec * w:(sec + 1) * w] = val.astype(ft_ref.dtype)

    def mm(x, y):
        return jnp.dot(x, y, preferred_element_type=F32)

    us = range(units)
    sls = [slice(u * chunk, (u + 1) * chunk) for u in us]
    a_st = [jnp.where(bd, _tile_heads(a16[sl]), 0.0) for sl in sls]
    v_st = [jnp.where(bd, _tile_heads(v16[sl]), 0.0) for sl in sls]
    kb_t = [jnp.concatenate([jnp.transpose(jnp.where(bd, _tile_heads(k16[sl]), 0.0)),
                             jnp.transpose(jnp.where(bd, _tile_heads(b16[sl]), 0.0))], axis=1) for sl in sls]
    sc = [mm(jnp.concatenate([a16[sls[u]], r16[sls[u]]], axis=0), kb_t[u]) for u in us]
    l_ab = [jnp.where(strict, sc[u][0:chunk, w:], 0.0) for u in us]
    l_ak = [jnp.where(strict, sc[u][0:chunk, 0:w], 0.0).astype(BF16) for u in us]
    a_rk = [jnp.where(incl, sc[u][chunk:, 0:w], 0.0).astype(BF16) for u in us]
    for u in us:
        put(u, FT_ARB, jnp.where(incl, sc[u][chunk:, w:], 0.0))
    xm = [-jnp.where(c16, l_ab[u], 0.0) for u in us]
    xm16 = [z.astype(BF16) for z in xm]
    off32 = [block_diag(jnp.where(c32 & jnp.logical_not(c16), l_ab[u], 0.0).astype(BF16)) for u in us]
    off64 = [block_diag(jnp.where(jnp.logical_not(c32), l_ab[u], 0.0).astype(BF16)) for u in us]
    x2 = [mm(xm16[u], block_diag(xm16[u])).astype(BF16) for u in us]
    x2_bd = [block_diag(z) for z in x2]
    lv = [mm(jnp.concatenate([l_ak[u], a_rk[u]], axis=0), v_st[u]) for u in us]
    wv = [block_diag(lv[u][0:chunk].astype(BF16)) for u in us]
    t_lo = [eye + xm[u] for u in us]
    tx = [mm(jnp.concatenate([t_lo[u].astype(BF16), x2[u]], axis=0), x2_bd[u]) for u in us]
    t_lo = [(t_lo[u] + tx[u][0:chunk]).astype(BF16) for u in us]
    x4 = [tx[u][chunk:] for u in us]
    x4b = [z.astype(BF16) for z in x4]
    x4_bd = [block_diag(z) for z in x4b]
    for u in us:
        put(u, FT_YV, lv[u][chunk:])
    x8_bd = [block_diag(mm(x4b[u], x4_bd[u]).astype(BF16)) for u in us]
    t_hi = [eye + x4[u] for u in us]
    t_hi = [block_diag((t_hi[u] + mm(t_hi[u].astype(BF16), x8_bd[u])).astype(BF16)) for u in us]
    t_inv = [mm(t_lo[u], t_hi[u]) for u in us]
    for off in (off32, off64):
        tb = [z.astype(BF16) for z in t_inv]
        mid = [mm(tb[u], off[u]).astype(BF16) for u in us]
        t_inv = [t_inv[u] - mm(mid[u], block_diag(tb[u])) for u in us]
    tb = [z.astype(BF16) for z in t_inv]
    for u in us:
        put(u, FT_TA, mm(tb[u], a_st[u]))
    for u in us:
        put(u, FT_UV, mm(tb[u], wv[u]))


def _rwkv_scan_stages(fa_ref, fb_ref, ft_ref, pc_ref, lnw_ref, lnb_ref, o_ref, state_sc, *, layer, nb, chunk, cps):
    seg = cps * chunk
    n = nb * seg
    w = RWKV_WIDTH
    hd = RWKV_HEAD
    lnw_ref, lnb_ref = _rows(layer, lnw_ref, lnb_ref)
    ri, ci, _ = _rwkv_masks(chunk)
    bd = _div(ri, chunk) == _div(ci, hd)
    bdv = _div(ri, hd) == _div(ci, hd)

    def sec(ref, b, j, s):
        return ref[b, j * chunk:(j + 1) * chunk, s * w:(s + 1) * w]

    ys = [[None] * cps for _ in range(nb)]
    bs = range(nb)
    for j in range(cps):
        gs = [state_sc[b] for b in bs]
        p1 = [lax.dot_general(jnp.concatenate([sec(ft_ref, b, j, FT_TA), sec(fa_ref, b, j, FA_RT)], axis=0),
                              gs[b].astype(BF16), (((1,), (1,)), ((), ())), preferred_element_type=F32)
              for b in bs]
        yield
        u = [(p1[b][0:chunk] + sec(ft_ref, b, j, FT_UV).astype(F32)).astype(BF16) for b in bs]
        upd = [lax.dot_general(jnp.concatenate([sec(fa_ref, b, j, FA_V), -u[b]], axis=0),
                               jnp.concatenate([sec(fb_ref, b, j, FB_KH), sec(fb_ref, b, j, FB_BH)], axis=0),
                               (((0,), (0,)), ((), ())), preferred_element_type=F32) for b in bs]
        for b in bs:
            state_sc[b] = gs[b] * pc_ref[b, pl.ds(pl.program_id(0) * cps + j, 1), :] + jnp.where(bdv, upd[b], 0.0)
        for b in bs:
            u_st = jnp.where(bd, _tile_heads(u[b]), 0.0)
            ys[b][j] = (p1[b][chunk:] + sec(ft_ref, b, j, FT_YV).astype(F32)
                        - jnp.dot(sec(ft_ref, b, j, FT_ARB), u_st, preferred_element_type=F32))
        yield

    y = jnp.concatenate([ys[b][j] for b in range(nb) for j in range(cps)], axis=0)
    mean = _segsum(y, hd) * (1.0 / hd)
    d = y - mean
    var = _segsum(d * d, hd) * (1.0 / hd)
    yn = d * lax.rsqrt(var + RWKV_LN_EPS) * lnw_ref[...] + lnb_ref[...]
    bonus = fb_ref[:, :, FB_BONUS * w:(FB_BONUS + 1) * w].reshape(n, w).astype(F32)
    gate = fb_ref[:, :, FB_GATE * w:(FB_GATE + 1) * w].reshape(n, w).astype(F32)
    o_ref[...] = ((yn + bonus) * gate).astype(o_ref.dtype).reshape(o_ref.shape)


def _rwkv_chunk(fa, batch, seq):
    chunk = RWKV_CHUNK
    w = RWKV_WIDTH
    assert RWKV_HEADS * chunk == w
    cps = RWKV_PREP_CHUNKS
    seg = cps * chunk
    return pl.pallas_call(
        functools.partial(_rwkv_chunk_kernel, nb=batch, chunk=chunk, cps=cps),
        out_shape=jax.ShapeDtypeStruct((batch, seq, 4 * w), BF16),
        grid=(seq // seg,),
        in_specs=[pl.BlockSpec((batch, seg, 5 * w), lambda c: (0, c, 0))],
        out_specs=pl.BlockSpec((batch, seg, 4 * w), lambda c: (0, c, 0)),
        compiler_params=_cparams(("parallel",)),
        name="rwkv7_chunk",
    )(fa.reshape(batch, seq, 5 * w))


ML_QK = 0
ML_V = 2 * MLSTM_HEADS * MLSTM_QK
ML_O = ML_V + MLSTM_WIDTH
ML_I = ML_O + MLSTM_WIDTH
ML_F = ML_I + LANES


def _cummax_rows(x):
    n = x.shape[0]
    row = lax.broadcasted_iota(jnp.int32, x.shape, 0)
    sh = 1
    while sh < n:
        x = jnp.maximum(x, jnp.where(row >= sh, pltpu.roll(x, sh, 0), -jnp.inf))
        sh *= 2
    return x


def _mlstm_stages(x_ref, cw_ref, cb_ref, ib_ref, fb_ref, on_ref, o_ref,
                  prev_sc, c_sc, n_sc, m_sc, *, layer, nb, chunk):
    n = nb * chunk
    nh = MLSTM_HEADS
    dk = MLSTM_QK
    dv = MLSTM_V
    qkw = nh * dk
    vw = MLSTM_WIDTH
    cb_ref, ib_ref, fb_ref, on_ref = _rows(layer, cb_ref, ib_ref, fb_ref, on_ref)
    x = x_ref[...].reshape(n, MLSTM_IN)
    qk_raw = x[:, ML_QK:ML_V]
    prev = prev_sc[...]
    conv = cb_ref[...] + qk_raw * cw_ref[MLSTM_CONV - 1:MLSTM_CONV, :]
    for s in range(1, MLSTM_CONV):
        conv = conv + _shift_rows(qk_raw, prev, s, chunk) * cw_ref[MLSTM_CONV - 1 - s:MLSTM_CONV - s, :]
    prev_sc[...] = qk_raw
    qk = conv * _sigmoid(conv)
    q_all = qk[:, 0:qkw] * (dk ** -0.5)
    k_all = qk[:, qkw:]
    v_all = x[:, ML_V:ML_O]
    o_pre = x[:, ML_O:ML_I]
    li_all = x[:, ML_I:ML_F] + ib_ref[...]
    lf_all = _log_sigmoid(x[:, ML_F:ML_F + LANES] + fb_ref[...])

    ri = lax.broadcasted_iota(jnp.int32, (chunk, chunk), 0)
    ci = lax.broadcasted_iota(jnp.int32, (chunk, chunk), 1)
    causal = ci <= ri
    tri = jnp.where(causal, 1.0, 0.0)
    lane_k = lax.broadcasted_iota(jnp.int32, (chunk, qkw), 1)
    lane_v = lax.broadcasted_iota(jnp.int32, (chunk, vw), 1)
    rc = lax.broadcasted_iota(jnp.int32, (qkw, vw), 0)
    cc = lax.broadcasted_iota(jnp.int32, (qkw, vw), 1)
    cmask = _div(rc, dk) == _div(cc, dv)
    expand_v = jnp.where(rc == _div(cc, dv), 1.0, 0.0).astype(BF16)
    rk = lax.broadcasted_iota(jnp.int32, (qkw, qkw), 0)
    ck = lax.broadcasted_iota(jnp.int32, (qkw, qkw), 1)
    expand_k = jnp.where(rk == _div(ck, dk), 1.0, 0.0).astype(BF16)
    gather_k = jnp.where(_div(rk, dk) == ck, 1.0, 0.0).astype(BF16)

    bs = range(nb)
    sls = [slice(b * chunk, (b + 1) * chunk) for b in bs]
    q = [q_all[sl] for sl in sls]
    k = [k_all[sl] for sl in sls]
    k16 = [z.astype(BF16) for z in k]
    v = [v_all[sl] for sl in sls]
    li = [li_all[sl] for sl in sls]
    c_old = [c_sc[b] for b in bs]
    n_old = [n_sc[b] for b in bs]
    m_prev = [m_sc[b] for b in bs]
    g = [_exact_left_dot(tri, lf_all[sl]) for sl in sls]
    lig = [li[b] - g[b] for b in bs]
    inter_log = [g[b] + m_prev[b] for b in bs]
    m_t = [jnp.maximum(inter_log[b], g[b] + _cummax_rows(lig[b])) for b in bs]
    inter_w = [jnp.exp(inter_log[b] - m_t[b]) for b in bs]
    log2e = math.log2(math.e)
    gm = [(g[b] - m_t[b]) * log2e for b in bs]
    lig_t = [jnp.transpose(z * log2e) for z in lig]
    qn = [_exact_right_dot(q[b] * n_old[b], gather_k, parts=2) for b in bs]
    q_c = [_bdot(q[b], c_old[b]) for b in bs]
    ssum = [jnp.zeros((chunk, LANES), F32) for _ in bs]
    num = [jnp.zeros((chunk, vw), F32) for _ in bs]
    yield
    for h in range(nh):
        mk = (lane_k >= h * dk) & (lane_k < (h + 1) * dk)
        mv = (lane_v >= h * dv) & (lane_v < (h + 1) * dv)
        qk_h = [lax.dot_general(jnp.where(mk, q[b], 0.0).astype(BF16), k16[b], (((1,), (1,)), ((), ())),
                                preferred_element_type=F32) for b in bs]
        d = [jnp.broadcast_to(gm[b][:, h:h + 1], (chunk, chunk)) + lig_t[b][h:h + 1, :] for b in bs]
        s = [qk_h[b] * jnp.exp2(jnp.where(causal, d[b], -jnp.inf)) for b in bs]
        ssum = [jnp.where(lane_k == h, jnp.sum(s[b], axis=-1, keepdims=True), ssum[b]) for b in bs]
        num = [num[b] + _bdot(s[b], jnp.where(mv, v[b], 0.0)) for b in bs]
        yield
    den = [inter_w[b] * qn[b] + ssum[b] for b in bs]
    rden = [1.0 / jnp.maximum(jnp.abs(den[b]), jnp.exp(-m_t[b])) for b in bs]
    g_last = [g[b][chunk - 1:chunk, :] for b in bs]
    a_all = [g_last[b] - g[b] + li[b] for b in bs]
    m_new = [jnp.maximum(g_last[b] + m_prev[b], jnp.max(a_all[b], axis=0, keepdims=True)) for b in bs]
    dec = [jnp.exp(g_last[b] + m_prev[b] - m_new[b]) for b in bs]
    wts = [jnp.exp(a_all[b] - m_new[b]) for b in bs]
    per_head = [jnp.concatenate([inter_w[b], rden[b], wts[b], jnp.broadcast_to(dec[b], (8, LANES))], axis=0)
                for b in bs]
    pieces = [_split3(per_head[b])[:2] for b in bs]
    on_v = [sum(jnp.dot(p, expand_v, preferred_element_type=F32) for p in pieces[b]) for b in bs]
    on_k = [sum(jnp.dot(p[2 * chunk:], expand_k, preferred_element_type=F32) for p in pieces[b]) for b in bs]
    hs = [(on_v[b][0:chunk] * q_c[b] + num[b]) * on_v[b][chunk:2 * chunk] for b in bs]
    for b in bs:
        c_sc[b] = c_old[b] * on_v[b][3 * chunk:3 * chunk + 1] + jnp.where(
            cmask, _bdot_tn(k16[b], on_v[b][2 * chunk:3 * chunk] * v[b]), 0.0)
        n_sc[b] = n_old[b] * on_k[b][chunk:chunk + 1] + jnp.sum(on_k[b][0:chunk] * k[b], axis=0, keepdims=True)
        m_sc[b] = m_new[b]

    hh = jnp.concatenate(hs, axis=0)
    ms = _segsum(hh * hh, dv) * (1.0 / dv)
    out = hh * lax.rsqrt(ms + NORM_EPS) * on_ref[...] * _sigmoid(o_pre)
    o_ref[...] = out.astype(o_ref.dtype).reshape(o_ref.shape)


def _recurrent_kernel(x_ref, cw_ref, cb_ref, ib_ref, fb_ref, on_ref, fa_ref, fbk_ref, ft_ref, pc_ref, lnw_ref, lnb_ref,
                      om_ref, or_ref, prev_sc, c_sc, n_sc, m_sc, state_sc, *, layer, nb, chunk, rwkv_chunk, cps):
    @pl.when(pl.program_id(0) == 0)
    def _():
        for sc in (prev_sc, c_sc, n_sc, m_sc, state_sc):
            sc[...] = jnp.zeros(sc.shape, F32)

    scan = _rwkv_scan_stages(fa_ref, fbk_ref, ft_ref, pc_ref, lnw_ref, lnb_ref, or_ref, state_sc,
                             layer=layer, nb=nb, chunk=rwkv_chunk, cps=cps)
    mlstm = _mlstm_stages(x_ref, cw_ref, cb_ref, ib_ref, fb_ref, on_ref, om_ref, prev_sc, c_sc, n_sc, m_sc,
                          layer=layer, nb=nb, chunk=chunk)
    order = [scan, mlstm] + [scan, scan, mlstm] * (MLSTM_HEADS - 1) + [scan, mlstm]
    pending = {id(scan): 2 * cps + 1, id(mlstm): MLSTM_HEADS + 2}
    for g in order:
        next(g, None)
        pending[id(g)] -= 1
    for g in (scan, mlstm):
        for _ in range(pending[id(g)]):
            next(g, None)


def _recurrent(l, mlstm_in, fa, fb, ft, pc, batch, seq, cw, cb, ib, fbias, on, ln_w, ln_b):
    chunk = MLSTM_CHUNK
    rchunk = RWKV_CHUNK
    assert chunk % rchunk == 0
    w = RWKV_WIDTH
    x = mlstm_in.reshape(batch, seq, MLSTM_IN)
    fa = fa.reshape(batch, seq, 5 * w)
    fb = fb.reshape(batch, seq, 4 * w)
    pc = pc.reshape(batch, seq // rchunk, w)
    blk = lambda width: pl.BlockSpec((batch, chunk, width), lambda c: (0, c, 0))
    om, orw = pl.pallas_call(
        functools.partial(_recurrent_kernel, layer=l, nb=batch, chunk=chunk, rwkv_chunk=rchunk, cps=chunk // rchunk),
        out_shape=(jax.ShapeDtypeStruct((batch, seq, MLSTM_WIDTH), BF16),
                   jax.ShapeDtypeStruct((batch, seq, w), BF16)),
        grid=(seq // chunk,),
        in_specs=[blk(MLSTM_IN), _resident(cw, l), _resident(cb), _resident(ib), _resident(fbias), _resident(on),
                  blk(2 * w), blk(4 * w), blk(4 * w), _resident(pc), _resident(ln_w), _resident(ln_b)],
        out_specs=(blk(MLSTM_WIDTH), blk(w)),
        scratch_shapes=[pltpu.VMEM((batch * chunk, 2 * MLSTM_HEADS * MLSTM_QK), F32),
                        pltpu.VMEM((batch, MLSTM_HEADS * MLSTM_QK, MLSTM_WIDTH), F32),
                        pltpu.VMEM((batch, 1, MLSTM_HEADS * MLSTM_QK), F32),
                        pltpu.VMEM((batch, 1, LANES), F32),
                        pltpu.VMEM((batch, w, w), F32)],
        compiler_params=_cparams(("arbitrary",)),
        name="mlstm_rwkv_scan",
    )(x, cw, cb, ib, fbias, on, fa, fb, ft, pc, ln_w, ln_b)
    return om.reshape(batch * seq, MLSTM_WIDTH), orw.reshape(batch * seq, w)


def _ffn_kernel(x_ref, ya_ref, yb_ref, yc_ref, wo_ref, g_ref, wg_ref, wu_ref, wd_ref, fg_ref,
                o_ref, act_sc, *, layer, final_norm, tf):
    (g_ref,) = _rows(layer, g_ref)
    y = jnp.concatenate([ya_ref[...], yb_ref[...], yc_ref[...]], axis=-1)
    x1 = x_ref[...] + jnp.dot(y, wo_ref[...], preferred_element_type=F32)
    h = _rms(x1, g_ref[...]).astype(BF16)
    for c in range(D_FF // tf):
        gate = jnp.dot(h, wg_ref[:, c * tf:(c + 1) * tf], preferred_element_type=F32)
        up = jnp.dot(h, wu_ref[:, c * tf:(c + 1) * tf], preferred_element_type=F32)
        act_sc[:, c * tf:(c + 1) * tf] = (gate * _sigmoid(gate) * up).astype(BF16)
    out = x1 + jnp.dot(act_sc[...], wd_ref[...], preferred_element_type=F32)
    if final_norm:
        out = _rms(out, fg_ref[...])
    o_ref[...] = out


def _out_ffn(l, x, ya, yb, yc, wo, g, wg, wu, wd, fg, final_norm):
    t = x.shape[0]
    tm, tf = TM_FFN, TF_FFN
    row = lambda w: pl.BlockSpec((tm, w), lambda i: (i, 0))
    return pl.pallas_call(
        functools.partial(_ffn_kernel, layer=l, final_norm=final_norm, tf=tf),
        out_shape=jax.ShapeDtypeStruct((t, D_MODEL), F32),
        grid=(t // tm,),
        in_specs=[row(D_MODEL), row(MLA_WIDTH), row(RWKV_WIDTH), row(MLSTM_WIDTH), _resident(wo, l),
                  _resident(g), _resident(wg, l), _resident(wu, l), _resident(wd, l), _resident(fg)],
        out_specs=row(D_MODEL),
        scratch_shapes=[pltpu.VMEM((tm, D_FF), BF16)],
        compiler_params=_cparams(("parallel",)),
        name="out_ffn",
    )(x, ya, yb, yc, wo, g, wg, wu, wd, fg)


def _pad_cols(w, width):
    return jnp.pad(w, [(0, 0)] * (w.ndim - 1) + [(0, width - w.shape[-1])])


def _pad_rows(w, height):
    return jnp.pad(w, [(0, 0)] * (w.ndim - 2) + [(0, height - w.shape[-2]), (0, 0)])


def _rot_half_cols(w):
    half = w.shape[-1] // 2
    return jnp.concatenate([-w[..., half:], w[..., :half]], axis=-1)


def _stacked_weights(w_in, mla_w_uq, mla_w_ukv, rwkv_w2, rwkv_a2, rwkv_g2):
    depth = w_in.shape[0]
    wt = jnp.swapaxes(w_in, 1, 2)
    c_q, c_kv, k_pe = wt[:, 0:256], wt[:, 256:512], wt[:, 512:576]
    rw = wt[:, 576:1472]
    ml = wt[:, 1472:2248]
    k_pe_rot = jnp.swapaxes(_rot_half_cols(jnp.swapaxes(k_pe, 1, 2)), 1, 2)
    w_mla = jnp.concatenate([c_q, c_kv, _pad_rows(k_pe, LANES), _pad_rows(k_pe_rot, LANES)], axis=1)
    w_mlstm = jnp.concatenate([ml[:, 0:256], ml[:, 256:512], ml[:, 520:776],
                               _pad_rows(ml[:, 512:516], LANES), _pad_rows(ml[:, 516:520], LANES)], axis=1)
    w_all = jnp.concatenate([w_mla, rw, w_mlstm], axis=1).astype(BF16)

    uq = mla_w_uq.reshape(depth, MLA_Q_LORA, MLA_HEADS, MLA_NOPE + MLA_ROPE)
    nope = uq[..., :MLA_NOPE].reshape(depth, MLA_Q_LORA, MLA_HEADS * MLA_NOPE)
    pe = _pad_cols(uq[..., MLA_NOPE:], LANES).reshape(depth, MLA_Q_LORA, MLA_HEADS * LANES)
    per = _pad_cols(_rot_half_cols(uq[..., MLA_NOPE:]), LANES).reshape(depth, MLA_Q_LORA, MLA_HEADS * LANES)
    wq = jnp.concatenate([nope, pe, per], axis=-1).astype(BF16)
    ukv = mla_w_ukv.reshape(depth, MLA_KV_LORA, MLA_HEADS, MLA_NOPE + MLA_VDIM)
    wkv = jnp.concatenate([ukv[..., :MLA_NOPE].reshape(depth, MLA_KV_LORA, -1),
                           ukv[..., MLA_NOPE:].reshape(depth, MLA_KV_LORA, -1)], axis=-1).astype(BF16)

    assert RWKV_DECAY_LORA + RWKV_AAA_LORA + RWKV_GATE_LORA == LANES
    rows = lambda before, wt: jnp.pad(wt, ((0, 0), (before, LANES - before - wt.shape[1]), (0, 0))).astype(BF16)
    w2p = rows(0, rwkv_w2)
    a2p = rows(RWKV_DECAY_LORA, rwkv_a2)
    g2p = rows(RWKV_DECAY_LORA + RWKV_AAA_LORA, rwkv_g2)
    return w_all, wq, wkv, w2p, a2p, g2p


def kernel(x, positions, mix_norm, w_in, mla_q_norm, mla_w_uq, mla_kv_norm, mla_w_ukv, mla_out_norm, rwkv_mu, rwkv_w0, rwkv_w2, rwkv_a0, rwkv_a2, rwkv_g2, rwkv_k_k, rwkv_k_a, rwkv_r_k, rwkv_ln_w, rwkv_ln_b, mlstm_conv_w, mlstm_conv_b, mlstm_i_bias, mlstm_f_bias, mlstm_out_norm, w_out, ffn_norm, w_gate, w_up, w_down, final_norm):
    batch, seq, _ = x.shape
    depth = w_in.shape[0]
    xt = x.reshape(batch * seq, D_MODEL)
    cos, sin, _ = _rope_tables_and_casts(positions, ())
    w_all, wq, wkv, w2p, a2p, g2p = _stacked_weights(w_in, mla_w_uq, mla_w_ukv, rwkv_w2, rwkv_a2, rwkv_g2)
    ml_ib = _pad_cols(mlstm_i_bias, LANES)
    ml_fb = _pad_cols(mlstm_f_bias, LANES)
    for l in range(depth):
        q, k, v, fa, fb, pc, mlstm_in = _inproj(
            l, xt, seq, mix_norm, w_all, cos, sin, mla_q_norm, mla_kv_norm, wq, wkv,
            rwkv_mu, rwkv_w0, rwkv_a0, rwkv_k_k, rwkv_k_a, rwkv_r_k,
            w2p, a2p, g2p)
        y_mla, cast = _mla_attention(l, q, k, v, mla_out_norm, batch, seq,
                                     (w_out, w_gate, w_up, w_down) if l == 0 else ())
        if l == 0:
            wo, wg, wu, wd = cast
        ft = _rwkv_chunk(fa, batch, seq)
        y_mlstm, y_rwkv = _recurrent(l, mlstm_in, fa, fb, ft, pc, batch, seq, mlstm_conv_w, mlstm_conv_b, ml_ib, ml_fb,
                                     mlstm_out_norm, rwkv_ln_w, rwkv_ln_b)
        xt = _out_ffn(l, xt, y_mla, y_rwkv, y_mlstm, wo, ffn_norm, wg, wu, wd,
                      final_norm.reshape(1, -1), final_norm=(l == depth - 1))
    return xt.reshape(batch, seq, D_MODEL)
```

```python
import functools
import math

import jax
import jax.numpy as jnp
from jax import lax
from jax.experimental import pallas as pl
from jax.experimental.pallas import tpu as pltpu

F32 = jnp.float32
BF16 = jnp.bfloat16

D_MODEL = 1024
MLA_HEADS = 4
MLA_NOPE = 128
MLA_ROPE = 64
MLA_VDIM = 128
MLA_Q_LORA = 256
MLA_KV_LORA = 256
MLA_WIDTH = MLA_HEADS * MLA_VDIM
MLA_QK = 256
ROPE_THETA = 10000.0
RWKV_HEADS = 4
RWKV_HEAD = 64
RWKV_WIDTH = 256
RWKV_DECAY_LORA = 32
RWKV_AAA_LORA = 32
RWKV_GATE_LORA = 64
RWKV_IN = 3 * RWKV_WIDTH + 128
RWKV_LN_EPS = 64e-5
MLSTM_HEADS = 4
MLSTM_QK = 32
MLSTM_V = 64
MLSTM_WIDTH = 256
MLSTM_CONV = 4
MLSTM_IN = 1024
D_FF = 2816
NORM_EPS = 1e-6
LANES = 128
SUBLANES = 8

MLA_CQ = 0
MLA_CKV = MLA_CQ + MLA_Q_LORA
MLA_KPE = MLA_CKV + MLA_KV_LORA
MLA_KPER = MLA_KPE + LANES
MLA_IN = MLA_KPER + LANES

TM_INPROJ = 1024
TQ_ATTN = 512
TK_ATTN = 512
HP_ATTN = 4
RWKV_CHUNK = 64
RWKV_PREP_CHUNKS = 2
MLSTM_CHUNK = 256
TM_FFN = 1024
TF_FFN = 256
VMEM_LIMIT = 56 * 1024 * 1024


def _cparams(sem):
    return pltpu.CompilerParams(dimension_semantics=sem, vmem_limit_bytes=VMEM_LIMIT)


def _resident(a, layer=None):
    if layer is None:
        nd = a.ndim
        return pl.BlockSpec(a.shape, lambda *_: (0,) * nd, pipeline_mode=pl.Buffered(1))
    nd = a.ndim - 1
    return pl.BlockSpec((None,) + a.shape[1:], lambda *_: (layer,) + (0,) * nd, pipeline_mode=pl.Buffered(1))


def _rows(layer, *refs):
    return [r.at[layer:layer + 1] for r in refs]


def _bdot(a, b):
    return jnp.dot(a.astype(BF16), b.astype(BF16), preferred_element_type=F32)


def _dot_nt(a, b):
    return lax.dot_general(a, b, (((1,), (1,)), ((), ())), preferred_element_type=F32)


def _bdot_tn(a, b):
    return lax.dot_general(a.astype(BF16), b.astype(BF16), (((0,), (0,)), ((), ())),
                           preferred_element_type=F32)


def _split3(x):
    h = x.astype(BF16)
    r1 = x - h.astype(F32)
    m = r1.astype(BF16)
    lo = (r1 - m.astype(F32)).astype(BF16)
    return h, m, lo


def _exact_left_dot(sel, x):
    h, m, lo = _split3(x)
    s = sel.astype(BF16)
    return (jnp.dot(s, h, preferred_element_type=F32) + jnp.dot(s, m, preferred_element_type=F32)
            + jnp.dot(s, lo, preferred_element_type=F32))


def _exact_right_dot(x, sel, parts=3):
    pieces = _split3(x)[:parts]
    s = sel.astype(BF16)
    out = jnp.dot(pieces[0], s, preferred_element_type=F32)
    for p in pieces[1:]:
        out = out + jnp.dot(p, s, preferred_element_type=F32)
    return out


def _rms(x, g):
    return x * lax.rsqrt(jnp.mean(x * x, axis=-1, keepdims=True) + NORM_EPS) * g


def _sigmoid(x):
    return 1.0 / (1.0 + jnp.exp(-x))


def _log_sigmoid(x):
    return jnp.minimum(x, 0.0) - jnp.log1p(jnp.exp(-jnp.abs(x)))


def _div(x, d):
    assert d & (d - 1) == 0
    return lax.shift_right_logical(x, d.bit_length() - 1)


def _mod(x, d):
    assert d & (d - 1) == 0
    return lax.bitwise_and(x, d - 1)


def _shift_rows(x, prev, s, chunk):
    n = x.shape[0]
    row = lax.broadcasted_iota(jnp.int32, x.shape, 0)
    return jnp.where(_mod(row, chunk) >= s, pltpu.roll(x, s, 0), pltpu.roll(prev, n - chunk + s, 0))


def _rope_kernel(pos_ref, invf_ref, cos_ref, sin_ref):
    ang = pos_ref[...].astype(F32) * invf_ref[...]
    cos_ref[...] = jnp.cos(ang)
    sin_ref[...] = jnp.sin(ang)


def _rope_tables(positions):
    t = positions.size
    tm = min(1024, t)
    inv_freq = ROPE_THETA ** (-jnp.arange(0, MLA_ROPE, 2, dtype=F32) / MLA_ROPE)
    invf = jnp.tile(inv_freq, LANES // (MLA_ROPE // 2))[None, :]
    slab = lambda c: pl.BlockSpec((tm, c), lambda i: (i, 0))
    return pl.pallas_call(
        _rope_kernel,
        out_shape=(jax.ShapeDtypeStruct((t, LANES), F32), jax.ShapeDtypeStruct((t, LANES), F32)),
        grid=(t // tm,),
        in_specs=[slab(1), pl.BlockSpec((1, LANES), lambda i: (0, 0))],
        out_specs=(slab(LANES), slab(LANES)),
        compiler_params=_cparams(("parallel",)),
        name="rope_tables",
    )(positions.reshape(t, 1), invf)


(FA_RT, FA_V, FA_AT, FA_KT, FA_BT) = range(5)
(FB_KH, FB_BH, FB_BONUS, FB_GATE) = range(4)
(FT_TA, FT_UV, FT_YV, FT_ARB) = range(4)
RWKV_GROUP = 256


def _segsum(x, seg):
    assert 2 * seg == LANES and x.shape[1] % LANES == 0
    out = []
    for blk in range(x.shape[1] // LANES):
        xb = x[:, blk * LANES:(blk + 1) * LANES]
        low = lax.broadcasted_iota(jnp.int32, xb.shape, 1) < seg
        s_low = jnp.sum(jnp.where(low, xb, 0.0), axis=-1, keepdims=True)
        s_high = jnp.sum(jnp.where(low, 0.0, xb), axis=-1, keepdims=True)
        out.append(jnp.where(low, s_low, s_high))
    return jnp.concatenate(out, axis=-1)


def _rwkv_masks(chunk):
    w = RWKV_WIDTH
    r = lax.broadcasted_iota(jnp.int32, (w, w), 0)
    c = lax.broadcasted_iota(jnp.int32, (w, w), 1)
    return r, c, _div(r, chunk) == _div(c, chunk)


def _rwkv_features(xs, r0, prm, fa_ref, fb_ref, pc_ref, chunk):
    w0_ref, a0_ref, kk_ref, ka_ref, rk_ref, w2_ref, a2_ref, g2_ref = prm
    w = RWKV_WIDTH
    hd = RWKV_HEAD
    n = xs.shape[0]
    r = xs[:, 0:w]
    k = xs[:, w:2 * w]
    v = xs[:, 2 * w:3 * w]
    lor = xs[:, 3 * w:]
    ld = -math.exp(-0.5) * _sigmoid(w0_ref[...] + _bdot(jnp.tanh(lor), w2_ref[...]))
    a = _sigmoid(a0_ref[...] + _bdot(lor, a2_ref[...]))
    g = _bdot(_sigmoid(lor), g2_ref[...])
    kk = k * kk_ref[...]
    kk = kk / jnp.maximum(jnp.sqrt(_segsum(kk * kk, hd)), 1e-12)
    k2 = k * (1.0 + (a - 1.0) * ka_ref[...])
    kb = kk * a
    bonus = _segsum(r * k2 * rk_ref[...], hd) * v

    assert n <= RWKV_WIDTH
    ri, ci, same_chunk = _rwkv_masks(chunk)
    tri = jnp.where(same_chunk & (ci <= ri), 1.0, 0.0)[0:n, 0:n]
    cl = _exact_left_dot(tri, ld)
    units = n // chunk
    cl_last = jnp.concatenate(
        [jnp.broadcast_to(cl[(u + 1) * chunk - 1:(u + 1) * chunk, :], (chunk, w)) for u in range(units)], axis=0)
    e_neg = jnp.exp(-cl)
    e_end = jnp.exp(cl_last - cl)
    rows = slice(r0, r0 + n)

    def put(ref, sec, val):
        ref[rows, sec * w:(sec + 1) * w] = val.astype(ref.dtype)

    put(fa_ref, FA_RT, r * jnp.exp(cl))
    put(fa_ref, FA_V, v)
    put(fa_ref, FA_AT, kk * jnp.exp(cl - ld))
    put(fa_ref, FA_KT, k2 * e_neg)
    put(fa_ref, FA_BT, kb * e_neg)
    put(fb_ref, FB_KH, k2 * e_end)
    put(fb_ref, FB_BH, kb * e_end)
    put(fb_ref, FB_BONUS, bonus)
    put(fb_ref, FB_GATE, g)
    for u in range(units):
        c = r0 // chunk + u
        pc_ref[c:c + 1, :] = jnp.exp(cl_last[u * chunk:u * chunk + 1, :])


def _inproj_kernel(x_ref, g_ref, w_ref, cos_ref, sin_ref, qn_ref, kvn_ref, wq_ref, wkv_ref,
                   mu_ref, w0_ref, a0_ref, kk_ref, ka_ref, rk_ref, w2_ref, a2_ref, g2_ref,
                   q_ref, k_ref, v_ref, fa_ref, fb_ref, pc_ref, mlstm_ref, prev_sc, *, layer, chunk, tiles_per_seq):
    tm = x_ref.shape[0]
    g_ref, qn_ref, kvn_ref, mu_ref, w0_ref, a0_ref, kk_ref, ka_ref, rk_ref = _rows(
        layer, g_ref, qn_ref, kvn_ref, mu_ref, w0_ref, a0_ref, kk_ref, ka_ref, rk_ref)

    @pl.when(pl.program_id(0) % tiles_per_seq == 0)
    def _():
        prev_sc[...] = jnp.zeros(prev_sc.shape, F32)

    hb = _rms(x_ref[...], g_ref[...]).astype(BF16)
    rw = _dot_nt(hb, w_ref[MLA_IN:MLA_IN + RWKV_IN, :])
    row = lax.broadcasted_iota(jnp.int32, rw.shape, 0)
    shifted = jnp.where(row >= 1, pltpu.roll(rw, 1, 0), prev_sc[0:1, :])
    prev_sc[0:1, :] = rw[tm - 1:tm, :]
    xs = rw + (shifted - rw) * mu_ref[...]
    prm = (w0_ref, a0_ref, kk_ref, ka_ref, rk_ref, w2_ref, a2_ref, g2_ref)

    mla = _dot_nt(hb, w_ref[0:MLA_IN, :])
    groups = tm // RWKV_GROUP
    for gi in range(groups // 2):
        _rwkv_features(xs[gi * RWKV_GROUP:(gi + 1) * RWKV_GROUP], gi * RWKV_GROUP, prm, fa_ref, fb_ref, pc_ref, chunk)
    mlstm_ref[...] = _dot_nt(hb, w_ref[MLA_IN + RWKV_IN:, :])
    for gi in range(groups // 2, groups):
        _rwkv_features(xs[gi * RWKV_GROUP:(gi + 1) * RWKV_GROUP], gi * RWKV_GROUP, prm, fa_ref, fb_ref, pc_ref, chunk)

    cos = cos_ref[...]
    sin = sin_ref[...]
    scale = (MLA_NOPE + MLA_ROPE) ** -0.5 * math.log2(math.e)
    hw = MLA_HEADS * LANES
    cqn = _rms(mla[:, MLA_CQ:MLA_CKV], qn_ref[...]).astype(BF16)
    q = jnp.dot(cqn, wq_ref[...], preferred_element_type=F32)
    ckvn = _rms(mla[:, MLA_CKV:MLA_KPE], kvn_ref[...]).astype(BF16)
    kv = jnp.dot(ckvn, wkv_ref[...], preferred_element_type=F32)
    kp = (mla[:, MLA_KPE:MLA_KPER] * cos + mla[:, MLA_KPER:MLA_IN] * sin).astype(BF16)
    for h in range(MLA_HEADS):
        c0 = h * LANES
        pe = q[:, hw + c0:hw + c0 + LANES] * cos + q[:, 2 * hw + c0:2 * hw + c0 + LANES] * sin
        q_ref[:, h * MLA_QK:h * MLA_QK + LANES] = (q[:, c0:c0 + LANES] * scale).astype(BF16)
        q_ref[:, h * MLA_QK + LANES:(h + 1) * MLA_QK] = (pe * scale).astype(BF16)
        k_ref[:, h * MLA_QK:h * MLA_QK + LANES] = kv[:, c0:c0 + LANES].astype(BF16)
        k_ref[:, h * MLA_QK + LANES:(h + 1) * MLA_QK] = kp
    v_ref[...] = kv[:, hw:].astype(BF16)


def _inproj(l, x, seq, g, w, cos, sin, qn, kvn, wq, wkv, mu, w0, a0, k_k, k_a, r_k, w2p, a2p, g2p):
    t = x.shape[0]
    tm = TM_INPROJ
    chunk = RWKV_CHUNK
    assert seq % tm == 0 and tm % RWKV_GROUP == 0 and RWKV_GROUP % chunk == 0
    row = lambda width: pl.BlockSpec((tm, width), lambda i: (i, 0))
    rw = RWKV_WIDTH
    return pl.pallas_call(
        functools.partial(_inproj_kernel, layer=l, chunk=chunk, tiles_per_seq=seq // tm),
        out_shape=(jax.ShapeDtypeStruct((t, MLA_HEADS * MLA_QK), BF16),
                   jax.ShapeDtypeStruct((t, MLA_HEADS * MLA_QK), BF16),
                   jax.ShapeDtypeStruct((t, MLA_WIDTH), BF16),
                   jax.ShapeDtypeStruct((t, 5 * rw), BF16),
                   jax.ShapeDtypeStruct((t, 4 * rw), BF16),
                   jax.ShapeDtypeStruct((t // chunk, rw), F32),
                   jax.ShapeDtypeStruct((t, MLSTM_IN), F32)),
        grid=(t // tm,),
        in_specs=[row(D_MODEL), _resident(g), _resident(w, l), row(LANES), row(LANES), _resident(qn),
                  _resident(kvn), _resident(wq, l), _resident(wkv, l),
                  _resident(mu), _resident(w0), _resident(a0), _resident(k_k), _resident(k_a),
                  _resident(r_k), _resident(w2p, l), _resident(a2p, l), _resident(g2p, l)],
        out_specs=(row(MLA_HEADS * MLA_QK), row(MLA_HEADS * MLA_QK), row(MLA_WIDTH), row(5 * rw), row(4 * rw),
                   pl.BlockSpec((tm // chunk, rw), lambda i: (i, 0)), row(MLSTM_IN)),
        scratch_shapes=[pltpu.VMEM((SUBLANES, RWKV_IN), F32)],
        compiler_params=_cparams(("arbitrary",)),
        name="inproj",
    )(x, g, w, cos, sin, qn, kvn, wq, wkv, mu, w0, a0, k_k, k_a, r_k, w2p, a2p, g2p)


def _attn_kernel(q_ref, k_ref, v_ref, g_ref, *refs, layer, tq, tk, hp):
    n_cast = (len(refs) - 5) // 2
    w_refs, o_ref, wo_refs = refs[:n_cast], refs[n_cast], refs[n_cast + 1:2 * n_cast + 1]
    m_sc, acc_sc, sa_sc, sb_sc = refs[2 * n_cast + 1:]
    for w_ref, wo_ref in zip(w_refs, wo_refs):
        wo_ref[...] = w_ref[...].astype(BF16)
    i = pl.program_id(2)
    m_sc[...] = jnp.full(m_sc.shape, -jnp.inf, F32)
    acc_sc[...] = jnp.zeros(acc_sc.shape, F32)
    sub = m_sc.shape[1]
    hs = range(hp)
    ones = jnp.ones((acc_sc.shape[1] - MLA_VDIM, tk), BF16)
    (g_ref,) = _rows(layer, g_ref)

    def produce(j, s_ref):
        off = pl.multiple_of(j * tk, tk)
        for h in hs:
            s_ref[h] = lax.dot_general(k_ref[0, pl.ds(off, tk), h * MLA_QK:(h + 1) * MLA_QK],
                                       q_ref[0, :, h * MLA_QK:(h + 1) * MLA_QK],
                                       (((1,), (1,)), ((), ())), preferred_element_type=F32)

    def consume(j, s_ref, masked):
        off = pl.multiple_of(j * tk, tk)
        s = [s_ref[h] for h in hs]
        if masked:
            keys = lax.broadcasted_iota(jnp.int32, (tk, tq), 0)
            queries = lax.broadcasted_iota(jnp.int32, (tk, tq), 1)
            s = [jnp.where(keys <= queries, s[h], -jnp.inf) for h in hs]
        m_old = [m_sc[h][0:1] for h in hs]
        m_new = [jnp.maximum(m_old[h], jnp.max(s[h], axis=0, keepdims=True)) for h in hs]
        p = [jnp.exp2(s[h] - m_new[h]) for h in hs]
        alpha = [jnp.exp2(m_old[h] - m_new[h]) for h in hs]
        for h in hs:
            m_sc[h] = jnp.broadcast_to(m_new[h], (sub, tq))
        v1 = [jnp.concatenate([jnp.transpose(v_ref[0, pl.ds(off, tk), h * MLA_VDIM:(h + 1) * MLA_VDIM]), ones],
                              axis=0) for h in hs]
        pv = [jnp.dot(v1[h], p[h].astype(BF16), preferred_element_type=F32) for h in hs]
        for h in hs:
            acc_sc[h] = alpha[h] * acc_sc[h] + pv[h]

    def pair(jj, c):
        j = 2 * jj
        produce(j + 1, sb_sc)
        consume(j, sa_sc, False)
        produce(j + 2, sa_sc)
        consume(j + 1, sb_sc, False)
        return c

    produce(0, sa_sc)
    lax.fori_loop(0, i // 2, pair, 0)

    @pl.when(i % 2 == 0)
    def _():
        consume(i, sa_sc, True)

    @pl.when(i % 2 == 1)
    def _():
        produce(i, sb_sc)
        consume(i - 1, sa_sc, False)
        consume(i, sb_sc, True)

    for h in hs:
        o = acc_sc[h, 0:MLA_VDIM] / acc_sc[h, MLA_VDIM:MLA_VDIM + 1]
        o = o * lax.rsqrt(jnp.mean(o * o, axis=0, keepdims=True) + NORM_EPS)
        o_ref[0, :, h * MLA_VDIM:(h + 1) * MLA_VDIM] = (
            jnp.transpose(o) * g_ref[:, h * MLA_VDIM:(h + 1) * MLA_VDIM]).astype(o_ref.dtype)


def _mla_attention(l, q, k, v, g, batch, seq, weights=()):
    tq, tk, hp = TQ_ATTN, TK_ATTN, HP_ATTN
    assert tq == tk and MLA_VDIM == LANES and hp == MLA_HEADS
    q = q.reshape(batch, seq, MLA_HEADS * MLA_QK)
    k = k.reshape(batch, seq, MLA_HEADS * MLA_QK)
    v = v.reshape(batch, seq, MLA_WIDTH)
    nh, nq = MLA_HEADS // hp, seq // tq
    steps = batch * nh * nq
    flat = [w.reshape(-1, w.shape[-1]) for w in weights]
    rows = [f.shape[0] // steps for f in flat]
    assert all(f.shape[0] % steps == 0 and r % 16 == 0 for f, r in zip(flat, rows))
    slabs = [pl.BlockSpec((r, f.shape[1]), lambda b, h, i: ((b * nh + h) * nq + i, 0)) for f, r in zip(flat, rows)]
    outs = pl.pallas_call(
        functools.partial(_attn_kernel, layer=l, tq=tq, tk=tk, hp=hp),
        out_shape=(jax.ShapeDtypeStruct((batch, seq, MLA_WIDTH), BF16),)
        + tuple(jax.ShapeDtypeStruct(f.shape, BF16) for f in flat),
        grid=(batch, nh, nq),
        in_specs=[pl.BlockSpec((1, tq, hp * MLA_QK), lambda b, h, i: (b, i, h)),
                  pl.BlockSpec((1, seq, hp * MLA_QK), lambda b, h, i: (b, 0, h)),
                  pl.BlockSpec((1, seq, hp * MLA_VDIM), lambda b, h, i: (b, 0, h)),
                  _resident(g)] + slabs,
        out_specs=(pl.BlockSpec((1, tq, hp * MLA_VDIM), lambda b, h, i: (b, i, h)),) + tuple(slabs),
        scratch_shapes=[pltpu.VMEM((hp, SUBLANES, tq), F32),
                        pltpu.VMEM((hp, MLA_VDIM + 2 * SUBLANES, tq), F32),
                        pltpu.VMEM((hp, tk, tq), F32), pltpu.VMEM((hp, tk, tq), F32)],
        compiler_params=_cparams(("parallel", "parallel", "arbitrary")),
        name="mla_attention",
    )(q, k, v, g, *flat)
    return outs[0].reshape(batch * seq, MLA_WIDTH), [o.reshape(w.shape) for o, w in zip(outs[1:], weights)]


def _tile_heads(z):
    return jnp.concatenate([z] * RWKV_HEADS, axis=0)


def _rwkv_chunk_kernel(fa_ref, ft_ref, *, nb, chunk, cps):
    seg = cps * chunk
    n = nb * seg
    w = RWKV_WIDTH
    hd = RWKV_HEAD
    x = fa_ref[...].reshape(n, fa_ref.shape[-1])
    r16, v16, a16, k16, b16 = (x[:, s * w:(s + 1) * w] for s in (FA_RT, FA_V, FA_AT, FA_KT, FA_BT))

    ri, ci, same_chunk = _rwkv_masks(chunk)
    bd = _div(ri, chunk) == _div(ci, hd)
    rt = lax.broadcasted_iota(jnp.int32, (chunk, w), 0)
    cs = _mod(lax.broadcasted_iota(jnp.int32, (chunk, w), 1), chunk)
    strict = cs < rt
    incl = cs <= rt
    c16 = _div(rt, 16) == _div(cs, 16)
    c32 = _div(rt, 32) == _div(cs, 32)
    eye = jnp.where(rt == cs, 1.0, 0.0)
    units = nb * cps

    def block_diag(z):
        return jnp.where(same_chunk, _tile_heads(z), 0.0)

    def put(u, sec, val):
        b, j = divmod(u, cps)
        ft_ref[b, j * chunk:(j + 1) * chunk, sec * w:(sec + 1) * w] = val.astype(ft_ref.dtype)

    def mm(x, y):
        return jnp.dot(x, y, preferred_element_type=F32)

    us = range(units)
    sls = [slice(u * chunk, (u + 1) * chunk) for u in us]
    a_st = [jnp.where(bd, _tile_heads(a16[sl]), 0.0) for sl in sls]
    v_st = [jnp.where(bd, _tile_heads(v16[sl]), 0.0) for sl in sls]
    kb_t = [jnp.concatenate([jnp.transpose(jnp.where(bd, _tile_heads(k16[sl]), 0.0)),
                             jnp.transpose(jnp.where(bd, _tile_heads(b16[sl]), 0.0))], axis=1) for sl in sls]
    sc = [mm(jnp.concatenate([a16[sls[u]], r16[sls[u]]], axis=0), kb_t[u]) for u in us]
    l_ab = [jnp.where(strict, sc[u][0:chunk, w:], 0.0) for u in us]
    l_ak = [jnp.where(strict, sc[u][0:chunk, 0:w], 0.0).astype(BF16) for u in us]
    a_rk = [jnp.where(incl, sc[u][chunk:, 0:w], 0.0).astype(BF16) for u in us]
    for u in us:
        put(u, FT_ARB, jnp.where(incl, sc[u][chunk:, w:], 0.0))
    xm = [-jnp.where(c16, l_ab[u], 0.0) for u in us]
    xm16 = [z.astype(BF16) for z in xm]
    off32 = [block_diag(jnp.where(c32 & jnp.logical_not(c16), l_ab[u], 0.0).astype(BF16)) for u in us]
    off64 = [block_diag(jnp.where(jnp.logical_not(c32), l_ab[u], 0.0).astype(BF16)) for u in us]
    x2 = [mm(xm16[u], block_diag(xm16[u])).astype(BF16) for u in us]
    x2_bd = [block_diag(z) for z in x2]
    lv = [mm(jnp.concatenate([l_ak[u], a_rk[u]], axis=0), v_st[u]) for u in us]
    wv = [block_diag(lv[u][0:chunk].astype(BF16)) for u in us]
    t_lo = [eye + xm[u] for u in us]
    tx = [mm(jnp.concatenate([t_lo[u].astype(BF16), x2[u]], axis=0), x2_bd[u]) for u in us]
    t_lo = [(t_lo[u] + tx[u][0:chunk]).astype(BF16) for u in us]
    x4 = [tx[u][chunk:] for u in us]
    x4b = [z.astype(BF16) for z in x4]
    x4_bd = [block_diag(z) for z in x4b]
    for u in us:
        put(u, FT_YV, lv[u][chunk:])
    x8_bd = [block_diag(mm(x4b[u], x4_bd[u]).astype(BF16)) for u in us]
    t_hi = [eye + x4[u] for u in us]
    t_hi = [block_diag((t_hi[u] + mm(t_hi[u].astype(BF16), x8_bd[u])).astype(BF16)) for u in us]
    t_inv = [mm(t_lo[u], t_hi[u]) for u in us]
    for off in (off32, off64):
        tb = [z.astype(BF16) for z in t_inv]
        mid = [mm(tb[u], off[u]).astype(BF16) for u in us]
        t_inv = [t_inv[u] - mm(mid[u], block_diag(tb[u])) for u in us]
    tb = [z.astype(BF16) for z in t_inv]
    for u in us:
        put(u, FT_TA, mm(tb[u], a_st[u]))
    for u in us:
        put(u, FT_UV, mm(tb[u], wv[u]))


def _rwkv_scan_stages(fa_ref, fb_ref, ft_ref, pc_ref, lnw_ref, lnb_ref, o_ref, state_sc, *, layer, nb, chunk, cps):
    seg = cps * chunk
    n = nb * seg
    w = RWKV_WIDTH
    hd = RWKV_HEAD
    lnw_ref, lnb_ref = _rows(layer, lnw_ref, lnb_ref)
    ri, ci, _ = _rwkv_masks(chunk)
    bd = _div(ri, chunk) == _div(ci, hd)
    bdv = _div(ri, hd) == _div(ci, hd)

    def sec(ref, b, j, s):
        return ref[b, j * chunk:(j + 1) * chunk, s * w:(s + 1) * w]

    ys = [[None] * cps for _ in range(nb)]
    bs = range(nb)
    for j in range(cps):
        gs = [state_sc[b] for b in bs]
        p1 = [lax.dot_general(jnp.concatenate([sec(ft_ref, b, j, FT_TA), sec(fa_ref, b, j, FA_RT)], axis=0),
                              gs[b].astype(BF16), (((1,), (1,)), ((), ())), preferred_element_type=F32)
              for b in bs]
        yield
        u = [(p1[b][0:chunk] + sec(ft_ref, b, j, FT_UV).astype(F32)).astype(BF16) for b in bs]
        upd = [lax.dot_general(jnp.concatenate([sec(fa_ref, b, j, FA_V), -u[b]], axis=0),
                               jnp.concatenate([sec(fb_ref, b, j, FB_KH), sec(fb_ref, b, j, FB_BH)], axis=0),
                               (((0,), (0,)), ((), ())), preferred_element_type=F32) for b in bs]
        for b in bs:
            state_sc[b] = gs[b] * pc_ref[b, pl.ds(pl.program_id(0) * cps + j, 1), :] + jnp.where(bdv, upd[b], 0.0)
        for b in bs:
            u_st = jnp.where(bd, _tile_heads(u[b]), 0.0)
            ys[b][j] = (p1[b][chunk:] + sec(ft_ref, b, j, FT_YV).astype(F32)
                        - jnp.dot(sec(ft_ref, b, j, FT_ARB), u_st, preferred_element_type=F32))
        yield

    y = jnp.concatenate([ys[b][j] for b in range(nb) for j in range(cps)], axis=0)
    mean = _segsum(y, hd) * (1.0 / hd)
    d = y - mean
    var = _segsum(d * d, hd) * (1.0 / hd)
    yn = d * lax.rsqrt(var + RWKV_LN_EPS) * lnw_ref[...] + lnb_ref[...]
    bonus = fb_ref[:, :, FB_BONUS * w:(FB_BONUS + 1) * w].reshape(n, w).astype(F32)
    gate = fb_ref[:, :, FB_GATE * w:(FB_GATE + 1) * w].reshape(n, w).astype(F32)
    o_ref[...] = ((yn + bonus) * gate).astype(o_ref.dtype).reshape(o_ref.shape)


def _rwkv_chunk(fa, batch, seq):
    chunk = RWKV_CHUNK
    w = RWKV_WIDTH
    assert RWKV_HEADS * chunk == w
    cps = RWKV_PREP_CHUNKS
    seg = cps * chunk
    return pl.pallas_call(
        functools.partial(_rwkv_chunk_kernel, nb=batch, chunk=chunk, cps=cps),
        out_shape=jax.ShapeDtypeStruct((batch, seq, 4 * w), BF16),
        grid=(seq // seg,),
        in_specs=[pl.BlockSpec((batch, seg, 5 * w), lambda c: (0, c, 0))],
        out_specs=pl.BlockSpec((batch, seg, 4 * w), lambda c: (0, c, 0)),
        compiler_params=_cparams(("parallel",)),
        name="rwkv7_chunk",
    )(fa.reshape(batch, seq, 5 * w))


ML_QK = 0
ML_V = 2 * MLSTM_HEADS * MLSTM_QK
ML_O = ML_V + MLSTM_WIDTH
ML_I = ML_O + MLSTM_WIDTH
ML_F = ML_I + LANES


def _cummax_rows(x):
    n = x.shape[0]
    row = lax.broadcasted_iota(jnp.int32, x.shape, 0)
    sh = 1
    while sh < n:
        x = jnp.maximum(x, jnp.where(row >= sh, pltpu.roll(x, sh, 0), -jnp.inf))
        sh *= 2
    return x


def _mlstm_stages(x_ref, cw_ref, cb_ref, ib_ref, fb_ref, on_ref, o_ref,
                  prev_sc, c_sc, n_sc, m_sc, *, layer, nb, chunk):
    n = nb * chunk
    nh = MLSTM_HEADS
    dk = MLSTM_QK
    dv = MLSTM_V
    qkw = nh * dk
    vw = MLSTM_WIDTH
    cb_ref, ib_ref, fb_ref, on_ref = _rows(layer, cb_ref, ib_ref, fb_ref, on_ref)
    x = x_ref[...].reshape(n, MLSTM_IN)
    qk_raw = x[:, ML_QK:ML_V]
    prev = prev_sc[...]
    conv = cb_ref[...] + qk_raw * cw_ref[MLSTM_CONV - 1:MLSTM_CONV, :]
    for s in range(1, MLSTM_CONV):
        conv = conv + _shift_rows(qk_raw, prev, s, chunk) * cw_ref[MLSTM_CONV - 1 - s:MLSTM_CONV - s, :]
    prev_sc[...] = qk_raw
    qk = conv * _sigmoid(conv)
    q_all = qk[:, 0:qkw] * (dk ** -0.5)
    k_all = qk[:, qkw:]
    v_all = x[:, ML_V:ML_O]
    o_pre = x[:, ML_O:ML_I]
    li_all = x[:, ML_I:ML_F] + ib_ref[...]
    lf_all = _log_sigmoid(x[:, ML_F:ML_F + LANES] + fb_ref[...])

    ri = lax.broadcasted_iota(jnp.int32, (chunk, chunk), 0)
    ci = lax.broadcasted_iota(jnp.int32, (chunk, chunk), 1)
    causal = ci <= ri
    tri = jnp.where(causal, 1.0, 0.0)
    lane_k = lax.broadcasted_iota(jnp.int32, (chunk, qkw), 1)
    lane_v = lax.broadcasted_iota(jnp.int32, (chunk, vw), 1)
    rc = lax.broadcasted_iota(jnp.int32, (qkw, vw), 0)
    cc = lax.broadcasted_iota(jnp.int32, (qkw, vw), 1)
    cmask = _div(rc, dk) == _div(cc, dv)
    expand_v = jnp.where(rc == _div(cc, dv), 1.0, 0.0).astype(BF16)
    rk = lax.broadcasted_iota(jnp.int32, (qkw, qkw), 0)
    ck = lax.broadcasted_iota(jnp.int32, (qkw, qkw), 1)
    expand_k = jnp.where(rk == _div(ck, dk), 1.0, 0.0).astype(BF16)
    gather_k = jnp.where(_div(rk, dk) == ck, 1.0, 0.0).astype(BF16)

    bs = range(nb)
    sls = [slice(b * chunk, (b + 1) * chunk) for b in bs]
    q = [q_all[sl] for sl in sls]
    k = [k_all[sl] for sl in sls]
    k16 = [z.astype(BF16) for z in k]
    v = [v_all[sl] for sl in sls]
    li = [li_all[sl] for sl in sls]
    c_old = [c_sc[b] for b in bs]
    n_old = [n_sc[b] for b in bs]
    m_prev = [m_sc[b] for b in bs]
    g = [_exact_left_dot(tri, lf_all[sl]) for sl in sls]
    lig = [li[b] - g[b] for b in bs]
    inter_log = [g[b] + m_prev[b] for b in bs]
    m_t = [jnp.maximum(inter_log[b], g[b] + _cummax_rows(lig[b])) for b in bs]
    inter_w = [jnp.exp(inter_log[b] - m_t[b]) for b in bs]
    log2e = math.log2(math.e)
    gm = [(g[b] - m_t[b]) * log2e for b in bs]
    lig_t = [jnp.transpose(z * log2e) for z in lig]
    qn = [_exact_right_dot(q[b] * n_old[b], gather_k, parts=2) for b in bs]
    q_c = [_bdot(q[b], c_old[b]) for b in bs]
    ssum = [jnp.zeros((chunk, LANES), F32) for _ in bs]
    num = [jnp.zeros((chunk, vw), F32) for _ in bs]
    yield
    for h in range(nh):
        mk = (lane_k >= h * dk) & (lane_k < (h + 1) * dk)
        mv = (lane_v >= h * dv) & (lane_v < (h + 1) * dv)
        qk_h = [lax.dot_general(jnp.where(mk, q[b], 0.0).astype(BF16), k16[b], (((1,), (1,)), ((), ())),
                                preferred_element_type=F32) for b in bs]
        d = [jnp.broadcast_to(gm[b][:, h:h + 1], (chunk, chunk)) + lig_t[b][h:h + 1, :] for b in bs]
        s = [qk_h[b] * jnp.exp2(jnp.where(causal, d[b], -jnp.inf)) for b in bs]
        ssum = [jnp.where(lane_k == h, jnp.sum(s[b], axis=-1, keepdims=True), ssum[b]) for b in bs]
        num = [num[b] + _bdot(s[b], jnp.where(mv, v[b], 0.0)) for b in bs]
        yield
    den = [inter_w[b] * qn[b] + ssum[b] for b in bs]
    rden = [1.0 / jnp.maximum(jnp.abs(den[b]), jnp.exp(-m_t[b])) for b in bs]
    g_last = [g[b][chunk - 1:chunk, :] for b in bs]
    a_all = [g_last[b] - g[b] + li[b] for b in bs]
    m_new = [jnp.maximum(g_last[b] + m_prev[b], jnp.max(a_all[b], axis=0, keepdims=True)) for b in bs]
    dec = [jnp.exp(g_last[b] + m_prev[b] - m_new[b]) for b in bs]
    wts = [jnp.exp(a_all[b] - m_new[b]) for b in bs]
    per_head = [jnp.concatenate([inter_w[b], rden[b], wts[b], jnp.broadcast_to(dec[b], (8, LANES))], axis=0)
                for b in bs]
    pieces = [_split3(per_head[b])[:2] for b in bs]
    on_v = [sum(jnp.dot(p, expand_v, preferred_element_type=F32) for p in pieces[b]) for b in bs]
    on_k = [sum(jnp.dot(p[2 * chunk:], expand_k, preferred_element_type=F32) for p in pieces[b]) for b in bs]
    hs = [(on_v[b][0:chunk] * q_c[b] + num[b]) * on_v[b][chunk:2 * chunk] for b in bs]
    for b in bs:
        c_sc[b] = c_old[b] * on_v[b][3 * chunk:3 * chunk + 1] + jnp.where(
            cmask, _bdot_tn(k16[b], on_v[b][2 * chunk:3 * chunk] * v[b]), 0.0)
        n_sc[b] = n_old[b] * on_k[b][chunk:chunk + 1] + jnp.sum(on_k[b][0:chunk] * k[b], axis=0, keepdims=True)
        m_sc[b] = m_new[b]

    hh = jnp.concatenate(hs, axis=0)
    ms = _segsum(hh * hh, dv) * (1.0 / dv)
    out = hh * lax.rsqrt(ms + NORM_EPS) * on_ref[...] * _sigmoid(o_pre)
    o_ref[...] = out.astype(o_ref.dtype).reshape(o_ref.shape)


def _recurrent_kernel(x_ref, cw_ref, cb_ref, ib_ref, fb_ref, on_ref, fa_ref, fbk_ref, ft_ref, pc_ref, lnw_ref, lnb_ref,
                      om_ref, or_ref, prev_sc, c_sc, n_sc, m_sc, state_sc, *, layer, nb, chunk, rwkv_chunk, cps):
    @pl.when(pl.program_id(0) == 0)
    def _():
        for sc in (prev_sc, c_sc, n_sc, m_sc, state_sc):
            sc[...] = jnp.zeros(sc.shape, F32)

    scan = _rwkv_scan_stages(fa_ref, fbk_ref, ft_ref, pc_ref, lnw_ref, lnb_ref, or_ref, state_sc,
                             layer=layer, nb=nb, chunk=rwkv_chunk, cps=cps)
    mlstm = _mlstm_stages(x_ref, cw_ref, cb_ref, ib_ref, fb_ref, on_ref, om_ref, prev_sc, c_sc, n_sc, m_sc,
                          layer=layer, nb=nb, chunk=chunk)
    order = [scan, mlstm] + [scan, scan, mlstm] * (MLSTM_HEADS - 1) + [scan, mlstm]
    pending = {id(scan): 2 * cps + 1, id(mlstm): MLSTM_HEADS + 2}
    for g in order:
        next(g, None)
        pending[id(g)] -= 1
    for g in (scan, mlstm):
        for _ in range(pending[id(g)]):
            next(g, None)


def _recurrent(l, mlstm_in, fa, fb, ft, pc, batch, seq, cw, cb, ib, fbias, on, ln_w, ln_b):
    chunk = MLSTM_CHUNK
    rchunk = RWKV_CHUNK
    assert chunk % rchunk == 0
    w = RWKV_WIDTH
    x = mlstm_in.reshape(batch, seq, MLSTM_IN)
    fa = fa.reshape(batch, seq, 5 * w)
    fb = fb.reshape(batch, seq, 4 * w)
    pc = pc.reshape(batch, seq // rchunk, w)
    blk = lambda width: pl.BlockSpec((batch, chunk, width), lambda c: (0, c, 0))
    om, orw = pl.pallas_call(
        functools.partial(_recurrent_kernel, layer=l, nb=batch, chunk=chunk, rwkv_chunk=rchunk, cps=chunk // rchunk),
        out_shape=(jax.ShapeDtypeStruct((batch, seq, MLSTM_WIDTH), BF16),
                   jax.ShapeDtypeStruct((batch, seq, w), BF16)),
        grid=(seq // chunk,),
        in_specs=[blk(MLSTM_IN), _resident(cw, l), _resident(cb), _resident(ib), _resident(fbias), _resident(on),
                  blk(2 * w), blk(4 * w), blk(4 * w), _resident(pc), _resident(ln_w), _resident(ln_b)],
        out_specs=(blk(MLSTM_WIDTH), blk(w)),
        scratch_shapes=[pltpu.VMEM((batch * chunk, 2 * MLSTM_HEADS * MLSTM_QK), F32),
                        pltpu.VMEM((batch, MLSTM_HEADS * MLSTM_QK, MLSTM_WIDTH), F32),
                        pltpu.VMEM((batch, 1, MLSTM_HEADS * MLSTM_QK), F32),
                        pltpu.VMEM((batch, 1, LANES), F32),
                        pltpu.VMEM((batch, w, w), F32)],
        compiler_params=_cparams(("arbitrary",)),
        name="mlstm_rwkv_scan",
    )(x, cw, cb, ib, fbias, on, fa, fb, ft, pc, ln_w, ln_b)
    return om.reshape(batch * seq, MLSTM_WIDTH), orw.reshape(batch * seq, w)


def _ffn_kernel(x_ref, ya_ref, yb_ref, yc_ref, wo_ref, g_ref, wg_ref, wu_ref, wd_ref, fg_ref,
                o_ref, act_sc, *, layer, final_norm, tf):
    (g_ref,) = _rows(layer, g_ref)
    y = jnp.concatenate([ya_ref[...], yb_ref[...], yc_ref[...]], axis=-1)
    x1 = x_ref[...] + jnp.dot(y, wo_ref[...], preferred_element_type=F32)
    h = _rms(x1, g_ref[...]).astype(BF16)
    for c in range(D_FF // tf):
        gate = jnp.dot(h, wg_ref[:, c * tf:(c + 1) * tf], preferred_element_type=F32)
        up = jnp.dot(h, wu_ref[:, c * tf:(c + 1) * tf], preferred_element_type=F32)
        act_sc[:, c * tf:(c + 1) * tf] = (gate * _sigmoid(gate) * up).astype(BF16)
    out = x1 + jnp.dot(act_sc[...], wd_ref[...], preferred_element_type=F32)
    if final_norm:
        out = _rms(out, fg_ref[...])
    o_ref[...] = out


def _out_ffn(l, x, ya, yb, yc, wo, g, wg, wu, wd, fg, final_norm):
    t = x.shape[0]
    tm, tf = TM_FFN, TF_FFN
    row = lambda w: pl.BlockSpec((tm, w), lambda i: (i, 0))
    return pl.pallas_call(
        functools.partial(_ffn_kernel, layer=l, final_norm=final_norm, tf=tf),
        out_shape=jax.ShapeDtypeStruct((t, D_MODEL), F32),
        grid=(t // tm,),
        in_specs=[row(D_MODEL), row(MLA_WIDTH), row(RWKV_WIDTH), row(MLSTM_WIDTH), _resident(wo, l),
                  _resident(g), _resident(wg, l), _resident(wu, l), _resident(wd, l), _resident(fg)],
        out_specs=row(D_MODEL),
        scratch_shapes=[pltpu.VMEM((tm, D_FF), BF16)],
        compiler_params=_cparams(("parallel",)),
        name="out_ffn",
    )(x, ya, yb, yc, wo, g, wg, wu, wd, fg)


def _pad_cols(w, width):
    return jnp.pad(w, [(0, 0)] * (w.ndim - 1) + [(0, width - w.shape[-1])])


def _pad_rows(w, height):
    return jnp.pad(w, [(0, 0)] * (w.ndim - 2) + [(0, height - w.shape[-2]), (0, 0)])


def _rot_half_cols(w):
    half = w.shape[-1] // 2
    return jnp.concatenate([-w[..., half:], w[..., :half]], axis=-1)


def _stacked_weights(w_in, mla_w_uq, mla_w_ukv, rwkv_w2, rwkv_a2, rwkv_g2):
    depth = w_in.shape[0]
    wt = jnp.swapaxes(w_in, 1, 2)
    c_q, c_kv, k_pe = wt[:, 0:256], wt[:, 256:512], wt[:, 512:576]
    rw = wt[:, 576:1472]
    ml = wt[:, 1472:2248]
    k_pe_rot = jnp.swapaxes(_rot_half_cols(jnp.swapaxes(k_pe, 1, 2)), 1, 2)
    w_mla = jnp.concatenate([c_q, c_kv, _pad_rows(k_pe, LANES), _pad_rows(k_pe_rot, LANES)], axis=1)
    w_mlstm = jnp.concatenate([ml[:, 0:256], ml[:, 256:512], ml[:, 520:776],
                               _pad_rows(ml[:, 512:516], LANES), _pad_rows(ml[:, 516:520], LANES)], axis=1)
    w_all = jnp.concatenate([w_mla, rw, w_mlstm], axis=1).astype(BF16)

    uq = mla_w_uq.reshape(depth, MLA_Q_LORA, MLA_HEADS, MLA_NOPE + MLA_ROPE)
    nope = uq[..., :MLA_NOPE].reshape(depth, MLA_Q_LORA, MLA_HEADS * MLA_NOPE)
    pe = _pad_cols(uq[..., MLA_NOPE:], LANES).reshape(depth, MLA_Q_LORA, MLA_HEADS * LANES)
    per = _pad_cols(_rot_half_cols(uq[..., MLA_NOPE:]), LANES).reshape(depth, MLA_Q_LORA, MLA_HEADS * LANES)
    wq = jnp.concatenate([nope, pe, per], axis=-1).astype(BF16)
    ukv = mla_w_ukv.reshape(depth, MLA_KV_LORA, MLA_HEADS, MLA_NOPE + MLA_VDIM)
    wkv = jnp.concatenate([ukv[..., :MLA_NOPE].reshape(depth, MLA_KV_LORA, -1),
                           ukv[..., MLA_NOPE:].reshape(depth, MLA_KV_LORA, -1)], axis=-1).astype(BF16)

    assert RWKV_DECAY_LORA + RWKV_AAA_LORA + RWKV_GATE_LORA == LANES
    rows = lambda before, wt: jnp.pad(wt, ((0, 0), (before, LANES - before - wt.shape[1]), (0, 0))).astype(BF16)
    w2p = rows(0, rwkv_w2)
    a2p = rows(RWKV_DECAY_LORA, rwkv_a2)
    g2p = rows(RWKV_DECAY_LORA + RWKV_AAA_LORA, rwkv_g2)
    return w_all, wq, wkv, w2p, a2p, g2p


def kernel(x, positions, mix_norm, w_in, mla_q_norm, mla_w_uq, mla_kv_norm, mla_w_ukv, mla_out_norm, rwkv_mu, rwkv_w0, rwkv_w2, rwkv_a0, rwkv_a2, rwkv_g2, rwkv_k_k, rwkv_k_a, rwkv_r_k, rwkv_ln_w, rwkv_ln_b, mlstm_conv_w, mlstm_conv_b, mlstm_i_bias, mlstm_f_bias, mlstm_out_norm, w_out, ffn_norm, w_gate, w_up, w_down, final_norm):
    batch, seq, _ = x.shape
    depth = w_in.shape[0]
    xt = x.reshape(batch * seq, D_MODEL)
    cos, sin = _rope_tables(positions)
    w_all, wq, wkv, w2p, a2p, g2p = _stacked_weights(w_in, mla_w_uq, mla_w_ukv, rwkv_w2, rwkv_a2, rwkv_g2)
    ml_ib = _pad_cols(mlstm_i_bias, LANES)
    ml_fb = _pad_cols(mlstm_f_bias, LANES)
    for l in range(depth):
        q, k, v, fa, fb, pc, mlstm_in = _inproj(
            l, xt, seq, mix_norm, w_all, cos, sin, mla_q_norm, mla_kv_norm, wq, wkv,
            rwkv_mu, rwkv_w0, rwkv_a0, rwkv_k_k, rwkv_k_a, rwkv_r_k,
            w2p, a2p, g2p)
        y_mla, cast = _mla_attention(l, q, k, v, mla_out_norm, batch, seq,
                                     (w_out, w_gate, w_up, w_down) if l == 0 else ())
        if l == 0:
            wo, wg, wu, wd = cast
        ft = _rwkv_chunk(fa, batch, seq)
        y_mlstm, y_rwkv = _recurrent(l, mlstm_in, fa, fb, ft, pc, batch, seq, mlstm_conv_w, mlstm_conv_b, ml_ib, ml_fb,
                                     mlstm_out_norm, rwkv_ln_w, rwkv_ln_b)
        xt = _out_ffn(l, xt, y_mla, y_rwkv, y_mlstm, wo, ffn_norm, wg, wu, wd,
                      final_norm.reshape(1, -1), final_norm=(l == depth - 1))
    return xt.reshape(batch, seq, D_MODEL)
```

```python
import functools
import math

import jax
import jax.numpy as jnp
from jax import lax
from jax.experimental import pallas as pl
from jax.experimental.pallas import tpu as pltpu

F32 = jnp.float32
BF16 = jnp.bfloat16

D_MODEL = 1024
MLA_HEADS = 4
MLA_NOPE = 128
MLA_ROPE = 64
MLA_VDIM = 128
MLA_Q_LORA = 256
MLA_KV_LORA = 256
MLA_WIDTH = MLA_HEADS * MLA_VDIM
MLA_QK = 256
ROPE_THETA = 10000.0
RWKV_HEADS = 4
RWKV_HEAD = 64
RWKV_WIDTH = 256
RWKV_DECAY_LORA = 32
RWKV_AAA_LORA = 32
RWKV_GATE_LORA = 64
RWKV_IN = 3 * RWKV_WIDTH + 128
RWKV_LN_EPS = 64e-5
MLSTM_HEADS = 4
MLSTM_QK = 32
MLSTM_V = 64
MLSTM_WIDTH = 256
MLSTM_CONV = 4
MLSTM_IN = 1024
D_FF = 2816
NORM_EPS = 1e-6
LANES = 128
SUBLANES = 8

MLA_CQ = 0
MLA_CKV = MLA_CQ + MLA_Q_LORA
MLA_KPE = MLA_CKV + MLA_KV_LORA
MLA_KPER = MLA_KPE + LANES
MLA_IN = MLA_KPER + LANES

TM_INPROJ = 1024
TQ_ATTN = 512
TK_ATTN = 512
HP_ATTN = 4
RWKV_CHUNK = 64
RWKV_PREP_CHUNKS = 2
MLSTM_CHUNK = 256
TM_FFN = 1024
TF_FFN = 256
VMEM_LIMIT = 56 * 1024 * 1024


def _cparams(sem):
    return pltpu.CompilerParams(dimension_semantics=sem, vmem_limit_bytes=VMEM_LIMIT)


def _resident(a, layer=None):
    if layer is None:
        nd = a.ndim
        return pl.BlockSpec(a.shape, lambda *_: (0,) * nd, pipeline_mode=pl.Buffered(1))
    nd = a.ndim - 1
    return pl.BlockSpec((None,) + a.shape[1:], lambda *_: (layer,) + (0,) * nd, pipeline_mode=pl.Buffered(1))


def _rows(layer, *refs):
    return [r.at[layer:layer + 1] for r in refs]


def _bdot(a, b):
    return jnp.dot(a.astype(BF16), b.astype(BF16), preferred_element_type=F32)


def _dot_nt(a, b):
    return lax.dot_general(a, b, (((1,), (1,)), ((), ())), preferred_element_type=F32)


def _bdot_tn(a, b):
    return lax.dot_general(a.astype(BF16), b.astype(BF16), (((0,), (0,)), ((), ())),
                           preferred_element_type=F32)


def _split3(x):
    h = x.astype(BF16)
    r1 = x - h.astype(F32)
    m = r1.astype(BF16)
    lo = (r1 - m.astype(F32)).astype(BF16)
    return h, m, lo


def _exact_left_dot(sel, x):
    h, m, lo = _split3(x)
    s = sel.astype(BF16)
    return (jnp.dot(s, h, preferred_element_type=F32) + jnp.dot(s, m, preferred_element_type=F32)
            + jnp.dot(s, lo, preferred_element_type=F32))


def _exact_right_dot(x, sel, parts=3):
    pieces = _split3(x)[:parts]
    s = sel.astype(BF16)
    out = jnp.dot(pieces[0], s, preferred_element_type=F32)
    for p in pieces[1:]:
        out = out + jnp.dot(p, s, preferred_element_type=F32)
    return out


def _rms(x, g):
    return x * lax.rsqrt(jnp.mean(x * x, axis=-1, keepdims=True) + NORM_EPS) * g


def _sigmoid(x):
    return 1.0 / (1.0 + jnp.exp(-x))


def _log_sigmoid(x):
    return jnp.minimum(x, 0.0) - jnp.log1p(jnp.exp(-jnp.abs(x)))


def _div(x, d):
    assert d & (d - 1) == 0
    return lax.shift_right_logical(x, d.bit_length() - 1)


def _mod(x, d):
    assert d & (d - 1) == 0
    return lax.bitwise_and(x, d - 1)


def _shift_rows(x, prev, s, chunk):
    n = x.shape[0]
    row = lax.broadcasted_iota(jnp.int32, x.shape, 0)
    return jnp.where(_mod(row, chunk) >= s, pltpu.roll(x, s, 0), pltpu.roll(prev, n - chunk + s, 0))


(FA_RT, FA_V, FA_AT, FA_KT, FA_BT) = range(5)
(FB_KH, FB_BH, FB_BONUS, FB_GATE) = range(4)
(FT_TA, FT_UV, FT_YV, FT_ARB) = range(4)
RWKV_GROUP = 256


def _segsum(x, seg):
    assert 2 * seg == LANES and x.shape[1] % LANES == 0
    out = []
    for blk in range(x.shape[1] // LANES):
        xb = x[:, blk * LANES:(blk + 1) * LANES]
        low = lax.broadcasted_iota(jnp.int32, xb.shape, 1) < seg
        s_low = jnp.sum(jnp.where(low, xb, 0.0), axis=-1, keepdims=True)
        s_high = jnp.sum(jnp.where(low, 0.0, xb), axis=-1, keepdims=True)
        out.append(jnp.where(low, s_low, s_high))
    return jnp.concatenate(out, axis=-1)


def _rwkv_masks(chunk):
    w = RWKV_WIDTH
    r = lax.broadcasted_iota(jnp.int32, (w, w), 0)
    c = lax.broadcasted_iota(jnp.int32, (w, w), 1)
    return r, c, _div(r, chunk) == _div(c, chunk)


def _rwkv_features(xs, r0, prm, fa_ref, fb_ref, pc_ref, chunk):
    w0_ref, a0_ref, kk_ref, ka_ref, rk_ref, w2_ref, a2_ref, g2_ref = prm
    w = RWKV_WIDTH
    hd = RWKV_HEAD
    n = xs.shape[0]
    r = xs[:, 0:w]
    k = xs[:, w:2 * w]
    v = xs[:, 2 * w:3 * w]
    lor = xs[:, 3 * w:]
    ld = -math.exp(-0.5) * _sigmoid(w0_ref[...] + _bdot(jnp.tanh(lor), w2_ref[...]))
    a = _sigmoid(a0_ref[...] + _bdot(lor, a2_ref[...]))
    g = _bdot(_sigmoid(lor), g2_ref[...])
    kk = k * kk_ref[...]
    kk = kk / jnp.maximum(jnp.sqrt(_segsum(kk * kk, hd)), 1e-12)
    k2 = k * (1.0 + (a - 1.0) * ka_ref[...])
    kb = kk * a
    bonus = _segsum(r * k2 * rk_ref[...], hd) * v

    assert n <= RWKV_WIDTH
    ri, ci, same_chunk = _rwkv_masks(chunk)
    tri = jnp.where(same_chunk & (ci <= ri), 1.0, 0.0)[0:n, 0:n]
    cl = _exact_left_dot(tri, ld)
    units = n // chunk
    cl_last = jnp.concatenate(
        [jnp.broadcast_to(cl[(u + 1) * chunk - 1:(u + 1) * chunk, :], (chunk, w)) for u in range(units)], axis=0)
    e_neg = jnp.exp(-cl)
    e_end = jnp.exp(cl_last - cl)
    rows = slice(r0, r0 + n)

    def put(ref, sec, val):
        ref[rows, sec * w:(sec + 1) * w] = val.astype(ref.dtype)

    put(fa_ref, FA_RT, r * jnp.exp(cl))
    put(fa_ref, FA_V, v)
    put(fa_ref, FA_AT, kk * jnp.exp(cl - ld))
    put(fa_ref, FA_KT, k2 * e_neg)
    put(fa_ref, FA_BT, kb * e_neg)
    put(fb_ref, FB_KH, k2 * e_end)
    put(fb_ref, FB_BH, kb * e_end)
    put(fb_ref, FB_BONUS, bonus)
    put(fb_ref, FB_GATE, g)
    for u in range(units):
        c = r0 // chunk + u
        pc_ref[c:c + 1, :] = jnp.exp(cl_last[u * chunk:u * chunk + 1, :])


def _inproj_kernel(x_ref, g_ref, w_ref, cos_ref, sin_ref, qn_ref, kvn_ref, wq_ref, wkv_ref,
                   mu_ref, w0_ref, a0_ref, kk_ref, ka_ref, rk_ref, w2_ref, a2_ref, g2_ref,
                   q_ref, k_ref, v_ref, fa_ref, fb_ref, pc_ref, mlstm_ref, *rest, layer, chunk, tiles_per_seq):
    *table_out, prev_sc = rest
    tm = x_ref.shape[0]
    g_ref, qn_ref, kvn_ref, mu_ref, w0_ref, a0_ref, kk_ref, ka_ref, rk_ref = _rows(
        layer, g_ref, qn_ref, kvn_ref, mu_ref, w0_ref, a0_ref, kk_ref, ka_ref, rk_ref)

    @pl.when(pl.program_id(0) % tiles_per_seq == 0)
    def _():
        prev_sc[...] = jnp.zeros(prev_sc.shape, F32)

    def make_table_rows(part, parts=4):
        if table_out:
            rows = slice(part * tm // parts, (part + 1) * tm // parts)
            ang = cos_ref[rows, :].astype(F32) * sin_ref[...]
            table_out[0][rows, :] = jnp.cos(ang)
            table_out[1][rows, :] = jnp.sin(ang)

    hb = _rms(x_ref[...], g_ref[...]).astype(BF16)
    make_table_rows(0)
    rw = _dot_nt(hb, w_ref[MLA_IN:MLA_IN + RWKV_IN, :])
    row = lax.broadcasted_iota(jnp.int32, rw.shape, 0)
    shifted = jnp.where(row >= 1, pltpu.roll(rw, 1, 0), prev_sc[0:1, :])
    prev_sc[0:1, :] = rw[tm - 1:tm, :]
    xs = rw + (shifted - rw) * mu_ref[...]
    prm = (w0_ref, a0_ref, kk_ref, ka_ref, rk_ref, w2_ref, a2_ref, g2_ref)

    make_table_rows(1)
    mla = _dot_nt(hb, w_ref[0:MLA_IN, :])
    groups = tm // RWKV_GROUP
    for gi in range(groups // 2):
        _rwkv_features(xs[gi * RWKV_GROUP:(gi + 1) * RWKV_GROUP], gi * RWKV_GROUP, prm, fa_ref, fb_ref, pc_ref, chunk)
    make_table_rows(2)
    mlstm_ref[...] = _dot_nt(hb, w_ref[MLA_IN + RWKV_IN:, :])
    for gi in range(groups // 2, groups):
        _rwkv_features(xs[gi * RWKV_GROUP:(gi + 1) * RWKV_GROUP], gi * RWKV_GROUP, prm, fa_ref, fb_ref, pc_ref, chunk)
    make_table_rows(3)

    cos, sin = (table_out[0][...], table_out[1][...]) if table_out else (cos_ref[...], sin_ref[...])
    scale = (MLA_NOPE + MLA_ROPE) ** -0.5 * math.log2(math.e)
    hw = MLA_HEADS * LANES
    cqn = _rms(mla[:, MLA_CQ:MLA_CKV], qn_ref[...]).astype(BF16)
    q = jnp.dot(cqn, wq_ref[...], preferred_element_type=F32)
    ckvn = _rms(mla[:, MLA_CKV:MLA_KPE], kvn_ref[...]).astype(BF16)
    kv = jnp.dot(ckvn, wkv_ref[...], preferred_element_type=F32)
    kp = (mla[:, MLA_KPE:MLA_KPER] * cos + mla[:, MLA_KPER:MLA_IN] * sin).astype(BF16)
    for h in range(MLA_HEADS):
        c0 = h * LANES
        pe = q[:, hw + c0:hw + c0 + LANES] * cos + q[:, 2 * hw + c0:2 * hw + c0 + LANES] * sin
        q_ref[:, h * MLA_QK:h * MLA_QK + LANES] = (q[:, c0:c0 + LANES] * scale).astype(BF16)
        q_ref[:, h * MLA_QK + LANES:(h + 1) * MLA_QK] = (pe * scale).astype(BF16)
        k_ref[:, h * MLA_QK:h * MLA_QK + LANES] = kv[:, c0:c0 + LANES].astype(BF16)
        k_ref[:, h * MLA_QK + LANES:(h + 1) * MLA_QK] = kp
    v_ref[...] = kv[:, hw:].astype(BF16)


def _inproj(l, x, seq, g, w, positions, tables, qn, kvn, wq, wkv, mu, w0, a0, k_k, k_a, r_k, w2p, a2p, g2p):
    t = x.shape[0]
    tm = TM_INPROJ
    chunk = RWKV_CHUNK
    assert seq % tm == 0 and tm % RWKV_GROUP == 0 and RWKV_GROUP % chunk == 0
    row = lambda width: pl.BlockSpec((tm, width), lambda i: (i, 0))
    rw = RWKV_WIDTH
    if tables is None:
        inv_freq = ROPE_THETA ** (-jnp.arange(0, MLA_ROPE, 2, dtype=F32) / MLA_ROPE)
        cos, sin = positions.reshape(t, 1), jnp.tile(inv_freq, LANES // (MLA_ROPE // 2))[None, :]
        table_specs = [row(1), pl.BlockSpec((1, LANES), lambda i: (0, 0))]
        extra_shapes = (jax.ShapeDtypeStruct((t, LANES), F32),) * 2
        extra_specs = (row(LANES), row(LANES))
    else:
        cos, sin = tables
        table_specs, extra_shapes, extra_specs = [row(LANES), row(LANES)], (), ()
    return pl.pallas_call(
        functools.partial(_inproj_kernel, layer=l, chunk=chunk, tiles_per_seq=seq // tm),
        out_shape=(jax.ShapeDtypeStruct((t, MLA_HEADS * MLA_QK), BF16),
                   jax.ShapeDtypeStruct((t, MLA_HEADS * MLA_QK), BF16),
                   jax.ShapeDtypeStruct((t, MLA_WIDTH), BF16),
                   jax.ShapeDtypeStruct((t, 5 * rw), BF16),
                   jax.ShapeDtypeStruct((t, 4 * rw), BF16),
                   jax.ShapeDtypeStruct((t // chunk, rw), F32),
                   jax.ShapeDtypeStruct((t, MLSTM_IN), F32)) + extra_shapes,
        grid=(t // tm,),
        in_specs=[row(D_MODEL), _resident(g), _resident(w, l)] + table_specs + [_resident(qn),
                  _resident(kvn), _resident(wq, l), _resident(wkv, l),
                  _resident(mu), _resident(w0), _resident(a0), _resident(k_k), _resident(k_a),
                  _resident(r_k), _resident(w2p, l), _resident(a2p, l), _resident(g2p, l)],
        out_specs=(row(MLA_HEADS * MLA_QK), row(MLA_HEADS * MLA_QK), row(MLA_WIDTH), row(5 * rw), row(4 * rw),
                   pl.BlockSpec((tm // chunk, rw), lambda i: (i, 0)), row(MLSTM_IN)) + extra_specs,
        scratch_shapes=[pltpu.VMEM((SUBLANES, RWKV_IN), F32)],
        compiler_params=_cparams(("arbitrary",)),
        name="inproj",
    )(x, g, w, cos, sin, qn, kvn, wq, wkv, mu, w0, a0, k_k, k_a, r_k, w2p, a2p, g2p)


def _attn_kernel(q_ref, k_ref, v_ref, g_ref, *refs, layer, tq, tk, hp):
    n_cast = (len(refs) - 5) // 2
    w_refs, o_ref, wo_refs = refs[:n_cast], refs[n_cast], refs[n_cast + 1:2 * n_cast + 1]
    m_sc, acc_sc, sa_sc, sb_sc = refs[2 * n_cast + 1:]
    for w_ref, wo_ref in zip(w_refs, wo_refs):
        wo_ref[...] = w_ref[...].astype(BF16)
    i = pl.program_id(2)
    m_sc[...] = jnp.full(m_sc.shape, -jnp.inf, F32)
    acc_sc[...] = jnp.zeros(acc_sc.shape, F32)
    sub = m_sc.shape[1]
    hs = range(hp)
    ones = jnp.ones((acc_sc.shape[1] - MLA_VDIM, tk), BF16)
    (g_ref,) = _rows(layer, g_ref)

    def produce(j, s_ref):
        off = pl.multiple_of(j * tk, tk)
        for h in hs:
            s_ref[h] = lax.dot_general(k_ref[0, pl.ds(off, tk), h * MLA_QK:(h + 1) * MLA_QK],
                                       q_ref[0, :, h * MLA_QK:(h + 1) * MLA_QK],
                                       (((1,), (1,)), ((), ())), preferred_element_type=F32)

    def consume(j, s_ref, masked):
        off = pl.multiple_of(j * tk, tk)
        s = [s_ref[h] for h in hs]
        if masked:
            keys = lax.broadcasted_iota(jnp.int32, (tk, tq), 0)
            queries = lax.broadcasted_iota(jnp.int32, (tk, tq), 1)
            s = [jnp.where(keys <= queries, s[h], -jnp.inf) for h in hs]
        m_old = [m_sc[h][0:1] for h in hs]
        m_new = [jnp.maximum(m_old[h], jnp.max(s[h], axis=0, keepdims=True)) for h in hs]
        p = [jnp.exp2(s[h] - m_new[h]) for h in hs]
        alpha = [jnp.exp2(m_old[h] - m_new[h]) for h in hs]
        for h in hs:
            m_sc[h] = jnp.broadcast_to(m_new[h], (sub, tq))
        v1 = [jnp.concatenate([jnp.transpose(v_ref[0, pl.ds(off, tk), h * MLA_VDIM:(h + 1) * MLA_VDIM]), ones],
                              axis=0) for h in hs]
        pv = [jnp.dot(v1[h], p[h].astype(BF16), preferred_element_type=F32) for h in hs]
        for h in hs:
            acc_sc[h] = alpha[h] * acc_sc[h] + pv[h]

    def pair(jj, c):
        j = 2 * jj
        produce(j + 1, sb_sc)
        consume(j, sa_sc, False)
        produce(j + 2, sa_sc)
        consume(j + 1, sb_sc, False)
        return c

    produce(0, sa_sc)
    lax.fori_loop(0, i // 2, pair, 0)

    @pl.when(i % 2 == 0)
    def _():
        consume(i, sa_sc, True)

    @pl.when(i % 2 == 1)
    def _():
        produce(i, sb_sc)
        consume(i - 1, sa_sc, False)
        consume(i, sb_sc, True)

    for h in hs:
        o = acc_sc[h, 0:MLA_VDIM] / acc_sc[h, MLA_VDIM:MLA_VDIM + 1]
        o = o * lax.rsqrt(jnp.mean(o * o, axis=0, keepdims=True) + NORM_EPS)
        o_ref[0, :, h * MLA_VDIM:(h + 1) * MLA_VDIM] = (
            jnp.transpose(o) * g_ref[:, h * MLA_VDIM:(h + 1) * MLA_VDIM]).astype(o_ref.dtype)


def _mla_attention(l, q, k, v, g, batch, seq, weights=()):
    tq, tk, hp = TQ_ATTN, TK_ATTN, HP_ATTN
    assert tq == tk and MLA_VDIM == LANES and hp == MLA_HEADS
    q = q.reshape(batch, seq, MLA_HEADS * MLA_QK)
    k = k.reshape(batch, seq, MLA_HEADS * MLA_QK)
    v = v.reshape(batch, seq, MLA_WIDTH)
    nh, nq = MLA_HEADS // hp, seq // tq
    steps = batch * nh * nq
    flat = [w.reshape(-1, w.shape[-1]) for w in weights]
    rows = [f.shape[0] // steps for f in flat]
    assert all(f.shape[0] % steps == 0 and r % 16 == 0 for f, r in zip(flat, rows))
    slabs = [pl.BlockSpec((r, f.shape[1]), lambda b, h, i: ((b * nh + h) * nq + i, 0)) for f, r in zip(flat, rows)]
    outs = pl.pallas_call(
        functools.partial(_attn_kernel, layer=l, tq=tq, tk=tk, hp=hp),
        out_shape=(jax.ShapeDtypeStruct((batch, seq, MLA_WIDTH), BF16),)
        + tuple(jax.ShapeDtypeStruct(f.shape, BF16) for f in flat),
        grid=(batch, nh, nq),
        in_specs=[pl.BlockSpec((1, tq, hp * MLA_QK), lambda b, h, i: (b, i, h)),
                  pl.BlockSpec((1, seq, hp * MLA_QK), lambda b, h, i: (b, 0, h)),
                  pl.BlockSpec((1, seq, hp * MLA_VDIM), lambda b, h, i: (b, 0, h)),
                  _resident(g)] + slabs,
        out_specs=(pl.BlockSpec((1, tq, hp * MLA_VDIM), lambda b, h, i: (b, i, h)),) + tuple(slabs),
        scratch_shapes=[pltpu.VMEM((hp, SUBLANES, tq), F32),
                        pltpu.VMEM((hp, MLA_VDIM + 2 * SUBLANES, tq), F32),
                        pltpu.VMEM((hp, tk, tq), F32), pltpu.VMEM((hp, tk, tq), F32)],
        compiler_params=_cparams(("parallel", "parallel", "arbitrary")),
        name="mla_attention",
    )(q, k, v, g, *flat)
    return outs[0].reshape(batch * seq, MLA_WIDTH), [o.reshape(w.shape) for o, w in zip(outs[1:], weights)]


def _tile_heads(z):
    return jnp.concatenate([z] * RWKV_HEADS, axis=0)


def _rwkv_chunk_kernel(fa_ref, ft_ref, *, nb, chunk, cps):
    seg = cps * chunk
    n = nb * seg
    w = RWKV_WIDTH
    hd = RWKV_HEAD
    x = fa_ref[...].reshape(n, fa_ref.shape[-1])
    r16, v16, a16, k16, b16 = (x[:, s * w:(s + 1) * w] for s in (FA_RT, FA_V, FA_AT, FA_KT, FA_BT))

    ri, ci, same_chunk = _rwkv_masks(chunk)
    bd = _div(ri, chunk) == _div(ci, hd)
    rt = lax.broadcasted_iota(jnp.int32, (chunk, w), 0)
    cs = _mod(lax.broadcasted_iota(jnp.int32, (chunk, w), 1), chunk)
    strict = cs < rt
    incl = cs <= rt
    c16 = _div(rt, 16) == _div(cs, 16)
    c32 = _div(rt, 32) == _div(cs, 32)
    eye = jnp.where(rt == cs, 1.0, 0.0)
    units = nb * cps

    def block_diag(z):
        return jnp.where(same_chunk, _tile_heads(z), 0.0)

    def put(u, sec, val):
        b, j = divmod(u, cps)
        ft_ref[b, j * chunk:(j + 1) * chunk, sec * w:(sec + 1) * w] = val.astype(ft_ref.dtype)

    def mm(x, y):
        return jnp.dot(x, y, preferred_element_type=F32)

    us = range(units)
    sls = [slice(u * chunk, (u + 1) * chunk) for u in us]
    a_st = [jnp.where(bd, _tile_heads(a16[sl]), 0.0) for sl in sls]
    v_st = [jnp.where(bd, _tile_heads(v16[sl]), 0.0) for sl in sls]
    kb_t = [jnp.concatenate([jnp.transpose(jnp.where(bd, _tile_heads(k16[sl]), 0.0)),
                             jnp.transpose(jnp.where(bd, _tile_heads(b16[sl]), 0.0))], axis=1) for sl in sls]
    sc = [mm(jnp.concatenate([a16[sls[u]], r16[sls[u]]], axis=0), kb_t[u]) for u in us]
    l_ab = [jnp.where(strict, sc[u][0:chunk, w:], 0.0) for u in us]
    l_ak = [jnp.where(strict, sc[u][0:chunk, 0:w], 0.0).astype(BF16) for u in us]
    a_rk = [jnp.where(incl, sc[u][chunk:, 0:w], 0.0).astype(BF16) for u in us]
    for u in us:
        put(u, FT_ARB, jnp.where(incl, sc[u][chunk:, w:], 0.0))
    xm = [-jnp.where(c16, l_ab[u], 0.0) for u in us]
    xm16 = [z.astype(BF16) for z in xm]
    off32 = [block_diag(jnp.where(c32 & jnp.logical_not(c16), l_ab[u], 0.0).astype(BF16)) for u in us]
    off64 = [block_diag(jnp.where(jnp.logical_not(c32), l_ab[u], 0.0).astype(BF16)) for u in us]
    x2 = [mm(xm16[u], block_diag(xm16[u])).astype(BF16) for u in us]
    x2_bd = [block_diag(z) for z in x2]
    lv = [mm(jnp.concatenate([l_ak[u], a_rk[u]], axis=0), v_st[u]) for u in us]
    wv = [block_diag(lv[u][0:chunk].astype(BF16)) for u in us]
    t_lo = [eye + xm[u] for u in us]
    tx = [mm(jnp.concatenate([t_lo[u].astype(BF16), x2[u]], axis=0), x2_bd[u]) for u in us]
    t_lo = [(t_lo[u] + tx[u][0:chunk]).astype(BF16) for u in us]
    x4 = [tx[u][chunk:] for u in us]
    x4b = [z.astype(BF16) for z in x4]
    x4_bd = [block_diag(z) for z in x4b]
    for u in us:
        put(u, FT_YV, lv[u][chunk:])
    x8_bd = [block_diag(mm(x4b[u], x4_bd[u]).astype(BF16)) for u in us]
    t_hi = [eye + x4[u] for u in us]
    t_hi = [block_diag((t_hi[u] + mm(t_hi[u].astype(BF16), x8_bd[u])).astype(BF16)) for u in us]
    t_inv = [mm(t_lo[u], t_hi[u]) for u in us]
    for off in (off32, off64):
        tb = [z.astype(BF16) for z in t_inv]
        mid = [mm(tb[u], off[u]).astype(BF16) for u in us]
        t_inv = [t_inv[u] - mm(mid[u], block_diag(tb[u])) for u in us]
    tb = [z.astype(BF16) for z in t_inv]
    for u in us:
        put(u, FT_TA, mm(tb[u], a_st[u]))
    for u in us:
        put(u, FT_UV, mm(tb[u], wv[u]))


def _rwkv_scan_stages(fa_ref, fb_ref, ft_ref, pc_ref, lnw_ref, lnb_ref, o_ref, state_sc, *, layer, nb, chunk, cps):
    seg = cps * chunk
    n = nb * seg
    w = RWKV_WIDTH
    hd = RWKV_HEAD
    lnw_ref, lnb_ref = _rows(layer, lnw_ref, lnb_ref)
    ri, ci, _ = _rwkv_masks(chunk)
    bd = _div(ri, chunk) == _div(ci, hd)
    bdv = _div(ri, hd) == _div(ci, hd)

    def sec(ref, b, j, s):
        return ref[b, j * chunk:(j + 1) * chunk, s * w:(s + 1) * w]

    ys = [[None] * cps for _ in range(nb)]
    bs = range(nb)
    for j in range(cps):
        gs = [state_sc[b] for b in bs]
        p1 = [lax.dot_general(jnp.concatenate([sec(ft_ref, b, j, FT_TA), sec(fa_ref, b, j, FA_RT)], axis=0),
                              gs[b].astype(BF16), (((1,), (1,)), ((), ())), preferred_element_type=F32)
              for b in bs]
        yield
        u = [(p1[b][0:chunk] + sec(ft_ref, b, j, FT_UV).astype(F32)).astype(BF16) for b in bs]
        upd = [lax.dot_general(jnp.concatenate([sec(fa_ref, b, j, FA_V), -u[b]], axis=0),
                               jnp.concatenate([sec(fb_ref, b, j, FB_KH), sec(fb_ref, b, j, FB_BH)], axis=0),
                               (((0,), (0,)), ((), ())), preferred_element_type=F32) for b in bs]
        for b in bs:
            state_sc[b] = gs[b] * pc_ref[b, pl.ds(pl.program_id(0) * cps + j, 1), :] + jnp.where(bdv, upd[b], 0.0)
        for b in bs:
            u_st = jnp.where(bd, _tile_heads(u[b]), 0.0)
            ys[b][j] = (p1[b][chunk:] + sec(ft_ref, b, j, FT_YV).astype(F32)
                        - jnp.dot(sec(ft_ref, b, j, FT_ARB), u_st, preferred_element_type=F32))
        yield

    y = jnp.concatenate([ys[b][j] for b in range(nb) for j in range(cps)], axis=0)
    mean = _segsum(y, hd) * (1.0 / hd)
    d = y - mean
    var = _segsum(d * d, hd) * (1.0 / hd)
    yn = d * lax.rsqrt(var + RWKV_LN_EPS) * lnw_ref[...] + lnb_ref[...]
    bonus = fb_ref[:, :, FB_BONUS * w:(FB_BONUS + 1) * w].reshape(n, w).astype(F32)
    gate = fb_ref[:, :, FB_GATE * w:(FB_GATE + 1) * w].reshape(n, w).astype(F32)
    o_ref[...] = ((yn + bonus) * gate).astype(o_ref.dtype).reshape(o_ref.shape)


def _rwkv_chunk(fa, batch, seq):
    chunk = RWKV_CHUNK
    w = RWKV_WIDTH
    assert RWKV_HEADS * chunk == w
    cps = RWKV_PREP_CHUNKS
    seg = cps * chunk
    return pl.pallas_call(
        functools.partial(_rwkv_chunk_kernel, nb=batch, chunk=chunk, cps=cps),
        out_shape=jax.ShapeDtypeStruct((batch, seq, 4 * w), BF16),
        grid=(seq // seg,),
        in_specs=[pl.BlockSpec((batch, seg, 5 * w), lambda c: (0, c, 0))],
        out_specs=pl.BlockSpec((batch, seg, 4 * w), lambda c: (0, c, 0)),
        compiler_params=_cparams(("parallel",)),
        name="rwkv7_chunk",
    )(fa.reshape(batch, seq, 5 * w))


ML_QK = 0
ML_V = 2 * MLSTM_HEADS * MLSTM_QK
ML_O = ML_V + MLSTM_WIDTH
ML_I = ML_O + MLSTM_WIDTH
ML_F = ML_I + LANES


def _cummax_rows(x):
    n = x.shape[0]
    row = lax.broadcasted_iota(jnp.int32, x.shape, 0)
    sh = 1
    while sh < n:
        x = jnp.maximum(x, jnp.where(row >= sh, pltpu.roll(x, sh, 0), -jnp.inf))
        sh *= 2
    return x


def _mlstm_stages(x_ref, cw_ref, cb_ref, ib_ref, fb_ref, on_ref, o_ref,
                  prev_sc, c_sc, n_sc, m_sc, *, layer, nb, chunk):
    n = nb * chunk
    nh = MLSTM_HEADS
    dk = MLSTM_QK
    dv = MLSTM_V
    qkw = nh * dk
    vw = MLSTM_WIDTH
    cb_ref, ib_ref, fb_ref, on_ref = _rows(layer, cb_ref, ib_ref, fb_ref, on_ref)
    x = x_ref[...].reshape(n, MLSTM_IN)
    qk_raw = x[:, ML_QK:ML_V]
    prev = prev_sc[...]
    conv = cb_ref[...] + qk_raw * cw_ref[MLSTM_CONV - 1:MLSTM_CONV, :]
    for s in range(1, MLSTM_CONV):
        conv = conv + _shift_rows(qk_raw, prev, s, chunk) * cw_ref[MLSTM_CONV - 1 - s:MLSTM_CONV - s, :]
    prev_sc[...] = qk_raw
    qk = conv * _sigmoid(conv)
    q_all = qk[:, 0:qkw] * (dk ** -0.5)
    k_all = qk[:, qkw:]
    v_all = x[:, ML_V:ML_O]
    o_pre = x[:, ML_O:ML_I]
    li_all = x[:, ML_I:ML_F] + ib_ref[...]
    lf_all = _log_sigmoid(x[:, ML_F:ML_F + LANES] + fb_ref[...])

    ri = lax.broadcasted_iota(jnp.int32, (chunk, chunk), 0)
    ci = lax.broadcasted_iota(jnp.int32, (chunk, chunk), 1)
    causal = ci <= ri
    tri = jnp.where(causal, 1.0, 0.0)
    lane_k = lax.broadcasted_iota(jnp.int32, (chunk, qkw), 1)
    lane_v = lax.broadcasted_iota(jnp.int32, (chunk, vw), 1)
    rc = lax.broadcasted_iota(jnp.int32, (qkw, vw), 0)
    cc = lax.broadcasted_iota(jnp.int32, (qkw, vw), 1)
    cmask = _div(rc, dk) == _div(cc, dv)
    expand_v = jnp.where(rc == _div(cc, dv), 1.0, 0.0).astype(BF16)
    rk = lax.broadcasted_iota(jnp.int32, (qkw, qkw), 0)
    ck = lax.broadcasted_iota(jnp.int32, (qkw, qkw), 1)
    expand_k = jnp.where(rk == _div(ck, dk), 1.0, 0.0).astype(BF16)
    gather_k = jnp.where(_div(rk, dk) == ck, 1.0, 0.0).astype(BF16)

    bs = range(nb)
    sls = [slice(b * chunk, (b + 1) * chunk) for b in bs]
    q = [q_all[sl] for sl in sls]
    k = [k_all[sl] for sl in sls]
    k16 = [z.astype(BF16) for z in k]
    v = [v_all[sl] for sl in sls]
    li = [li_all[sl] for sl in sls]
    c_old = [c_sc[b] for b in bs]
    n_old = [n_sc[b] for b in bs]
    m_prev = [m_sc[b] for b in bs]
    g = [_exact_left_dot(tri, lf_all[sl]) for sl in sls]
    lig = [li[b] - g[b] for b in bs]
    inter_log = [g[b] + m_prev[b] for b in bs]
    m_t = [jnp.maximum(inter_log[b], g[b] + _cummax_rows(lig[b])) for b in bs]
    inter_w = [jnp.exp(inter_log[b] - m_t[b]) for b in bs]
    log2e = math.log2(math.e)
    gm = [(g[b] - m_t[b]) * log2e for b in bs]
    lig_t = [jnp.transpose(z * log2e) for z in lig]
    qn = [_exact_right_dot(q[b] * n_old[b], gather_k, parts=2) for b in bs]
    q_c = [_bdot(q[b], c_old[b]) for b in bs]
    ssum = [jnp.zeros((chunk, LANES), F32) for _ in bs]
    num = [jnp.zeros((chunk, vw), F32) for _ in bs]
    yield
    for h in range(nh):
        mk = (lane_k >= h * dk) & (lane_k < (h + 1) * dk)
        mv = (lane_v >= h * dv) & (lane_v < (h + 1) * dv)
        qk_h = [lax.dot_general(jnp.where(mk, q[b], 0.0).astype(BF16), k16[b], (((1,), (1,)), ((), ())),
                                preferred_element_type=F32) for b in bs]
        d = [jnp.broadcast_to(gm[b][:, h:h + 1], (chunk, chunk)) + lig_t[b][h:h + 1, :] for b in bs]
        s = [qk_h[b] * jnp.exp2(jnp.where(causal, d[b], -jnp.inf)) for b in bs]
        ssum = [jnp.where(lane_k == h, jnp.sum(s[b], axis=-1, keepdims=True), ssum[b]) for b in bs]
        num = [num[b] + _bdot(s[b], jnp.where(mv, v[b], 0.0)) for b in bs]
        yield
    den = [inter_w[b] * qn[b] + ssum[b] for b in bs]
    rden = [1.0 / jnp.maximum(jnp.abs(den[b]), jnp.exp(-m_t[b])) for b in bs]
    g_last = [g[b][chunk - 1:chunk, :] for b in bs]
    a_all = [g_last[b] - g[b] + li[b] for b in bs]
    m_new = [jnp.maximum(g_last[b] + m_prev[b], jnp.max(a_all[b], axis=0, keepdims=True)) for b in bs]
    dec = [jnp.exp(g_last[b] + m_prev[b] - m_new[b]) for b in bs]
    wts = [jnp.exp(a_all[b] - m_new[b]) for b in bs]
    per_head = [jnp.concatenate([inter_w[b], rden[b], wts[b], jnp.broadcast_to(dec[b], (8, LANES))], axis=0)
                for b in bs]
    pieces = [_split3(per_head[b])[:2] for b in bs]
    on_v = [sum(jnp.dot(p, expand_v, preferred_element_type=F32) for p in pieces[b]) for b in bs]
    on_k = [sum(jnp.dot(p[2 * chunk:], expand_k, preferred_element_type=F32) for p in pieces[b]) for b in bs]
    hs = [(on_v[b][0:chunk] * q_c[b] + num[b]) * on_v[b][chunk:2 * chunk] for b in bs]
    for b in bs:
        c_sc[b] = c_old[b] * on_v[b][3 * chunk:3 * chunk + 1] + jnp.where(
            cmask, _bdot_tn(k16[b], on_v[b][2 * chunk:3 * chunk] * v[b]), 0.0)
        n_sc[b] = n_old[b] * on_k[b][chunk:chunk + 1] + jnp.sum(on_k[b][0:chunk] * k[b], axis=0, keepdims=True)
        m_sc[b] = m_new[b]

    hh = jnp.concatenate(hs, axis=0)
    ms = _segsum(hh * hh, dv) * (1.0 / dv)
    out = hh * lax.rsqrt(ms + NORM_EPS) * on_ref[...] * _sigmoid(o_pre)
    o_ref[...] = out.astype(o_ref.dtype).reshape(o_ref.shape)


def _recurrent_kernel(x_ref, cw_ref, cb_ref, ib_ref, fb_ref, on_ref, fa_ref, fbk_ref, ft_ref, pc_ref, lnw_ref, lnb_ref,
                      om_ref, or_ref, prev_sc, c_sc, n_sc, m_sc, state_sc, *, layer, nb, chunk, rwkv_chunk, cps):
    @pl.when(pl.program_id(0) == 0)
    def _():
        for sc in (prev_sc, c_sc, n_sc, m_sc, state_sc):
            sc[...] = jnp.zeros(sc.shape, F32)

    scan = _rwkv_scan_stages(fa_ref, fbk_ref, ft_ref, pc_ref, lnw_ref, lnb_ref, or_ref, state_sc,
                             layer=layer, nb=nb, chunk=rwkv_chunk, cps=cps)
    mlstm = _mlstm_stages(x_ref, cw_ref, cb_ref, ib_ref, fb_ref, on_ref, om_ref, prev_sc, c_sc, n_sc, m_sc,
                          layer=layer, nb=nb, chunk=chunk)
    order = [scan, mlstm] + [scan, scan, mlstm] * (MLSTM_HEADS - 1) + [scan, mlstm]
    pending = {id(scan): 2 * cps + 1, id(mlstm): MLSTM_HEADS + 2}
    for g in order:
        next(g, None)
        pending[id(g)] -= 1
    for g in (scan, mlstm):
        for _ in range(pending[id(g)]):
            next(g, None)


def _recurrent(l, mlstm_in, fa, fb, ft, pc, batch, seq, cw, cb, ib, fbias, on, ln_w, ln_b):
    chunk = MLSTM_CHUNK
    rchunk = RWKV_CHUNK
    assert chunk % rchunk == 0
    w = RWKV_WIDTH
    x = mlstm_in.reshape(batch, seq, MLSTM_IN)
    fa = fa.reshape(batch, seq, 5 * w)
    fb = fb.reshape(batch, seq, 4 * w)
    pc = pc.reshape(batch, seq // rchunk, w)
    blk = lambda width: pl.BlockSpec((batch, chunk, width), lambda c: (0, c, 0))
    om, orw = pl.pallas_call(
        functools.partial(_recurrent_kernel, layer=l, nb=batch, chunk=chunk, rwkv_chunk=rchunk, cps=chunk // rchunk),
        out_shape=(jax.ShapeDtypeStruct((batch, seq, MLSTM_WIDTH), BF16),
                   jax.ShapeDtypeStruct((batch, seq, w), BF16)),
        grid=(seq // chunk,),
        in_specs=[blk(MLSTM_IN), _resident(cw, l), _resident(cb), _resident(ib), _resident(fbias), _resident(on),
                  blk(2 * w), blk(4 * w), blk(4 * w), _resident(pc), _resident(ln_w), _resident(ln_b)],
        out_specs=(blk(MLSTM_WIDTH), blk(w)),
        scratch_shapes=[pltpu.VMEM((batch * chunk, 2 * MLSTM_HEADS * MLSTM_QK), F32),
                        pltpu.VMEM((batch, MLSTM_HEADS * MLSTM_QK, MLSTM_WIDTH), F32),
                        pltpu.VMEM((batch, 1, MLSTM_HEADS * MLSTM_QK), F32),
                        pltpu.VMEM((batch, 1, LANES), F32),
                        pltpu.VMEM((batch, w, w), F32)],
        compiler_params=_cparams(("arbitrary",)),
        name="mlstm_rwkv_scan",
    )(x, cw, cb, ib, fbias, on, fa, fb, ft, pc, ln_w, ln_b)
    return om.reshape(batch * seq, MLSTM_WIDTH), orw.reshape(batch * seq, w)


def _ffn_kernel(x_ref, ya_ref, yb_ref, yc_ref, wo_ref, g_ref, wg_ref, wu_ref, wd_ref, fg_ref,
                o_ref, act_sc, *, layer, final_norm, tf):
    (g_ref,) = _rows(layer, g_ref)
    y = jnp.concatenate([ya_ref[...], yb_ref[...], yc_ref[...]], axis=-1)
    x1 = x_ref[...] + jnp.dot(y, wo_ref[...], preferred_element_type=F32)
    h = _rms(x1, g_ref[...]).astype(BF16)
    for c in range(D_FF // tf):
        gate = jnp.dot(h, wg_ref[:, c * tf:(c + 1) * tf], preferred_element_type=F32)
        up = jnp.dot(h, wu_ref[:, c * tf:(c + 1) * tf], preferred_element_type=F32)
        act_sc[:, c * tf:(c + 1) * tf] = (gate * _sigmoid(gate) * up).astype(BF16)
    out = x1 + jnp.dot(act_sc[...], wd_ref[...], preferred_element_type=F32)
    if final_norm:
        out = _rms(out, fg_ref[...])
    o_ref[...] = out


def _out_ffn(l, x, ya, yb, yc, wo, g, wg, wu, wd, fg, final_norm):
    t = x.shape[0]
    tm, tf = TM_FFN, TF_FFN
    row = lambda w: pl.BlockSpec((tm, w), lambda i: (i, 0))
    return pl.pallas_call(
        functools.partial(_ffn_kernel, layer=l, final_norm=final_norm, tf=tf),
        out_shape=jax.ShapeDtypeStruct((t, D_MODEL), F32),
        grid=(t // tm,),
        in_specs=[row(D_MODEL), row(MLA_WIDTH), row(RWKV_WIDTH), row(MLSTM_WIDTH), _resident(wo, l),
                  _resident(g), _resident(wg, l), _resident(wu, l), _resident(wd, l), _resident(fg)],
        out_specs=row(D_MODEL),
        scratch_shapes=[pltpu.VMEM((tm, D_FF), BF16)],
        compiler_params=_cparams(("parallel",)),
        name="out_ffn",
    )(x, ya, yb, yc, wo, g, wg, wu, wd, fg)


def _pad_cols(w, width):
    return jnp.pad(w, [(0, 0)] * (w.ndim - 1) + [(0, width - w.shape[-1])])


def _pad_rows(w, height):
    return jnp.pad(w, [(0, 0)] * (w.ndim - 2) + [(0, height - w.shape[-2]), (0, 0)])


def _rot_half_cols(w):
    half = w.shape[-1] // 2
    return jnp.concatenate([-w[..., half:], w[..., :half]], axis=-1)


def _stacked_weights(w_in, mla_w_uq, mla_w_ukv, rwkv_w2, rwkv_a2, rwkv_g2):
    depth = w_in.shape[0]
    wt = jnp.swapaxes(w_in, 1, 2)
    c_q, c_kv, k_pe = wt[:, 0:256], wt[:, 256:512], wt[:, 512:576]
    rw = wt[:, 576:1472]
    ml = wt[:, 1472:2248]
    k_pe_rot = jnp.swapaxes(_rot_half_cols(jnp.swapaxes(k_pe, 1, 2)), 1, 2)
    w_mla = jnp.concatenate([c_q, c_kv, _pad_rows(k_pe, LANES), _pad_rows(k_pe_rot, LANES)], axis=1)
    w_mlstm = jnp.concatenate([ml[:, 0:256], ml[:, 256:512], ml[:, 520:776],
                               _pad_rows(ml[:, 512:516], LANES), _pad_rows(ml[:, 516:520], LANES)], axis=1)
    w_all = jnp.concatenate([w_mla, rw, w_mlstm], axis=1).astype(BF16)

    uq = mla_w_uq.reshape(depth, MLA_Q_LORA, MLA_HEADS, MLA_NOPE + MLA_ROPE)
    nope = uq[..., :MLA_NOPE].reshape(depth, MLA_Q_LORA, MLA_HEADS * MLA_NOPE)
    pe = _pad_cols(uq[..., MLA_NOPE:], LANES).reshape(depth, MLA_Q_LORA, MLA_HEADS * LANES)
    per = _pad_cols(_rot_half_cols(uq[..., MLA_NOPE:]), LANES).reshape(depth, MLA_Q_LORA, MLA_HEADS * LANES)
    wq = jnp.concatenate([nope, pe, per], axis=-1).astype(BF16)
    ukv = mla_w_ukv.reshape(depth, MLA_KV_LORA, MLA_HEADS, MLA_NOPE + MLA_VDIM)
    wkv = jnp.concatenate([ukv[..., :MLA_NOPE].reshape(depth, MLA_KV_LORA, -1),
                           ukv[..., MLA_NOPE:].reshape(depth, MLA_KV_LORA, -1)], axis=-1).astype(BF16)

    assert RWKV_DECAY_LORA + RWKV_AAA_LORA + RWKV_GATE_LORA == LANES
    rows = lambda before, wt: jnp.pad(wt, ((0, 0), (before, LANES - before - wt.shape[1]), (0, 0))).astype(BF16)
    w2p = rows(0, rwkv_w2)
    a2p = rows(RWKV_DECAY_LORA, rwkv_a2)
    g2p = rows(RWKV_DECAY_LORA + RWKV_AAA_LORA, rwkv_g2)
    return w_all, wq, wkv, w2p, a2p, g2p


def kernel(x, positions, mix_norm, w_in, mla_q_norm, mla_w_uq, mla_kv_norm, mla_w_ukv, mla_out_norm, rwkv_mu, rwkv_w0, rwkv_w2, rwkv_a0, rwkv_a2, rwkv_g2, rwkv_k_k, rwkv_k_a, rwkv_r_k, rwkv_ln_w, rwkv_ln_b, mlstm_conv_w, mlstm_conv_b, mlstm_i_bias, mlstm_f_bias, mlstm_out_norm, w_out, ffn_norm, w_gate, w_up, w_down, final_norm):
    batch, seq, _ = x.shape
    depth = w_in.shape[0]
    xt = x.reshape(batch * seq, D_MODEL)
    tables = None
    w_all, wq, wkv, w2p, a2p, g2p = _stacked_weights(w_in, mla_w_uq, mla_w_ukv, rwkv_w2, rwkv_a2, rwkv_g2)
    ml_ib = _pad_cols(mlstm_i_bias, LANES)
    ml_fb = _pad_cols(mlstm_f_bias, LANES)
    for l in range(depth):
        q, k, v, fa, fb, pc, mlstm_in, *made = _inproj(
            l, xt, seq, mix_norm, w_all, positions, tables, mla_q_norm, mla_kv_norm, wq, wkv,
            rwkv_mu, rwkv_w0, rwkv_a0, rwkv_k_k, rwkv_k_a, rwkv_r_k,
            w2p, a2p, g2p)
        tables = tuple(made) or tables
        y_mla, cast = _mla_attention(l, q, k, v, mla_out_norm, batch, seq,
                                     (w_out, w_gate, w_up, w_down) if l == 0 else ())
        if l == 0:
            wo, wg, wu, wd = cast
        ft = _rwkv_chunk(fa, batch, seq)
        y_mlstm, y_rwkv = _recurrent(l, mlstm_in, fa, fb, ft, pc, batch, seq, mlstm_conv_w, mlstm_conv_b, ml_ib, ml_fb,
                                     mlstm_out_norm, rwkv_ln_w, rwkv_ln_b)
        xt = _out_ffn(l, xt, y_mla, y_rwkv, y_mlstm, wo, ffn_norm, wg, wu, wd,
                      final_norm.reshape(1, -1), final_norm=(l == depth - 1))
    return xt.reshape(batch, seq, D_MODEL)
```
